```python
import math
import jax, jax.numpy as jnp
from jax import lax
import numpy as np

D_MODEL = 1024
BATCH = 4
SEQ = 4096
DEPTH = 1

N_MLSTM_HEADS = 4
MLSTM_HEAD_DIM = 128
D_MLSTM = N_MLSTM_HEADS * MLSTM_HEAD_DIM
MLSTM_CHUNK = 64
CONV_WIDTH = 3
N_MLA_HEADS = 8
MLA_NOPE_DIM = 64
MLA_ROPE_DIM = 32
MLA_QK_DIM = MLA_NOPE_DIM + MLA_ROPE_DIM
MLA_V_DIM = 64
D_MLA = N_MLA_HEADS * MLA_V_DIM
Q_LORA_RANK = 256
KV_LORA_RANK = 128
ROPE_THETA = 10000.0
Q_BLOCK = 128
D_MIX = D_MLSTM + D_MLA
N_GATE = 4 * N_MLSTM_HEADS
IN_SIZES = [D_MLSTM, D_MLSTM, D_MLSTM, D_MLSTM, N_GATE, Q_LORA_RANK, KV_LORA_RANK, MLA_ROPE_DIM]
IN_OFFSETS = np.cumsum(IN_SIZES)[:-1].tolist()
D_IN = int(sum(IN_SIZES))
N_MEM = 256
N_MEM_HEADS = 4
MEM_HEAD_DIM = 256
D_MEM_INNER = N_MEM_HEADS * MEM_HEAD_DIM
N_EXPERTS = 16
EC_CAPACITY_FACTOR = 2
D_FF_EXPERT = 2048
EPS = 1e-6

kernel_name = "hybrid_mlstm_mla_memxattn_ec_moe"


def rmsnorm(x, g):
    xf = x.astype(jnp.float32)
    y = xf * lax.rsqrt(jnp.mean(xf * xf, axis=-1, keepdims=True) + EPS)
    return (y * g).astype(x.dtype)


def head_rmsnorm(t, g):
    B, S, H, d = t.shape
    tf = t.astype(jnp.float32)
    y = tf * lax.rsqrt(jnp.mean(tf * tf, axis=-1, keepdims=True) + EPS)
    return (y.reshape(B, S, H * d) * g).astype(t.dtype)


def rope_tables(positions):
    inv_freq = ROPE_THETA ** (-jnp.arange(0, MLA_ROPE_DIM, 2, dtype=jnp.float32) / MLA_ROPE_DIM)
    ang = positions.astype(jnp.float32)[..., None] * inv_freq
    return jnp.cos(ang), jnp.sin(ang)


def apply_rope(t, cos, sin):
    tf = t.astype(jnp.float32)
    t1, t2 = jnp.split(tf, 2, axis=-1)
    return jnp.concatenate([t1 * cos - t2 * sin, t2 * cos + t1 * sin], axis=-1).astype(t.dtype)


def centred_conv(u, w):
    S = u.shape[1]
    pad = CONV_WIDTH // 2
    up = jnp.pad(u, ((0, 0), (pad, pad), (0, 0)))
    out = up[:, 0:S] * w[0]
    for j in range(1, CONV_WIDTH):
        out = out + up[:, j:j + S] * w[j]
    return out


def mlstm_chunkwise(q, k, v, log_i, log_f):
    B, H, S, d = q.shape
    T = MLSTM_CHUNK
    NC = S // T
    chunk = lambda t: jnp.moveaxis(t.reshape(B, H, NC, T, *t.shape[3:]), 2, 0)
    mask = jnp.tril(jnp.ones((T, T), dtype=bool))

    def step(carry, inp):
        C, n, m = carry
        qc, kc, vc, li, lf = inp
        b = jnp.cumsum(lf, axis=-1)
        g = b[..., -1]
        Dm = b[..., :, None] - b[..., None, :] + li[..., None, :]
        Dm = jnp.where(mask, Dm, -jnp.inf)
        inter = b + m[..., None]
        m_t = jnp.maximum(inter, jnp.max(Dm, axis=-1))
        Pw = jnp.exp(Dm - m_t[..., None]) * jnp.einsum('bhtd,bhsd->bhts', qc, kc)
        inter_w = jnp.exp(inter - m_t)
        num = inter_w[..., None] * jnp.einsum('bhtd,bhde->bhte', qc, C) + jnp.einsum('bhts,bhse->bhte', Pw, vc)
        den = inter_w * jnp.einsum('bhtd,bhd->bht', qc, n) + jnp.sum(Pw, axis=-1)
        h = num / jnp.maximum(jnp.abs(den), jnp.exp(-m_t))[..., None]
        w_s = g[..., None] - b + li
        m_new = jnp.maximum(g + m, jnp.max(w_s, axis=-1))
        decay = jnp.exp(g + m - m_new)
        ws = jnp.exp(w_s - m_new[..., None])
        C_new = decay[..., None, None] * C + jnp.einsum('bhs,bhsd,bhse->bhde', ws, kc, vc)
        n_new = decay[..., None] * n + jnp.einsum('bhs,bhsd->bhd', ws, kc)
        return (C_new, n_new, m_new), h

    init = (jnp.zeros((B, H, d, d), jnp.float32), jnp.zeros((B, H, d), jnp.float32), jnp.zeros((B, H), jnp.float32))
    _, hs = lax.scan(step, init, (chunk(q), chunk(k), chunk(v), chunk(log_i), chunk(log_f)))
    return jnp.moveaxis(hs, 0, 2).reshape(B, H, S, d)


def mla_attention(qn, qr, kn, kr, v):
    B, S, H, _ = qn.shape
    nb = S // Q_BLOCK
    scale = MLA_QK_DIM ** -0.5
    blk = lambda t: t.reshape(B, nb, Q_BLOCK, *t.shape[2:]).swapaxes(0, 1)

    def one(args):
        qn_b, qr_b = args
        s = jnp.einsum('bqhd,bkhd->bhqk', qn_b, kn) + jnp.einsum('bqhr,bkr->bhqk', qr_b, kr)
        p = jax.nn.softmax(s.astype(jnp.float32) * scale, axis=-1).astype(v.dtype)
        return jnp.einsum('bhqk,bkhd->bqhd', p, v)

    out = lax.map(one, (blk(qn), blk(qr)))
    return out.swapaxes(0, 1).reshape(B, S, H, MLA_V_DIM)


def hybrid_mixer(h, cos, sin, w_in, b_gates, conv_qk, g_q_a, w_q_b, g_kv_a, w_kv_b,
                 g_head_mlstm, g_head_mla, w_out):
    B, S, _ = h.shape
    proj = h @ w_in
    q_m, k_m, v_m, o_m, gates, c_q, c_kv, k_rope = jnp.split(proj, IN_OFFSETS, axis=-1)

    qk = jax.nn.silu(centred_conv(jnp.concatenate([q_m, k_m], axis=-1), conv_qk))
    q_m, k_m = jnp.split(qk, 2, axis=-1)
    to_heads = lambda t: t.reshape(B, S, N_MLSTM_HEADS, MLSTM_HEAD_DIM).transpose(0, 2, 1, 3).astype(jnp.float32)
    qh = to_heads(q_m)
    kh = to_heads(k_m) * (MLSTM_HEAD_DIM ** -0.5)
    vh = to_heads(v_m)
    gts = (gates + b_gates).astype(jnp.float32).reshape(B, S, 4, N_MLSTM_HEADS).transpose(2, 0, 3, 1)
    i_f, f_f, i_b, f_b = gts[0], gts[1], gts[2], gts[3]
    h_fwd = mlstm_chunkwise(qh, kh, vh, i_f, jax.nn.log_sigmoid(f_f))
    flip = lambda t: jnp.flip(t, axis=2)
    h_bwd = flip(mlstm_chunkwise(flip(qh), flip(kh), flip(vh), flip(i_b), flip(jax.nn.log_sigmoid(f_b))))
    h_m = (h_fwd + h_bwd).transpose(0, 2, 1, 3).astype(h.dtype)
    y_m = jax.nn.sigmoid(o_m) * head_rmsnorm(h_m, g_head_mlstm)

    q = (rmsnorm(c_q, g_q_a) @ w_q_b).reshape(B, S, N_MLA_HEADS, MLA_QK_DIM)
    qn, qr = q[..., :MLA_NOPE_DIM], q[..., MLA_NOPE_DIM:]
    qr = apply_rope(qr, cos[:, :, None], sin[:, :, None])
    kv = (rmsnorm(c_kv, g_kv_a) @ w_kv_b).reshape(B, S, N_MLA_HEADS, MLA_NOPE_DIM + MLA_V_DIM)
    kn, v = kv[..., :MLA_NOPE_DIM], kv[..., MLA_NOPE_DIM:]
    kr = apply_rope(k_rope, cos, sin)
    y_a = head_rmsnorm(mla_attention(qn, qr, kn, kr, v), g_head_mla)

    return jnp.concatenate([y_m, y_a], axis=-1) @ w_out


def memory_cross_attention(h, m, w_q, w_k, w_v, w_o):
    B, S, _ = h.shape
    M = m.shape[1]
    q = (h @ w_q).reshape(B, S, N_MEM_HEADS, MEM_HEAD_DIM)
    k = (m @ w_k).reshape(B, M, N_MEM_HEADS, MEM_HEAD_DIM)
    v = (m @ w_v).reshape(B, M, N_MEM_HEADS, MEM_HEAD_DIM)
    s = jnp.einsum('bqhd,bkhd->bhqk', q, k).astype(jnp.float32) * (MEM_HEAD_DIM ** -0.5)
    p = jax.nn.softmax(s, axis=-1).astype(v.dtype)
    o = jnp.einsum('bhqk,bkhd->bqhd', p, v).reshape(B, S, D_MEM_INNER)
    return o @ w_o


def expert_choice_ffn(h, w_router, w_gate, w_up, w_down):
    B, S, D = h.shape
    cap = EC_CAPACITY_FACTOR * S // N_EXPERTS
    aff = jax.nn.softmax((h @ w_router).astype(jnp.float32), axis=-1)
    gate_vals, idx = lax.top_k(aff.transpose(0, 2, 1), cap)
    xe = jax.vmap(lambda t, i: t[i])(h, idx)
    hid = jax.nn.silu(jnp.einsum('becd,edf->becf', xe, w_gate)) * jnp.einsum('becd,edf->becf', xe, w_up)
    ye = jnp.einsum('becf,efd->becd', hid, w_down) * gate_vals[..., None].astype(h.dtype)
    return jax.vmap(lambda y, i: jnp.zeros((S, D), y.dtype).at[i.reshape(-1)].add(y.reshape(-1, D)))(ye, idx)


def setup_inputs(seed: int = 0) -> dict:
    key = jax.random.key(seed)
    ks = jax.random.split(key, 32)
    f32 = jnp.float32

    def w(k, shape, fan_in):
        return jax.random.normal(k, (DEPTH,) + shape, f32) * (fan_in ** -0.5)

    def gain(k, n, depth=True):
        shape = (DEPTH, n) if depth else (n,)
        return 1.0 + 0.05 * jax.random.normal(k, shape, f32)

    x = jax.random.normal(ks[0], (BATCH, SEQ, D_MODEL), f32)
    mem = jax.random.normal(ks[1], (BATCH, N_MEM, D_MODEL), f32)
    offsets = jax.random.randint(ks[2], (BATCH, 1), 0, 1024, dtype=jnp.int32)
    positions = offsets + jnp.arange(SEQ, dtype=jnp.int32)[None, :]
    forget_bias = jnp.linspace(3.0, 6.0, N_MLSTM_HEADS, dtype=f32)
    zeros_h = jnp.zeros((N_MLSTM_HEADS,), f32)
    gate_base = jnp.concatenate([zeros_h, forget_bias, zeros_h, forget_bias])
    b_gates = gate_base[None] + 0.1 * jax.random.normal(ks[3], (DEPTH, N_GATE), f32)
    return {
        "x": x,
        "mem": mem,
        "positions": positions,
        "g_mix": gain(ks[4], D_MODEL),
        "w_in": w(ks[5], (D_MODEL, D_IN), D_MODEL),
        "b_gates": b_gates,
        "conv_qk": w(ks[6], (CONV_WIDTH, 2 * D_MLSTM), CONV_WIDTH),
        "g_q_a": gain(ks[7], Q_LORA_RANK),
        "w_q_b": w(ks[8], (Q_LORA_RANK, N_MLA_HEADS * MLA_QK_DIM), Q_LORA_RANK),
        "g_kv_a": gain(ks[9], KV_LORA_RANK),
        "w_kv_b": w(ks[10], (KV_LORA_RANK, N_MLA_HEADS * (MLA_NOPE_DIM + MLA_V_DIM)), KV_LORA_RANK),
        "g_head_mlstm": gain(ks[11], D_MLSTM),
        "g_head_mla": gain(ks[12], D_MLA),
        "w_out": w(ks[13], (D_MIX, D_MODEL), D_MIX),
        "g_mem_x": gain(ks[14], D_MODEL),
        "g_mem_kv": gain(ks[15], D_MODEL),
        "w_mem_q": w(ks[16], (D_MODEL, D_MEM_INNER), D_MODEL),
        "w_mem_k": w(ks[17], (D_MODEL, D_MEM_INNER), D_MODEL),
        "w_mem_v": w(ks[18], (D_MODEL, D_MEM_INNER), D_MODEL),
        "w_mem_o": w(ks[19], (D_MEM_INNER, D_MODEL), D_MEM_INNER),
        "g_ffn": gain(ks[20], D_MODEL),
        "w_router": w(ks[21], (D_MODEL, N_EXPERTS), D_MODEL),
        "w_exp_gate": w(ks[22], (N_EXPERTS, D_MODEL, D_FF_EXPERT), D_MODEL),
        "w_exp_up": w(ks[23], (N_EXPERTS, D_MODEL, D_FF_EXPERT), D_MODEL),
        "w_exp_down": w(ks[24], (N_EXPERTS, D_FF_EXPERT, D_MODEL), D_FF_EXPERT),
        "g_final": gain(ks[25], D_MODEL, depth=False),
    }


def reference(x, mem, positions, g_mix, w_in, b_gates, conv_qk, g_q_a, w_q_b, g_kv_a, w_kv_b,
              g_head_mlstm, g_head_mla, w_out, g_mem_x, g_mem_kv, w_mem_q, w_mem_k, w_mem_v, w_mem_o,
              g_ffn, w_router, w_exp_gate, w_exp_up, w_exp_down, g_final):
    cos, sin = rope_tables(positions)
    for l in range(DEPTH):
        x = x + hybrid_mixer(rmsnorm(x, g_mix[l]), cos, sin, w_in[l], b_gates[l], conv_qk[l],
                             g_q_a[l], w_q_b[l], g_kv_a[l], w_kv_b[l],
                             g_head_mlstm[l], g_head_mla[l], w_out[l])
        x = x + memory_cross_attention(rmsnorm(x, g_mem_x[l]), rmsnorm(mem, g_mem_kv[l]),
                                       w_mem_q[l], w_mem_k[l], w_mem_v[l], w_mem_o[l])
        x = x + expert_choice_ffn(rmsnorm(x, g_ffn[l]), w_router[l], w_exp_gate[l], w_exp_up[l], w_exp_down[l])
    return rmsnorm(x, g_final)
```

```python
import functools
import math

import jax
import jax.numpy as jnp
from jax import lax
from jax.experimental import pallas as pl
from jax.experimental.pallas import tpu as pltpu

F32 = jnp.float32
BF16 = jnp.bfloat16
I32 = jnp.int32

EPS = 1e-6
N_MLSTM_HEADS = 4
MLSTM_HEAD_DIM = 128
D_MLSTM = N_MLSTM_HEADS * MLSTM_HEAD_DIM
N_MLA_HEADS = 8
MLA_NOPE_DIM = 64
MLA_ROPE_DIM = 32
MLA_QK_DIM = MLA_NOPE_DIM + MLA_ROPE_DIM
MLA_V_DIM = 64
ROPE_THETA = 10000.0
N_GATE = 4 * N_MLSTM_HEADS
N_MEM_HEADS = 4
N_EXPERTS = 16
EC_CAPACITY_FACTOR = 2

LANES = 128
SUBLANES = 8
VMEM_LIMIT_BYTES = 56 * 1024 * 1024

MLSTM_CHUNK = 128
TM_INPROJ = 256
TQ_MLA = 256
TK_MLA = 512
TM_POST = 256
TF_FFN = 512
TM_FINAL = 512
CUM_BLOCK = 256


def _cparams(sem):
    return pltpu.CompilerParams(dimension_semantics=sem, vmem_limit_bytes=VMEM_LIMIT_BYTES)


def _dot(a, b):
    return jnp.dot(a, b, preferred_element_type=F32)


def _dot_nt(a, b):
    return lax.dot_general(a, b, (((1,), (1,)), ((), ())), preferred_element_type=F32)


def _rms(x, g):
    return x * lax.rsqrt(jnp.mean(x * x, axis=-1, keepdims=True) + EPS) * g


def _sigmoid(x):
    return 1.0 / (1.0 + jnp.exp(-x))


def _full(shape):
    return pl.BlockSpec(shape, lambda *_: (0,) * len(shape))


def _inproj_kernel(x_ref, xp_ref, xn_ref, pos_ref, gmix_ref, wqk_ref, wvo_ref, wmisc_ref, bg_ref,
                   conv_ref, gqa_ref, wq_ref, gkva_ref, wk_ref, wv_ref, invf_ref,
                   q_ref, kT_ref, v_ref, o_ref, gc_ref, gt_ref, qcat_ref, kcat_ref, va_ref):
    tm = x_ref.shape[1]
    i = pl.program_id(1)
    last = pl.num_programs(1) - 1
    g = gmix_ref[...]
    xm = _rms(x_ref[0], g)
    xprev = _rms(xp_ref[0], g) * (i > 0).astype(F32)
    xnext = _rms(xn_ref[0], g) * (i < last).astype(F32)
    xm_bf = xm.astype(BF16)
    lhs = jnp.concatenate([xprev.astype(BF16), xm_bf, xnext.astype(BF16)], axis=0)

    pqk = _dot(lhs, wqk_ref[...])
    rows = tm + 2 * SUBLANES
    up = pltpu.roll(pqk, 1, 0)[SUBLANES:SUBLANES + tm]
    dn = pltpu.roll(pqk, rows - 1, 0)[SUBLANES:SUBLANES + tm]
    mid = pqk[SUBLANES:SUBLANES + tm]
    cw = conv_ref[...]
    conv = up * cw[0:1] + mid * cw[1:2] + dn * cw[2:3]
    act = conv * _sigmoid(conv)
    q_ref[0] = act[:, :D_MLSTM].astype(BF16)
    k = act[:, D_MLSTM:] * (MLSTM_HEAD_DIM ** -0.5)
    kT_ref[0] = k.T.astype(BF16)

    vo = _dot(xm_bf, wvo_ref[...])
    v_ref[0] = vo[:, :D_MLSTM].astype(BF16)
    o_ref[0] = vo[:, D_MLSTM:]

    misc = _dot(xm_bf, wmisc_ref[...])
    cq = misc[:, :256]
    ckv = misc[:, 256:384]
    kr_raw = misc[:, 384:512]
    gp = misc[:, 512:640] + bg_ref[...]

    lane = lax.broadcasted_iota(I32, (tm, LANES), 1)
    is_f = ((lane >= 4) & (lane < 8)) | ((lane >= 12) & (lane < 16))
    logsig = jnp.minimum(gp, 0.0) - jnp.log1p(jnp.exp(-jnp.abs(gp)))
    lf = jnp.where(is_f, logsig, gp)
    rc = lax.broadcasted_iota(I32, (tm, LANES), 0) & (MLSTM_CHUNK - 1)
    pre = lf
    suf = lf
    step = 1
    while step < MLSTM_CHUNK:
        pre = pre + jnp.where(rc >= step, pltpu.roll(pre, step, 0), 0.0)
        suf = suf + jnp.where(rc + step < MLSTM_CHUNK, pltpu.roll(suf, tm - step, 0), 0.0)
        step *= 2
    cum = jnp.where(lane < 8, pre, suf)
    a = lf - pltpu.roll(cum, LANES - 4, 1)
    gc = jnp.where(is_f, cum, a)
    gc_ref[0] = gc
    gt_ref[0] = gc.T[:N_GATE]

    ang = pos_ref[0] * invf_ref[...]
    cos = jnp.cos(ang)
    sin = jnp.sin(ang)
    first_half = lane < (MLA_NOPE_DIM + MLA_ROPE_DIM // 2)
    sin_signed = jnp.where(first_half, -sin, sin)

    def rope(t):
        partner = jnp.where(first_half, pltpu.roll(t, LANES - MLA_ROPE_DIM // 2, 1),
                            pltpu.roll(t, MLA_ROPE_DIM // 2, 1))
        return t * cos + partner * sin_signed

    cqn = _rms(cq, gqa_ref[...]).astype(BF16)
    qa = _dot(cqn, wq_ref[...])
    ckvn = _rms(ckv, gkva_ref[...]).astype(BF16)
    kn = _dot(ckvn, wk_ref[...])
    kr = rope(kr_raw)
    for h in range(N_MLA_HEADS):
        sl = slice(h * LANES, (h + 1) * LANES)
        qcat_ref[0, :, sl] = rope(qa[:, sl]).astype(BF16)
        kcat_ref[0, :, sl] = (kn[:, sl] + kr).astype(BF16)
    va_ref[0] = _dot(ckvn, wv_ref[...]).astype(BF16)


def _inproj(x, pos_f, g_mix, w_in, b_gates, conv_qk, g_q_a, w_q_b, g_kv_a, w_kv_b):
    B, S, D = x.shape
    tm = TM_INPROJ
    nt = S // tm
    hb = tm // SUBLANES
    off = [0, 512, 1024, 1536, 2048, 2064, 2320, 2448, 2480]
    wqk = w_in[:, off[0]:off[2]].astype(BF16)
    wvo = w_in[:, off[2]:off[4]].astype(BF16)
    w_gate = w_in[:, off[4]:off[5]]
    w_cq = w_in[:, off[5]:off[6]]
    w_ckv = w_in[:, off[6]:off[7]]
    w_kr = w_in[:, off[7]:off[8]]
    kr_blk = jnp.zeros((D, LANES), F32).at[:, MLA_NOPE_DIM:MLA_QK_DIM].set(w_kr)
    gate_blk = jnp.zeros((D, LANES), F32).at[:, :N_GATE].set(w_gate)
    wmisc = jnp.concatenate([w_cq, w_ckv, kr_blk, gate_blk], axis=1).astype(BF16)
    bg = jnp.zeros((1, LANES), F32).at[0, :N_GATE].set(b_gates)
    wq3 = w_q_b.reshape(-1, N_MLA_HEADS, MLA_QK_DIM)
    wq = jnp.pad(wq3, ((0, 0), (0, 0), (0, LANES - MLA_QK_DIM))).reshape(-1, N_MLA_HEADS * LANES).astype(BF16)
    wkv3 = w_kv_b.reshape(-1, N_MLA_HEADS, MLA_NOPE_DIM + MLA_V_DIM)
    wk = jnp.pad(wkv3[:, :, :MLA_NOPE_DIM], ((0, 0), (0, 0), (0, LANES - MLA_NOPE_DIM)))
    wk = wk.reshape(-1, N_MLA_HEADS * LANES).astype(BF16)
    wv = wkv3[:, :, MLA_NOPE_DIM:].reshape(-1, N_MLA_HEADS * MLA_V_DIM).astype(BF16)
    inv_freq = ROPE_THETA ** (-jnp.arange(0, MLA_ROPE_DIM, 2, dtype=F32) / MLA_ROPE_DIM)
    invf = jnp.zeros((1, LANES), F32).at[0, MLA_NOPE_DIM:MLA_QK_DIM].set(jnp.concatenate([inv_freq, inv_freq]))

    row = lambda w: pl.BlockSpec((1, tm, w), lambda b, i: (b, i, 0))
    in_specs = [
        row(D),
        pl.BlockSpec((1, SUBLANES, D), lambda b, i: (b, jnp.maximum(i * hb - 1, 0), 0)),
        pl.BlockSpec((1, SUBLANES, D), lambda b, i: (b, jnp.minimum((i + 1) * hb, S // SUBLANES - 1), 0)),
        row(1),
        _full((1, D)), _full(wqk.shape), _full(wvo.shape), _full(wmisc.shape), _full((1, LANES)),
        _full(conv_qk.shape), _full((1, 256)), _full(wq.shape), _full((1, 128)), _full(wk.shape),
        _full(wv.shape), _full((1, LANES)),
    ]
    out_shape = [
        jax.ShapeDtypeStruct((B, S, D_MLSTM), BF16),
        jax.ShapeDtypeStruct((B, D_MLSTM, S), BF16),
        jax.ShapeDtypeStruct((B, S, D_MLSTM), BF16),
        jax.ShapeDtypeStruct((B, S, D_MLSTM), F32),
        jax.ShapeDtypeStruct((B, S, LANES), F32),
        jax.ShapeDtypeStruct((B, N_GATE, S), F32),
        jax.ShapeDtypeStruct((B, S, N_MLA_HEADS * LANES), BF16),
        jax.ShapeDtypeStruct((B, S, N_MLA_HEADS * LANES), BF16),
        jax.ShapeDtypeStruct((B, S, N_MLA_HEADS * MLA_V_DIM), BF16),
    ]
    out_specs = [
        row(D_MLSTM),
        pl.BlockSpec((1, D_MLSTM, tm), lambda b, i: (b, 0, i)),
        row(D_MLSTM), row(D_MLSTM), row(LANES),
        pl.BlockSpec((1, N_GATE, tm), lambda b, i: (b, 0, i)),
        row(N_MLA_HEADS * LANES), row(N_MLA_HEADS * LANES), row(N_MLA_HEADS * MLA_V_DIM),
    ]
    return pl.pallas_call(
        _inproj_kernel, grid=(B, nt), in_specs=in_specs, out_specs=out_specs, out_shape=out_shape,
        compiler_params=_cparams(("parallel", "parallel")), name="inproj",
    )(x, x, x, pos_f, g_mix.reshape(1, D), wqk, wvo, wmisc, bg, conv_qk, g_q_a.reshape(1, -1), wq,
      g_kv_a.reshape(1, -1), wk, wv, invf)


def _mlstm_kernel(qf_ref, kf_ref, vf_ref, gcf_ref, gtf_ref, qb_ref, kb_ref, vb_ref, gcb_ref, gtb_ref,
                  hf_ref, hb_ref, cn_ref, m_ref):
    T = MLSTM_CHUNK
    hd = MLSTM_HEAD_DIM

    @pl.when(pl.program_id(1) == 0)
    def _():
        cn_ref[...] = jnp.zeros_like(cn_ref)
        m_ref[...] = jnp.zeros_like(m_ref)

    t_idx = lax.broadcasted_iota(I32, (T, T), 0)
    s_idx = lax.broadcasted_iota(I32, (T, T), 1)
    ones_col = (lax.broadcasted_iota(I32, (T, hd), 1) == 0).astype(BF16)
    neg_inf = F32(-jnp.inf)

    for d, (q_ref, kT_ref, v_ref, gc_ref, gt_ref, h_ref) in enumerate(
            ((qf_ref, kf_ref, vf_ref, gcf_ref, gtf_ref, hf_ref),
             (qb_ref, kb_ref, vb_ref, gcb_ref, gtb_ref, hb_ref))):
        mask = (s_idx <= t_idx) if d == 0 else (s_idx >= t_idx)
        end = T - 1 if d == 0 else 0
        gc = gc_ref[0]
        gt = gt_ref[0]
        for h in range(N_MLSTM_HEADS):
            st = d * N_MLSTM_HEADS + h
            la = d * 8 + h
            a_col = gc[:, la:la + 1]
            b_col = gc[:, la + 4:la + 5]
            a_row = gt[la:la + 1, :]
            m_prev = m_ref[st][0:1, 0:1]
            qh = q_ref[0, :, h * hd:(h + 1) * hd]
            kTh = kT_ref[0, h * hd:(h + 1) * hd, :]
            vh = v_ref[0, :, h * hd:(h + 1) * hd]
            cn = cn_ref[st]

            cm = jnp.max(jnp.where(mask, a_row, neg_inf), axis=1, keepdims=True)
            m_run = jnp.maximum(cm, m_prev)
            e = jnp.exp(jnp.where(mask, a_row - m_run, neg_inf))
            pw = e * _dot(qh, kTh)
            qcn = _dot(qh, cn.astype(BF16))
            pv = _dot(pw.astype(BF16), vh)
            iw = jnp.exp(m_prev - m_run)
            num = iw * qcn[:, :hd] + pv
            den = iw * qcn[:, hd:hd + 1] + jnp.sum(pw, axis=1, keepdims=True)
            floor = jnp.exp(-(b_col + m_run))
            h_ref[0, :, h * hd:(h + 1) * hd] = num / jnp.maximum(jnp.abs(den), floor)

            m_end = m_run[end:end + 1]
            g_sum = b_col[end:end + 1]
            decay = jnp.exp(m_prev - m_end)
            ws_row = jnp.exp(a_row - m_end)
            kw = (kTh.astype(F32) * ws_row).astype(BF16)
            v_ext = jnp.concatenate([vh, ones_col], axis=1)
            cn_ref[st] = decay * cn + _dot(kw, v_ext)
            m_ref[st] = jnp.broadcast_to(g_sum + m_end, (SUBLANES, LANES))


def _mlstm(q, kT, v, gc, gt):
    B, S, _ = q.shape
    T = MLSTM_CHUNK
    nc = S // T
    fwd = lambda b, j: (b, j, 0)
    bwd = lambda b, j: (b, nc - 1 - j, 0)
    fwd_t = lambda b, j: (b, 0, j)
    bwd_t = lambda b, j: (b, 0, nc - 1 - j)

    def specs(row_map, col_map):
        return [pl.BlockSpec((1, T, D_MLSTM), row_map), pl.BlockSpec((1, D_MLSTM, T), col_map),
                pl.BlockSpec((1, T, D_MLSTM), row_map), pl.BlockSpec((1, T, LANES), row_map),
                pl.BlockSpec((1, N_GATE, T), col_map)]

    n_state = 2 * N_MLSTM_HEADS
    return pl.pallas_call(
        _mlstm_kernel, grid=(B, nc),
        in_specs=specs(fwd, fwd_t) + specs(bwd, bwd_t),
        out_specs=[pl.BlockSpec((1, T, D_MLSTM), fwd), pl.BlockSpec((1, T, D_MLSTM), bwd)],
        out_shape=[jax.ShapeDtypeStruct((B, S, D_MLSTM), F32)] * 2,
        scratch_shapes=[pltpu.VMEM((n_state, MLSTM_HEAD_DIM, 2 * MLSTM_HEAD_DIM), F32),
                        pltpu.VMEM((n_state, SUBLANES, LANES), F32)],
        compiler_params=_cparams(("parallel", "arbitrary")), name="mlstm",
    )(q, kT, v, gc, gt, q, kT, v, gc, gt)


def _mla_kernel(q_ref, k_ref, v_ref, g_ref, o_ref):
    tq = q_ref.shape[1]
    S = k_ref.shape[1]
    scale = MLA_QK_DIM ** -0.5
    q = q_ref[0]
    qs = (q[:, :LANES], q[:, LANES:])

    def body(kb, carry):
        ks = k_ref[0, pl.ds(pl.multiple_of(kb * TK_MLA, TK_MLA), TK_MLA), :]
        vs = v_ref[0, pl.ds(pl.multiple_of(kb * TK_MLA, TK_MLA), TK_MLA), :]
        out = []
        for h in range(2):
            m, l, acc = carry[h]
            s = _dot_nt(qs[h], ks[:, h * LANES:(h + 1) * LANES])
            m_new = jnp.maximum(m, jnp.max(s, axis=1, keepdims=True))
            p = jnp.exp((s - m_new) * scale)
            alpha = jnp.exp((m - m_new) * scale)
            l = alpha * l + jnp.sum(p, axis=1, keepdims=True)
            acc = alpha * acc + _dot(p.astype(BF16), vs)
            out.append((m_new, l, acc))
        return tuple(out)

    init = tuple((jnp.full((tq, 1), -jnp.inf, F32), jnp.zeros((tq, 1), F32), jnp.zeros((tq, LANES), F32))
                 for _ in range(2))
    res = lax.fori_loop(0, S // TK_MLA, body, init)
    lane = lax.broadcasted_iota(I32, (tq, LANES), 1)
    lo = lane < MLA_V_DIM
    o = jnp.where(lo, res[0][2] / res[0][1], res[1][2] / res[1][1])
    sq = o * o
    ms0 = jnp.sum(jnp.where(lo, sq, 0.0), axis=1, keepdims=True) / MLA_V_DIM
    ms1 = jnp.sum(jnp.where(lo, 0.0, sq), axis=1, keepdims=True) / MLA_V_DIM
    ms = jnp.where(lo, ms0, ms1)
    o_ref[0] = o * lax.rsqrt(ms + EPS) * g_ref[...]


def _mla(qcat, kcat, va, g_head_mla):
    B, S, _ = qcat.shape
    tq = TQ_MLA
    npair = N_MLA_HEADS // 2
    return pl.pallas_call(
        _mla_kernel, grid=(B, npair, S // tq),
        in_specs=[pl.BlockSpec((1, tq, 2 * LANES), lambda b, p, i: (b, i, p)),
                  pl.BlockSpec((1, S, 2 * LANES), lambda b, p, i: (b, 0, p)),
                  pl.BlockSpec((1, S, LANES), lambda b, p, i: (b, 0, p)),
                  pl.BlockSpec((1, LANES), lambda b, p, i: (0, p))],
        out_specs=pl.BlockSpec((1, tq, LANES), lambda b, p, i: (b, i, p)),
        out_shape=jax.ShapeDtypeStruct((B, S, N_MLA_HEADS * MLA_V_DIM), F32),
        compiler_params=_cparams(("parallel", "parallel", "parallel")), name="mla_attn",
    )(qcat, kcat, va, g_head_mla.reshape(1, -1))


def _memkv_kernel(mem_ref, g_ref, wk_ref, wv_ref, k_ref, v_ref):
    mn = _rms(mem_ref[0], g_ref[...]).astype(BF16)
    k_ref[0] = _dot(mn, wk_ref[...]).astype(BF16)
    v_ref[0] = _dot(mn, wv_ref[...]).astype(BF16)


def _memkv(mem, g_mem_kv, w_k, w_v):
    B, M, D = mem.shape
    blk = pl.BlockSpec((1, M, D), lambda b: (b, 0, 0))
    return pl.pallas_call(
        _memkv_kernel, grid=(B,),
        in_specs=[blk, _full((1, D)), _full(w_k.shape), _full(w_v.shape)],
        out_specs=[blk, blk], out_shape=[jax.ShapeDtypeStruct((B, M, D), BF16)] * 2,
        compiler_params=_cparams(("parallel",)), name="memkv",
    )(mem, g_mem_kv.reshape(1, D), w_k.astype(BF16), w_v.astype(BF16))


def _post_kernel(x_ref, hf_ref, hb_ref, om_ref, ya_ref, km_ref, vm_ref, ghm_ref, wout_ref, gmx_ref, wmq_ref,
                 wmo_ref, gffn_ref, wrh_ref, wrl_ref, x2_ref, hn_ref, aff_ref):
    tm = x_ref.shape[1]
    D = x_ref.shape[2]
    hm = hf_ref[0] + hb_ref[0]
    gate = _sigmoid(om_ref[0])
    ghm = ghm_ref[...]
    parts = []
    for h in range(N_MLSTM_HEADS):
        sl = slice(h * MLSTM_HEAD_DIM, (h + 1) * MLSTM_HEAD_DIM)
        parts.append(gate[:, sl] * _rms(hm[:, sl], ghm[:, sl]))
    ycat = jnp.concatenate(parts + [ya_ref[0]], axis=1).astype(BF16)
    x1 = x_ref[0] + _dot(ycat, wout_ref[...])

    dm = D // N_MEM_HEADS
    qm = _dot(_rms(x1, gmx_ref[...]).astype(BF16), wmq_ref[...]).astype(BF16)
    km = km_ref[0]
    vm = vm_ref[0]
    outs = []
    for h in range(N_MEM_HEADS):
        sl = slice(h * dm, (h + 1) * dm)
        s = _dot_nt(qm[:, sl], km[:, sl]) * (dm ** -0.5)
        e = jnp.exp(s - jnp.max(s, axis=1, keepdims=True))
        p = e / jnp.sum(e, axis=1, keepdims=True)
        outs.append(_dot(p.astype(BF16), vm[:, sl]))
    om = jnp.concatenate(outs, axis=1).astype(BF16)
    x2 = x1 + _dot(om, wmo_ref[...])
    x2_ref[0] = x2

    hn = _rms(x2, gffn_ref[...])
    hn_ref[0] = hn

    hi = hn.astype(BF16)
    lo = (hn - hi.astype(F32)).astype(BF16)
    wrh = wrh_ref[...]
    logits = _dot(hi, wrh) + (_dot(hi, wrl_ref[...]) + _dot(lo, wrh))
    lane = lax.broadcasted_iota(I32, (tm, LANES), 1)
    valid = lane < N_EXPERTS
    logits = jnp.where(valid, logits, -jnp.inf)
    e = jnp.exp(logits - jnp.max(logits, axis=1, keepdims=True))
    aff_ref[0] = e / jnp.sum(e, axis=1, keepdims=True)


def _post(x, hf, hb, om, ya, kmem, vmem, g_head_mlstm, w_out, g_mem_x, w_mem_q, w_mem_o, g_ffn, w_router):
    B, S, D = x.shape
    tm = TM_POST
    M = kmem.shape[1]
    wr = jnp.zeros((D, LANES), F32).at[:, :N_EXPERTS].set(w_router)
    wrh = wr.astype(BF16)
    wrl = (wr - wrh.astype(F32)).astype(BF16)
    row = lambda w: pl.BlockSpec((1, tm, w), lambda b, i: (b, i, 0))
    mem_spec = pl.BlockSpec((1, M, D), lambda b, i: (b, 0, 0))
    return pl.pallas_call(
        _post_kernel, grid=(B, S // tm),
        in_specs=[row(D), row(D_MLSTM), row(D_MLSTM), row(D_MLSTM), row(D_MLSTM), mem_spec, mem_spec,
                  _full((1, D_MLSTM)), _full((D, D)), _full((1, D)), _full((D, D)), _full((D, D)),
                  _full((1, D)), _full((D, LANES)), _full((D, LANES))],
        out_specs=[row(D), row(D), row(LANES)],
        out_shape=[jax.ShapeDtypeStruct((B, S, D), F32), jax.ShapeDtypeStruct((B, S, D), F32),
                   jax.ShapeDtypeStruct((B, S, LANES), F32)],
        compiler_params=_cparams(("parallel", "parallel")), name="post_mixer",
    )(x, hf, hb, om, ya, kmem, vmem, g_head_mlstm.reshape(1, -1), w_out.astype(BF16), g_mem_x.reshape(1, D),
      w_mem_q.astype(BF16), w_mem_o.astype(BF16), g_ffn.reshape(1, D), wrh, wrl)


def _topk_kernel(aff_ref, idx_ref, cum_ref):
    S = aff_ref.shape[1]
    cap = idx_ref.shape[2]
    nblk = S // CUM_BLOCK

    valid = lax.broadcasted_iota(I32, (1, LANES), 1) < N_EXPERTS

    def open_interval(c):
        lo, hi = c
        mid = 0.5 * (lo + hi)
        return jnp.max(jnp.where(valid & (mid > lo) & (mid < hi), 1.0, 0.0)) > 0.0

    def bisect(c):
        lo, hi = c
        mid = 0.5 * (lo + hi)
        cnt = jnp.sum((aff_ref[0] >= mid).astype(F32), axis=0, keepdims=True)
        ge = cnt >= cap
        return jnp.where(ge, mid, lo), jnp.where(ge, hi, mid)

    lo, hi = lax.while_loop(open_interval, bisect, (jnp.zeros((1, LANES), F32), jnp.full((1, LANES), 2.0, F32)))
    need = cap - jnp.sum((aff_ref[0] >= hi).astype(F32), axis=0, keepdims=True)

    r = lax.broadcasted_iota(I32, (CUM_BLOCK, CUM_BLOCK), 0)
    c = lax.broadcasted_iota(I32, (CUM_BLOCK, CUM_BLOCK), 1)
    tri = (c <= r).astype(BF16)

    def cum_body(blk, carry):
        ce, cs = carry
        rows = pl.ds(pl.multiple_of(blk * CUM_BLOCK, CUM_BLOCK), CUM_BLOCK)
        a = aff_ref[0, rows, :]
        sure = a >= hi
        tie = (a >= lo) & jnp.logical_not(sure)
        eq = tie.astype(F32)
        eq_incl = _dot(tri, eq.astype(BF16)) + ce
        sel = jnp.where(sure | (tie & (eq_incl - eq < need)), 1.0, 0.0)
        sel_incl = _dot(tri, sel.astype(BF16)) + cs
        cum_ref[rows, :] = sel_incl
        return eq_incl[CUM_BLOCK - 1:CUM_BLOCK], sel_incl[CUM_BLOCK - 1:CUM_BLOCK]

    zero = jnp.zeros((1, LANES), F32)
    lax.fori_loop(0, nblk, cum_body, (zero, zero))

    j_row = lax.broadcasted_iota(I32, (1, cap), 1).astype(F32)
    for e in range(N_EXPERTS):
        def cnt_body(blk, acc):
            rows = pl.ds(pl.multiple_of(blk * CUM_BLOCK, CUM_BLOCK), CUM_BLOCK)
            col = cum_ref[rows, e:e + 1]
            return acc + jnp.sum((col <= j_row).astype(F32), axis=0, keepdims=True)
        acc = lax.fori_loop(0, nblk, cnt_body, jnp.zeros((1, cap), F32))
        idx_ref[0, e:e + 1, :] = acc.astype(I32)


def _topk(aff, cap):
    B, S, _ = aff.shape
    return pl.pallas_call(
        _topk_kernel, grid=(B,),
        in_specs=[pl.BlockSpec((1, S, LANES), lambda b: (b, 0, 0))],
        out_specs=pl.BlockSpec((1, N_EXPERTS, cap), lambda b: (b, 0, 0)),
        out_shape=jax.ShapeDtypeStruct((B, N_EXPERTS, cap), I32),
        scratch_shapes=[pltpu.VMEM((S, LANES), F32)],
        compiler_params=_cparams(("parallel",)), name="topk",
    )(aff)


def _gather_kernel(idx_ref, hn_ref, aff_ref, xe_ref, gs_ref, rows_sc):
    cap = xe_ref.shape[2]

    def body(j, _):
        i = idx_ref[0, 0, j]
        rows_sc[pl.ds(j, 1), :] = hn_ref[0, pl.ds(i, 1), :]
        gs_ref[0, 0, pl.ds(j, 1), :] = aff_ref[0, pl.ds(i, 1), :]
        return 0

    lax.fori_loop(0, cap, body, 0, unroll=8)
    xe_ref[0, 0] = rows_sc[...].astype(BF16)


def _gather(idx3, hn, aff, cap):
    B, S, D = hn.shape
    E = N_EXPERTS
    return pl.pallas_call(
        _gather_kernel, grid=(B, E),
        in_specs=[pl.BlockSpec((1, 1, cap), lambda b, e: (b * E + e, 0, 0), memory_space=pltpu.SMEM),
                  pl.BlockSpec((1, S, D), lambda b, e: (b, 0, 0)),
                  pl.BlockSpec((1, S, LANES), lambda b, e: (b, 0, 0))],
        out_specs=[pl.BlockSpec((1, 1, cap, D), lambda b, e: (b, e, 0, 0)),
                   pl.BlockSpec((1, 1, cap, LANES), lambda b, e: (b, e, 0, 0))],
        out_shape=[jax.ShapeDtypeStruct((B, E, cap, D), BF16), jax.ShapeDtypeStruct((B, E, cap, LANES), F32)],
        scratch_shapes=[pltpu.VMEM((cap, D), F32)],
        compiler_params=_cparams(("parallel", "arbitrary")), name="moe_gather",
    )(idx3, hn, aff)


def _ffn_kernel(xe_ref, gs_ref, wg_ref, wu_ref, wd_ref, ye_ref, wg_sc, wu_sc, wd_sc):
    e = pl.program_id(0)
    f = pl.program_id(1)
    nb = xe_ref.shape[0]
    cap = xe_ref.shape[2]

    wg_sc[...] = wg_ref[0].astype(BF16)
    wu_sc[...] = wu_ref[0].astype(BF16)
    wd_sc[...] = wd_ref[0].astype(BF16)
    lane = lax.broadcasted_iota(I32, (cap, LANES), 1)
    for b in range(nb):
        xb = xe_ref[b, 0]
        h1 = _dot(xb, wg_sc[...])
        h2 = _dot(xb, wu_sc[...])
        hid = (h1 * _sigmoid(h1) * h2).astype(BF16)
        y = _dot(hid, wd_sc[...])

        @pl.when(f == 0)
        def _():
            ye_ref[b, 0] = y

        @pl.when(f > 0)
        def _():
            ye_ref[b, 0] += y

        @pl.when(f == pl.num_programs(1) - 1)
        def _():
            gate = jnp.sum(jnp.where(lane == e, gs_ref[b, 0], 0.0), axis=1, keepdims=True)
            ye_ref[b, 0] *= gate


def _ffn(xe, gs, w_gate, w_up, w_down):
    B, E, cap, D = xe.shape
    F = w_gate.shape[2]
    tf = TF_FFN
    return pl.pallas_call(
        _ffn_kernel, grid=(E, F // tf),
        in_specs=[pl.BlockSpec((B, 1, cap, D), lambda e, f: (0, e, 0, 0)),
                  pl.BlockSpec((B, 1, cap, LANES), lambda e, f: (0, e, 0, 0)),
                  pl.BlockSpec((1, D, tf), lambda e, f: (e, 0, f)),
                  pl.BlockSpec((1, D, tf), lambda e, f: (e, 0, f)),
                  pl.BlockSpec((1, tf, D), lambda e, f: (e, f, 0))],
        out_specs=pl.BlockSpec((B, 1, cap, D), lambda e, f: (0, e, 0, 0)),
        out_shape=jax.ShapeDtypeStruct((B, E, cap, D), F32),
        scratch_shapes=[pltpu.VMEM((D, tf), BF16), pltpu.VMEM((D, tf), BF16), pltpu.VMEM((tf, D), BF16)],
        compiler_params=_cparams(("parallel", "arbitrary")), name="moe_ffn",
    )(xe, gs, w_gate, w_up, w_down)


def _scatter_kernel(idx_ref, ye_ref, out_ref):
    cap = ye_ref.shape[2]

    @pl.when(pl.program_id(1) == 0)
    def _():
        out_ref[...] = jnp.zeros_like(out_ref)

    def body(j, _):
        i = idx_ref[0, 0, j]
        out_ref[0, pl.ds(i, 1), :] += ye_ref[0, 0, pl.ds(j, 1), :]
        return 0

    lax.fori_loop(0, cap, body, 0, unroll=8)


def _scatter(idx3, ye, S):
    B, E, cap, D = ye.shape
    return pl.pallas_call(
        _scatter_kernel, grid=(B, E),
        in_specs=[pl.BlockSpec((1, 1, cap), lambda b, e: (b * E + e, 0, 0), memory_space=pltpu.SMEM),
                  pl.BlockSpec((1, 1, cap, D), lambda b, e: (b, e, 0, 0))],
        out_specs=pl.BlockSpec((1, S, D), lambda b, e: (b, 0, 0)),
        out_shape=jax.ShapeDtypeStruct((B, S, D), F32),
        compiler_params=_cparams(("parallel", "arbitrary")), name="moe_scatter",
    )(idx3, ye)


def _final_kernel(x_ref, y_ref, g_ref, o_ref):
    o_ref[0] = _rms(x_ref[0] + y_ref[0], g_ref[...])


def _final(x2, moe, g_final):
    B, S, D = x2.shape
    row = pl.BlockSpec((1, TM_FINAL, D), lambda b, i: (b, i, 0))
    return pl.pallas_call(
        _final_kernel, grid=(B, S // TM_FINAL), in_specs=[row, row, _full((1, D))], out_specs=row,
        out_shape=jax.ShapeDtypeStruct((B, S, D), F32),
        compiler_params=_cparams(("parallel", "parallel")), name="final_norm",
    )(x2, moe, g_final.reshape(1, D))


def kernel(x, mem, positions, g_mix, w_in, b_gates, conv_qk, g_q_a, w_q_b, g_kv_a, w_kv_b, g_head_mlstm,
           g_head_mla, w_out, g_mem_x, g_mem_kv, w_mem_q, w_mem_k, w_mem_v, w_mem_o, g_ffn, w_router,
           w_exp_gate, w_exp_up, w_exp_down, g_final):
    B, S, D = x.shape
    depth = g_mix.shape[0]
    assert depth == 1, "the MoE residual is folded into the final norm kernel, which assumes one layer"
    cap = EC_CAPACITY_FACTOR * S // N_EXPERTS
    pos_f = positions.astype(F32).reshape(B, S, 1)
    for l in range(depth):
        q, kT, v, om, gc, gt, qcat, kcat, va = _inproj(
            x, pos_f, g_mix[l], w_in[l], b_gates[l], conv_qk[l], g_q_a[l], w_q_b[l], g_kv_a[l], w_kv_b[l])
        hf, hb = _mlstm(q, kT, v, gc, gt)
        ya = _mla(qcat, kcat, va, g_head_mla[l])
        kmem, vmem = _memkv(mem, g_mem_kv[l], w_mem_k[l], w_mem_v[l])
        x2, hn, aff = _post(x, hf, hb, om, ya, kmem, vmem, g_head_mlstm[l], w_out[l], g_mem_x[l],
                            w_mem_q[l], w_mem_o[l], g_ffn[l], w_router[l])
        idx = _topk(aff, cap)
        idx3 = idx.reshape(B * N_EXPERTS, 1, cap)
        xe, gs = _gather(idx3, hn, aff, cap)
        ye = _ffn(xe, gs, w_exp_gate[l], w_exp_up[l], w_exp_down[l])
        moe = _scatter(idx3, ye, S)
        x = x2
    return _final(x, moe, g_final)
```

```python
import functools
import math

import jax
import jax.numpy as jnp
from jax import lax
from jax.experimental import pallas as pl
from jax.experimental.pallas import tpu as pltpu

F32 = jnp.float32
BF16 = jnp.bfloat16
I32 = jnp.int32

EPS = 1e-6
N_MLSTM_HEADS = 4
MLSTM_HEAD_DIM = 128
D_MLSTM = N_MLSTM_HEADS * MLSTM_HEAD_DIM
N_MLA_HEADS = 8
MLA_NOPE_DIM = 64
MLA_ROPE_DIM = 32
MLA_QK_DIM = MLA_NOPE_DIM + MLA_ROPE_DIM
MLA_V_DIM = 64
ROPE_THETA = 10000.0
N_GATE = 4 * N_MLSTM_HEADS
N_MEM_HEADS = 4
N_EXPERTS = 16
EC_CAPACITY_FACTOR = 2

LANES = 128
SUBLANES = 8
VMEM_LIMIT_BYTES = 56 * 1024 * 1024

MLSTM_CHUNK = 128
TM_INPROJ = 256
TQ_MLA = 256
TM_POST = 256
TF_FFN = 512
TM_FINAL = 512
CUM_BLOCK = 256


def _cparams(sem):
    return pltpu.CompilerParams(dimension_semantics=sem, vmem_limit_bytes=VMEM_LIMIT_BYTES)


def _dot(a, b):
    return jnp.dot(a, b, preferred_element_type=F32)


def _dot_nt(a, b):
    return lax.dot_general(a, b, (((1,), (1,)), ((), ())), preferred_element_type=F32)


def _rms(x, g):
    return x * lax.rsqrt(jnp.mean(x * x, axis=-1, keepdims=True) + EPS) * g


def _sigmoid(x):
    return 1.0 / (1.0 + jnp.exp(-x))


def _full(shape):
    return pl.BlockSpec(shape, lambda *_: (0,) * len(shape))


def _inproj_kernel(x_ref, xp_ref, xn_ref, pos_ref, posr_ref, gmix_ref, wqk_ref, wvo_ref, wmisc_ref, bg_ref,
                   conv_ref, gqa_ref, wq_ref, gkva_ref, wk_ref, wv_ref, invf_ref, invfc_ref,
                   q_ref, kT_ref, v_ref, o_ref, gc_ref, gt_ref, qT_ref, kcat_ref, vT_ref):
    tm = x_ref.shape[1]
    i = pl.program_id(1)
    last = pl.num_programs(1) - 1
    g = gmix_ref[...]
    xm = _rms(x_ref[0], g)
    xprev = _rms(xp_ref[0], g) * (i > 0).astype(F32)
    xnext = _rms(xn_ref[0], g) * (i < last).astype(F32)
    xm_bf = xm.astype(BF16)
    lhs = jnp.concatenate([xprev.astype(BF16), xm_bf, xnext.astype(BF16)], axis=0)

    pqk = _dot(lhs, wqk_ref[...])
    rows = tm + 2 * SUBLANES
    up = pltpu.roll(pqk, 1, 0)[SUBLANES:SUBLANES + tm]
    dn = pltpu.roll(pqk, rows - 1, 0)[SUBLANES:SUBLANES + tm]
    mid = pqk[SUBLANES:SUBLANES + tm]
    cw = conv_ref[...]
    conv = up * cw[0:1] + mid * cw[1:2] + dn * cw[2:3]
    act = conv * _sigmoid(conv)
    q_ref[0] = act[:, :D_MLSTM].astype(BF16)
    k = act[:, D_MLSTM:] * (MLSTM_HEAD_DIM ** -0.5)
    kT_ref[0] = k.T.astype(BF16)

    vo = _dot(xm_bf, wvo_ref[...])
    v_ref[0] = vo[:, :D_MLSTM].astype(BF16)
    o_ref[0] = vo[:, D_MLSTM:]

    misc = _dot(xm_bf, wmisc_ref[...])
    cq = misc[:, :256]
    ckv = misc[:, 256:384]
    kr_raw = misc[:, 384:512]
    gp = misc[:, 512:640] + bg_ref[...]

    lane = lax.broadcasted_iota(I32, (tm, LANES), 1)
    is_f = ((lane >= 4) & (lane < 8)) | ((lane >= 12) & (lane < 16))
    logsig = jnp.minimum(gp, 0.0) - jnp.log1p(jnp.exp(-jnp.abs(gp)))
    lf = jnp.where(is_f, logsig, gp)
    rc = lax.broadcasted_iota(I32, (tm, LANES), 0) & (MLSTM_CHUNK - 1)
    pre = lf
    suf = lf
    step = 1
    while step < MLSTM_CHUNK:
        pre = pre + jnp.where(rc >= step, pltpu.roll(pre, step, 0), 0.0)
        suf = suf + jnp.where(rc + step < MLSTM_CHUNK, pltpu.roll(suf, tm - step, 0), 0.0)
        step *= 2
    cum = jnp.where(lane < 8, pre, suf)
    a = lf - pltpu.roll(cum, LANES - 4, 1)
    gc = jnp.where(is_f, cum, a)
    gc_ref[0] = gc
    gt_ref[0] = gc.T[:N_GATE]

    ang = pos_ref[0] * invf_ref[...]
    cos = jnp.cos(ang)
    sin = jnp.sin(ang)
    first_half = lane < (MLA_NOPE_DIM + MLA_ROPE_DIM // 2)
    sin_signed = jnp.where(first_half, -sin, sin)

    def rope(t):
        partner = jnp.where(first_half, pltpu.roll(t, LANES - MLA_ROPE_DIM // 2, 1),
                            pltpu.roll(t, MLA_ROPE_DIM // 2, 1))
        return t * cos + partner * sin_signed

    ckvn = _rms(ckv, gkva_ref[...]).astype(BF16)
    kn = _dot(ckvn, wk_ref[...])
    kr = rope(kr_raw)
    for h in range(N_MLA_HEADS):
        sl = slice(h * LANES, (h + 1) * LANES)
        kcat_ref[0, :, sl] = (kn[:, sl] + kr).astype(BF16)

    cqn = _rms(cq, gqa_ref[...]).astype(BF16)
    qT = _dot_nt(wq_ref[...], cqn)
    ang_t = invfc_ref[...] * posr_ref[0]
    cos_t = jnp.cos(ang_t)
    sin_t = jnp.sin(ang_t)
    r0, r1, r2 = MLA_NOPE_DIM, MLA_NOPE_DIM + MLA_ROPE_DIM // 2, MLA_QK_DIM
    for h in range(N_MLA_HEADS):
        blk = qT[h * LANES:(h + 1) * LANES]
        t1 = blk[r0:r1]
        t2 = blk[r1:r2]
        roped = jnp.concatenate([blk[:r0], t1 * cos_t - t2 * sin_t, t2 * cos_t + t1 * sin_t, blk[r2:]], axis=0)
        qT_ref[0, h * LANES:(h + 1) * LANES, :] = roped.astype(BF16)
    vT = _dot_nt(wv_ref[...], ckvn)
    rid = lax.broadcasted_iota(I32, vT.shape, 0) & (LANES - 1)
    vT_ref[0] = jnp.where(rid == MLA_V_DIM, 1.0, vT).astype(BF16)


def _inproj(x, pos_f, g_mix, w_in, b_gates, conv_qk, g_q_a, w_q_b, g_kv_a, w_kv_b):
    B, S, D = x.shape
    tm = TM_INPROJ
    nt = S // tm
    hb = tm // SUBLANES
    off = [0, 512, 1024, 1536, 2048, 2064, 2320, 2448, 2480]
    wqk = w_in[:, off[0]:off[2]].astype(BF16)
    wvo = w_in[:, off[2]:off[4]].astype(BF16)
    w_gate = w_in[:, off[4]:off[5]]
    w_cq = w_in[:, off[5]:off[6]]
    w_ckv = w_in[:, off[6]:off[7]]
    w_kr = w_in[:, off[7]:off[8]]
    kr_blk = jnp.zeros((D, LANES), F32).at[:, MLA_NOPE_DIM:MLA_QK_DIM].set(w_kr)
    gate_blk = jnp.zeros((D, LANES), F32).at[:, :N_GATE].set(w_gate)
    wmisc = jnp.concatenate([w_cq, w_ckv, kr_blk, gate_blk], axis=1).astype(BF16)
    bg = jnp.zeros((1, LANES), F32).at[0, :N_GATE].set(b_gates)
    pad_heads = lambda w3: jnp.pad(w3, ((0, 0), (0, 0), (0, LANES - w3.shape[2]))).reshape(w3.shape[0], -1)
    wq = pad_heads(w_q_b.reshape(-1, N_MLA_HEADS, MLA_QK_DIM)).T.astype(BF16)
    wkv3 = w_kv_b.reshape(-1, N_MLA_HEADS, MLA_NOPE_DIM + MLA_V_DIM)
    wk = pad_heads(wkv3[:, :, :MLA_NOPE_DIM]).astype(BF16)
    wv = pad_heads(wkv3[:, :, MLA_NOPE_DIM:]).T.astype(BF16)
    inv_freq = ROPE_THETA ** (-jnp.arange(0, MLA_ROPE_DIM, 2, dtype=F32) / MLA_ROPE_DIM)
    invf = jnp.zeros((1, LANES), F32).at[0, MLA_NOPE_DIM:MLA_QK_DIM].set(jnp.concatenate([inv_freq, inv_freq]))
    invf_col = inv_freq.reshape(-1, 1)
    pos_row = pos_f.reshape(B, 1, S)

    row = lambda w: pl.BlockSpec((1, tm, w), lambda b, i: (b, i, 0))
    in_specs = [
        row(D),
        pl.BlockSpec((1, SUBLANES, D), lambda b, i: (b, jnp.maximum(i * hb - 1, 0), 0)),
        pl.BlockSpec((1, SUBLANES, D), lambda b, i: (b, jnp.minimum((i + 1) * hb, S // SUBLANES - 1), 0)),
        row(1),
        pl.BlockSpec((1, 1, tm), lambda b, i: (b, 0, i)),
        _full((1, D)), _full(wqk.shape), _full(wvo.shape), _full(wmisc.shape), _full((1, LANES)),
        _full(conv_qk.shape), _full((1, 256)), _full(wq.shape), _full((1, 128)), _full(wk.shape),
        _full(wv.shape), _full((1, LANES)), _full(invf_col.shape),
    ]
    out_shape = [
        jax.ShapeDtypeStruct((B, S, D_MLSTM), BF16),
        jax.ShapeDtypeStruct((B, D_MLSTM, S), BF16),
        jax.ShapeDtypeStruct((B, S, D_MLSTM), BF16),
        jax.ShapeDtypeStruct((B, S, D_MLSTM), F32),
        jax.ShapeDtypeStruct((B, S, LANES), F32),
        jax.ShapeDtypeStruct((B, N_GATE, S), F32),
        jax.ShapeDtypeStruct((B, N_MLA_HEADS * LANES, S), BF16),
        jax.ShapeDtypeStruct((B, S, N_MLA_HEADS * LANES), BF16),
        jax.ShapeDtypeStruct((B, N_MLA_HEADS * LANES, S), BF16),
    ]
    out_specs = [
        row(D_MLSTM),
        pl.BlockSpec((1, D_MLSTM, tm), lambda b, i: (b, 0, i)),
        row(D_MLSTM), row(D_MLSTM), row(LANES),
        pl.BlockSpec((1, N_GATE, tm), lambda b, i: (b, 0, i)),
        pl.BlockSpec((1, N_MLA_HEADS * LANES, tm), lambda b, i: (b, 0, i)),
        row(N_MLA_HEADS * LANES),
        pl.BlockSpec((1, N_MLA_HEADS * LANES, tm), lambda b, i: (b, 0, i)),
    ]
    return pl.pallas_call(
        _inproj_kernel, grid=(B, nt), in_specs=in_specs, out_specs=out_specs, out_shape=out_shape,
        compiler_params=_cparams(("parallel", "parallel")), name="inproj",
    )(x, x, x, pos_f, pos_row, g_mix.reshape(1, D), wqk, wvo, wmisc, bg, conv_qk, g_q_a.reshape(1, -1), wq,
      g_kv_a.reshape(1, -1), wk, wv, invf, invf_col)


def _mlstm_kernel(qf_ref, kf_ref, vf_ref, gcf_ref, gtf_ref, qb_ref, kb_ref, vb_ref, gcb_ref, gtb_ref,
                  hf_ref, hb_ref, cn_ref, m_ref):
    T = MLSTM_CHUNK
    hd = MLSTM_HEAD_DIM

    @pl.when(pl.program_id(1) == 0)
    def _():
        cn_ref[...] = jnp.zeros_like(cn_ref)
        m_ref[...] = jnp.zeros_like(m_ref)

    t_idx = lax.broadcasted_iota(I32, (T, T), 0)
    s_idx = lax.broadcasted_iota(I32, (T, T), 1)
    ones_col = (lax.broadcasted_iota(I32, (T, hd), 1) == 0).astype(BF16)
    neg_inf = F32(-jnp.inf)

    for d, (q_ref, kT_ref, v_ref, gc_ref, gt_ref, h_ref) in enumerate(
            ((qf_ref, kf_ref, vf_ref, gcf_ref, gtf_ref, hf_ref),
             (qb_ref, kb_ref, vb_ref, gcb_ref, gtb_ref, hb_ref))):
        mask = (s_idx <= t_idx) if d == 0 else (s_idx >= t_idx)
        end = T - 1 if d == 0 else 0
        gc = gc_ref[0]
        gt = gt_ref[0]
        for h in range(N_MLSTM_HEADS):
            st = d * N_MLSTM_HEADS + h
            la = d * 8 + h
            a_col = gc[:, la:la + 1]
            b_col = gc[:, la + 4:la + 5]
            a_row = gt[la:la + 1, :]
            m_prev = m_ref[st][0:1, 0:1]
            qh = q_ref[0, :, h * hd:(h + 1) * hd]
            kTh = kT_ref[0, h * hd:(h + 1) * hd, :]
            vh = v_ref[0, :, h * hd:(h + 1) * hd]
            cn = cn_ref[st]

            cm = jnp.max(jnp.where(mask, a_row, neg_inf), axis=1, keepdims=True)
            m_run = jnp.maximum(cm, m_prev)
            e = jnp.exp(jnp.where(mask, a_row - m_run, neg_inf))
            pw = e * _dot(qh, kTh)
            qcn = _dot(qh, cn.astype(BF16))
            pv = _dot(pw.astype(BF16), vh)
            iw = jnp.exp(m_prev - m_run)
            num = iw * qcn[:, :hd] + pv
            den = iw * qcn[:, hd:hd + 1] + jnp.sum(pw, axis=1, keepdims=True)
            floor = jnp.exp(-(b_col + m_run))
            h_ref[0, :, h * hd:(h + 1) * hd] = num / jnp.maximum(jnp.abs(den), floor)

            m_end = m_run[end:end + 1]
            g_sum = b_col[end:end + 1]
            decay = jnp.exp(m_prev - m_end)
            ws_row = jnp.exp(a_row - m_end)
            kw = (kTh.astype(F32) * ws_row).astype(BF16)
            v_ext = jnp.concatenate([vh, ones_col], axis=1)
            cn_ref[st] = decay * cn + _dot(kw, v_ext)
            m_ref[st] = jnp.broadcast_to(g_sum + m_end, (SUBLANES, LANES))


def _mlstm(q, kT, v, gc, gt):
    B, S, _ = q.shape
    T = MLSTM_CHUNK
    nc = S // T
    fwd = lambda b, j: (b, j, 0)
    bwd = lambda b, j: (b, nc - 1 - j, 0)
    fwd_t = lambda b, j: (b, 0, j)
    bwd_t = lambda b, j: (b, 0, nc - 1 - j)

    def specs(row_map, col_map):
        return [pl.BlockSpec((1, T, D_MLSTM), row_map), pl.BlockSpec((1, D_MLSTM, T), col_map),
                pl.BlockSpec((1, T, D_MLSTM), row_map), pl.BlockSpec((1, T, LANES), row_map),
                pl.BlockSpec((1, N_GATE, T), col_map)]

    n_state = 2 * N_MLSTM_HEADS
    return pl.pallas_call(
        _mlstm_kernel, grid=(B, nc),
        in_specs=specs(fwd, fwd_t) + specs(bwd, bwd_t),
        out_specs=[pl.BlockSpec((1, T, D_MLSTM), fwd), pl.BlockSpec((1, T, D_MLSTM), bwd)],
        out_shape=[jax.ShapeDtypeStruct((B, S, D_MLSTM), F32)] * 2,
        scratch_shapes=[pltpu.VMEM((n_state, MLSTM_HEAD_DIM, 2 * MLSTM_HEAD_DIM), F32),
                        pltpu.VMEM((n_state, SUBLANES, LANES), F32)],
        compiler_params=_cparams(("parallel", "arbitrary")), name="mlstm",
    )(q, kT, v, gc, gt, q, kT, v, gc, gt)


def _mla_kernel(qT_ref, k_ref, vT_ref, g_ref, o_ref):
    c = (MLA_QK_DIM ** -0.5) * math.log2(math.e)
    ss = [_dot(k_ref[0, :, h * LANES:(h + 1) * LANES], qT_ref[0, h * LANES:(h + 1) * LANES, :])
          for h in range(2)]
    ys = []
    for h in range(2):
        s = ss[h]
        m = jnp.max(s, axis=0, keepdims=True)
        p = jnp.exp2((s - m) * c).astype(BF16)
        acc = _dot(vT_ref[0, h * LANES:(h + 1) * LANES, :], p)
        o = acc[:MLA_V_DIM] / acc[MLA_V_DIM:MLA_V_DIM + 1]
        ms = jnp.mean(o * o, axis=0, keepdims=True)
        ys.append(o * lax.rsqrt(ms + EPS))
    o_ref[0] = jnp.concatenate(ys, axis=0).T * g_ref[...]


def _mla(qT, kcat, vT, g_head_mla):
    B, S, _ = kcat.shape
    tq = TQ_MLA
    npair = N_MLA_HEADS // 2
    return pl.pallas_call(
        _mla_kernel, grid=(B, npair, S // tq),
        in_specs=[pl.BlockSpec((1, 2 * LANES, tq), lambda b, p, i: (b, p, i)),
                  pl.BlockSpec((1, S, 2 * LANES), lambda b, p, i: (b, 0, p)),
                  pl.BlockSpec((1, 2 * LANES, S), lambda b, p, i: (b, p, 0)),
                  pl.BlockSpec((1, LANES), lambda b, p, i: (0, p))],
        out_specs=pl.BlockSpec((1, tq, LANES), lambda b, p, i: (b, i, p)),
        out_shape=jax.ShapeDtypeStruct((B, S, N_MLA_HEADS * MLA_V_DIM), F32),
        compiler_params=_cparams(("parallel", "parallel", "parallel")), name="mla_attn",
    )(qT, kcat, vT, g_head_mla.reshape(1, -1))


def _memkv_kernel(mem_ref, g_ref, wk_ref, wv_ref, k_ref, v_ref):
    mn = _rms(mem_ref[0], g_ref[...]).astype(BF16)
    k_ref[0] = _dot(mn, wk_ref[...]).astype(BF16)
    v_ref[0] = _dot(mn, wv_ref[...]).astype(BF16)


def _memkv(mem, g_mem_kv, w_k, w_v):
    B, M, D = mem.shape
    blk = pl.BlockSpec((1, M, D), lambda b: (b, 0, 0))
    return pl.pallas_call(
        _memkv_kernel, grid=(B,),
        in_specs=[blk, _full((1, D)), _full(w_k.shape), _full(w_v.shape)],
        out_specs=[blk, blk], out_shape=[jax.ShapeDtypeStruct((B, M, D), BF16)] * 2,
        compiler_params=_cparams(("parallel",)), name="memkv",
    )(mem, g_mem_kv.reshape(1, D), w_k.astype(BF16), w_v.astype(BF16))


def _post_kernel(x_ref, hf_ref, hb_ref, om_ref, ya_ref, km_ref, vm_ref, ghm_ref, wout_ref, gmx_ref, wmq_ref,
                 wmo_ref, gffn_ref, wrh_ref, wrl_ref, x2_ref, hn_ref, aff_ref):
    tm = x_ref.shape[1]
    D = x_ref.shape[2]
    hm = hf_ref[0] + hb_ref[0]
    gate = _sigmoid(om_ref[0])
    ghm = ghm_ref[...]
    parts = []
    for h in range(N_MLSTM_HEADS):
        sl = slice(h * MLSTM_HEAD_DIM, (h + 1) * MLSTM_HEAD_DIM)
        parts.append(gate[:, sl] * _rms(hm[:, sl], ghm[:, sl]))
    ycat = jnp.concatenate(parts + [ya_ref[0]], axis=1).astype(BF16)
    x1 = x_ref[0] + _dot(ycat, wout_ref[...])

    dm = D // N_MEM_HEADS
    qm = _dot(_rms(x1, gmx_ref[...]).astype(BF16), wmq_ref[...]).astype(BF16)
    km = km_ref[0]
    vm = vm_ref[0]
    outs = []
    for h in range(N_MEM_HEADS):
        sl = slice(h * dm, (h + 1) * dm)
        s = _dot_nt(qm[:, sl], km[:, sl]) * (dm ** -0.5)
        e = jnp.exp(s - jnp.max(s, axis=1, keepdims=True))
        p = e / jnp.sum(e, axis=1, keepdims=True)
        outs.append(_dot(p.astype(BF16), vm[:, sl]))
    om = jnp.concatenate(outs, axis=1).astype(BF16)
    x2 = x1 + _dot(om, wmo_ref[...])
    x2_ref[0] = x2

    hn = _rms(x2, gffn_ref[...])
    hn_ref[0] = hn

    hi = hn.astype(BF16)
    lo = (hn - hi.astype(F32)).astype(BF16)
    wrh = wrh_ref[...]
    logits = _dot(hi, wrh) + (_dot(hi, wrl_ref[...]) + _dot(lo, wrh))
    lane = lax.broadcasted_iota(I32, (tm, LANES), 1)
    valid = lane < N_EXPERTS
    logits = jnp.where(valid, logits, -jnp.inf)
    e = jnp.exp(logits - jnp.max(logits, axis=1, keepdims=True))
    aff_ref[0] = e / jnp.sum(e, axis=1, keepdims=True)


def _post(x, hf, hb, om, ya, kmem, vmem, g_head_mlstm, w_out, g_mem_x, w_mem_q, w_mem_o, g_ffn, w_router):
    B, S, D = x.shape
    tm = TM_POST
    M = kmem.shape[1]
    wr = jnp.zeros((D, LANES), F32).at[:, :N_EXPERTS].set(w_router)
    wrh = wr.astype(BF16)
    wrl = (wr - wrh.astype(F32)).astype(BF16)
    row = lambda w: pl.BlockSpec((1, tm, w), lambda b, i: (b, i, 0))
    mem_spec = pl.BlockSpec((1, M, D), lambda b, i: (b, 0, 0))
    return pl.pallas_call(
        _post_kernel, grid=(B, S // tm),
        in_specs=[row(D), row(D_MLSTM), row(D_MLSTM), row(D_MLSTM), row(D_MLSTM), mem_spec, mem_spec,
                  _full((1, D_MLSTM)), _full((D, D)), _full((1, D)), _full((D, D)), _full((D, D)),
                  _full((1, D)), _full((D, LANES)), _full((D, LANES))],
        out_specs=[row(D), row(D), row(LANES)],
        out_shape=[jax.ShapeDtypeStruct((B, S, D), F32), jax.ShapeDtypeStruct((B, S, D), F32),
                   jax.ShapeDtypeStruct((B, S, LANES), F32)],
        compiler_params=_cparams(("parallel", "parallel")), name="post_mixer",
    )(x, hf, hb, om, ya, kmem, vmem, g_head_mlstm.reshape(1, -1), w_out.astype(BF16), g_mem_x.reshape(1, D),
      w_mem_q.astype(BF16), w_mem_o.astype(BF16), g_ffn.reshape(1, D), wrh, wrl)


def _topk_kernel(aff_ref, idx_ref, cum_ref):
    S = aff_ref.shape[1]
    cap = idx_ref.shape[2]
    nblk = S // CUM_BLOCK

    valid = lax.broadcasted_iota(I32, (1, LANES), 1) < N_EXPERTS

    def open_interval(c):
        lo, hi = c
        mid = 0.5 * (lo + hi)
        return jnp.max(jnp.where(valid & (mid > lo) & (mid < hi), 1.0, 0.0)) > 0.0

    def bisect(c):
        lo, hi = c
        mid = 0.5 * (lo + hi)
        cnt = jnp.sum((aff_ref[0] >= mid).astype(F32), axis=0, keepdims=True)
        ge = cnt >= cap
        return jnp.where(ge, mid, lo), jnp.where(ge, hi, mid)

    lo, hi = lax.while_loop(open_interval, bisect, (jnp.zeros((1, LANES), F32), jnp.full((1, LANES), 2.0, F32)))
    need = cap - jnp.sum((aff_ref[0] >= hi).astype(F32), axis=0, keepdims=True)

    r = lax.broadcasted_iota(I32, (CUM_BLOCK, CUM_BLOCK), 0)
    c = lax.broadcasted_iota(I32, (CUM_BLOCK, CUM_BLOCK), 1)
    tri = (c <= r).astype(BF16)

    def cum_body(blk, carry):
        ce, cs = carry
        rows = pl.ds(pl.multiple_of(blk * CUM_BLOCK, CUM_BLOCK), CUM_BLOCK)
        a = aff_ref[0, rows, :]
        sure = a >= hi
        tie = (a >= lo) & jnp.logical_not(sure)
        eq = tie.astype(F32)
        eq_incl = _dot(tri, eq.astype(BF16)) + ce
        sel = jnp.where(sure | (tie & (eq_incl - eq < need)), 1.0, 0.0)
        sel_incl = _dot(tri, sel.astype(BF16)) + cs
        cum_ref[rows, :] = sel_incl
        return eq_incl[CUM_BLOCK - 1:CUM_BLOCK], sel_incl[CUM_BLOCK - 1:CUM_BLOCK]

    zero = jnp.zeros((1, LANES), F32)
    lax.fori_loop(0, nblk, cum_body, (zero, zero))

    j_row = lax.broadcasted_iota(I32, (1, cap), 1).astype(F32)
    for e in range(N_EXPERTS):
        def cnt_body(blk, acc):
            rows = pl.ds(pl.multiple_of(blk * CUM_BLOCK, CUM_BLOCK), CUM_BLOCK)
            col = cum_ref[rows, e:e + 1]
            return acc + jnp.sum((col <= j_row).astype(F32), axis=0, keepdims=True)
        acc = lax.fori_loop(0, nblk, cnt_body, jnp.zeros((1, cap), F32))
        idx_ref[0, e:e + 1, :] = acc.astype(I32)


def _topk(aff, cap):
    B, S, _ = aff.shape
    return pl.pallas_call(
        _topk_kernel, grid=(B,),
        in_specs=[pl.BlockSpec((1, S, LANES), lambda b: (b, 0, 0))],
        out_specs=pl.BlockSpec((1, N_EXPERTS, cap), lambda b: (b, 0, 0)),
        out_shape=jax.ShapeDtypeStruct((B, N_EXPERTS, cap), I32),
        scratch_shapes=[pltpu.VMEM((S, LANES), F32)],
        compiler_params=_cparams(("parallel",)), name="topk",
    )(aff)


def _gather_kernel(idx_ref, hn_ref, aff_ref, xe_ref, gs_ref, rows_sc):
    cap = xe_ref.shape[2]

    def body(j, _):
        i = idx_ref[0, 0, j]
        rows_sc[pl.ds(j, 1), :] = hn_ref[0, pl.ds(i, 1), :]
        gs_ref[0, 0, pl.ds(j, 1), :] = aff_ref[0, pl.ds(i, 1), :]
        return 0

    lax.fori_loop(0, cap, body, 0, unroll=8)
    xe_ref[0, 0] = rows_sc[...].astype(BF16)


def _gather(idx3, hn, aff, cap):
    B, S, D = hn.shape
    E = N_EXPERTS
    return pl.pallas_call(
        _gather_kernel, grid=(B, E),
        in_specs=[pl.BlockSpec((1, 1, cap), lambda b, e: (b * E + e, 0, 0), memory_space=pltpu.SMEM),
                  pl.BlockSpec((1, S, D), lambda b, e: (b, 0, 0)),
                  pl.BlockSpec((1, S, LANES), lambda b, e: (b, 0, 0))],
        out_specs=[pl.BlockSpec((1, 1, cap, D), lambda b, e: (b, e, 0, 0)),
                   pl.BlockSpec((1, 1, cap, LANES), lambda b, e: (b, e, 0, 0))],
        out_shape=[jax.ShapeDtypeStruct((B, E, cap, D), BF16), jax.ShapeDtypeStruct((B, E, cap, LANES), F32)],
        scratch_shapes=[pltpu.VMEM((cap, D), F32)],
        compiler_params=_cparams(("parallel", "arbitrary")), name="moe_gather",
    )(idx3, hn, aff)


def _ffn_kernel(xe_ref, gs_ref, wg_ref, wu_ref, wd_ref, ye_ref, wg_sc, wu_sc, wd_sc):
    e = pl.program_id(0)
    f = pl.program_id(1)
    nb = xe_ref.shape[0]
    cap = xe_ref.shape[2]

    wg_sc[...] = wg_ref[0].astype(BF16)
    wu_sc[...] = wu_ref[0].astype(BF16)
    wd_sc[...] = wd_ref[0].astype(BF16)
    lane = lax.broadcasted_iota(I32, (cap, LANES), 1)
    for b in range(nb):
        xb = xe_ref[b, 0]
        h1 = _dot(xb, wg_sc[...])
        h2 = _dot(xb, wu_sc[...])
        hid = (h1 * _sigmoid(h1) * h2).astype(BF16)
        y = _dot(hid, wd_sc[...])

        @pl.when(f == 0)
        def _():
            ye_ref[b, 0] = y

        @pl.when(f > 0)
        def _():
            ye_ref[b, 0] += y

        @pl.when(f == pl.num_programs(1) - 1)
        def _():
            gate = jnp.sum(jnp.where(lane == e, gs_ref[b, 0], 0.0), axis=1, keepdims=True)
            ye_ref[b, 0] *= gate


def _ffn(xe, gs, w_gate, w_up, w_down):
    B, E, cap, D = xe.shape
    F = w_gate.shape[2]
    tf = TF_FFN
    return pl.pallas_call(
        _ffn_kernel, grid=(E, F // tf),
        in_specs=[pl.BlockSpec((B, 1, cap, D), lambda e, f: (0, e, 0, 0)),
                  pl.BlockSpec((B, 1, cap, LANES), lambda e, f: (0, e, 0, 0)),
                  pl.BlockSpec((1, D, tf), lambda e, f: (e, 0, f)),
                  pl.BlockSpec((1, D, tf), lambda e, f: (e, 0, f)),
                  pl.BlockSpec((1, tf, D), lambda e, f: (e, f, 0))],
        out_specs=pl.BlockSpec((B, 1, cap, D), lambda e, f: (0, e, 0, 0)),
        out_shape=jax.ShapeDtypeStruct((B, E, cap, D), F32),
        scratch_shapes=[pltpu.VMEM((D, tf), BF16), pltpu.VMEM((D, tf), BF16), pltpu.VMEM((tf, D), BF16)],
        compiler_params=_cparams(("parallel", "arbitrary")), name="moe_ffn",
    )(xe, gs, w_gate, w_up, w_down)


def _scatter_kernel(idx_ref, ye_ref, out_ref):
    cap = ye_ref.shape[2]

    @pl.when(pl.program_id(1) == 0)
    def _():
        out_ref[...] = jnp.zeros_like(out_ref)

    def body(j, _):
        i = idx_ref[0, 0, j]
        out_ref[0, pl.ds(i, 1), :] += ye_ref[0, 0, pl.ds(j, 1), :]
        return 0

    lax.fori_loop(0, cap, body, 0, unroll=8)


def _scatter(idx3, ye, S):
    B, E, cap, D = ye.shape
    return pl.pallas_call(
        _scatter_kernel, grid=(B, E),
        in_specs=[pl.BlockSpec((1, 1, cap), lambda b, e: (b * E + e, 0, 0), memory_space=pltpu.SMEM),
                  pl.BlockSpec((1, 1, cap, D), lambda b, e: (b, e, 0, 0))],
        out_specs=pl.BlockSpec((1, S, D), lambda b, e: (b, 0, 0)),
        out_shape=jax.ShapeDtypeStruct((B, S, D), F32),
        compiler_params=_cparams(("parallel", "arbitrary")), name="moe_scatter",
    )(idx3, ye)


def _final_kernel(x_ref, y_ref, g_ref, o_ref):
    o_ref[0] = _rms(x_ref[0] + y_ref[0], g_ref[...])


def _final(x2, moe, g_final):
    B, S, D = x2.shape
    row = pl.BlockSpec((1, TM_FINAL, D), lambda b, i: (b, i, 0))
    return pl.pallas_call(
        _final_kernel, grid=(B, S // TM_FINAL), in_specs=[row, row, _full((1, D))], out_specs=row,
        out_shape=jax.ShapeDtypeStruct((B, S, D), F32),
        compiler_params=_cparams(("parallel", "parallel")), name="final_norm",
    )(x2, moe, g_final.reshape(1, D))


def kernel(x, mem, positions, g_mix, w_in, b_gates, conv_qk, g_q_a, w_q_b, g_kv_a, w_kv_b, g_head_mlstm,
           g_head_mla, w_out, g_mem_x, g_mem_kv, w_mem_q, w_mem_k, w_mem_v, w_mem_o, g_ffn, w_router,
           w_exp_gate, w_exp_up, w_exp_down, g_final):
    B, S, D = x.shape
    depth = g_mix.shape[0]
    assert depth == 1, "the MoE residual is folded into the final norm kernel, which assumes one layer"
    cap = EC_CAPACITY_FACTOR * S // N_EXPERTS
    pos_f = positions.astype(F32).reshape(B, S, 1)
    for l in range(depth):
        q, kT, v, om, gc, gt, qT, kcat, vT = _inproj(
            x, pos_f, g_mix[l], w_in[l], b_gates[l], conv_qk[l], g_q_a[l], w_q_b[l], g_kv_a[l], w_kv_b[l])
        hf, hb = _mlstm(q, kT, v, gc, gt)
        ya = _mla(qT, kcat, vT, g_head_mla[l])
        kmem, vmem = _memkv(mem, g_mem_kv[l], w_mem_k[l], w_mem_v[l])
        x2, hn, aff = _post(x, hf, hb, om, ya, kmem, vmem, g_head_mlstm[l], w_out[l], g_mem_x[l],
                            w_mem_q[l], w_mem_o[l], g_ffn[l], w_router[l])
        idx = _topk(aff, cap)
        idx3 = idx.reshape(B * N_EXPERTS, 1, cap)
        xe, gs = _gather(idx3, hn, aff, cap)
        ye = _ffn(xe, gs, w_exp_gate[l], w_exp_up[l], w_exp_down[l])
        moe = _scatter(idx3, ye, S)
        x = x2
    return _final(x, moe, g_final)
```

```python
import functools
import math

import jax
import jax.numpy as jnp
from jax import lax
from jax.experimental import pallas as pl
from jax.experimental.pallas import tpu as pltpu

F32 = jnp.float32
BF16 = jnp.bfloat16
I32 = jnp.int32

EPS = 1e-6
N_MLSTM_HEADS = 4
MLSTM_HEAD_DIM = 128
D_MLSTM = N_MLSTM_HEADS * MLSTM_HEAD_DIM
N_MLA_HEADS = 8
MLA_NOPE_DIM = 64
MLA_ROPE_DIM = 32
MLA_QK_DIM = MLA_NOPE_DIM + MLA_ROPE_DIM
MLA_V_DIM = 64
ROPE_THETA = 10000.0
N_GATE = 4 * N_MLSTM_HEADS
N_MEM_HEADS = 4
N_EXPERTS = 16
EC_CAPACITY_FACTOR = 2

LANES = 128
SUBLANES = 8
VMEM_LIMIT_BYTES = 56 * 1024 * 1024

MLSTM_CHUNK = 128
TM_INPROJ = 256
TQ_MLA = 256
MLA_CHUNKS = 8
TM_POST = 256
TF_FFN = 512
TM_FINAL = 512
CUM_BLOCK = 256


def _cparams(sem):
    return pltpu.CompilerParams(dimension_semantics=sem, vmem_limit_bytes=VMEM_LIMIT_BYTES)


def _dot(a, b):
    return jnp.dot(a, b, preferred_element_type=F32)


def _dot_nt(a, b):
    return lax.dot_general(a, b, (((1,), (1,)), ((), ())), preferred_element_type=F32)


def _rms(x, g):
    return x * lax.rsqrt(jnp.mean(x * x, axis=-1, keepdims=True) + EPS) * g


def _sigmoid(x):
    return 1.0 / (1.0 + jnp.exp(-x))


def _full(shape):
    return pl.BlockSpec(shape, lambda *_: (0,) * len(shape))


def _inproj_kernel(x_ref, xp_ref, xn_ref, pos_ref, posr_ref, gmix_ref, wqk_ref, wvo_ref, wmisc_ref, bg_ref,
                   conv_ref, gqa_ref, wq_ref, gkva_ref, wk_ref, wv_ref, invf_ref, invfc_ref,
                   q_ref, kT_ref, v_ref, o_ref, gc_ref, gt_ref, qT_ref, kcat_ref, vT_ref):
    tm = x_ref.shape[1]
    i = pl.program_id(1)
    last = pl.num_programs(1) - 1
    g = gmix_ref[...]
    xm = _rms(x_ref[0], g)
    xprev = _rms(xp_ref[0], g) * (i > 0).astype(F32)
    xnext = _rms(xn_ref[0], g) * (i < last).astype(F32)
    xm_bf = xm.astype(BF16)
    lhs = jnp.concatenate([xprev.astype(BF16), xm_bf, xnext.astype(BF16)], axis=0)

    pqk = _dot(lhs, wqk_ref[...])
    rows = tm + 2 * SUBLANES
    up = pltpu.roll(pqk, 1, 0)[SUBLANES:SUBLANES + tm]
    dn = pltpu.roll(pqk, rows - 1, 0)[SUBLANES:SUBLANES + tm]
    mid = pqk[SUBLANES:SUBLANES + tm]
    cw = conv_ref[...]
    conv = up * cw[0:1] + mid * cw[1:2] + dn * cw[2:3]
    act = conv * _sigmoid(conv)
    q_ref[0] = act[:, :D_MLSTM].astype(BF16)
    k = act[:, D_MLSTM:] * (MLSTM_HEAD_DIM ** -0.5)
    kT_ref[0] = k.T.astype(BF16)

    vo = _dot(xm_bf, wvo_ref[...])
    v_ref[0] = vo[:, :D_MLSTM].astype(BF16)
    o_ref[0] = vo[:, D_MLSTM:]

    misc = _dot(xm_bf, wmisc_ref[...])
    cq = misc[:, :256]
    ckv = misc[:, 256:384]
    kr_raw = misc[:, 384:512]
    gp = misc[:, 512:640] + bg_ref[...]

    lane = lax.broadcasted_iota(I32, (tm, LANES), 1)
    is_f = ((lane >= 4) & (lane < 8)) | ((lane >= 12) & (lane < 16))
    logsig = jnp.minimum(gp, 0.0) - jnp.log1p(jnp.exp(-jnp.abs(gp)))
    lf = jnp.where(is_f, logsig, gp)
    rc = lax.broadcasted_iota(I32, (tm, LANES), 0) & (MLSTM_CHUNK - 1)
    pre = lf
    suf = lf
    step = 1
    while step < MLSTM_CHUNK:
        pre = pre + jnp.where(rc >= step, pltpu.roll(pre, step, 0), 0.0)
        suf = suf + jnp.where(rc + step < MLSTM_CHUNK, pltpu.roll(suf, tm - step, 0), 0.0)
        step *= 2
    cum = jnp.where(lane < 8, pre, suf)
    a = lf - pltpu.roll(cum, LANES - 4, 1)
    gc = jnp.where(is_f, cum, a)
    gc_ref[0] = gc
    gt_ref[0] = gc.T[:N_GATE]

    ang = pos_ref[0] * invf_ref[...]
    cos = jnp.cos(ang)
    sin = jnp.sin(ang)
    first_half = lane < (MLA_NOPE_DIM + MLA_ROPE_DIM // 2)
    sin_signed = jnp.where(first_half, -sin, sin)

    def rope(t):
        partner = jnp.where(first_half, pltpu.roll(t, LANES - MLA_ROPE_DIM // 2, 1),
                            pltpu.roll(t, MLA_ROPE_DIM // 2, 1))
        return t * cos + partner * sin_signed

    ckvn = _rms(ckv, gkva_ref[...]).astype(BF16)
    kn = _dot(ckvn, wk_ref[...])
    kr = rope(kr_raw)
    for h in range(N_MLA_HEADS):
        sl = slice(h * LANES, (h + 1) * LANES)
        kcat_ref[0, :, sl] = (kn[:, sl] + kr).astype(BF16)

    cqn = _rms(cq, gqa_ref[...]).astype(BF16)
    qT = _dot_nt(wq_ref[...], cqn)
    ang_t = invfc_ref[...] * posr_ref[0]
    cos_t = jnp.cos(ang_t)
    sin_t = jnp.sin(ang_t)
    r0, r1, r2 = MLA_NOPE_DIM, MLA_NOPE_DIM + MLA_ROPE_DIM // 2, MLA_QK_DIM
    for h in range(N_MLA_HEADS):
        blk = qT[h * LANES:(h + 1) * LANES]
        t1 = blk[r0:r1]
        t2 = blk[r1:r2]
        roped = jnp.concatenate([blk[:r0], t1 * cos_t - t2 * sin_t, t2 * cos_t + t1 * sin_t, blk[r2:]], axis=0)
        qT_ref[0, h * LANES:(h + 1) * LANES, :] = roped.astype(BF16)
    vT = _dot_nt(wv_ref[...], ckvn)
    rid = lax.broadcasted_iota(I32, vT.shape, 0) & (LANES - 1)
    vT_ref[0] = jnp.where(rid == MLA_V_DIM, 1.0, vT).astype(BF16)


def _inproj(x, pos_f, g_mix, w_in, b_gates, conv_qk, g_q_a, w_q_b, g_kv_a, w_kv_b):
    B, S, D = x.shape
    tm = TM_INPROJ
    nt = S // tm
    hb = tm // SUBLANES
    off = [0, 512, 1024, 1536, 2048, 2064, 2320, 2448, 2480]
    wqk = w_in[:, off[0]:off[2]].astype(BF16)
    wvo = w_in[:, off[2]:off[4]].astype(BF16)
    w_gate = w_in[:, off[4]:off[5]]
    w_cq = w_in[:, off[5]:off[6]]
    w_ckv = w_in[:, off[6]:off[7]]
    w_kr = w_in[:, off[7]:off[8]]
    kr_blk = jnp.zeros((D, LANES), F32).at[:, MLA_NOPE_DIM:MLA_QK_DIM].set(w_kr)
    gate_blk = jnp.zeros((D, LANES), F32).at[:, :N_GATE].set(w_gate)
    wmisc = jnp.concatenate([w_cq, w_ckv, kr_blk, gate_blk], axis=1).astype(BF16)
    bg = jnp.zeros((1, LANES), F32).at[0, :N_GATE].set(b_gates)
    pad_heads = lambda w3: jnp.pad(w3, ((0, 0), (0, 0), (0, LANES - w3.shape[2]))).reshape(w3.shape[0], -1)
    wq = pad_heads(w_q_b.reshape(-1, N_MLA_HEADS, MLA_QK_DIM)).T.astype(BF16)
    wkv3 = w_kv_b.reshape(-1, N_MLA_HEADS, MLA_NOPE_DIM + MLA_V_DIM)
    wk = pad_heads(wkv3[:, :, :MLA_NOPE_DIM]).astype(BF16)
    wv = pad_heads(wkv3[:, :, MLA_NOPE_DIM:]).T.astype(BF16)
    inv_freq = ROPE_THETA ** (-jnp.arange(0, MLA_ROPE_DIM, 2, dtype=F32) / MLA_ROPE_DIM)
    invf = jnp.zeros((1, LANES), F32).at[0, MLA_NOPE_DIM:MLA_QK_DIM].set(jnp.concatenate([inv_freq, inv_freq]))
    invf_col = inv_freq.reshape(-1, 1)
    pos_row = pos_f.reshape(B, 1, S)

    row = lambda w: pl.BlockSpec((1, tm, w), lambda b, i: (b, i, 0))
    in_specs = [
        row(D),
        pl.BlockSpec((1, SUBLANES, D), lambda b, i: (b, jnp.maximum(i * hb - 1, 0), 0)),
        pl.BlockSpec((1, SUBLANES, D), lambda b, i: (b, jnp.minimum((i + 1) * hb, S // SUBLANES - 1), 0)),
        row(1),
        pl.BlockSpec((1, 1, tm), lambda b, i: (b, 0, i)),
        _full((1, D)), _full(wqk.shape), _full(wvo.shape), _full(wmisc.shape), _full((1, LANES)),
        _full(conv_qk.shape), _full((1, 256)), _full(wq.shape), _full((1, 128)), _full(wk.shape),
        _full(wv.shape), _full((1, LANES)), _full(invf_col.shape),
    ]
    out_shape = [
        jax.ShapeDtypeStruct((B, S, D_MLSTM), BF16),
        jax.ShapeDtypeStruct((B, D_MLSTM, S), BF16),
        jax.ShapeDtypeStruct((B, S, D_MLSTM), BF16),
        jax.ShapeDtypeStruct((B, S, D_MLSTM), F32),
        jax.ShapeDtypeStruct((B, S, LANES), F32),
        jax.ShapeDtypeStruct((B, N_GATE, S), F32),
        jax.ShapeDtypeStruct((B, N_MLA_HEADS * LANES, S), BF16),
        jax.ShapeDtypeStruct((B, S, N_MLA_HEADS * LANES), BF16),
        jax.ShapeDtypeStruct((B, N_MLA_HEADS * LANES, S), BF16),
    ]
    out_specs = [
        row(D_MLSTM),
        pl.BlockSpec((1, D_MLSTM, tm), lambda b, i: (b, 0, i)),
        row(D_MLSTM), row(D_MLSTM), row(LANES),
        pl.BlockSpec((1, N_GATE, tm), lambda b, i: (b, 0, i)),
        pl.BlockSpec((1, N_MLA_HEADS * LANES, tm), lambda b, i: (b, 0, i)),
        row(N_MLA_HEADS * LANES),
        pl.BlockSpec((1, N_MLA_HEADS * LANES, tm), lambda b, i: (b, 0, i)),
    ]
    return pl.pallas_call(
        _inproj_kernel, grid=(B, nt), in_specs=in_specs, out_specs=out_specs, out_shape=out_shape,
        compiler_params=_cparams(("parallel", "parallel")), name="inproj",
    )(x, x, x, pos_f, pos_row, g_mix.reshape(1, D), wqk, wvo, wmisc, bg, conv_qk, g_q_a.reshape(1, -1), wq,
      g_kv_a.reshape(1, -1), wk, wv, invf, invf_col)


def _mlstm_kernel(qf_ref, kf_ref, vf_ref, gcf_ref, gtf_ref, qb_ref, kb_ref, vb_ref, gcb_ref, gtb_ref,
                  hf_ref, hb_ref, cn_ref, m_ref):
    T = MLSTM_CHUNK
    hd = MLSTM_HEAD_DIM

    @pl.when(pl.program_id(1) == 0)
    def _():
        cn_ref[...] = jnp.zeros_like(cn_ref)
        m_ref[...] = jnp.zeros_like(m_ref)

    t_idx = lax.broadcasted_iota(I32, (T, T), 0)
    s_idx = lax.broadcasted_iota(I32, (T, T), 1)
    ones_col = (lax.broadcasted_iota(I32, (T, hd), 1) == 0).astype(BF16)
    neg_inf = F32(-jnp.inf)

    for d, (q_ref, kT_ref, v_ref, gc_ref, gt_ref, h_ref) in enumerate(
            ((qf_ref, kf_ref, vf_ref, gcf_ref, gtf_ref, hf_ref),
             (qb_ref, kb_ref, vb_ref, gcb_ref, gtb_ref, hb_ref))):
        mask = (s_idx <= t_idx) if d == 0 else (s_idx >= t_idx)
        end = T - 1 if d == 0 else 0
        gc = gc_ref[0]
        gt = gt_ref[0]
        for h in range(N_MLSTM_HEADS):
            st = d * N_MLSTM_HEADS + h
            la = d * 8 + h
            a_col = gc[:, la:la + 1]
            b_col = gc[:, la + 4:la + 5]
            a_row = gt[la:la + 1, :]
            m_prev = m_ref[st][0:1, 0:1]
            qh = q_ref[0, :, h * hd:(h + 1) * hd]
            kTh = kT_ref[0, h * hd:(h + 1) * hd, :]
            vh = v_ref[0, :, h * hd:(h + 1) * hd]
            cn = cn_ref[st]

            cm = jnp.max(jnp.where(mask, a_row, neg_inf), axis=1, keepdims=True)
            m_run = jnp.maximum(cm, m_prev)
            e = jnp.exp(jnp.where(mask, a_row - m_run, neg_inf))
            pw = e * _dot(qh, kTh)
            qcn = _dot(qh, cn.astype(BF16))
            pv = _dot(pw.astype(BF16), vh)
            iw = jnp.exp(m_prev - m_run)
            num = iw * qcn[:, :hd] + pv
            den = iw * qcn[:, hd:hd + 1] + jnp.sum(pw, axis=1, keepdims=True)
            floor = jnp.exp(-(b_col + m_run))
            h_ref[0, :, h * hd:(h + 1) * hd] = num / jnp.maximum(jnp.abs(den), floor)

            m_end = m_run[end:end + 1]
            g_sum = b_col[end:end + 1]
            decay = jnp.exp(m_prev - m_end)
            ws_row = jnp.exp(a_row - m_end)
            kw = (kTh.astype(F32) * ws_row).astype(BF16)
            v_ext = jnp.concatenate([vh, ones_col], axis=1)
            cn_ref[st] = decay * cn + _dot(kw, v_ext)
            m_ref[st] = jnp.broadcast_to(g_sum + m_end, (SUBLANES, LANES))


def _mlstm(q, kT, v, gc, gt):
    B, S, _ = q.shape
    T = MLSTM_CHUNK
    nc = S // T
    fwd = lambda b, j: (b, j, 0)
    bwd = lambda b, j: (b, nc - 1 - j, 0)
    fwd_t = lambda b, j: (b, 0, j)
    bwd_t = lambda b, j: (b, 0, nc - 1 - j)

    def specs(row_map, col_map):
        return [pl.BlockSpec((1, T, D_MLSTM), row_map), pl.BlockSpec((1, D_MLSTM, T), col_map),
                pl.BlockSpec((1, T, D_MLSTM), row_map), pl.BlockSpec((1, T, LANES), row_map),
                pl.BlockSpec((1, N_GATE, T), col_map)]

    n_state = 2 * N_MLSTM_HEADS
    return pl.pallas_call(
        _mlstm_kernel, grid=(B, nc),
        in_specs=specs(fwd, fwd_t) + specs(bwd, bwd_t),
        out_specs=[pl.BlockSpec((1, T, D_MLSTM), fwd), pl.BlockSpec((1, T, D_MLSTM), bwd)],
        out_shape=[jax.ShapeDtypeStruct((B, S, D_MLSTM), F32)] * 2,
        scratch_shapes=[pltpu.VMEM((n_state, MLSTM_HEAD_DIM, 2 * MLSTM_HEAD_DIM), F32),
                        pltpu.VMEM((n_state, SUBLANES, LANES), F32)],
        compiler_params=_cparams(("parallel", "arbitrary")), name="mlstm",
    )(q, kT, v, gc, gt, q, kT, v, gc, gt)


def _mla_kernel(qT_ref, k_ref, vT_ref, g_ref, o_ref, sa_sc, sb_sc, ma_sc, mb_sc):
    t = pl.program_id(0)
    c = (MLA_QK_DIM ** -0.5) * math.log2(math.e)
    S = k_ref.shape[1]
    tq = qT_ref.shape[2]
    ck = S // MLA_CHUNKS

    @pl.when(t == 0)
    def _():
        sb_sc[...] = jnp.zeros(sb_sc.shape, F32)
        mb_sc[...] = jnp.zeros(mb_sc.shape, F32)

    def step(s_cur, m_cur, s_prev, m_prev_ref):
        m_prev = [m_prev_ref[h][0:1] for h in range(2)]
        m_run = [jnp.full((1, tq), -jnp.inf, F32) for _ in range(2)]
        accs = [None, None]
        for j in range(MLA_CHUNKS):
            rows = slice(j * ck, (j + 1) * ck)
            for h in range(2):
                s = _dot(k_ref[0, rows, h * LANES:(h + 1) * LANES], qT_ref[0, h * LANES:(h + 1) * LANES, :])
                m_run[h] = jnp.maximum(m_run[h], jnp.max(s, axis=0, keepdims=True))
                s_cur[h, rows, :] = s
            for h in range(2):
                p = jnp.exp2((s_prev[h, rows, :] - m_prev[h]) * c).astype(BF16)
                part = _dot(vT_ref[0, h * LANES:(h + 1) * LANES, rows], p)
                accs[h] = part if j == 0 else accs[h] + part
        for h in range(2):
            m_cur[h] = jnp.broadcast_to(m_run[h], (SUBLANES, tq))
        ys = []
        for h in range(2):
            acc = accs[h]
            o = acc[:MLA_V_DIM] / acc[MLA_V_DIM:MLA_V_DIM + 1]
            ms = jnp.mean(o * o, axis=0, keepdims=True)
            ys.append(o * lax.rsqrt(ms + EPS))
        o_ref[0] = jnp.concatenate(ys, axis=0).T * g_ref[...]

    @pl.when(t % 2 == 0)
    def _():
        step(sa_sc, ma_sc, sb_sc, mb_sc)

    @pl.when(t % 2 == 1)
    def _():
        step(sb_sc, mb_sc, sa_sc, ma_sc)


def _mla(qT, kcat, vT, g_head_mla):
    B, S, _ = kcat.shape
    tq = TQ_MLA
    npair = N_MLA_HEADS // 2
    nq = S // tq
    nt = B * npair * nq

    def tile(t):
        return t // (npair * nq), (t // nq) % npair, t % nq

    def a_map(f):
        return lambda t: f(*tile(jnp.minimum(t, nt - 1)))

    def b_map(f):
        return lambda t: f(*tile(jnp.maximum(t - 1, 0)))

    return pl.pallas_call(
        _mla_kernel, grid=(nt + 1,),
        in_specs=[pl.BlockSpec((1, 2 * LANES, tq), a_map(lambda b, p, i: (b, p, i))),
                  pl.BlockSpec((1, S, 2 * LANES), a_map(lambda b, p, i: (b, 0, p))),
                  pl.BlockSpec((1, 2 * LANES, S), b_map(lambda b, p, i: (b, p, 0))),
                  pl.BlockSpec((1, LANES), b_map(lambda b, p, i: (0, p)))],
        out_specs=pl.BlockSpec((1, tq, LANES), b_map(lambda b, p, i: (b, i, p))),
        out_shape=jax.ShapeDtypeStruct((B, S, N_MLA_HEADS * MLA_V_DIM), F32),
        scratch_shapes=[pltpu.VMEM((2, S, tq), F32), pltpu.VMEM((2, S, tq), F32),
                        pltpu.VMEM((2, SUBLANES, tq), F32), pltpu.VMEM((2, SUBLANES, tq), F32)],
        compiler_params=_cparams(("arbitrary",)), name="mla_attn",
    )(qT, kcat, vT, g_head_mla.reshape(1, -1))


def _memkv_kernel(mem_ref, g_ref, wk_ref, wv_ref, k_ref, v_ref):
    mn = _rms(mem_ref[0], g_ref[...]).astype(BF16)
    k_ref[0] = _dot(mn, wk_ref[...]).astype(BF16)
    v_ref[0] = _dot(mn, wv_ref[...]).astype(BF16)


def _memkv(mem, g_mem_kv, w_k, w_v):
    B, M, D = mem.shape
    blk = pl.BlockSpec((1, M, D), lambda b: (b, 0, 0))
    return pl.pallas_call(
        _memkv_kernel, grid=(B,),
        in_specs=[blk, _full((1, D)), _full(w_k.shape), _full(w_v.shape)],
        out_specs=[blk, blk], out_shape=[jax.ShapeDtypeStruct((B, M, D), BF16)] * 2,
        compiler_params=_cparams(("parallel",)), name="memkv",
    )(mem, g_mem_kv.reshape(1, D), w_k.astype(BF16), w_v.astype(BF16))


def _post_kernel(x_ref, hf_ref, hb_ref, om_ref, ya_ref, km_ref, vm_ref, ghm_ref, wout_ref, gmx_ref, wmq_ref,
                 wmo_ref, gffn_ref, wrh_ref, wrl_ref, x2_ref, hn_ref, aff_ref):
    tm = x_ref.shape[1]
    D = x_ref.shape[2]
    hm = hf_ref[0] + hb_ref[0]
    gate = _sigmoid(om_ref[0])
    ghm = ghm_ref[...]
    parts = []
    for h in range(N_MLSTM_HEADS):
        sl = slice(h * MLSTM_HEAD_DIM, (h + 1) * MLSTM_HEAD_DIM)
        parts.append(gate[:, sl] * _rms(hm[:, sl], ghm[:, sl]))
    ycat = jnp.concatenate(parts + [ya_ref[0]], axis=1).astype(BF16)
    x1 = x_ref[0] + _dot(ycat, wout_ref[...])

    dm = D // N_MEM_HEADS
    qm = _dot(_rms(x1, gmx_ref[...]).astype(BF16), wmq_ref[...]).astype(BF16)
    km = km_ref[0]
    vm = vm_ref[0]
    outs = []
    for h in range(N_MEM_HEADS):
        sl = slice(h * dm, (h + 1) * dm)
        s = _dot_nt(qm[:, sl], km[:, sl]) * (dm ** -0.5)
        e = jnp.exp(s - jnp.max(s, axis=1, keepdims=True))
        p = e / jnp.sum(e, axis=1, keepdims=True)
        outs.append(_dot(p.astype(BF16), vm[:, sl]))
    om = jnp.concatenate(outs, axis=1).astype(BF16)
    x2 = x1 + _dot(om, wmo_ref[...])
    x2_ref[0] = x2

    hn = _rms(x2, gffn_ref[...])
    hn_ref[0] = hn

    hi = hn.astype(BF16)
    lo = (hn - hi.astype(F32)).astype(BF16)
    wrh = wrh_ref[...]
    logits = _dot(hi, wrh) + (_dot(hi, wrl_ref[...]) + _dot(lo, wrh))
    lane = lax.broadcasted_iota(I32, (tm, LANES), 1)
    valid = lane < N_EXPERTS
    logits = jnp.where(valid, logits, -jnp.inf)
    e = jnp.exp(logits - jnp.max(logits, axis=1, keepdims=True))
    aff_ref[0] = e / jnp.sum(e, axis=1, keepdims=True)


def _post(x, hf, hb, om, ya, kmem, vmem, g_head_mlstm, w_out, g_mem_x, w_mem_q, w_mem_o, g_ffn, w_router):
    B, S, D = x.shape
    tm = TM_POST
    M = kmem.shape[1]
    wr = jnp.zeros((D, LANES), F32).at[:, :N_EXPERTS].set(w_router)
    wrh = wr.astype(BF16)
    wrl = (wr - wrh.astype(F32)).astype(BF16)
    row = lambda w: pl.BlockSpec((1, tm, w), lambda b, i: (b, i, 0))
    mem_spec = pl.BlockSpec((1, M, D), lambda b, i: (b, 0, 0))
    return pl.pallas_call(
        _post_kernel, grid=(B, S // tm),
        in_specs=[row(D), row(D_MLSTM), row(D_MLSTM), row(D_MLSTM), row(D_MLSTM), mem_spec, mem_spec,
                  _full((1, D_MLSTM)), _full((D, D)), _full((1, D)), _full((D, D)), _full((D, D)),
                  _full((1, D)), _full((D, LANES)), _full((D, LANES))],
        out_specs=[row(D), row(D), row(LANES)],
        out_shape=[jax.ShapeDtypeStruct((B, S, D), F32), jax.ShapeDtypeStruct((B, S, D), F32),
                   jax.ShapeDtypeStruct((B, S, LANES), F32)],
        compiler_params=_cparams(("parallel", "parallel")), name="post_mixer",
    )(x, hf, hb, om, ya, kmem, vmem, g_head_mlstm.reshape(1, -1), w_out.astype(BF16), g_mem_x.reshape(1, D),
      w_mem_q.astype(BF16), w_mem_o.astype(BF16), g_ffn.reshape(1, D), wrh, wrl)


def _topk_kernel(aff_ref, idx_ref, cum_ref):
    S = aff_ref.shape[1]
    cap = idx_ref.shape[2]
    nblk = S // CUM_BLOCK

    valid = lax.broadcasted_iota(I32, (1, LANES), 1) < N_EXPERTS

    def open_interval(c):
        lo, hi = c
        mid = 0.5 * (lo + hi)
        return jnp.max(jnp.where(valid & (mid > lo) & (mid < hi), 1.0, 0.0)) > 0.0

    def bisect(c):
        lo, hi = c
        mid = 0.5 * (lo + hi)
        cnt = jnp.sum((aff_ref[0] >= mid).astype(F32), axis=0, keepdims=True)
        ge = cnt >= cap
        return jnp.where(ge, mid, lo), jnp.where(ge, hi, mid)

    lo, hi = lax.while_loop(open_interval, bisect, (jnp.zeros((1, LANES), F32), jnp.full((1, LANES), 2.0, F32)))
    need = cap - jnp.sum((aff_ref[0] >= hi).astype(F32), axis=0, keepdims=True)

    r = lax.broadcasted_iota(I32, (CUM_BLOCK, CUM_BLOCK), 0)
    c = lax.broadcasted_iota(I32, (CUM_BLOCK, CUM_BLOCK), 1)
    tri = (c <= r).astype(BF16)

    def cum_body(blk, carry):
        ce, cs = carry
        rows = pl.ds(pl.multiple_of(blk * CUM_BLOCK, CUM_BLOCK), CUM_BLOCK)
        a = aff_ref[0, rows, :]
        sure = a >= hi
        tie = (a >= lo) & jnp.logical_not(sure)
        eq = tie.astype(F32)
        eq_incl = _dot(tri, eq.astype(BF16)) + ce
        sel = jnp.where(sure | (tie & (eq_incl - eq < need)), 1.0, 0.0)
        sel_incl = _dot(tri, sel.astype(BF16)) + cs
        cum_ref[rows, :] = sel_incl
        return eq_incl[CUM_BLOCK - 1:CUM_BLOCK], sel_incl[CUM_BLOCK - 1:CUM_BLOCK]

    zero = jnp.zeros((1, LANES), F32)
    lax.fori_loop(0, nblk, cum_body, (zero, zero))

    j_row = lax.broadcasted_iota(I32, (1, cap), 1).astype(F32)
    for e in range(N_EXPERTS):
        def cnt_body(blk, acc):
            rows = pl.ds(pl.multiple_of(blk * CUM_BLOCK, CUM_BLOCK), CUM_BLOCK)
            col = cum_ref[rows, e:e + 1]
            return acc + jnp.sum((col <= j_row).astype(F32), axis=0, keepdims=True)
        acc = lax.fori_loop(0, nblk, cnt_body, jnp.zeros((1, cap), F32))
        idx_ref[0, e:e + 1, :] = acc.astype(I32)


def _topk(aff, cap):
    B, S, _ = aff.shape
    return pl.pallas_call(
        _topk_kernel, grid=(B,),
        in_specs=[pl.BlockSpec((1, S, LANES), lambda b: (b, 0, 0))],
        out_specs=pl.BlockSpec((1, N_EXPERTS, cap), lambda b: (b, 0, 0)),
        out_shape=jax.ShapeDtypeStruct((B, N_EXPERTS, cap), I32),
        scratch_shapes=[pltpu.VMEM((S, LANES), F32)],
        compiler_params=_cparams(("parallel",)), name="topk",
    )(aff)


def _gather_kernel(idx_ref, hn_ref, aff_ref, xe_ref, gs_ref, rows_sc):
    cap = xe_ref.shape[2]

    def body(j, _):
        i = idx_ref[0, 0, j]
        rows_sc[pl.ds(j, 1), :] = hn_ref[0, pl.ds(i, 1), :]
        gs_ref[0, 0, pl.ds(j, 1), :] = aff_ref[0, pl.ds(i, 1), :]
        return 0

    lax.fori_loop(0, cap, body, 0, unroll=8)
    xe_ref[0, 0] = rows_sc[...].astype(BF16)


def _gather(idx3, hn, aff, cap):
    B, S, D = hn.shape
    E = N_EXPERTS
    return pl.pallas_call(
        _gather_kernel, grid=(B, E),
        in_specs=[pl.BlockSpec((1, 1, cap), lambda b, e: (b * E + e, 0, 0), memory_space=pltpu.SMEM),
                  pl.BlockSpec((1, S, D), lambda b, e: (b, 0, 0)),
                  pl.BlockSpec((1, S, LANES), lambda b, e: (b, 0, 0))],
        out_specs=[pl.BlockSpec((1, 1, cap, D), lambda b, e: (b, e, 0, 0)),
                   pl.BlockSpec((1, 1, cap, LANES), lambda b, e: (b, e, 0, 0))],
        out_shape=[jax.ShapeDtypeStruct((B, E, cap, D), BF16), jax.ShapeDtypeStruct((B, E, cap, LANES), F32)],
        scratch_shapes=[pltpu.VMEM((cap, D), F32)],
        compiler_params=_cparams(("parallel", "arbitrary")), name="moe_gather",
    )(idx3, hn, aff)


def _ffn_kernel(xe_ref, gs_ref, wg_ref, wu_ref, wd_ref, ye_ref, wg_sc, wu_sc, wd_sc):
    e = pl.program_id(0)
    f = pl.program_id(1)
    nb = xe_ref.shape[0]
    cap = xe_ref.shape[2]

    wg_sc[...] = wg_ref[0].astype(BF16)
    wu_sc[...] = wu_ref[0].astype(BF16)
    wd_sc[...] = wd_ref[0].astype(BF16)
    lane = lax.broadcasted_iota(I32, (cap, LANES), 1)
    for b in range(nb):
        xb = xe_ref[b, 0]
        h1 = _dot(xb, wg_sc[...])
        h2 = _dot(xb, wu_sc[...])
        hid = (h1 * _sigmoid(h1) * h2).astype(BF16)
        y = _dot(hid, wd_sc[...])

        @pl.when(f == 0)
        def _():
            ye_ref[b, 0] = y

        @pl.when(f > 0)
        def _():
            ye_ref[b, 0] += y

        @pl.when(f == pl.num_programs(1) - 1)
        def _():
            gate = jnp.sum(jnp.where(lane == e, gs_ref[b, 0], 0.0), axis=1, keepdims=True)
            ye_ref[b, 0] *= gate


def _ffn(xe, gs, w_gate, w_up, w_down):
    B, E, cap, D = xe.shape
    F = w_gate.shape[2]
    tf = TF_FFN
    return pl.pallas_call(
        _ffn_kernel, grid=(E, F // tf),
        in_specs=[pl.BlockSpec((B, 1, cap, D), lambda e, f: (0, e, 0, 0)),
                  pl.BlockSpec((B, 1, cap, LANES), lambda e, f: (0, e, 0, 0)),
                  pl.BlockSpec((1, D, tf), lambda e, f: (e, 0, f)),
                  pl.BlockSpec((1, D, tf), lambda e, f: (e, 0, f)),
                  pl.BlockSpec((1, tf, D), lambda e, f: (e, f, 0))],
        out_specs=pl.BlockSpec((B, 1, cap, D), lambda e, f: (0, e, 0, 0)),
        out_shape=jax.ShapeDtypeStruct((B, E, cap, D), F32),
        scratch_shapes=[pltpu.VMEM((D, tf), BF16), pltpu.VMEM((D, tf), BF16), pltpu.VMEM((tf, D), BF16)],
        compiler_params=_cparams(("parallel", "arbitrary")), name="moe_ffn",
    )(xe, gs, w_gate, w_up, w_down)


def _scatter_kernel(idx_ref, ye_ref, out_ref):
    cap = ye_ref.shape[2]

    @pl.when(pl.program_id(1) == 0)
    def _():
        out_ref[...] = jnp.zeros_like(out_ref)

    def body(j, _):
        i = idx_ref[0, 0, j]
        out_ref[0, pl.ds(i, 1), :] += ye_ref[0, 0, pl.ds(j, 1), :]
        return 0

    lax.fori_loop(0, cap, body, 0, unroll=8)


def _scatter(idx3, ye, S):
    B, E, cap, D = ye.shape
    return pl.pallas_call(
        _scatter_kernel, grid=(B, E),
        in_specs=[pl.BlockSpec((1, 1, cap), lambda b, e: (b * E + e, 0, 0), memory_space=pltpu.SMEM),
                  pl.BlockSpec((1, 1, cap, D), lambda b, e: (b, e, 0, 0))],
        out_specs=pl.BlockSpec((1, S, D), lambda b, e: (b, 0, 0)),
        out_shape=jax.ShapeDtypeStruct((B, S, D), F32),
        compiler_params=_cparams(("parallel", "arbitrary")), name="moe_scatter",
    )(idx3, ye)


def _final_kernel(x_ref, y_ref, g_ref, o_ref):
    o_ref[0] = _rms(x_ref[0] + y_ref[0], g_ref[...])


def _final(x2, moe, g_final):
    B, S, D = x2.shape
    row = pl.BlockSpec((1, TM_FINAL, D), lambda b, i: (b, i, 0))
    return pl.pallas_call(
        _final_kernel, grid=(B, S // TM_FINAL), in_specs=[row, row, _full((1, D))], out_specs=row,
        out_shape=jax.ShapeDtypeStruct((B, S, D), F32),
        compiler_params=_cparams(("parallel", "parallel")), name="final_norm",
    )(x2, moe, g_final.reshape(1, D))


def kernel(x, mem, positions, g_mix, w_in, b_gates, conv_qk, g_q_a, w_q_b, g_kv_a, w_kv_b, g_head_mlstm,
           g_head_mla, w_out, g_mem_x, g_mem_kv, w_mem_q, w_mem_k, w_mem_v, w_mem_o, g_ffn, w_router,
           w_exp_gate, w_exp_up, w_exp_down, g_final):
    B, S, D = x.shape
    depth = g_mix.shape[0]
    assert depth == 1, "the MoE residual is folded into the final norm kernel, which assumes one layer"
    cap = EC_CAPACITY_FACTOR * S // N_EXPERTS
    pos_f = positions.astype(F32).reshape(B, S, 1)
    for l in range(depth):
        q, kT, v, om, gc, gt, qT, kcat, vT = _inproj(
            x, pos_f, g_mix[l], w_in[l], b_gates[l], conv_qk[l], g_q_a[l], w_q_b[l], g_kv_a[l], w_kv_b[l])
        hf, hb = _mlstm(q, kT, v, gc, gt)
        ya = _mla(qT, kcat, vT, g_head_mla[l])
        kmem, vmem = _memkv(mem, g_mem_kv[l], w_mem_k[l], w_mem_v[l])
        x2, hn, aff = _post(x, hf, hb, om, ya, kmem, vmem, g_head_mlstm[l], w_out[l], g_mem_x[l],
                            w_mem_q[l], w_mem_o[l], g_ffn[l], w_router[l])
        idx = _topk(aff, cap)
        idx3 = idx.reshape(B * N_EXPERTS, 1, cap)
        xe, gs = _gather(idx3, hn, aff, cap)
        ye = _ffn(xe, gs, w_exp_gate[l], w_exp_up[l], w_exp_down[l])
        moe = _scatter(idx3, ye, S)
        x = x2
    return _final(x, moe, g_final)
```

```python
import functools
import math

import jax
import jax.numpy as jnp
from jax import lax
from jax.experimental import pallas as pl
from jax.experimental.pallas import tpu as pltpu

F32 = jnp.float32
BF16 = jnp.bfloat16
I32 = jnp.int32

EPS = 1e-6
N_MLSTM_HEADS = 4
MLSTM_HEAD_DIM = 128
D_MLSTM = N_MLSTM_HEADS * MLSTM_HEAD_DIM
N_MLA_HEADS = 8
MLA_NOPE_DIM = 64
MLA_ROPE_DIM = 32
MLA_QK_DIM = MLA_NOPE_DIM + MLA_ROPE_DIM
MLA_V_DIM = 64
ROPE_THETA = 10000.0
N_GATE = 4 * N_MLSTM_HEADS
N_MEM_HEADS = 4
N_EXPERTS = 16
EC_CAPACITY_FACTOR = 2

LANES = 128
SUBLANES = 8
VMEM_LIMIT_BYTES = 56 * 1024 * 1024

MLSTM_CHUNK = 128
TM_INPROJ = 256
TQ_MLA = 256
MLA_CHUNKS = 8
TM_POST = 256
TF_FFN = 512
TM_FINAL = 512
CUM_BLOCK = 256


def _cparams(sem):
    return pltpu.CompilerParams(dimension_semantics=sem, vmem_limit_bytes=VMEM_LIMIT_BYTES)


def _dot(a, b):
    return jnp.dot(a, b, preferred_element_type=F32)


def _dot_nt(a, b):
    return lax.dot_general(a, b, (((1,), (1,)), ((), ())), preferred_element_type=F32)


def _rms(x, g):
    return x * lax.rsqrt(jnp.mean(x * x, axis=-1, keepdims=True) + EPS) * g


def _sigmoid(x):
    return 1.0 / (1.0 + jnp.exp(-x))


def _full(shape):
    return pl.BlockSpec(shape, lambda *_: (0,) * len(shape))


def _inproj_kernel(x_ref, xp_ref, xn_ref, pos_ref, posr_ref, gmix_ref, wqk_ref, wvo_ref, wmisc_ref, bg_ref,
                   conv_ref, gqa_ref, wq_ref, gkva_ref, wk_ref, wv_ref, invf_ref, invfc_ref,
                   q_ref, kT_ref, v_ref, o_ref, gc_ref, gt_ref, qT_ref, kcat_ref, vT_ref):
    tm = x_ref.shape[1]
    i = pl.program_id(1)
    last = pl.num_programs(1) - 1
    g = gmix_ref[...]
    xm = _rms(x_ref[0], g)
    xprev = _rms(xp_ref[0], g) * (i > 0).astype(F32)
    xnext = _rms(xn_ref[0], g) * (i < last).astype(F32)
    xm_bf = xm.astype(BF16)
    lhs = jnp.concatenate([xprev.astype(BF16), xm_bf, xnext.astype(BF16)], axis=0)

    pqk = _dot(lhs, wqk_ref[...])
    rows = tm + 2 * SUBLANES
    up = pltpu.roll(pqk, 1, 0)[SUBLANES:SUBLANES + tm]
    dn = pltpu.roll(pqk, rows - 1, 0)[SUBLANES:SUBLANES + tm]
    mid = pqk[SUBLANES:SUBLANES + tm]
    cw = conv_ref[...]
    conv = up * cw[0:1] + mid * cw[1:2] + dn * cw[2:3]
    act = conv * _sigmoid(conv)
    q_ref[0] = act[:, :D_MLSTM].astype(BF16)
    k = act[:, D_MLSTM:] * (MLSTM_HEAD_DIM ** -0.5)
    kT_ref[0] = k.T.astype(BF16)

    vo = _dot(xm_bf, wvo_ref[...])
    v_ref[0] = vo[:, :D_MLSTM].astype(BF16)
    o_ref[0] = vo[:, D_MLSTM:]

    misc = _dot(xm_bf, wmisc_ref[...])
    cq = misc[:, :256]
    ckv = misc[:, 256:384]
    kr_raw = misc[:, 384:512]
    gp = misc[:, 512:640] + bg_ref[...]

    lane = lax.broadcasted_iota(I32, (tm, LANES), 1)
    is_f = ((lane >= 4) & (lane < 8)) | ((lane >= 12) & (lane < 16))
    logsig = jnp.minimum(gp, 0.0) - jnp.log1p(jnp.exp(-jnp.abs(gp)))
    lf = jnp.where(is_f, logsig, gp)
    rc = lax.broadcasted_iota(I32, (tm, LANES), 0) & (MLSTM_CHUNK - 1)
    pre = lf
    suf = lf
    step = 1
    while step < MLSTM_CHUNK:
        pre = pre + jnp.where(rc >= step, pltpu.roll(pre, step, 0), 0.0)
        suf = suf + jnp.where(rc + step < MLSTM_CHUNK, pltpu.roll(suf, tm - step, 0), 0.0)
        step *= 2
    cum = jnp.where(lane < 8, pre, suf)
    a = lf - pltpu.roll(cum, LANES - 4, 1)
    gc = jnp.where(is_f, cum, a)
    pmax = gc
    smax = gc
    step = 1
    while step < MLSTM_CHUNK:
        pmax = jnp.maximum(pmax, jnp.where(rc >= step, pltpu.roll(pmax, step, 0), -jnp.inf))
        smax = jnp.maximum(smax, jnp.where(rc + step < MLSTM_CHUNK, pltpu.roll(smax, tm - step, 0), -jnp.inf))
        step *= 2
    run_max = pltpu.roll(jnp.where(lane < 8, pmax, smax), N_GATE, 1)
    gc_ref[0] = jnp.where(lane < N_GATE, gc, run_max)
    gt_ref[0] = gc.T[:N_GATE]

    ang = pos_ref[0] * invf_ref[...]
    cos = jnp.cos(ang)
    sin = jnp.sin(ang)
    first_half = lane < (MLA_NOPE_DIM + MLA_ROPE_DIM // 2)
    sin_signed = jnp.where(first_half, -sin, sin)

    def rope(t):
        partner = jnp.where(first_half, pltpu.roll(t, LANES - MLA_ROPE_DIM // 2, 1),
                            pltpu.roll(t, MLA_ROPE_DIM // 2, 1))
        return t * cos + partner * sin_signed

    ckvn = _rms(ckv, gkva_ref[...]).astype(BF16)
    kn = _dot(ckvn, wk_ref[...])
    kr = rope(kr_raw)
    for h in range(N_MLA_HEADS):
        sl = slice(h * LANES, (h + 1) * LANES)
        kcat_ref[0, :, sl] = (kn[:, sl] + kr).astype(BF16)

    cqn = _rms(cq, gqa_ref[...]).astype(BF16)
    qT = _dot_nt(wq_ref[...], cqn)
    ang_t = invfc_ref[...] * posr_ref[0]
    cos_t = jnp.cos(ang_t)
    sin_t = jnp.sin(ang_t)
    r0, r1, r2 = MLA_NOPE_DIM, MLA_NOPE_DIM + MLA_ROPE_DIM // 2, MLA_QK_DIM
    for h in range(N_MLA_HEADS):
        blk = qT[h * LANES:(h + 1) * LANES]
        t1 = blk[r0:r1]
        t2 = blk[r1:r2]
        roped = jnp.concatenate([blk[:r0], t1 * cos_t - t2 * sin_t, t2 * cos_t + t1 * sin_t, blk[r2:]], axis=0)
        qT_ref[0, h * LANES:(h + 1) * LANES, :] = roped.astype(BF16)
    vT = _dot_nt(wv_ref[...], ckvn)
    rid = lax.broadcasted_iota(I32, vT.shape, 0) & (LANES - 1)
    vT_ref[0] = jnp.where(rid == MLA_V_DIM, 1.0, vT).astype(BF16)


def _inproj(x, pos_f, g_mix, w_in, b_gates, conv_qk, g_q_a, w_q_b, g_kv_a, w_kv_b):
    B, S, D = x.shape
    tm = TM_INPROJ
    nt = S // tm
    hb = tm // SUBLANES
    off = [0, 512, 1024, 1536, 2048, 2064, 2320, 2448, 2480]
    wqk = w_in[:, off[0]:off[2]].astype(BF16)
    wvo = w_in[:, off[2]:off[4]].astype(BF16)
    w_gate = w_in[:, off[4]:off[5]]
    w_cq = w_in[:, off[5]:off[6]]
    w_ckv = w_in[:, off[6]:off[7]]
    w_kr = w_in[:, off[7]:off[8]]
    kr_blk = jnp.zeros((D, LANES), F32).at[:, MLA_NOPE_DIM:MLA_QK_DIM].set(w_kr)
    gate_blk = jnp.zeros((D, LANES), F32).at[:, :N_GATE].set(w_gate)
    wmisc = jnp.concatenate([w_cq, w_ckv, kr_blk, gate_blk], axis=1).astype(BF16)
    bg = jnp.zeros((1, LANES), F32).at[0, :N_GATE].set(b_gates)
    pad_heads = lambda w3: jnp.pad(w3, ((0, 0), (0, 0), (0, LANES - w3.shape[2]))).reshape(w3.shape[0], -1)
    wq = pad_heads(w_q_b.reshape(-1, N_MLA_HEADS, MLA_QK_DIM)).T.astype(BF16)
    wkv3 = w_kv_b.reshape(-1, N_MLA_HEADS, MLA_NOPE_DIM + MLA_V_DIM)
    wk = pad_heads(wkv3[:, :, :MLA_NOPE_DIM]).astype(BF16)
    wv = pad_heads(wkv3[:, :, MLA_NOPE_DIM:]).T.astype(BF16)
    inv_freq = ROPE_THETA ** (-jnp.arange(0, MLA_ROPE_DIM, 2, dtype=F32) / MLA_ROPE_DIM)
    invf = jnp.zeros((1, LANES), F32).at[0, MLA_NOPE_DIM:MLA_QK_DIM].set(jnp.concatenate([inv_freq, inv_freq]))
    invf_col = inv_freq.reshape(-1, 1)
    pos_row = pos_f.reshape(B, 1, S)

    row = lambda w: pl.BlockSpec((1, tm, w), lambda b, i: (b, i, 0))
    in_specs = [
        row(D),
        pl.BlockSpec((1, SUBLANES, D), lambda b, i: (b, jnp.maximum(i * hb - 1, 0), 0)),
        pl.BlockSpec((1, SUBLANES, D), lambda b, i: (b, jnp.minimum((i + 1) * hb, S // SUBLANES - 1), 0)),
        row(1),
        pl.BlockSpec((1, 1, tm), lambda b, i: (b, 0, i)),
        _full((1, D)), _full(wqk.shape), _full(wvo.shape), _full(wmisc.shape), _full((1, LANES)),
        _full(conv_qk.shape), _full((1, 256)), _full(wq.shape), _full((1, 128)), _full(wk.shape),
        _full(wv.shape), _full((1, LANES)), _full(invf_col.shape),
    ]
    out_shape = [
        jax.ShapeDtypeStruct((B, S, D_MLSTM), BF16),
        jax.ShapeDtypeStruct((B, D_MLSTM, S), BF16),
        jax.ShapeDtypeStruct((B, S, D_MLSTM), BF16),
        jax.ShapeDtypeStruct((B, S, D_MLSTM), F32),
        jax.ShapeDtypeStruct((B, S, LANES), F32),
        jax.ShapeDtypeStruct((B, N_GATE, S), F32),
        jax.ShapeDtypeStruct((B, N_MLA_HEADS * LANES, S), BF16),
        jax.ShapeDtypeStruct((B, S, N_MLA_HEADS * LANES), BF16),
        jax.ShapeDtypeStruct((B, N_MLA_HEADS * LANES, S), BF16),
    ]
    out_specs = [
        row(D_MLSTM),
        pl.BlockSpec((1, D_MLSTM, tm), lambda b, i: (b, 0, i)),
        row(D_MLSTM), row(D_MLSTM), row(LANES),
        pl.BlockSpec((1, N_GATE, tm), lambda b, i: (b, 0, i)),
        pl.BlockSpec((1, N_MLA_HEADS * LANES, tm), lambda b, i: (b, 0, i)),
        row(N_MLA_HEADS * LANES),
        pl.BlockSpec((1, N_MLA_HEADS * LANES, tm), lambda b, i: (b, 0, i)),
    ]
    return pl.pallas_call(
        _inproj_kernel, grid=(B, nt), in_specs=in_specs, out_specs=out_specs, out_shape=out_shape,
        compiler_params=_cparams(("parallel", "parallel")), name="inproj",
    )(x, x, x, pos_f, pos_row, g_mix.reshape(1, D), wqk, wvo, wmisc, bg, conv_qk, g_q_a.reshape(1, -1), wq,
      g_kv_a.reshape(1, -1), wk, wv, invf, invf_col)


def _mlstm_kernel(qf_ref, kf_ref, vf_ref, gcf_ref, gtf_ref, qb_ref, kb_ref, vb_ref, gcb_ref, gtb_ref,
                  hf_ref, hb_ref, cn_ref, m_ref):
    T = MLSTM_CHUNK
    hd = MLSTM_HEAD_DIM

    @pl.when(pl.program_id(1) == 0)
    def _():
        cn_ref[...] = jnp.zeros_like(cn_ref)
        m_ref[...] = jnp.zeros_like(m_ref)

    t_idx = lax.broadcasted_iota(I32, (T, T), 0)
    s_idx = lax.broadcasted_iota(I32, (T, T), 1)
    ones_col = (lax.broadcasted_iota(I32, (T, hd), 1) == 0).astype(BF16)
    neg_inf = F32(-jnp.inf)

    streams = []
    for d, (q_ref, kT_ref, v_ref, gc_ref, gt_ref, h_ref) in enumerate(
            ((qf_ref, kf_ref, vf_ref, gcf_ref, gtf_ref, hf_ref),
             (qb_ref, kb_ref, vb_ref, gcb_ref, gtb_ref, hb_ref))):
        end = T - 1 if d == 0 else 0
        gc = gc_ref[0]
        gt = gt_ref[0]
        for h in range(N_MLSTM_HEADS):
            st = d * N_MLSTM_HEADS + h
            la = d * 8 + h
            b_col = gc[:, la + 4:la + 5]
            a_row = gt[la:la + 1, :]
            m_prev = m_ref[st][0:1, 0:1]
            qh = q_ref[0, :, h * hd:(h + 1) * hd]
            kTh = kT_ref[0, h * hd:(h + 1) * hd, :]
            v_ext = jnp.concatenate([v_ref[0, :, h * hd:(h + 1) * hd], ones_col], axis=1)
            cn = cn_ref[st]
            m_run = jnp.maximum(gc[:, N_GATE + la:N_GATE + la + 1], m_prev)
            m_end = m_run[end:end + 1]
            kw = (kTh.astype(F32) * jnp.exp(a_row - m_end)).astype(BF16)
            streams.append(dict(
                st=st, d=d, h=h, h_ref=h_ref, a_row=a_row, b_col=b_col, m_prev=m_prev, m_run=m_run,
                m_end=m_end, g_sum=b_col[end:end + 1], cn=cn, v_ext=v_ext,
                s=_dot(qh, kTh),
                qcn=_dot(qh, cn.astype(BF16)),
                upd=_dot(kw, v_ext)))
    for x in streams:
        mask = (s_idx <= t_idx) if x["d"] == 0 else (s_idx >= t_idx)
        e = jnp.exp(jnp.where(mask, x["a_row"] - x["m_run"], neg_inf))
        x["pw"] = (e * x["s"]).astype(BF16)
    for x in streams:
        x["pv"] = _dot(x["pw"], x["v_ext"])
    for x in streams:
        st, h = x["st"], x["h"]
        iw = jnp.exp(x["m_prev"] - x["m_run"])
        num = iw * x["qcn"][:, :hd] + x["pv"][:, :hd]
        den = iw * x["qcn"][:, hd:hd + 1] + x["pv"][:, hd:hd + 1]
        floor = jnp.exp(-(x["b_col"] + x["m_run"]))
        x["h_ref"][0, :, h * hd:(h + 1) * hd] = num / jnp.maximum(jnp.abs(den), floor)
        cn_ref[st] = jnp.exp(x["m_prev"] - x["m_end"]) * x["cn"] + x["upd"]
        m_ref[st] = jnp.broadcast_to(x["g_sum"] + x["m_end"], (SUBLANES, LANES))


def _mlstm(q, kT, v, gc, gt):
    B, S, _ = q.shape
    T = MLSTM_CHUNK
    nc = S // T
    fwd = lambda b, j: (b, j, 0)
    bwd = lambda b, j: (b, nc - 1 - j, 0)
    fwd_t = lambda b, j: (b, 0, j)
    bwd_t = lambda b, j: (b, 0, nc - 1 - j)

    def specs(row_map, col_map):
        return [pl.BlockSpec((1, T, D_MLSTM), row_map), pl.BlockSpec((1, D_MLSTM, T), col_map),
                pl.BlockSpec((1, T, D_MLSTM), row_map), pl.BlockSpec((1, T, LANES), row_map),
                pl.BlockSpec((1, N_GATE, T), col_map)]

    n_state = 2 * N_MLSTM_HEADS
    return pl.pallas_call(
        _mlstm_kernel, grid=(B, nc),
        in_specs=specs(fwd, fwd_t) + specs(bwd, bwd_t),
        out_specs=[pl.BlockSpec((1, T, D_MLSTM), fwd), pl.BlockSpec((1, T, D_MLSTM), bwd)],
        out_shape=[jax.ShapeDtypeStruct((B, S, D_MLSTM), F32)] * 2,
        scratch_shapes=[pltpu.VMEM((n_state, MLSTM_HEAD_DIM, 2 * MLSTM_HEAD_DIM), F32),
                        pltpu.VMEM((n_state, SUBLANES, LANES), F32)],
        compiler_params=_cparams(("parallel", "arbitrary")), name="mlstm",
    )(q, kT, v, gc, gt, q, kT, v, gc, gt)


def _mla_kernel(qT_ref, k_ref, vT_ref, g_ref, o_ref, sa_sc, sb_sc, ma_sc, mb_sc):
    t = pl.program_id(0)
    c = (MLA_QK_DIM ** -0.5) * math.log2(math.e)
    S = k_ref.shape[1]
    tq = qT_ref.shape[2]
    ck = S // MLA_CHUNKS

    @pl.when(t == 0)
    def _():
        sb_sc[...] = jnp.zeros(sb_sc.shape, F32)
        mb_sc[...] = jnp.zeros(mb_sc.shape, F32)

    def step(s_cur, m_cur, s_prev, m_prev_ref):
        m_prev = [m_prev_ref[h][0:1] for h in range(2)]
        m_run = [jnp.full((1, tq), -jnp.inf, F32) for _ in range(2)]
        accs = [None, None]
        for j in range(MLA_CHUNKS):
            rows = slice(j * ck, (j + 1) * ck)
            for h in range(2):
                s = _dot(k_ref[0, rows, h * LANES:(h + 1) * LANES], qT_ref[0, h * LANES:(h + 1) * LANES, :])
                m_run[h] = jnp.maximum(m_run[h], jnp.max(s, axis=0, keepdims=True))
                s_cur[h, rows, :] = s
            for h in range(2):
                p = jnp.exp2((s_prev[h, rows, :] - m_prev[h]) * c).astype(BF16)
                part = _dot(vT_ref[0, h * LANES:(h + 1) * LANES, rows], p)
                accs[h] = part if j == 0 else accs[h] + part
        for h in range(2):
            m_cur[h] = jnp.broadcast_to(m_run[h], (SUBLANES, tq))
        ys = []
        for h in range(2):
            acc = accs[h]
            o = acc[:MLA_V_DIM] / acc[MLA_V_DIM:MLA_V_DIM + 1]
            ms = jnp.mean(o * o, axis=0, keepdims=True)
            ys.append(o * lax.rsqrt(ms + EPS))
        o_ref[0] = jnp.concatenate(ys, axis=0).T * g_ref[...]

    @pl.when(t % 2 == 0)
    def _():
        step(sa_sc, ma_sc, sb_sc, mb_sc)

    @pl.when(t % 2 == 1)
    def _():
        step(sb_sc, mb_sc, sa_sc, ma_sc)


def _mla(qT, kcat, vT, g_head_mla):
    B, S, _ = kcat.shape
    tq = TQ_MLA
    npair = N_MLA_HEADS // 2
    nq = S // tq
    nt = B * npair * nq

    def tile(t):
        return t // (npair * nq), (t // nq) % npair, t % nq

    def a_map(f):
        return lambda t: f(*tile(jnp.minimum(t, nt - 1)))

    def b_map(f):
        return lambda t: f(*tile(jnp.maximum(t - 1, 0)))

    return pl.pallas_call(
        _mla_kernel, grid=(nt + 1,),
        in_specs=[pl.BlockSpec((1, 2 * LANES, tq), a_map(lambda b, p, i: (b, p, i))),
                  pl.BlockSpec((1, S, 2 * LANES), a_map(lambda b, p, i: (b, 0, p))),
                  pl.BlockSpec((1, 2 * LANES, S), b_map(lambda b, p, i: (b, p, 0))),
                  pl.BlockSpec((1, LANES), b_map(lambda b, p, i: (0, p)))],
        out_specs=pl.BlockSpec((1, tq, LANES), b_map(lambda b, p, i: (b, i, p))),
        out_shape=jax.ShapeDtypeStruct((B, S, N_MLA_HEADS * MLA_V_DIM), F32),
        scratch_shapes=[pltpu.VMEM((2, S, tq), F32), pltpu.VMEM((2, S, tq), F32),
                        pltpu.VMEM((2, SUBLANES, tq), F32), pltpu.VMEM((2, SUBLANES, tq), F32)],
        compiler_params=_cparams(("arbitrary",)), name="mla_attn",
    )(qT, kcat, vT, g_head_mla.reshape(1, -1))


def _memkv_kernel(mem_ref, g_ref, wk_ref, wv_ref, k_ref, v_ref):
    mn = _rms(mem_ref[0], g_ref[...]).astype(BF16)
    k_ref[0] = _dot(mn, wk_ref[...]).astype(BF16)
    v_ref[0] = _dot(mn, wv_ref[...]).astype(BF16)


def _memkv(mem, g_mem_kv, w_k, w_v):
    B, M, D = mem.shape
    blk = pl.BlockSpec((1, M, D), lambda b: (b, 0, 0))
    return pl.pallas_call(
        _memkv_kernel, grid=(B,),
        in_specs=[blk, _full((1, D)), _full(w_k.shape), _full(w_v.shape)],
        out_specs=[blk, blk], out_shape=[jax.ShapeDtypeStruct((B, M, D), BF16)] * 2,
        compiler_params=_cparams(("parallel",)), name="memkv",
    )(mem, g_mem_kv.reshape(1, D), w_k.astype(BF16), w_v.astype(BF16))


def _post_kernel(x_ref, hf_ref, hb_ref, om_ref, ya_ref, km_ref, vm_ref, ghm_ref, wout_ref, gmx_ref, wmq_ref,
                 wmo_ref, gffn_ref, wrh_ref, wrl_ref, x2_ref, hn_ref, aff_ref, afft_ref):
    tm = x_ref.shape[1]
    D = x_ref.shape[2]
    hm = hf_ref[0] + hb_ref[0]
    gate = _sigmoid(om_ref[0])
    ghm = ghm_ref[...]
    parts = []
    for h in range(N_MLSTM_HEADS):
        sl = slice(h * MLSTM_HEAD_DIM, (h + 1) * MLSTM_HEAD_DIM)
        parts.append(gate[:, sl] * _rms(hm[:, sl], ghm[:, sl]))
    ycat = jnp.concatenate(parts + [ya_ref[0]], axis=1).astype(BF16)
    x1 = x_ref[0] + _dot(ycat, wout_ref[...])

    dm = D // N_MEM_HEADS
    qm = _dot(_rms(x1, gmx_ref[...]).astype(BF16), wmq_ref[...]).astype(BF16)
    km = km_ref[0]
    vm = vm_ref[0]
    outs = []
    for h in range(N_MEM_HEADS):
        sl = slice(h * dm, (h + 1) * dm)
        s = _dot_nt(qm[:, sl], km[:, sl]) * (dm ** -0.5)
        e = jnp.exp(s - jnp.max(s, axis=1, keepdims=True))
        p = e / jnp.sum(e, axis=1, keepdims=True)
        outs.append(_dot(p.astype(BF16), vm[:, sl]))
    om = jnp.concatenate(outs, axis=1).astype(BF16)
    x2 = x1 + _dot(om, wmo_ref[...])
    x2_ref[0] = x2

    hn = _rms(x2, gffn_ref[...])
    hn_ref[0] = hn

    hi = hn.astype(BF16)
    lo = (hn - hi.astype(F32)).astype(BF16)
    wrh = wrh_ref[...]
    logits = _dot(hi, wrh) + (_dot(hi, wrl_ref[...]) + _dot(lo, wrh))
    lane = lax.broadcasted_iota(I32, (tm, LANES), 1)
    valid = lane < N_EXPERTS
    logits = jnp.where(valid, logits, -jnp.inf)
    e = jnp.exp(logits - jnp.max(logits, axis=1, keepdims=True))
    aff = e / jnp.sum(e, axis=1, keepdims=True)
    aff_ref[0] = aff
    afft_ref[0] = aff.T[:N_EXPERTS]


def _post(x, hf, hb, om, ya, kmem, vmem, g_head_mlstm, w_out, g_mem_x, w_mem_q, w_mem_o, g_ffn, w_router):
    B, S, D = x.shape
    tm = TM_POST
    M = kmem.shape[1]
    wr = jnp.zeros((D, LANES), F32).at[:, :N_EXPERTS].set(w_router)
    wrh = wr.astype(BF16)
    wrl = (wr - wrh.astype(F32)).astype(BF16)
    row = lambda w: pl.BlockSpec((1, tm, w), lambda b, i: (b, i, 0))
    mem_spec = pl.BlockSpec((1, M, D), lambda b, i: (b, 0, 0))
    return pl.pallas_call(
        _post_kernel, grid=(B, S // tm),
        in_specs=[row(D), row(D_MLSTM), row(D_MLSTM), row(D_MLSTM), row(D_MLSTM), mem_spec, mem_spec,
                  _full((1, D_MLSTM)), _full((D, D)), _full((1, D)), _full((D, D)), _full((D, D)),
                  _full((1, D)), _full((D, LANES)), _full((D, LANES))],
        out_specs=[row(D), row(D), row(LANES), pl.BlockSpec((1, N_EXPERTS, tm), lambda b, i: (b, 0, i))],
        out_shape=[jax.ShapeDtypeStruct((B, S, D), F32), jax.ShapeDtypeStruct((B, S, D), F32),
                   jax.ShapeDtypeStruct((B, S, LANES), F32), jax.ShapeDtypeStruct((B, N_EXPERTS, S), F32)],
        compiler_params=_cparams(("parallel", "parallel")), name="post_mixer",
    )(x, hf, hb, om, ya, kmem, vmem, g_head_mlstm.reshape(1, -1), w_out.astype(BF16), g_mem_x.reshape(1, D),
      w_mem_q.astype(BF16), w_mem_o.astype(BF16), g_ffn.reshape(1, D), wrh, wrl)


def _topk_kernel(aff_ref, afft_ref, idx_ref, idx_sc):
    S = aff_ref.shape[1]
    cap = idx_ref.shape[2]
    nblk = S // CUM_BLOCK

    def open_interval(c):
        lo, hi = c
        mid = 0.5 * (lo + hi)
        return jnp.max(jnp.where((mid > lo) & (mid < hi), 1.0, 0.0)) > 0.0

    def bisect(c):
        lo, hi = c
        mid = 0.5 * (lo + hi)
        cnt = jnp.sum((afft_ref[0] >= mid).astype(F32), axis=1, keepdims=True)
        ge = cnt >= cap
        return jnp.where(ge, mid, lo), jnp.where(ge, hi, mid)

    lo_c, hi_c = lax.while_loop(open_interval, bisect,
                                (jnp.zeros((N_EXPERTS, 1), F32), jnp.full((N_EXPERTS, 1), 2.0, F32)))
    need_c = cap - jnp.sum((afft_ref[0] >= hi_c).astype(F32), axis=1, keepdims=True)
    eye = (lax.broadcasted_iota(I32, (N_EXPERTS, LANES), 0) == lax.broadcasted_iota(I32, (N_EXPERTS, LANES), 1))
    to_row = lambda col: jnp.sum(jnp.where(eye, col, 0.0), axis=0, keepdims=True)
    lo, hi, need = to_row(lo_c), to_row(hi_c), to_row(need_c)

    r = lax.broadcasted_iota(I32, (CUM_BLOCK, CUM_BLOCK), 0)
    c = lax.broadcasted_iota(I32, (CUM_BLOCK, CUM_BLOCK), 1)
    tri = (c <= r).astype(BF16)

    def cum_body(blk, carry):
        ce, cs = carry
        rows = pl.ds(pl.multiple_of(blk * CUM_BLOCK, CUM_BLOCK), CUM_BLOCK)
        a = aff_ref[0, rows, :]
        sure = a >= hi
        tie = (a >= lo) & jnp.logical_not(sure)
        eq = tie.astype(F32)
        eq_incl = _dot(tri, eq.astype(BF16)) + ce
        sel = jnp.where(sure | (tie & (eq_incl - eq < need)), 1.0, 0.0)
        local_incl = _dot(tri, sel.astype(BF16))
        rank_local = jnp.where(sel > 0.0, local_incl - sel, -2.0).astype(BF16)
        base = (blk * CUM_BLOCK).astype(F32)
        for e in range(N_EXPERTS):
            slot_local = jnp.clip(j_row - cs[:, e:e + 1], -1.0, CUM_BLOCK + 1.0).astype(BF16)
            onehot = jnp.where(rank_local[:, e:e + 1] == slot_local, one_bf, zero_bf)
            hit = _dot(tok_rows, onehot)
            idx_sc[e:e + 1, :] += hit[0:1] + base * hit[1:2]
        return eq_incl[CUM_BLOCK - 1:CUM_BLOCK], cs + local_incl[CUM_BLOCK - 1:CUM_BLOCK]

    j_row = lax.broadcasted_iota(I32, (1, cap), 1).astype(F32)
    sub = lax.broadcasted_iota(I32, (SUBLANES, CUM_BLOCK), 0)
    tok = lax.broadcasted_iota(I32, (SUBLANES, CUM_BLOCK), 1).astype(F32)
    tok_rows = jnp.where(sub == 0, tok, jnp.where(sub == 1, 1.0, 0.0)).astype(BF16)
    one_bf = jnp.ones((CUM_BLOCK, cap), BF16)
    zero_bf = jnp.zeros((CUM_BLOCK, cap), BF16)
    idx_sc[...] = jnp.zeros_like(idx_sc)
    zero = jnp.zeros((1, LANES), F32)
    lax.fori_loop(0, nblk, cum_body, (zero, zero))
    idx_ref[0] = idx_sc[...].astype(I32)


def _topk(aff, aff_t, cap):
    B, S, _ = aff.shape
    return pl.pallas_call(
        _topk_kernel, grid=(B,),
        in_specs=[pl.BlockSpec((1, S, LANES), lambda b: (b, 0, 0)),
                  pl.BlockSpec((1, N_EXPERTS, S), lambda b: (b, 0, 0))],
        out_specs=pl.BlockSpec((1, N_EXPERTS, cap), lambda b: (b, 0, 0)),
        out_shape=jax.ShapeDtypeStruct((B, N_EXPERTS, cap), I32),
        scratch_shapes=[pltpu.VMEM((N_EXPERTS, cap), F32)],
        compiler_params=_cparams(("parallel",)), name="topk",
    )(aff, aff_t)


def _gather_kernel(idx_ref, hn_ref, aff_ref, xe_ref, gs_ref, rows_sc):
    cap = xe_ref.shape[2]

    def body(j, _):
        i = idx_ref[0, 0, j]
        rows_sc[pl.ds(j, 1), :] = hn_ref[0, pl.ds(i, 1), :]
        gs_ref[0, 0, pl.ds(j, 1), :] = aff_ref[0, pl.ds(i, 1), :]
        return 0

    lax.fori_loop(0, cap, body, 0, unroll=8)
    xe_ref[0, 0] = rows_sc[...].astype(BF16)


def _gather(idx3, hn, aff, cap):
    B, S, D = hn.shape
    E = N_EXPERTS
    return pl.pallas_call(
        _gather_kernel, grid=(B, E),
        in_specs=[pl.BlockSpec((1, 1, cap), lambda b, e: (b * E + e, 0, 0), memory_space=pltpu.SMEM),
                  pl.BlockSpec((1, S, D), lambda b, e: (b, 0, 0)),
                  pl.BlockSpec((1, S, LANES), lambda b, e: (b, 0, 0))],
        out_specs=[pl.BlockSpec((1, 1, cap, D), lambda b, e: (b, e, 0, 0)),
                   pl.BlockSpec((1, 1, cap, LANES), lambda b, e: (b, e, 0, 0))],
        out_shape=[jax.ShapeDtypeStruct((B, E, cap, D), BF16), jax.ShapeDtypeStruct((B, E, cap, LANES), F32)],
        scratch_shapes=[pltpu.VMEM((cap, D), F32)],
        compiler_params=_cparams(("parallel", "arbitrary")), name="moe_gather",
    )(idx3, hn, aff)


def _ffn_kernel(xe_ref, gs_ref, wg_ref, wu_ref, wd_ref, ye_ref, wg_sc, wu_sc, wd_sc):
    e = pl.program_id(0)
    f = pl.program_id(1)
    nb = xe_ref.shape[0]
    cap = xe_ref.shape[2]

    @pl.when(f == 0)
    def _():
        ye_ref[...] = jnp.zeros_like(ye_ref)

    wg_sc[...] = wg_ref[0].astype(BF16)
    wu_sc[...] = wu_ref[0].astype(BF16)
    wd_sc[...] = wd_ref[0].astype(BF16)
    lane = lax.broadcasted_iota(I32, (cap, LANES), 1)

    def up(b):
        xb = xe_ref[b, 0]
        return _dot(xb, wg_sc[...]), _dot(xb, wu_sc[...])

    def down(b, h1, h2):
        hid = (h1 * _sigmoid(h1) * h2).astype(BF16)
        gate = jnp.sum(jnp.where(lane == e, gs_ref[b, 0], 0.0), axis=1, keepdims=True)
        ye_ref[b, 0] += _dot(hid, wd_sc[...]) * gate

    hs = up(0)
    for b in range(nb):
        nxt = up(b + 1) if b + 1 < nb else None
        down(b, *hs)
        hs = nxt


def _ffn(xe, gs, w_gate, w_up, w_down):
    B, E, cap, D = xe.shape
    F = w_gate.shape[2]
    tf = TF_FFN
    return pl.pallas_call(
        _ffn_kernel, grid=(E, F // tf),
        in_specs=[pl.BlockSpec((B, 1, cap, D), lambda e, f: (0, e, 0, 0)),
                  pl.BlockSpec((B, 1, cap, LANES), lambda e, f: (0, e, 0, 0)),
                  pl.BlockSpec((1, D, tf), lambda e, f: (e, 0, f)),
                  pl.BlockSpec((1, D, tf), lambda e, f: (e, 0, f)),
                  pl.BlockSpec((1, tf, D), lambda e, f: (e, f, 0))],
        out_specs=pl.BlockSpec((B, 1, cap, D), lambda e, f: (0, e, 0, 0)),
        out_shape=jax.ShapeDtypeStruct((B, E, cap, D), F32),
        scratch_shapes=[pltpu.VMEM((D, tf), BF16), pltpu.VMEM((D, tf), BF16), pltpu.VMEM((tf, D), BF16)],
        compiler_params=_cparams(("parallel", "arbitrary")), name="moe_ffn",
    )(xe, gs, w_gate, w_up, w_down)


def _scatter_kernel(idx_ref, ye_ref, out_ref):
    cap = ye_ref.shape[2]

    @pl.when(pl.program_id(1) == 0)
    def _():
        out_ref[...] = jnp.zeros_like(out_ref)

    def body(j, _):
        i = idx_ref[0, 0, j]
        out_ref[0, pl.ds(i, 1), :] += ye_ref[0, 0, pl.ds(j, 1), :]
        return 0

    lax.fori_loop(0, cap, body, 0, unroll=8)


def _scatter(idx3, ye, S):
    B, E, cap, D = ye.shape
    return pl.pallas_call(
        _scatter_kernel, grid=(B, E),
        in_specs=[pl.BlockSpec((1, 1, cap), lambda b, e: (b * E + e, 0, 0), memory_space=pltpu.SMEM),
                  pl.BlockSpec((1, 1, cap, D), lambda b, e: (b, e, 0, 0))],
        out_specs=pl.BlockSpec((1, S, D), lambda b, e: (b, 0, 0)),
        out_shape=jax.ShapeDtypeStruct((B, S, D), F32),
        compiler_params=_cparams(("parallel", "arbitrary")), name="moe_scatter",
    )(idx3, ye)


def _final_kernel(x_ref, y_ref, g_ref, o_ref):
    o_ref[0] = _rms(x_ref[0] + y_ref[0], g_ref[...])


def _final(x2, moe, g_final):
    B, S, D = x2.shape
    row = pl.BlockSpec((1, TM_FINAL, D), lambda b, i: (b, i, 0))
    return pl.pallas_call(
        _final_kernel, grid=(B, S // TM_FINAL), in_specs=[row, row, _full((1, D))], out_specs=row,
        out_shape=jax.ShapeDtypeStruct((B, S, D), F32),
        compiler_params=_cparams(("parallel", "parallel")), name="final_norm",
    )(x2, moe, g_final.reshape(1, D))


def kernel(x, mem, positions, g_mix, w_in, b_gates, conv_qk, g_q_a, w_q_b, g_kv_a, w_kv_b, g_head_mlstm,
           g_head_mla, w_out, g_mem_x, g_mem_kv, w_mem_q, w_mem_k, w_mem_v, w_mem_o, g_ffn, w_router,
           w_exp_gate, w_exp_up, w_exp_down, g_final):
    B, S, D = x.shape
    depth = g_mix.shape[0]
    assert depth == 1, "the MoE residual is folded into the final norm kernel, which assumes one layer"
    cap = EC_CAPACITY_FACTOR * S // N_EXPERTS
    pos_f = positions.astype(F32).reshape(B, S, 1)
    for l in range(depth):
        q, kT, v, om, gc, gt, qT, kcat, vT = _inproj(
            x, pos_f, g_mix[l], w_in[l], b_gates[l], conv_qk[l], g_q_a[l], w_q_b[l], g_kv_a[l], w_kv_b[l])
        hf, hb = _mlstm(q, kT, v, gc, gt)
        ya = _mla(qT, kcat, vT, g_head_mla[l])
        kmem, vmem = _memkv(mem, g_mem_kv[l], w_mem_k[l], w_mem_v[l])
        x2, hn, aff, aff_t = _post(x, hf, hb, om, ya, kmem, vmem, g_head_mlstm[l], w_out[l], g_mem_x[l],
                            w_mem_q[l], w_mem_o[l], g_ffn[l], w_router[l])
        idx = _topk(aff, aff_t, cap)
        idx3 = idx.reshape(B * N_EXPERTS, 1, cap)
        xe, gs = _gather(idx3, hn, aff, cap)
        ye = _ffn(xe, gs, w_exp_gate[l], w_exp_up[l], w_exp_down[l])
        moe = _scatter(idx3, ye, S)
        x = x2
    return _final(x, moe, g_final)
```

```python
import functools
import math

import jax
import jax.numpy as jnp
from jax import lax
from jax.experimental import pallas as pl
from jax.experimental.pallas import tpu as pltpu

F32 = jnp.float32
BF16 = jnp.bfloat16
I32 = jnp.int32

EPS = 1e-6
N_MLSTM_HEADS = 4
MLSTM_HEAD_DIM = 128
D_MLSTM = N_MLSTM_HEADS * MLSTM_HEAD_DIM
N_MLA_HEADS = 8
MLA_NOPE_DIM = 64
MLA_ROPE_DIM = 32
MLA_QK_DIM = MLA_NOPE_DIM + MLA_ROPE_DIM
MLA_V_DIM = 64
ROPE_THETA = 10000.0
N_GATE = 4 * N_MLSTM_HEADS
N_MEM_HEADS = 4
N_EXPERTS = 16
EC_CAPACITY_FACTOR = 2

LANES = 128
SUBLANES = 8
VMEM_LIMIT_BYTES = 56 * 1024 * 1024

MLSTM_CHUNK = 128
TM_INPROJ = 256
TQ_MLA = 256
MLA_CHUNKS = 16
TM_POST = 256
TF_FFN = 512
TM_FINAL = 512
CUM_BLOCK = 256
SCATTER_GROUP = 8


def _cparams(sem):
    return pltpu.CompilerParams(dimension_semantics=sem, vmem_limit_bytes=VMEM_LIMIT_BYTES)


def _dot(a, b):
    return jnp.dot(a, b, preferred_element_type=F32)


def _dot_nt(a, b):
    return lax.dot_general(a, b, (((1,), (1,)), ((), ())), preferred_element_type=F32)


def _rms(x, g):
    return x * lax.rsqrt(jnp.mean(x * x, axis=-1, keepdims=True) + EPS) * g


def _sigmoid(x):
    return 1.0 / (1.0 + jnp.exp(-x))


def _full(shape):
    return pl.BlockSpec(shape, lambda *_: (0,) * len(shape))


def _inproj_kernel(x_ref, xp_ref, xn_ref, pos_ref, posr_ref, gmix_ref, wqk_ref, wvo_ref, wmisc_ref, bg_ref,
                   conv_ref, gqa_ref, wq_ref, gkva_ref, wk_ref, wv_ref, invf_ref, invfc_ref,
                   q_ref, kT_ref, v_ref, o_ref, gc_ref, gt_ref, qT_ref, kcat_ref, vT_ref):
    tm = x_ref.shape[1]
    i = pl.program_id(1)
    last = pl.num_programs(1) - 1
    g = gmix_ref[...]
    xm = _rms(x_ref[0], g)
    xprev = _rms(xp_ref[0], g) * (i > 0).astype(F32)
    xnext = _rms(xn_ref[0], g) * (i < last).astype(F32)
    xm_bf = xm.astype(BF16)
    lhs = jnp.concatenate([xprev.astype(BF16), xm_bf, xnext.astype(BF16)], axis=0)

    pqk = _dot(lhs, wqk_ref[...])
    rows = tm + 2 * SUBLANES
    up = pltpu.roll(pqk, 1, 0)[SUBLANES:SUBLANES + tm]
    dn = pltpu.roll(pqk, rows - 1, 0)[SUBLANES:SUBLANES + tm]
    mid = pqk[SUBLANES:SUBLANES + tm]
    cw = conv_ref[...]
    conv = up * cw[0:1] + mid * cw[1:2] + dn * cw[2:3]
    act = conv * _sigmoid(conv)
    q_ref[0] = act[:, :D_MLSTM].astype(BF16)
    k = act[:, D_MLSTM:] * (MLSTM_HEAD_DIM ** -0.5)
    kT_ref[0] = k.T.astype(BF16)

    vo = _dot(xm_bf, wvo_ref[...])
    v_ref[0] = vo[:, :D_MLSTM].astype(BF16)
    o_ref[0] = vo[:, D_MLSTM:]

    misc = _dot(xm_bf, wmisc_ref[...])
    cq = misc[:, :256]
    ckv = misc[:, 256:384]
    kr_raw = misc[:, 384:512]
    gp = misc[:, 512:640] + bg_ref[...]

    lane = lax.broadcasted_iota(I32, (tm, LANES), 1)
    is_f = ((lane >= 4) & (lane < 8)) | ((lane >= 12) & (lane < 16))
    logsig = jnp.minimum(gp, 0.0) - jnp.log1p(jnp.exp(-jnp.abs(gp)))
    lf = jnp.where(is_f, logsig, gp)
    rc = lax.broadcasted_iota(I32, (tm, LANES), 0) & (MLSTM_CHUNK - 1)
    pre = lf
    suf = lf
    step = 1
    while step < MLSTM_CHUNK:
        pre = pre + jnp.where(rc >= step, pltpu.roll(pre, step, 0), 0.0)
        suf = suf + jnp.where(rc + step < MLSTM_CHUNK, pltpu.roll(suf, tm - step, 0), 0.0)
        step *= 2
    cum = jnp.where(lane < 8, pre, suf)
    a = lf - pltpu.roll(cum, LANES - 4, 1)
    gc = jnp.where(is_f, cum, a)
    pmax = gc
    smax = gc
    step = 1
    while step < MLSTM_CHUNK:
        pmax = jnp.maximum(pmax, jnp.where(rc >= step, pltpu.roll(pmax, step, 0), -jnp.inf))
        smax = jnp.maximum(smax, jnp.where(rc + step < MLSTM_CHUNK, pltpu.roll(smax, tm - step, 0), -jnp.inf))
        step *= 2
    run_max = pltpu.roll(jnp.where(lane < 8, pmax, smax), N_GATE, 1)
    gc_ref[0] = jnp.where(lane < N_GATE, gc, run_max)
    gt_ref[0] = gc.T[:N_GATE]

    ang = pos_ref[0] * invf_ref[...]
    cos = jnp.cos(ang)
    sin = jnp.sin(ang)
    first_half = lane < (MLA_NOPE_DIM + MLA_ROPE_DIM // 2)
    sin_signed = jnp.where(first_half, -sin, sin)

    def rope(t):
        partner = jnp.where(first_half, pltpu.roll(t, LANES - MLA_ROPE_DIM // 2, 1),
                            pltpu.roll(t, MLA_ROPE_DIM // 2, 1))
        return t * cos + partner * sin_signed

    ckvn = _rms(ckv, gkva_ref[...]).astype(BF16)
    kn = _dot(ckvn, wk_ref[...])
    kr = rope(kr_raw)
    for h in range(N_MLA_HEADS):
        sl = slice(h * LANES, (h + 1) * LANES)
        kcat_ref[0, :, sl] = (kn[:, sl] + kr).astype(BF16)

    cqn = _rms(cq, gqa_ref[...]).astype(BF16)
    qT = _dot_nt(wq_ref[...], cqn)
    ang_t = invfc_ref[...] * posr_ref[0]
    cos_t = jnp.cos(ang_t)
    sin_t = jnp.sin(ang_t)
    r0, r1, r2 = MLA_NOPE_DIM, MLA_NOPE_DIM + MLA_ROPE_DIM // 2, MLA_QK_DIM
    for h in range(N_MLA_HEADS):
        blk = qT[h * LANES:(h + 1) * LANES]
        t1 = blk[r0:r1]
        t2 = blk[r1:r2]
        roped = jnp.concatenate([blk[:r0], t1 * cos_t - t2 * sin_t, t2 * cos_t + t1 * sin_t, blk[r2:]], axis=0)
        qT_ref[0, h * LANES:(h + 1) * LANES, :] = roped.astype(BF16)
    vT = _dot_nt(wv_ref[...], ckvn)
    rid = lax.broadcasted_iota(I32, vT.shape, 0) & (LANES - 1)
    vT_ref[0] = jnp.where(rid == MLA_V_DIM, 1.0, vT).astype(BF16)


def _inproj(x, pos_f, g_mix, w_in, b_gates, conv_qk, g_q_a, w_q_b, g_kv_a, w_kv_b):
    B, S, D = x.shape
    tm = TM_INPROJ
    nt = S // tm
    hb = tm // SUBLANES
    off = [0, 512, 1024, 1536, 2048, 2064, 2320, 2448, 2480]
    wqk = w_in[:, off[0]:off[2]].astype(BF16)
    wvo = w_in[:, off[2]:off[4]].astype(BF16)
    w_gate = w_in[:, off[4]:off[5]]
    w_cq = w_in[:, off[5]:off[6]]
    w_ckv = w_in[:, off[6]:off[7]]
    w_kr = w_in[:, off[7]:off[8]]
    kr_blk = jnp.zeros((D, LANES), F32).at[:, MLA_NOPE_DIM:MLA_QK_DIM].set(w_kr)
    gate_blk = jnp.zeros((D, LANES), F32).at[:, :N_GATE].set(w_gate)
    wmisc = jnp.concatenate([w_cq, w_ckv, kr_blk, gate_blk], axis=1).astype(BF16)
    bg = jnp.zeros((1, LANES), F32).at[0, :N_GATE].set(b_gates)
    pad_heads = lambda w3: jnp.pad(w3, ((0, 0), (0, 0), (0, LANES - w3.shape[2]))).reshape(w3.shape[0], -1)
    wq = pad_heads(w_q_b.reshape(-1, N_MLA_HEADS, MLA_QK_DIM)).T.astype(BF16)
    wkv3 = w_kv_b.reshape(-1, N_MLA_HEADS, MLA_NOPE_DIM + MLA_V_DIM)
    wk = pad_heads(wkv3[:, :, :MLA_NOPE_DIM]).astype(BF16)
    wv = pad_heads(wkv3[:, :, MLA_NOPE_DIM:]).T.astype(BF16)
    inv_freq = ROPE_THETA ** (-jnp.arange(0, MLA_ROPE_DIM, 2, dtype=F32) / MLA_ROPE_DIM)
    invf = jnp.zeros((1, LANES), F32).at[0, MLA_NOPE_DIM:MLA_QK_DIM].set(jnp.concatenate([inv_freq, inv_freq]))
    invf_col = inv_freq.reshape(-1, 1)
    pos_row = pos_f.reshape(B, 1, S)

    row = lambda w: pl.BlockSpec((1, tm, w), lambda b, i: (b, i, 0))
    in_specs = [
        row(D),
        pl.BlockSpec((1, SUBLANES, D), lambda b, i: (b, jnp.maximum(i * hb - 1, 0), 0)),
        pl.BlockSpec((1, SUBLANES, D), lambda b, i: (b, jnp.minimum((i + 1) * hb, S // SUBLANES - 1), 0)),
        row(1),
        pl.BlockSpec((1, 1, tm), lambda b, i: (b, 0, i)),
        _full((1, D)), _full(wqk.shape), _full(wvo.shape), _full(wmisc.shape), _full((1, LANES)),
        _full(conv_qk.shape), _full((1, 256)), _full(wq.shape), _full((1, 128)), _full(wk.shape),
        _full(wv.shape), _full((1, LANES)), _full(invf_col.shape),
    ]
    out_shape = [
        jax.ShapeDtypeStruct((B, S, D_MLSTM), BF16),
        jax.ShapeDtypeStruct((B, D_MLSTM, S), BF16),
        jax.ShapeDtypeStruct((B, S, D_MLSTM), BF16),
        jax.ShapeDtypeStruct((B, S, D_MLSTM), F32),
        jax.ShapeDtypeStruct((B, S, LANES), F32),
        jax.ShapeDtypeStruct((B, N_GATE, S), F32),
        jax.ShapeDtypeStruct((B, N_MLA_HEADS * LANES, S), BF16),
        jax.ShapeDtypeStruct((B, S, N_MLA_HEADS * LANES), BF16),
        jax.ShapeDtypeStruct((B, N_MLA_HEADS * LANES, S), BF16),
    ]
    out_specs = [
        row(D_MLSTM),
        pl.BlockSpec((1, D_MLSTM, tm), lambda b, i: (b, 0, i)),
        row(D_MLSTM), row(D_MLSTM), row(LANES),
        pl.BlockSpec((1, N_GATE, tm), lambda b, i: (b, 0, i)),
        pl.BlockSpec((1, N_MLA_HEADS * LANES, tm), lambda b, i: (b, 0, i)),
        row(N_MLA_HEADS * LANES),
        pl.BlockSpec((1, N_MLA_HEADS * LANES, tm), lambda b, i: (b, 0, i)),
    ]
    return pl.pallas_call(
        _inproj_kernel, grid=(B, nt), in_specs=in_specs, out_specs=out_specs, out_shape=out_shape,
        compiler_params=_cparams(("parallel", "parallel")), name="inproj",
    )(x, x, x, pos_f, pos_row, g_mix.reshape(1, D), wqk, wvo, wmisc, bg, conv_qk, g_q_a.reshape(1, -1), wq,
      g_kv_a.reshape(1, -1), wk, wv, invf, invf_col)


def _mlstm_kernel(qf_ref, kf_ref, vf_ref, gcf_ref, gtf_ref, qb_ref, kb_ref, vb_ref, gcb_ref, gtb_ref,
                  hf_ref, hb_ref, cn_ref, m_ref):
    T = MLSTM_CHUNK
    hd = MLSTM_HEAD_DIM

    @pl.when(pl.program_id(1) == 0)
    def _():
        cn_ref[...] = jnp.zeros_like(cn_ref)
        m_ref[...] = jnp.zeros_like(m_ref)

    t_idx = lax.broadcasted_iota(I32, (T, T), 0)
    s_idx = lax.broadcasted_iota(I32, (T, T), 1)
    ones_col = (lax.broadcasted_iota(I32, (T, hd), 1) == 0).astype(BF16)
    neg_inf = F32(-jnp.inf)

    streams = []
    for d, (q_ref, kT_ref, v_ref, gc_ref, gt_ref, h_ref) in enumerate(
            ((qf_ref, kf_ref, vf_ref, gcf_ref, gtf_ref, hf_ref),
             (qb_ref, kb_ref, vb_ref, gcb_ref, gtb_ref, hb_ref))):
        end = T - 1 if d == 0 else 0
        gc = gc_ref[0]
        gt = gt_ref[0]
        for h in range(N_MLSTM_HEADS):
            st = d * N_MLSTM_HEADS + h
            la = d * 8 + h
            b_col = gc[:, la + 4:la + 5]
            a_row = gt[la:la + 1, :]
            m_prev = m_ref[st][0:1, 0:1]
            qh = q_ref[0, :, h * hd:(h + 1) * hd]
            kTh = kT_ref[0, h * hd:(h + 1) * hd, :]
            v_ext = jnp.concatenate([v_ref[0, :, h * hd:(h + 1) * hd], ones_col], axis=1)
            cn = cn_ref[st]
            m_run = jnp.maximum(gc[:, N_GATE + la:N_GATE + la + 1], m_prev)
            m_end = m_run[end:end + 1]
            kw = (kTh.astype(F32) * jnp.exp(a_row - m_end)).astype(BF16)
            streams.append(dict(
                st=st, d=d, h=h, h_ref=h_ref, a_row=a_row, b_col=b_col, m_prev=m_prev, m_run=m_run,
                m_end=m_end, g_sum=b_col[end:end + 1], cn=cn, v_ext=v_ext,
                s=_dot(qh, kTh),
                qcn=_dot(qh, cn.astype(BF16)),
                upd=_dot(kw, v_ext)))
    for x in streams:
        mask = (s_idx <= t_idx) if x["d"] == 0 else (s_idx >= t_idx)
        e = jnp.exp(jnp.where(mask, x["a_row"] - x["m_run"], neg_inf))
        x["pw"] = (e * x["s"]).astype(BF16)
    for x in streams:
        x["pv"] = _dot(x["pw"], x["v_ext"])
    for x in streams:
        st, h = x["st"], x["h"]
        iw = jnp.exp(x["m_prev"] - x["m_run"])
        num = iw * x["qcn"][:, :hd] + x["pv"][:, :hd]
        den = iw * x["qcn"][:, hd:hd + 1] + x["pv"][:, hd:hd + 1]
        floor = jnp.exp(-(x["b_col"] + x["m_run"]))
        x["h_ref"][0, :, h * hd:(h + 1) * hd] = num / jnp.maximum(jnp.abs(den), floor)
        cn_ref[st] = jnp.exp(x["m_prev"] - x["m_end"]) * x["cn"] + x["upd"]
        m_ref[st] = jnp.broadcast_to(x["g_sum"] + x["m_end"], (SUBLANES, LANES))


def _mlstm(q, kT, v, gc, gt):
    B, S, _ = q.shape
    T = MLSTM_CHUNK
    nc = S // T
    fwd = lambda b, j: (b, j, 0)
    bwd = lambda b, j: (b, nc - 1 - j, 0)
    fwd_t = lambda b, j: (b, 0, j)
    bwd_t = lambda b, j: (b, 0, nc - 1 - j)

    def specs(row_map, col_map):
        return [pl.BlockSpec((1, T, D_MLSTM), row_map), pl.BlockSpec((1, D_MLSTM, T), col_map),
                pl.BlockSpec((1, T, D_MLSTM), row_map), pl.BlockSpec((1, T, LANES), row_map),
                pl.BlockSpec((1, N_GATE, T), col_map)]

    n_state = 2 * N_MLSTM_HEADS
    return pl.pallas_call(
        _mlstm_kernel, grid=(B, nc),
        in_specs=specs(fwd, fwd_t) + specs(bwd, bwd_t),
        out_specs=[pl.BlockSpec((1, T, D_MLSTM), fwd), pl.BlockSpec((1, T, D_MLSTM), bwd)],
        out_shape=[jax.ShapeDtypeStruct((B, S, D_MLSTM), F32)] * 2,
        scratch_shapes=[pltpu.VMEM((n_state, MLSTM_HEAD_DIM, 2 * MLSTM_HEAD_DIM), F32),
                        pltpu.VMEM((n_state, SUBLANES, LANES), F32)],
        compiler_params=_cparams(("parallel", "arbitrary")), name="mlstm",
    )(q, kT, v, gc, gt, q, kT, v, gc, gt)


def _mla_kernel(qT_ref, k_ref, vT_ref, g_ref, o_ref, sa_sc, sb_sc, ma_sc, mb_sc):
    t = pl.program_id(0)
    c = (MLA_QK_DIM ** -0.5) * math.log2(math.e)
    S = k_ref.shape[1]
    tq = qT_ref.shape[2]
    ck = S // MLA_CHUNKS

    @pl.when(t == 0)
    def _():
        sb_sc[...] = jnp.zeros(sb_sc.shape, F32)
        mb_sc[...] = jnp.zeros(mb_sc.shape, F32)

    def step(s_cur, m_cur, s_prev, m_prev_ref):
        m_prev = [m_prev_ref[h][0:1] for h in range(2)]
        m_run = [jnp.full((1, tq), -jnp.inf, F32) for _ in range(2)]
        accs = [None, None]
        for j in range(MLA_CHUNKS):
            rows = slice(j * ck, (j + 1) * ck)
            for h in range(2):
                s = _dot(k_ref[0, rows, h * LANES:(h + 1) * LANES], qT_ref[0, h * LANES:(h + 1) * LANES, :])
                m_run[h] = jnp.maximum(m_run[h], jnp.max(s, axis=0, keepdims=True))
                s_cur[h, rows, :] = s
            for h in range(2):
                p = jnp.exp2((s_prev[h, rows, :] - m_prev[h]) * c).astype(BF16)
                part = _dot(vT_ref[0, h * LANES:(h + 1) * LANES, rows], p)
                accs[h] = part if j == 0 else accs[h] + part
        for h in range(2):
            m_cur[h] = jnp.broadcast_to(m_run[h], (SUBLANES, tq))
        ys = []
        for h in range(2):
            acc = accs[h]
            o = acc[:MLA_V_DIM] / acc[MLA_V_DIM:MLA_V_DIM + 1]
            ms = jnp.mean(o * o, axis=0, keepdims=True)
            ys.append(o * lax.rsqrt(ms + EPS))
        o_ref[0] = jnp.concatenate(ys, axis=0).T * g_ref[...]

    @pl.when(t % 2 == 0)
    def _():
        step(sa_sc, ma_sc, sb_sc, mb_sc)

    @pl.when(t % 2 == 1)
    def _():
        step(sb_sc, mb_sc, sa_sc, ma_sc)


def _mla(qT, kcat, vT, g_head_mla):
    B, S, _ = kcat.shape
    tq = TQ_MLA
    npair = N_MLA_HEADS // 2
    nq = S // tq
    nt = B * npair * nq

    def tile(t):
        return t // (npair * nq), (t // nq) % npair, t % nq

    def a_map(f):
        return lambda t: f(*tile(jnp.minimum(t, nt - 1)))

    def b_map(f):
        return lambda t: f(*tile(jnp.maximum(t - 1, 0)))

    return pl.pallas_call(
        _mla_kernel, grid=(nt + 1,),
        in_specs=[pl.BlockSpec((1, 2 * LANES, tq), a_map(lambda b, p, i: (b, p, i))),
                  pl.BlockSpec((1, S, 2 * LANES), a_map(lambda b, p, i: (b, 0, p))),
                  pl.BlockSpec((1, 2 * LANES, S), b_map(lambda b, p, i: (b, p, 0))),
                  pl.BlockSpec((1, LANES), b_map(lambda b, p, i: (0, p)))],
        out_specs=pl.BlockSpec((1, tq, LANES), b_map(lambda b, p, i: (b, i, p))),
        out_shape=jax.ShapeDtypeStruct((B, S, N_MLA_HEADS * MLA_V_DIM), F32),
        scratch_shapes=[pltpu.VMEM((2, S, tq), F32), pltpu.VMEM((2, S, tq), F32),
                        pltpu.VMEM((2, SUBLANES, tq), F32), pltpu.VMEM((2, SUBLANES, tq), F32)],
        compiler_params=_cparams(("arbitrary",)), name="mla_attn",
    )(qT, kcat, vT, g_head_mla.reshape(1, -1))


def _memkv_kernel(mem_ref, g_ref, wk_ref, wv_ref, k_ref, v_ref):
    mn = _rms(mem_ref[0], g_ref[...]).astype(BF16)
    k_ref[0] = _dot(mn, wk_ref[...]).astype(BF16)
    v_ref[0] = _dot(mn, wv_ref[...]).astype(BF16)


def _memkv(mem, g_mem_kv, w_k, w_v):
    B, M, D = mem.shape
    blk = pl.BlockSpec((1, M, D), lambda b: (b, 0, 0))
    return pl.pallas_call(
        _memkv_kernel, grid=(B,),
        in_specs=[blk, _full((1, D)), _full(w_k.shape), _full(w_v.shape)],
        out_specs=[blk, blk], out_shape=[jax.ShapeDtypeStruct((B, M, D), BF16)] * 2,
        compiler_params=_cparams(("parallel",)), name="memkv",
    )(mem, g_mem_kv.reshape(1, D), w_k.astype(BF16), w_v.astype(BF16))


def _post_kernel(x_ref, hf_ref, hb_ref, om_ref, ya_ref, km_ref, vm_ref, ghm_ref, wout_ref, gmx_ref, wmq_ref,
                 wmo_ref, gffn_ref, wrh_ref, wrl_ref, x2_ref, hn_ref, aff_ref, afft_ref):
    tm = x_ref.shape[1]
    D = x_ref.shape[2]
    hm = hf_ref[0] + hb_ref[0]
    gate = _sigmoid(om_ref[0])
    ghm = ghm_ref[...]
    parts = []
    for h in range(N_MLSTM_HEADS):
        sl = slice(h * MLSTM_HEAD_DIM, (h + 1) * MLSTM_HEAD_DIM)
        parts.append(gate[:, sl] * _rms(hm[:, sl], ghm[:, sl]))
    ycat = jnp.concatenate(parts + [ya_ref[0]], axis=1).astype(BF16)
    x1 = x_ref[0] + _dot(ycat, wout_ref[...])

    dm = D // N_MEM_HEADS
    qm = _dot(_rms(x1, gmx_ref[...]).astype(BF16), wmq_ref[...]).astype(BF16)
    km = km_ref[0]
    vm = vm_ref[0]
    outs = []
    for h in range(N_MEM_HEADS):
        sl = slice(h * dm, (h + 1) * dm)
        s = _dot_nt(qm[:, sl], km[:, sl]) * (dm ** -0.5)
        e = jnp.exp(s - jnp.max(s, axis=1, keepdims=True))
        p = e / jnp.sum(e, axis=1, keepdims=True)
        outs.append(_dot(p.astype(BF16), vm[:, sl]))
    om = jnp.concatenate(outs, axis=1).astype(BF16)
    x2 = x1 + _dot(om, wmo_ref[...])
    x2_ref[0] = x2

    hn = _rms(x2, gffn_ref[...])
    hn_ref[0] = hn.reshape(tm, SUBLANES, D // SUBLANES)

    hi = hn.astype(BF16)
    lo = (hn - hi.astype(F32)).astype(BF16)
    wrh = wrh_ref[...]
    logits = _dot(hi, wrh) + (_dot(hi, wrl_ref[...]) + _dot(lo, wrh))
    lane = lax.broadcasted_iota(I32, (tm, LANES), 1)
    valid = lane < N_EXPERTS
    logits = jnp.where(valid, logits, -jnp.inf)
    e = jnp.exp(logits - jnp.max(logits, axis=1, keepdims=True))
    aff = e / jnp.sum(e, axis=1, keepdims=True)
    aff_ref[0] = aff
    afft_ref[0] = aff.T[:N_EXPERTS]


def _post(x, hf, hb, om, ya, kmem, vmem, g_head_mlstm, w_out, g_mem_x, w_mem_q, w_mem_o, g_ffn, w_router):
    B, S, D = x.shape
    tm = TM_POST
    M = kmem.shape[1]
    wr = jnp.zeros((D, LANES), F32).at[:, :N_EXPERTS].set(w_router)
    wrh = wr.astype(BF16)
    wrl = (wr - wrh.astype(F32)).astype(BF16)
    row = lambda w: pl.BlockSpec((1, tm, w), lambda b, i: (b, i, 0))
    mem_spec = pl.BlockSpec((1, M, D), lambda b, i: (b, 0, 0))
    return pl.pallas_call(
        _post_kernel, grid=(B, S // tm),
        in_specs=[row(D), row(D_MLSTM), row(D_MLSTM), row(D_MLSTM), row(D_MLSTM), mem_spec, mem_spec,
                  _full((1, D_MLSTM)), _full((D, D)), _full((1, D)), _full((D, D)), _full((D, D)),
                  _full((1, D)), _full((D, LANES)), _full((D, LANES))],
        out_specs=[row(D), pl.BlockSpec((1, tm, SUBLANES, D // SUBLANES), lambda b, i: (b, i, 0, 0)), row(LANES),
                   pl.BlockSpec((1, N_EXPERTS, tm), lambda b, i: (b, 0, i))],
        out_shape=[jax.ShapeDtypeStruct((B, S, D), F32), jax.ShapeDtypeStruct((B, S, SUBLANES, D // SUBLANES), F32),
                   jax.ShapeDtypeStruct((B, S, LANES), F32), jax.ShapeDtypeStruct((B, N_EXPERTS, S), F32)],
        compiler_params=_cparams(("parallel", "parallel")), name="post_mixer",
    )(x, hf, hb, om, ya, kmem, vmem, g_head_mlstm.reshape(1, -1), w_out.astype(BF16), g_mem_x.reshape(1, D),
      w_mem_q.astype(BF16), w_mem_o.astype(BF16), g_ffn.reshape(1, D), wrh, wrl)


def _topk_kernel(aff_ref, afft_ref, affb_ref, idx_ref, gate_ref, idx_sc, gate_sc):
    S = aff_ref.shape[1]
    cap = idx_ref.shape[2]
    nblk = S // CUM_BLOCK

    def open_interval(c):
        lo, hi = c
        mid = 0.5 * (lo + hi)
        return jnp.max(jnp.where((mid > lo) & (mid < hi), 1.0, 0.0)) > 0.0

    def bisect(c):
        lo, hi = c
        mid = 0.5 * (lo + hi)
        cnt = jnp.sum((afft_ref[0] >= mid).astype(F32), axis=1, keepdims=True)
        ge = cnt >= cap
        return jnp.where(ge, mid, lo), jnp.where(ge, hi, mid)

    lo_c, hi_c = lax.while_loop(open_interval, bisect,
                                (jnp.zeros((N_EXPERTS, 1), F32), jnp.full((N_EXPERTS, 1), 2.0, F32)))
    need_c = cap - jnp.sum((afft_ref[0] >= hi_c).astype(F32), axis=1, keepdims=True)
    eye = (lax.broadcasted_iota(I32, (N_EXPERTS, LANES), 0) == lax.broadcasted_iota(I32, (N_EXPERTS, LANES), 1))
    to_row = lambda col: jnp.sum(jnp.where(eye, col, 0.0), axis=0, keepdims=True)
    lo, hi, need = to_row(lo_c), to_row(hi_c), to_row(need_c)

    r = lax.broadcasted_iota(I32, (CUM_BLOCK, CUM_BLOCK), 0)
    c = lax.broadcasted_iota(I32, (CUM_BLOCK, CUM_BLOCK), 1)
    tri = (c <= r).astype(BF16)

    def cum_body(blk, carry):
        ce, cs = carry
        rows = pl.ds(pl.multiple_of(blk * CUM_BLOCK, CUM_BLOCK), CUM_BLOCK)
        a = aff_ref[0, rows, :]
        sure = a >= hi
        tie = (a >= lo) & jnp.logical_not(sure)
        eq = tie.astype(F32)
        eq_incl = _dot(tri, eq.astype(BF16)) + ce
        sel = jnp.where(sure | (tie & (eq_incl - eq < need)), 1.0, 0.0)
        local_incl = _dot(tri, sel.astype(BF16))
        rank_local = jnp.where(sel > 0.0, local_incl - sel, -2.0).astype(BF16)
        base = lax.convert_element_type(blk * CUM_BLOCK, F32)
        for e in range(N_EXPERTS):
            slot_local = jnp.clip(j_row - cs[:, e:e + 1], -1.0, CUM_BLOCK + 1.0).astype(BF16)
            onehot = jnp.where(rank_local[:, e:e + 1] == slot_local, one_bf, zero_bf)
            a_row = affb_ref[0, e, pl.ds(blk, 1), :]
            a_hi = a_row.astype(BF16).astype(F32)
            a_mid = (a_row - a_hi).astype(BF16).astype(F32)
            a_lo = a_row - a_hi - a_mid
            lhs = jnp.concatenate([tok_rows, a_hi, a_mid, a_lo, pad_rows], axis=0).astype(BF16)
            hit = _dot(lhs, onehot)
            idx_sc[e:e + 1, :] += hit[0:1] + base * hit[1:2]
            gate_sc[e:e + 1, :] += hit[2:3] + hit[3:4] + hit[4:5]
        return eq_incl[CUM_BLOCK - 1:CUM_BLOCK], cs + local_incl[CUM_BLOCK - 1:CUM_BLOCK]

    j_row = lax.broadcasted_iota(I32, (1, cap), 1).astype(F32)
    tok = lax.broadcasted_iota(I32, (1, CUM_BLOCK), 1).astype(F32)
    tok_rows = jnp.concatenate([tok, jnp.ones((1, CUM_BLOCK), F32)], axis=0)
    pad_rows = jnp.zeros((SUBLANES - 5, CUM_BLOCK), F32)
    one_bf = jnp.ones((CUM_BLOCK, cap), BF16)
    zero_bf = jnp.zeros((CUM_BLOCK, cap), BF16)
    idx_sc[...] = jnp.zeros_like(idx_sc)
    gate_sc[...] = jnp.zeros_like(gate_sc)
    zero = jnp.zeros((1, LANES), F32)
    lax.fori_loop(0, nblk, cum_body, (zero, zero))
    idx_ref[0] = idx_sc[...].astype(I32)
    gate_ref[0] = gate_sc[...]


def _topk(aff, aff_t, cap):
    B, S, _ = aff.shape
    nblk = S // CUM_BLOCK
    out = pl.BlockSpec((1, N_EXPERTS, cap), lambda b: (b, 0, 0))
    return pl.pallas_call(
        _topk_kernel, grid=(B,),
        in_specs=[pl.BlockSpec((1, S, LANES), lambda b: (b, 0, 0)),
                  pl.BlockSpec((1, N_EXPERTS, S), lambda b: (b, 0, 0)),
                  pl.BlockSpec((1, N_EXPERTS, nblk, CUM_BLOCK), lambda b: (b, 0, 0, 0))],
        out_specs=[out, out],
        out_shape=[jax.ShapeDtypeStruct((B, N_EXPERTS, cap), I32), jax.ShapeDtypeStruct((B, N_EXPERTS, cap), F32)],
        scratch_shapes=[pltpu.VMEM((N_EXPERTS, cap), F32), pltpu.VMEM((N_EXPERTS, cap), F32)],
        compiler_params=_cparams(("parallel",)), name="topk",
    )(aff, aff_t, aff_t.reshape(B, N_EXPERTS, nblk, CUM_BLOCK))


def _gather_kernel(idx_ref, hn_ref, xe_ref, rows_sc):
    cap = xe_ref.shape[2]

    def body(j, _):
        i = idx_ref[0, 0, j]
        rows_sc[pl.ds(j, 1)] = hn_ref[0, pl.ds(i, 1)]
        return 0

    lax.fori_loop(0, cap, body, 0, unroll=8)
    xe_ref[0, 0] = rows_sc[...].reshape(cap, xe_ref.shape[3]).astype(BF16)


def _gather(idx3, hn, cap):
    B, S, sub, lanes = hn.shape
    E = N_EXPERTS
    D = sub * lanes
    return pl.pallas_call(
        _gather_kernel, grid=(B, E),
        in_specs=[pl.BlockSpec((1, 1, cap), lambda b, e: (b * E + e, 0, 0), memory_space=pltpu.SMEM),
                  pl.BlockSpec((1, S, sub, lanes), lambda b, e: (b, 0, 0, 0))],
        out_specs=pl.BlockSpec((1, 1, cap, D), lambda b, e: (b, e, 0, 0)),
        out_shape=jax.ShapeDtypeStruct((B, E, cap, D), BF16),
        scratch_shapes=[pltpu.VMEM((cap, sub, lanes), F32)],
        compiler_params=_cparams(("parallel", "arbitrary")), name="moe_gather",
    )(idx3, hn)


def _ffn_kernel(xe_ref, wg_ref, wu_ref, wd_ref, ye_ref, wg_sc, wu_sc, wd_sc):
    f = pl.program_id(1)
    nb = xe_ref.shape[0]

    @pl.when(f == 0)
    def _():
        ye_ref[...] = jnp.zeros_like(ye_ref)

    wg_sc[...] = wg_ref[0].astype(BF16)
    wu_sc[...] = wu_ref[0].astype(BF16)
    wd_sc[...] = wd_ref[0].astype(BF16)

    def up(b):
        xb = xe_ref[b, 0]
        return _dot(xb, wg_sc[...]), _dot(xb, wu_sc[...])

    def down(b, h1, h2):
        hid = (h1 * _sigmoid(h1) * h2).astype(BF16)
        ye_ref[b, 0] += _dot(hid, wd_sc[...])

    hs = up(0)
    for b in range(nb):
        nxt = up(b + 1) if b + 1 < nb else None
        down(b, *hs)
        hs = nxt


def _ffn(xe, w_gate, w_up, w_down):
    B, E, cap, D = xe.shape
    F = w_gate.shape[2]
    tf = TF_FFN
    return pl.pallas_call(
        _ffn_kernel, grid=(E, F // tf),
        in_specs=[pl.BlockSpec((B, 1, cap, D), lambda e, f: (0, e, 0, 0)),
                  pl.BlockSpec((1, D, tf), lambda e, f: (e, 0, f)),
                  pl.BlockSpec((1, D, tf), lambda e, f: (e, 0, f)),
                  pl.BlockSpec((1, tf, D), lambda e, f: (e, f, 0))],
        out_specs=pl.BlockSpec((B, 1, cap, D), lambda e, f: (0, e, 0, 0)),
        out_shape=jax.ShapeDtypeStruct((B, E, cap, D), F32),
        scratch_shapes=[pltpu.VMEM((D, tf), BF16), pltpu.VMEM((D, tf), BF16), pltpu.VMEM((tf, D), BF16)],
        compiler_params=_cparams(("parallel", "arbitrary")), name="moe_ffn",
    )(xe, w_gate, w_up, w_down)


def _scatter_kernel(idx_ref, gate_ref, ye_ref, out_ref, ye_sc):
    cap = ye_ref.shape[2]

    @pl.when(pl.program_id(1) == 0)
    def _():
        out_ref[...] = jnp.zeros_like(out_ref)

    ye_sc[...] = ye_ref[0, 0].reshape(ye_sc.shape)

    def body(jb, _):
        base = pl.multiple_of(jb * SCATTER_GROUP, SCATTER_GROUP)
        ids = [idx_ref[0, 0, base + u] for u in range(SCATTER_GROUP)]
        new = [out_ref[0, pl.ds(ids[u], 1)] + ye_sc[pl.ds(base + u, 1)] * gate_ref[0, 0, base + u]
               for u in range(SCATTER_GROUP)]
        for u in range(SCATTER_GROUP):
            out_ref[0, pl.ds(ids[u], 1)] = new[u]
        return 0

    lax.fori_loop(0, cap // SCATTER_GROUP, body, 0)


def _scatter(idx3, gate3, ye, S):
    B, E, cap, D = ye.shape
    lanes = D // SUBLANES
    smem = lambda: pl.BlockSpec((1, 1, cap), lambda b, e: (b * E + e, 0, 0), memory_space=pltpu.SMEM)
    return pl.pallas_call(
        _scatter_kernel, grid=(B, E),
        in_specs=[smem(), smem(), pl.BlockSpec((1, 1, cap, D), lambda b, e: (b, e, 0, 0))],
        out_specs=pl.BlockSpec((1, S, SUBLANES, lanes), lambda b, e: (b, 0, 0, 0)),
        out_shape=jax.ShapeDtypeStruct((B, S, SUBLANES, lanes), F32),
        scratch_shapes=[pltpu.VMEM((cap, SUBLANES, lanes), F32)],
        compiler_params=_cparams(("parallel", "arbitrary")), name="moe_scatter",
    )(idx3, gate3, ye)


def _final_kernel(x_ref, y_ref, g_ref, o_ref):
    o_ref[0] = _rms(x_ref[0] + y_ref[0].reshape(x_ref.shape[1:]), g_ref[...])


def _final(x2, moe, g_final):
    B, S, D = x2.shape
    row = pl.BlockSpec((1, TM_FINAL, D), lambda b, i: (b, i, 0))
    tiles = pl.BlockSpec((1, TM_FINAL) + moe.shape[2:], lambda b, i: (b, i, 0, 0))
    return pl.pallas_call(
        _final_kernel, grid=(B, S // TM_FINAL), in_specs=[row, tiles, _full((1, D))], out_specs=row,
        out_shape=jax.ShapeDtypeStruct((B, S, D), F32),
        compiler_params=_cparams(("parallel", "parallel")), name="final_norm",
    )(x2, moe, g_final.reshape(1, D))


def kernel(x, mem, positions, g_mix, w_in, b_gates, conv_qk, g_q_a, w_q_b, g_kv_a, w_kv_b, g_head_mlstm,
           g_head_mla, w_out, g_mem_x, g_mem_kv, w_mem_q, w_mem_k, w_mem_v, w_mem_o, g_ffn, w_router,
           w_exp_gate, w_exp_up, w_exp_down, g_final):
    B, S, D = x.shape
    depth = g_mix.shape[0]
    assert depth == 1, "the MoE residual is folded into the final norm kernel, which assumes one layer"
    cap = EC_CAPACITY_FACTOR * S // N_EXPERTS
    pos_f = positions.astype(F32).reshape(B, S, 1)
    for l in range(depth):
        q, kT, v, om, gc, gt, qT, kcat, vT = _inproj(
            x, pos_f, g_mix[l], w_in[l], b_gates[l], conv_qk[l], g_q_a[l], w_q_b[l], g_kv_a[l], w_kv_b[l])
        hf, hb = _mlstm(q, kT, v, gc, gt)
        ya = _mla(qT, kcat, vT, g_head_mla[l])
        kmem, vmem = _memkv(mem, g_mem_kv[l], w_mem_k[l], w_mem_v[l])
        x2, hn, aff, aff_t = _post(x, hf, hb, om, ya, kmem, vmem, g_head_mlstm[l], w_out[l], g_mem_x[l],
                            w_mem_q[l], w_mem_o[l], g_ffn[l], w_router[l])
        idx, gate = _topk(aff, aff_t, cap)
        idx3 = idx.reshape(B * N_EXPERTS, 1, cap)
        xe = _gather(idx3, hn, cap)
        ye = _ffn(xe, w_exp_gate[l], w_exp_up[l], w_exp_down[l])
        moe = _scatter(idx3, gate.reshape(B * N_EXPERTS, 1, cap), ye, S)
        x = x2
    return _final(x, moe, g_final)
```

```python
import functools
import math

import jax
import jax.numpy as jnp
from jax import lax
from jax.experimental import pallas as pl
from jax.experimental.pallas import tpu as pltpu

F32 = jnp.float32
BF16 = jnp.bfloat16
I32 = jnp.int32

EPS = 1e-6
N_MLSTM_HEADS = 4
MLSTM_HEAD_DIM = 128
D_MLSTM = N_MLSTM_HEADS * MLSTM_HEAD_DIM
N_MLA_HEADS = 8
MLA_NOPE_DIM = 64
MLA_ROPE_DIM = 32
MLA_QK_DIM = MLA_NOPE_DIM + MLA_ROPE_DIM
MLA_V_DIM = 64
ROPE_THETA = 10000.0
N_GATE = 4 * N_MLSTM_HEADS
N_MEM_HEADS = 4
N_EXPERTS = 16
EC_CAPACITY_FACTOR = 2

LANES = 128
SUBLANES = 8
VMEM_LIMIT_BYTES = 56 * 1024 * 1024

MLSTM_CHUNK = 256
TM_INPROJ = 256
TQ_MLA = 256
MLA_CHUNKS = 16
TM_POST = 256
TF_FFN = 512
TM_FINAL = 512
CUM_BLOCK = 256
SCATTER_GROUP = 8


def _cparams(sem):
    return pltpu.CompilerParams(dimension_semantics=sem, vmem_limit_bytes=VMEM_LIMIT_BYTES)


def _dot(a, b):
    return jnp.dot(a, b, preferred_element_type=F32)


def _dot_nt(a, b):
    return lax.dot_general(a, b, (((1,), (1,)), ((), ())), preferred_element_type=F32)


def _rms(x, g):
    return x * lax.rsqrt(jnp.mean(x * x, axis=-1, keepdims=True) + EPS) * g


def _sigmoid(x):
    return 1.0 / (1.0 + jnp.exp(-x))


def _full(shape):
    return pl.BlockSpec(shape, lambda *_: (0,) * len(shape))


def _inproj_kernel(x_ref, xp_ref, xn_ref, pos_ref, posr_ref, gmix_ref, wqk_ref, wvt_ref, wo_ref, wmisc_ref, bg_ref,
                   conv_ref, gqa_ref, wq_ref, gkva_ref, wk_ref, wv_ref, invf_ref, invfc_ref,
                   qT_m_ref, k_ref, vT_m_ref, o_ref, gc_ref, gt_ref, qT_ref, kcat_ref, vT_ref):
    tm = x_ref.shape[1]
    i = pl.program_id(1)
    last = pl.num_programs(1) - 1
    g = gmix_ref[...]
    xm = _rms(x_ref[0], g)
    xprev = _rms(xp_ref[0], g) * (i > 0).astype(F32)
    xnext = _rms(xn_ref[0], g) * (i < last).astype(F32)
    xm_bf = xm.astype(BF16)
    lhs = jnp.concatenate([xprev.astype(BF16), xm_bf, xnext.astype(BF16)], axis=0)

    pqk = _dot(lhs, wqk_ref[...])
    rows = tm + 2 * SUBLANES
    up = pltpu.roll(pqk, 1, 0)[SUBLANES:SUBLANES + tm]
    dn = pltpu.roll(pqk, rows - 1, 0)[SUBLANES:SUBLANES + tm]
    mid = pqk[SUBLANES:SUBLANES + tm]
    cw = conv_ref[...]
    conv = up * cw[0:1] + mid * cw[1:2] + dn * cw[2:3]
    act = conv * _sigmoid(conv)
    qT_m_ref[0] = act[:, :D_MLSTM].T.astype(BF16)
    k_ref[0] = (act[:, D_MLSTM:] * (MLSTM_HEAD_DIM ** -0.5)).astype(BF16)

    vT_m_ref[0] = _dot_nt(wvt_ref[...], xm_bf).astype(BF16)
    o_ref[0] = _dot(xm_bf, wo_ref[...])

    misc = _dot(xm_bf, wmisc_ref[...])
    cq = misc[:, :256]
    ckv = misc[:, 256:384]
    kr_raw = misc[:, 384:512]
    gp = misc[:, 512:640] + bg_ref[...]

    lane = lax.broadcasted_iota(I32, (tm, LANES), 1)
    is_f = ((lane >= 4) & (lane < 8)) | ((lane >= 12) & (lane < 16))
    logsig = jnp.minimum(gp, 0.0) - jnp.log1p(jnp.exp(-jnp.abs(gp)))
    lf = jnp.where(is_f, logsig, gp)
    rc = lax.broadcasted_iota(I32, (tm, LANES), 0) & (MLSTM_CHUNK - 1)
    pre = lf
    suf = lf
    step = 1
    while step < MLSTM_CHUNK:
        pre = pre + jnp.where(rc >= step, pltpu.roll(pre, step, 0), 0.0)
        suf = suf + jnp.where(rc + step < MLSTM_CHUNK, pltpu.roll(suf, tm - step, 0), 0.0)
        step *= 2
    cum = jnp.where(lane < 8, pre, suf)
    a = lf - pltpu.roll(cum, LANES - 4, 1)
    gc = jnp.where(is_f, cum, a)
    pmax = gc
    smax = gc
    step = 1
    while step < MLSTM_CHUNK:
        pmax = jnp.maximum(pmax, jnp.where(rc >= step, pltpu.roll(pmax, step, 0), -jnp.inf))
        smax = jnp.maximum(smax, jnp.where(rc + step < MLSTM_CHUNK, pltpu.roll(smax, tm - step, 0), -jnp.inf))
        step *= 2
    run_max = pltpu.roll(jnp.where(lane < 8, pmax, smax), N_GATE, 1)
    table = jnp.where(lane < N_GATE, gc, run_max)
    gc_ref[0] = table
    gt_ref[0] = table.T[:2 * N_GATE]

    ang = pos_ref[0] * invf_ref[...]
    cos = jnp.cos(ang)
    sin = jnp.sin(ang)
    first_half = lane < (MLA_NOPE_DIM + MLA_ROPE_DIM // 2)
    sin_signed = jnp.where(first_half, -sin, sin)

    def rope(t):
        partner = jnp.where(first_half, pltpu.roll(t, LANES - MLA_ROPE_DIM // 2, 1),
                            pltpu.roll(t, MLA_ROPE_DIM // 2, 1))
        return t * cos + partner * sin_signed

    ckvn = _rms(ckv, gkva_ref[...]).astype(BF16)
    kn = _dot(ckvn, wk_ref[...])
    kr = rope(kr_raw)
    for h in range(N_MLA_HEADS):
        sl = slice(h * LANES, (h + 1) * LANES)
        kcat_ref[0, :, sl] = (kn[:, sl] + kr).astype(BF16)

    cqn = _rms(cq, gqa_ref[...]).astype(BF16)
    qT = _dot_nt(wq_ref[...], cqn)
    ang_t = invfc_ref[...] * posr_ref[0]
    cos_t = jnp.cos(ang_t)
    sin_t = jnp.sin(ang_t)
    r0, r1, r2 = MLA_NOPE_DIM, MLA_NOPE_DIM + MLA_ROPE_DIM // 2, MLA_QK_DIM
    for h in range(N_MLA_HEADS):
        blk = qT[h * LANES:(h + 1) * LANES]
        t1 = blk[r0:r1]
        t2 = blk[r1:r2]
        roped = jnp.concatenate([blk[:r0], t1 * cos_t - t2 * sin_t, t2 * cos_t + t1 * sin_t, blk[r2:]], axis=0)
        qT_ref[0, h * LANES:(h + 1) * LANES, :] = roped.astype(BF16)
    vT = _dot_nt(wv_ref[...], ckvn)
    rid = lax.broadcasted_iota(I32, vT.shape, 0) & (LANES - 1)
    vT_ref[0] = jnp.where(rid == MLA_V_DIM, 1.0, vT).astype(BF16)


def _inproj(x, pos_f, g_mix, w_in, b_gates, conv_qk, g_q_a, w_q_b, g_kv_a, w_kv_b):
    B, S, D = x.shape
    tm = TM_INPROJ
    nt = S // tm
    hb = tm // SUBLANES
    off = [0, 512, 1024, 1536, 2048, 2064, 2320, 2448, 2480]
    wqk = w_in[:, off[0]:off[2]].astype(BF16)
    wvt = w_in[:, off[2]:off[3]].T.astype(BF16)
    wo = w_in[:, off[3]:off[4]].astype(BF16)
    w_gate = w_in[:, off[4]:off[5]]
    w_cq = w_in[:, off[5]:off[6]]
    w_ckv = w_in[:, off[6]:off[7]]
    w_kr = w_in[:, off[7]:off[8]]
    kr_blk = jnp.zeros((D, LANES), F32).at[:, MLA_NOPE_DIM:MLA_QK_DIM].set(w_kr)
    gate_blk = jnp.zeros((D, LANES), F32).at[:, :N_GATE].set(w_gate)
    wmisc = jnp.concatenate([w_cq, w_ckv, kr_blk, gate_blk], axis=1).astype(BF16)
    bg = jnp.zeros((1, LANES), F32).at[0, :N_GATE].set(b_gates)
    pad_heads = lambda w3: jnp.pad(w3, ((0, 0), (0, 0), (0, LANES - w3.shape[2]))).reshape(w3.shape[0], -1)
    wq = pad_heads(w_q_b.reshape(-1, N_MLA_HEADS, MLA_QK_DIM)).T.astype(BF16)
    wkv3 = w_kv_b.reshape(-1, N_MLA_HEADS, MLA_NOPE_DIM + MLA_V_DIM)
    wk = pad_heads(wkv3[:, :, :MLA_NOPE_DIM]).astype(BF16)
    wv = pad_heads(wkv3[:, :, MLA_NOPE_DIM:]).T.astype(BF16)
    inv_freq = ROPE_THETA ** (-jnp.arange(0, MLA_ROPE_DIM, 2, dtype=F32) / MLA_ROPE_DIM)
    invf = jnp.zeros((1, LANES), F32).at[0, MLA_NOPE_DIM:MLA_QK_DIM].set(jnp.concatenate([inv_freq, inv_freq]))
    invf_col = inv_freq.reshape(-1, 1)
    pos_row = pos_f.reshape(B, 1, S)

    row = lambda w: pl.BlockSpec((1, tm, w), lambda b, i: (b, i, 0))
    in_specs = [
        row(D),
        pl.BlockSpec((1, SUBLANES, D), lambda b, i: (b, jnp.maximum(i * hb - 1, 0), 0)),
        pl.BlockSpec((1, SUBLANES, D), lambda b, i: (b, jnp.minimum((i + 1) * hb, S // SUBLANES - 1), 0)),
        row(1),
        pl.BlockSpec((1, 1, tm), lambda b, i: (b, 0, i)),
        _full((1, D)), _full(wqk.shape), _full(wvt.shape), _full(wo.shape), _full(wmisc.shape), _full((1, LANES)),
        _full(conv_qk.shape), _full((1, 256)), _full(wq.shape), _full((1, 128)), _full(wk.shape),
        _full(wv.shape), _full((1, LANES)), _full(invf_col.shape),
    ]
    out_shape = [
        jax.ShapeDtypeStruct((B, D_MLSTM, S), BF16),
        jax.ShapeDtypeStruct((B, S, D_MLSTM), BF16),
        jax.ShapeDtypeStruct((B, D_MLSTM, S), BF16),
        jax.ShapeDtypeStruct((B, S, D_MLSTM), F32),
        jax.ShapeDtypeStruct((B, S, LANES), F32),
        jax.ShapeDtypeStruct((B, 2 * N_GATE, S), F32),
        jax.ShapeDtypeStruct((B, N_MLA_HEADS * LANES, S), BF16),
        jax.ShapeDtypeStruct((B, S, N_MLA_HEADS * LANES), BF16),
        jax.ShapeDtypeStruct((B, N_MLA_HEADS * LANES, S), BF16),
    ]
    out_specs = [
        pl.BlockSpec((1, D_MLSTM, tm), lambda b, i: (b, 0, i)),
        row(D_MLSTM),
        pl.BlockSpec((1, D_MLSTM, tm), lambda b, i: (b, 0, i)),
        row(D_MLSTM), row(LANES),
        pl.BlockSpec((1, 2 * N_GATE, tm), lambda b, i: (b, 0, i)),
        pl.BlockSpec((1, N_MLA_HEADS * LANES, tm), lambda b, i: (b, 0, i)),
        row(N_MLA_HEADS * LANES),
        pl.BlockSpec((1, N_MLA_HEADS * LANES, tm), lambda b, i: (b, 0, i)),
    ]
    return pl.pallas_call(
        _inproj_kernel, grid=(B, nt), in_specs=in_specs, out_specs=out_specs, out_shape=out_shape,
        compiler_params=_cparams(("parallel", "parallel")), name="inproj",
    )(x, x, x, pos_f, pos_row, g_mix.reshape(1, D), wqk, wvt, wo, wmisc, bg, conv_qk, g_q_a.reshape(1, -1), wq,
      g_kv_a.reshape(1, -1), wk, wv, invf, invf_col)


def _mlstm_kernel(kf_ref, qf_ref, vf_ref, gcf_ref, gtf_ref, kb_ref, qb_ref, vb_ref, gcb_ref, gtb_ref,
                  hf_ref, hb_ref, cn_ref, m_ref):
    T = MLSTM_CHUNK
    hd = MLSTM_HEAD_DIM

    @pl.when(pl.program_id(1) == 0)
    def _():
        cn_ref[...] = jnp.zeros_like(cn_ref)
        m_ref[...] = jnp.zeros_like(m_ref)

    s_idx = lax.broadcasted_iota(I32, (T, T), 0)
    t_idx = lax.broadcasted_iota(I32, (T, T), 1)
    ones_rows = (lax.broadcasted_iota(I32, (hd, T), 0) == 0).astype(BF16)
    neg_inf = F32(-jnp.inf)

    streams = []
    for d, (k_ref, qT_ref, vT_ref, gc_ref, gt_ref, h_ref) in enumerate(
            ((kf_ref, qf_ref, vf_ref, gcf_ref, gtf_ref, hf_ref),
             (kb_ref, qb_ref, vb_ref, gcb_ref, gtb_ref, hb_ref))):
        end = T - 1 if d == 0 else 0
        gc = gc_ref[0]
        gt = gt_ref[0]
        for h in range(N_MLSTM_HEADS):
            st = d * N_MLSTM_HEADS + h
            la = d * 8 + h
            a_row = gt[la:la + 1]
            b_row = gt[la + 4:la + 5]
            m_prev = m_ref[st][0:1, 0:1]
            kh = k_ref[0, :, h * hd:(h + 1) * hd]
            qTh = qT_ref[0, h * hd:(h + 1) * hd, :]
            v_ext = jnp.concatenate([vT_ref[0, h * hd:(h + 1) * hd, :], ones_rows], axis=0)
            cn = cn_ref[st]
            m_run = jnp.maximum(gt[N_GATE + la:N_GATE + la + 1], m_prev)
            m_end = m_run[:, end:end + 1]
            vw = (v_ext.astype(F32) * jnp.exp(a_row - m_end)).astype(BF16)
            streams.append(dict(
                st=st, d=d, h=h, h_ref=h_ref, a_col=gc[:, la:la + 1], b_row=b_row, m_prev=m_prev,
                m_run=m_run, m_end=m_end, g_sum=b_row[:, end:end + 1], cn=cn, v_ext=v_ext,
                s=_dot(kh, qTh),
                qcn=_dot(cn.astype(BF16), qTh),
                upd=_dot(vw, kh)))
    for x in streams:
        mask = (s_idx <= t_idx) if x["d"] == 0 else (s_idx >= t_idx)
        e = jnp.exp(jnp.where(mask, x["a_col"] - x["m_run"], neg_inf))
        x["pw"] = (e * x["s"]).astype(BF16)
    for x in streams:
        x["pv"] = _dot(x["v_ext"], x["pw"])
    for x in streams:
        st, h = x["st"], x["h"]
        iw = jnp.exp(x["m_prev"] - x["m_run"])
        num = iw * x["qcn"][:hd] + x["pv"][:hd]
        den = iw * x["qcn"][hd:hd + 1] + x["pv"][hd:hd + 1]
        floor = jnp.exp(-(x["b_row"] + x["m_run"]))
        x["h_ref"][0, h * hd:(h + 1) * hd, :] = num / jnp.maximum(jnp.abs(den), floor)
        cn_ref[st] = jnp.exp(x["m_prev"] - x["m_end"]) * x["cn"] + x["upd"]
        m_ref[st] = jnp.broadcast_to(x["g_sum"] + x["m_end"], (SUBLANES, LANES))


def _mlstm(k, qT, vT, gc, gt):
    B, S, _ = k.shape
    T = MLSTM_CHUNK
    nc = S // T
    fwd = lambda b, j: (b, j, 0)
    bwd = lambda b, j: (b, nc - 1 - j, 0)
    fwd_t = lambda b, j: (b, 0, j)
    bwd_t = lambda b, j: (b, 0, nc - 1 - j)

    def specs(row_map, col_map):
        return [pl.BlockSpec((1, T, D_MLSTM), row_map), pl.BlockSpec((1, D_MLSTM, T), col_map),
                pl.BlockSpec((1, D_MLSTM, T), col_map), pl.BlockSpec((1, T, LANES), row_map),
                pl.BlockSpec((1, 2 * N_GATE, T), col_map)]

    n_state = 2 * N_MLSTM_HEADS
    return pl.pallas_call(
        _mlstm_kernel, grid=(B, nc),
        in_specs=specs(fwd, fwd_t) + specs(bwd, bwd_t),
        out_specs=[pl.BlockSpec((1, D_MLSTM, T), fwd_t), pl.BlockSpec((1, D_MLSTM, T), bwd_t)],
        out_shape=[jax.ShapeDtypeStruct((B, D_MLSTM, S), F32)] * 2,
        scratch_shapes=[pltpu.VMEM((n_state, 2 * MLSTM_HEAD_DIM, MLSTM_HEAD_DIM), F32),
                        pltpu.VMEM((n_state, SUBLANES, LANES), F32)],
        compiler_params=_cparams(("parallel", "arbitrary")), name="mlstm",
    )(k, qT, vT, gc, gt, k, qT, vT, gc, gt)


def _mla_kernel(qT_ref, k_ref, vT_ref, g_ref, o_ref, sa_sc, sb_sc, ma_sc, mb_sc):
    t = pl.program_id(0)
    c = (MLA_QK_DIM ** -0.5) * math.log2(math.e)
    S = k_ref.shape[1]
    tq = qT_ref.shape[2]
    ck = S // MLA_CHUNKS

    @pl.when(t == 0)
    def _():
        sb_sc[...] = jnp.zeros(sb_sc.shape, F32)
        mb_sc[...] = jnp.zeros(mb_sc.shape, F32)

    def step(s_cur, m_cur, s_prev, m_prev_ref):
        m_prev = [m_prev_ref[h][0:1] for h in range(2)]
        m_run = [jnp.full((1, tq), -jnp.inf, F32) for _ in range(2)]
        accs = [None, None]
        for j in range(MLA_CHUNKS):
            rows = slice(j * ck, (j + 1) * ck)
            for h in range(2):
                s = _dot(k_ref[0, rows, h * LANES:(h + 1) * LANES], qT_ref[0, h * LANES:(h + 1) * LANES, :])
                m_run[h] = jnp.maximum(m_run[h], jnp.max(s, axis=0, keepdims=True))
                s_cur[h, rows, :] = s
            for h in range(2):
                p = jnp.exp2((s_prev[h, rows, :] - m_prev[h]) * c).astype(BF16)
                part = _dot(vT_ref[0, h * LANES:(h + 1) * LANES, rows], p)
                accs[h] = part if j == 0 else accs[h] + part
        for h in range(2):
            m_cur[h] = jnp.broadcast_to(m_run[h], (SUBLANES, tq))
        ys = []
        for h in range(2):
            acc = accs[h]
            o = acc[:MLA_V_DIM] / acc[MLA_V_DIM:MLA_V_DIM + 1]
            ms = jnp.mean(o * o, axis=0, keepdims=True)
            ys.append(o * lax.rsqrt(ms + EPS))
        o_ref[0] = jnp.concatenate(ys, axis=0).T * g_ref[...]

    @pl.when(t % 2 == 0)
    def _():
        step(sa_sc, ma_sc, sb_sc, mb_sc)

    @pl.when(t % 2 == 1)
    def _():
        step(sb_sc, mb_sc, sa_sc, ma_sc)


def _mla(qT, kcat, vT, g_head_mla):
    B, S, _ = kcat.shape
    tq = TQ_MLA
    npair = N_MLA_HEADS // 2
    nq = S // tq
    nt = B * npair * nq

    def tile(t):
        return t // (npair * nq), (t // nq) % npair, t % nq

    def a_map(f):
        return lambda t: f(*tile(jnp.minimum(t, nt - 1)))

    def b_map(f):
        return lambda t: f(*tile(jnp.maximum(t - 1, 0)))

    return pl.pallas_call(
        _mla_kernel, grid=(nt + 1,),
        in_specs=[pl.BlockSpec((1, 2 * LANES, tq), a_map(lambda b, p, i: (b, p, i))),
                  pl.BlockSpec((1, S, 2 * LANES), a_map(lambda b, p, i: (b, 0, p))),
                  pl.BlockSpec((1, 2 * LANES, S), b_map(lambda b, p, i: (b, p, 0))),
                  pl.BlockSpec((1, LANES), b_map(lambda b, p, i: (0, p)))],
        out_specs=pl.BlockSpec((1, tq, LANES), b_map(lambda b, p, i: (b, i, p))),
        out_shape=jax.ShapeDtypeStruct((B, S, N_MLA_HEADS * MLA_V_DIM), F32),
        scratch_shapes=[pltpu.VMEM((2, S, tq), F32), pltpu.VMEM((2, S, tq), F32),
                        pltpu.VMEM((2, SUBLANES, tq), F32), pltpu.VMEM((2, SUBLANES, tq), F32)],
        compiler_params=_cparams(("arbitrary",)), name="mla_attn",
    )(qT, kcat, vT, g_head_mla.reshape(1, -1))


def _memkv_kernel(mem_ref, g_ref, wk_ref, wv_ref, k_ref, v_ref):
    mn = _rms(mem_ref[0], g_ref[...]).astype(BF16)
    k_ref[0] = _dot(mn, wk_ref[...]).astype(BF16)
    v_ref[0] = _dot(mn, wv_ref[...]).astype(BF16)


def _memkv(mem, g_mem_kv, w_k, w_v):
    B, M, D = mem.shape
    blk = pl.BlockSpec((1, M, D), lambda b: (b, 0, 0))
    return pl.pallas_call(
        _memkv_kernel, grid=(B,),
        in_specs=[blk, _full((1, D)), _full(w_k.shape), _full(w_v.shape)],
        out_specs=[blk, blk], out_shape=[jax.ShapeDtypeStruct((B, M, D), BF16)] * 2,
        compiler_params=_cparams(("parallel",)), name="memkv",
    )(mem, g_mem_kv.reshape(1, D), w_k.astype(BF16), w_v.astype(BF16))


def _post_kernel(x_ref, hf_ref, hb_ref, om_ref, ya_ref, km_ref, vm_ref, ghm_ref, wout_ref, gmx_ref, wmq_ref,
                 wmo_ref, gffn_ref, wrh_ref, wrl_ref, x2_ref, hn_ref, aff_ref, afft_ref):
    tm = x_ref.shape[1]
    D = x_ref.shape[2]
    hm = (hf_ref[0] + hb_ref[0]).T
    gate = _sigmoid(om_ref[0])
    ghm = ghm_ref[...]
    parts = []
    for h in range(N_MLSTM_HEADS):
        sl = slice(h * MLSTM_HEAD_DIM, (h + 1) * MLSTM_HEAD_DIM)
        parts.append(gate[:, sl] * _rms(hm[:, sl], ghm[:, sl]))
    ycat = jnp.concatenate(parts + [ya_ref[0]], axis=1).astype(BF16)
    x1 = x_ref[0] + _dot(ycat, wout_ref[...])

    dm = D // N_MEM_HEADS
    qm = _dot(_rms(x1, gmx_ref[...]).astype(BF16), wmq_ref[...]).astype(BF16)
    km = km_ref[0]
    vm = vm_ref[0]
    outs = []
    for h in range(N_MEM_HEADS):
        sl = slice(h * dm, (h + 1) * dm)
        s = _dot_nt(qm[:, sl], km[:, sl]) * (dm ** -0.5)
        e = jnp.exp(s - jnp.max(s, axis=1, keepdims=True))
        p = e / jnp.sum(e, axis=1, keepdims=True)
        outs.append(_dot(p.astype(BF16), vm[:, sl]))
    om = jnp.concatenate(outs, axis=1).astype(BF16)
    x2 = x1 + _dot(om, wmo_ref[...])
    x2_ref[0] = x2

    hn = _rms(x2, gffn_ref[...])
    hn_ref[0] = hn.reshape(tm, SUBLANES, D // SUBLANES)

    hi = hn.astype(BF16)
    lo = (hn - hi.astype(F32)).astype(BF16)
    wrh = wrh_ref[...]
    logits = _dot(hi, wrh) + (_dot(hi, wrl_ref[...]) + _dot(lo, wrh))
    lane = lax.broadcasted_iota(I32, (tm, LANES), 1)
    valid = lane < N_EXPERTS
    logits = jnp.where(valid, logits, -jnp.inf)
    e = jnp.exp(logits - jnp.max(logits, axis=1, keepdims=True))
    aff = e / jnp.sum(e, axis=1, keepdims=True)
    aff_ref[0] = aff
    afft_ref[0] = aff.T[:N_EXPERTS]


def _post(x, hf, hb, om, ya, kmem, vmem, g_head_mlstm, w_out, g_mem_x, w_mem_q, w_mem_o, g_ffn, w_router):
    B, S, D = x.shape
    tm = TM_POST
    M = kmem.shape[1]
    wr = jnp.zeros((D, LANES), F32).at[:, :N_EXPERTS].set(w_router)
    wrh = wr.astype(BF16)
    wrl = (wr - wrh.astype(F32)).astype(BF16)
    row = lambda w: pl.BlockSpec((1, tm, w), lambda b, i: (b, i, 0))
    col = lambda w: pl.BlockSpec((1, w, tm), lambda b, i: (b, 0, i))
    mem_spec = pl.BlockSpec((1, M, D), lambda b, i: (b, 0, 0))
    return pl.pallas_call(
        _post_kernel, grid=(B, S // tm),
        in_specs=[row(D), col(D_MLSTM), col(D_MLSTM), row(D_MLSTM), row(D_MLSTM), mem_spec, mem_spec,
                  _full((1, D_MLSTM)), _full((D, D)), _full((1, D)), _full((D, D)), _full((D, D)),
                  _full((1, D)), _full((D, LANES)), _full((D, LANES))],
        out_specs=[row(D), pl.BlockSpec((1, tm, SUBLANES, D // SUBLANES), lambda b, i: (b, i, 0, 0)), row(LANES),
                   pl.BlockSpec((1, N_EXPERTS, tm), lambda b, i: (b, 0, i))],
        out_shape=[jax.ShapeDtypeStruct((B, S, D), F32), jax.ShapeDtypeStruct((B, S, SUBLANES, D // SUBLANES), F32),
                   jax.ShapeDtypeStruct((B, S, LANES), F32), jax.ShapeDtypeStruct((B, N_EXPERTS, S), F32)],
        compiler_params=_cparams(("parallel", "parallel")), name="post_mixer",
    )(x, hf, hb, om, ya, kmem, vmem, g_head_mlstm.reshape(1, -1), w_out.astype(BF16), g_mem_x.reshape(1, D),
      w_mem_q.astype(BF16), w_mem_o.astype(BF16), g_ffn.reshape(1, D), wrh, wrl)


def _topk_kernel(aff_ref, afft_ref, affb_ref, idx_ref, gate_ref, idx_sc, gate_sc):
    S = aff_ref.shape[1]
    cap = idx_ref.shape[2]
    nblk = S // CUM_BLOCK

    def open_interval(c):
        lo, hi = c
        mid = 0.5 * (lo + hi)
        return jnp.max(jnp.where((mid > lo) & (mid < hi), 1.0, 0.0)) > 0.0

    def bisect(c):
        lo, hi = c
        mid = 0.5 * (lo + hi)
        cnt = jnp.sum((afft_ref[0] >= mid).astype(F32), axis=1, keepdims=True)
        ge = cnt >= cap
        return jnp.where(ge, mid, lo), jnp.where(ge, hi, mid)

    lo_c, hi_c = lax.while_loop(open_interval, bisect,
                                (jnp.zeros((N_EXPERTS, 1), F32), jnp.full((N_EXPERTS, 1), 2.0, F32)))
    need_c = cap - jnp.sum((afft_ref[0] >= hi_c).astype(F32), axis=1, keepdims=True)
    eye = (lax.broadcasted_iota(I32, (N_EXPERTS, LANES), 0) == lax.broadcasted_iota(I32, (N_EXPERTS, LANES), 1))
    to_row = lambda col: jnp.sum(jnp.where(eye, col, 0.0), axis=0, keepdims=True)
    lo, hi, need = to_row(lo_c), to_row(hi_c), to_row(need_c)

    r = lax.broadcasted_iota(I32, (CUM_BLOCK, CUM_BLOCK), 0)
    c = lax.broadcasted_iota(I32, (CUM_BLOCK, CUM_BLOCK), 1)
    tri = (c <= r).astype(BF16)

    def cum_body(blk, carry):
        ce, cs = carry
        rows = pl.ds(pl.multiple_of(blk * CUM_BLOCK, CUM_BLOCK), CUM_BLOCK)
        a = aff_ref[0, rows, :]
        sure = a >= hi
        tie = (a >= lo) & jnp.logical_not(sure)
        eq = tie.astype(F32)
        eq_incl = _dot(tri, eq.astype(BF16)) + ce
        sel = jnp.where(sure | (tie & (eq_incl - eq < need)), 1.0, 0.0)
        local_incl = _dot(tri, sel.astype(BF16))
        rank_local = jnp.where(sel > 0.0, local_incl - sel, -2.0).astype(BF16)
        base = lax.convert_element_type(blk * CUM_BLOCK, F32)
        for e in range(N_EXPERTS):
            slot_local = jnp.clip(j_row - cs[:, e:e + 1], -1.0, CUM_BLOCK + 1.0).astype(BF16)
            onehot = jnp.where(rank_local[:, e:e + 1] == slot_local, one_bf, zero_bf)
            a_row = affb_ref[0, e, pl.ds(blk, 1), :]
            a_hi = a_row.astype(BF16).astype(F32)
            a_mid = (a_row - a_hi).astype(BF16).astype(F32)
            a_lo = a_row - a_hi - a_mid
            lhs = jnp.concatenate([tok_rows, a_hi, a_mid, a_lo, pad_rows], axis=0).astype(BF16)
            hit = _dot(lhs, onehot)
            idx_sc[e:e + 1, :] += hit[0:1] + base * hit[1:2]
            gate_sc[e:e + 1, :] += hit[2:3] + hit[3:4] + hit[4:5]
        return eq_incl[CUM_BLOCK - 1:CUM_BLOCK], cs + local_incl[CUM_BLOCK - 1:CUM_BLOCK]

    j_row = lax.broadcasted_iota(I32, (1, cap), 1).astype(F32)
    tok = lax.broadcasted_iota(I32, (1, CUM_BLOCK), 1).astype(F32)
    tok_rows = jnp.concatenate([tok, jnp.ones((1, CUM_BLOCK), F32)], axis=0)
    pad_rows = jnp.zeros((SUBLANES - 5, CUM_BLOCK), F32)
    one_bf = jnp.ones((CUM_BLOCK, cap), BF16)
    zero_bf = jnp.zeros((CUM_BLOCK, cap), BF16)
    idx_sc[...] = jnp.zeros_like(idx_sc)
    gate_sc[...] = jnp.zeros_like(gate_sc)
    zero = jnp.zeros((1, LANES), F32)
    lax.fori_loop(0, nblk, cum_body, (zero, zero))
    idx_ref[0] = idx_sc[...].astype(I32)
    gate_ref[0] = gate_sc[...]


def _topk(aff, aff_t, cap):
    B, S, _ = aff.shape
    nblk = S // CUM_BLOCK
    out = pl.BlockSpec((1, N_EXPERTS, cap), lambda b: (b, 0, 0))
    return pl.pallas_call(
        _topk_kernel, grid=(B,),
        in_specs=[pl.BlockSpec((1, S, LANES), lambda b: (b, 0, 0)),
                  pl.BlockSpec((1, N_EXPERTS, S), lambda b: (b, 0, 0)),
                  pl.BlockSpec((1, N_EXPERTS, nblk, CUM_BLOCK), lambda b: (b, 0, 0, 0))],
        out_specs=[out, out],
        out_shape=[jax.ShapeDtypeStruct((B, N_EXPERTS, cap), I32), jax.ShapeDtypeStruct((B, N_EXPERTS, cap), F32)],
        scratch_shapes=[pltpu.VMEM((N_EXPERTS, cap), F32), pltpu.VMEM((N_EXPERTS, cap), F32)],
        compiler_params=_cparams(("parallel",)), name="topk",
    )(aff, aff_t, aff_t.reshape(B, N_EXPERTS, nblk, CUM_BLOCK))


def _gather_kernel(idx_ref, hn_ref, xe_ref, rows_sc):
    cap = xe_ref.shape[2]

    def body(j, _):
        i = idx_ref[0, 0, j]
        rows_sc[pl.ds(j, 1)] = hn_ref[0, pl.ds(i, 1)]
        return 0

    lax.fori_loop(0, cap, body, 0, unroll=8)
    xe_ref[0, 0] = rows_sc[...].reshape(cap, xe_ref.shape[3]).astype(BF16)


def _gather(idx3, hn, cap):
    B, S, sub, lanes = hn.shape
    E = N_EXPERTS
    D = sub * lanes
    return pl.pallas_call(
        _gather_kernel, grid=(B, E),
        in_specs=[pl.BlockSpec((1, 1, cap), lambda b, e: (b * E + e, 0, 0), memory_space=pltpu.SMEM),
                  pl.BlockSpec((1, S, sub, lanes), lambda b, e: (b, 0, 0, 0))],
        out_specs=pl.BlockSpec((1, 1, cap, D), lambda b, e: (b, e, 0, 0)),
        out_shape=jax.ShapeDtypeStruct((B, E, cap, D), BF16),
        scratch_shapes=[pltpu.VMEM((cap, sub, lanes), F32)],
        compiler_params=_cparams(("parallel", "arbitrary")), name="moe_gather",
    )(idx3, hn)


def _ffn_kernel(xe_ref, wg_ref, wu_ref, wd_ref, ye_ref, wg_sc, wu_sc, wd_sc):
    f = pl.program_id(1)
    nb = xe_ref.shape[0]

    @pl.when(f == 0)
    def _():
        ye_ref[...] = jnp.zeros_like(ye_ref)

    wg_sc[...] = wg_ref[0].astype(BF16)
    wu_sc[...] = wu_ref[0].astype(BF16)
    wd_sc[...] = wd_ref[0].astype(BF16)

    def up(b):
        xb = xe_ref[b, 0]
        return _dot(xb, wg_sc[...]), _dot(xb, wu_sc[...])

    def down(b, h1, h2):
        hid = (h1 * _sigmoid(h1) * h2).astype(BF16)
        ye_ref[b, 0] += _dot(hid, wd_sc[...])

    hs = up(0)
    for b in range(nb):
        nxt = up(b + 1) if b + 1 < nb else None
        down(b, *hs)
        hs = nxt


def _ffn(xe, w_gate, w_up, w_down):
    B, E, cap, D = xe.shape
    F = w_gate.shape[2]
    tf = TF_FFN
    return pl.pallas_call(
        _ffn_kernel, grid=(E, F // tf),
        in_specs=[pl.BlockSpec((B, 1, cap, D), lambda e, f: (0, e, 0, 0)),
                  pl.BlockSpec((1, D, tf), lambda e, f: (e, 0, f)),
                  pl.BlockSpec((1, D, tf), lambda e, f: (e, 0, f)),
                  pl.BlockSpec((1, tf, D), lambda e, f: (e, f, 0))],
        out_specs=pl.BlockSpec((B, 1, cap, D), lambda e, f: (0, e, 0, 0)),
        out_shape=jax.ShapeDtypeStruct((B, E, cap, D), F32),
        scratch_shapes=[pltpu.VMEM((D, tf), BF16), pltpu.VMEM((D, tf), BF16), pltpu.VMEM((tf, D), BF16)],
        compiler_params=_cparams(("parallel", "arbitrary")), name="moe_ffn",
    )(xe, w_gate, w_up, w_down)


def _scatter_kernel(idx_ref, gate_ref, ye_ref, out_ref, ye_sc):
    cap = ye_ref.shape[2]

    @pl.when(pl.program_id(1) == 0)
    def _():
        out_ref[...] = jnp.zeros_like(out_ref)

    ye_sc[...] = ye_ref[0, 0].reshape(ye_sc.shape)

    def body(jb, _):
        base = pl.multiple_of(jb * SCATTER_GROUP, SCATTER_GROUP)
        ids = [idx_ref[0, 0, base + u] for u in range(SCATTER_GROUP)]
        new = [out_ref[0, pl.ds(ids[u], 1)] + ye_sc[pl.ds(base + u, 1)] * gate_ref[0, 0, base + u]
               for u in range(SCATTER_GROUP)]
        for u in range(SCATTER_GROUP):
            out_ref[0, pl.ds(ids[u], 1)] = new[u]
        return 0

    lax.fori_loop(0, cap // SCATTER_GROUP, body, 0)


def _scatter(idx3, gate3, ye, S):
    B, E, cap, D = ye.shape
    lanes = D // SUBLANES
    smem = lambda: pl.BlockSpec((1, 1, cap), lambda b, e: (b * E + e, 0, 0), memory_space=pltpu.SMEM)
    return pl.pallas_call(
        _scatter_kernel, grid=(B, E),
        in_specs=[smem(), smem(), pl.BlockSpec((1, 1, cap, D), lambda b, e: (b, e, 0, 0))],
        out_specs=pl.BlockSpec((1, S, SUBLANES, lanes), lambda b, e: (b, 0, 0, 0)),
        out_shape=jax.ShapeDtypeStruct((B, S, SUBLANES, lanes), F32),
        scratch_shapes=[pltpu.VMEM((cap, SUBLANES, lanes), F32)],
        compiler_params=_cparams(("parallel", "arbitrary")), name="moe_scatter",
    )(idx3, gate3, ye)


def _final_kernel(x_ref, y_ref, g_ref, o_ref):
    o_ref[0] = _rms(x_ref[0] + y_ref[0].reshape(x_ref.shape[1:]), g_ref[...])


def _final(x2, moe, g_final):
    B, S, D = x2.shape
    row = pl.BlockSpec((1, TM_FINAL, D), lambda b, i: (b, i, 0))
    tiles = pl.BlockSpec((1, TM_FINAL) + moe.shape[2:], lambda b, i: (b, i, 0, 0))
    return pl.pallas_call(
        _final_kernel, grid=(B, S // TM_FINAL), in_specs=[row, tiles, _full((1, D))], out_specs=row,
        out_shape=jax.ShapeDtypeStruct((B, S, D), F32),
        compiler_params=_cparams(("parallel", "parallel")), name="final_norm",
    )(x2, moe, g_final.reshape(1, D))


def kernel(x, mem, positions, g_mix, w_in, b_gates, conv_qk, g_q_a, w_q_b, g_kv_a, w_kv_b, g_head_mlstm,
           g_head_mla, w_out, g_mem_x, g_mem_kv, w_mem_q, w_mem_k, w_mem_v, w_mem_o, g_ffn, w_router,
           w_exp_gate, w_exp_up, w_exp_down, g_final):
    B, S, D = x.shape
    depth = g_mix.shape[0]
    assert depth == 1, "the MoE residual is folded into the final norm kernel, which assumes one layer"
    cap = EC_CAPACITY_FACTOR * S // N_EXPERTS
    pos_f = positions.astype(F32).reshape(B, S, 1)
    for l in range(depth):
        qT_m, k_m, vT_m, om, gc, gt, qT, kcat, vT = _inproj(
            x, pos_f, g_mix[l], w_in[l], b_gates[l], conv_qk[l], g_q_a[l], w_q_b[l], g_kv_a[l], w_kv_b[l])
        hf, hb = _mlstm(k_m, qT_m, vT_m, gc, gt)
        ya = _mla(qT, kcat, vT, g_head_mla[l])
        kmem, vmem = _memkv(mem, g_mem_kv[l], w_mem_k[l], w_mem_v[l])
        x2, hn, aff, aff_t = _post(x, hf, hb, om, ya, kmem, vmem, g_head_mlstm[l], w_out[l], g_mem_x[l],
                            w_mem_q[l], w_mem_o[l], g_ffn[l], w_router[l])
        idx, gate = _topk(aff, aff_t, cap)
        idx3 = idx.reshape(B * N_EXPERTS, 1, cap)
        xe = _gather(idx3, hn, cap)
        ye = _ffn(xe, w_exp_gate[l], w_exp_up[l], w_exp_down[l])
        moe = _scatter(idx3, gate.reshape(B * N_EXPERTS, 1, cap), ye, S)
        x = x2
    return _final(x, moe, g_final)
```

```python
import functools
import math

import jax
import jax.numpy as jnp
from jax import lax
from jax.experimental import pallas as pl
from jax.experimental.pallas import tpu as pltpu

F32 = jnp.float32
BF16 = jnp.bfloat16
I32 = jnp.int32

EPS = 1e-6
N_MLSTM_HEADS = 4
MLSTM_HEAD_DIM = 128
D_MLSTM = N_MLSTM_HEADS * MLSTM_HEAD_DIM
N_MLA_HEADS = 8
MLA_NOPE_DIM = 64
MLA_ROPE_DIM = 32
MLA_QK_DIM = MLA_NOPE_DIM + MLA_ROPE_DIM
MLA_V_DIM = 64
ROPE_THETA = 10000.0
N_GATE = 4 * N_MLSTM_HEADS
N_MEM_HEADS = 4
N_EXPERTS = 16
EC_CAPACITY_FACTOR = 2

LANES = 128
SUBLANES = 8
VMEM_LIMIT_BYTES = 56 * 1024 * 1024

MLSTM_CHUNK = 256
TM_INPROJ = 256
TQ_MLA = 256
MLA_CHUNKS = 16
TM_POST = 512
POST_SPLIT = 2
TF_FFN = 512
TM_FINAL = 512
CUM_BLOCK = 256
SCATTER_GROUP = 8


def _cparams(sem):
    return pltpu.CompilerParams(dimension_semantics=sem, vmem_limit_bytes=VMEM_LIMIT_BYTES)


def _dot(a, b):
    return jnp.dot(a, b, preferred_element_type=F32)


def _dot_nt(a, b):
    return lax.dot_general(a, b, (((1,), (1,)), ((), ())), preferred_element_type=F32)


def _rms(x, g):
    return x * lax.rsqrt(jnp.mean(x * x, axis=-1, keepdims=True) + EPS) * g


def _sigmoid(x):
    return 1.0 / (1.0 + jnp.exp(-x))


def _full(shape):
    return pl.BlockSpec(shape, lambda *_: (0,) * len(shape))


def _inproj_kernel(x_ref, xp_ref, xn_ref, posr_ref, gmix_ref, wqk_ref, wvt_ref, wo_ref, wmisc_ref, wkr_ref, bg_ref,
                   conv_ref, gqa_ref, wq_ref, gkva_ref, wk_ref, wv_ref, invfc_ref,
                   qT_m_ref, k_ref, vT_m_ref, o_ref, gc_ref, gt_ref, qT_ref, kcat_ref, vT_ref):
    tm = x_ref.shape[1]
    i = pl.program_id(1)
    last = pl.num_programs(1) - 1
    g = gmix_ref[...]
    xm = _rms(x_ref[0], g)
    xprev = _rms(xp_ref[0], g) * (i > 0).astype(F32)
    xnext = _rms(xn_ref[0], g) * (i < last).astype(F32)
    xm_bf = xm.astype(BF16)
    lhs = jnp.concatenate([xprev.astype(BF16), xm_bf, xnext.astype(BF16)], axis=0)

    pqk = _dot(lhs, wqk_ref[...])
    rows = tm + 2 * SUBLANES
    up = pltpu.roll(pqk, 1, 0)[SUBLANES:SUBLANES + tm]
    dn = pltpu.roll(pqk, rows - 1, 0)[SUBLANES:SUBLANES + tm]
    mid = pqk[SUBLANES:SUBLANES + tm]
    cw = conv_ref[...]
    conv = up * cw[0:1] + mid * cw[1:2] + dn * cw[2:3]
    act = conv * _sigmoid(conv)
    qT_m_ref[0] = act[:, :D_MLSTM].T.astype(BF16)
    k_ref[0] = (act[:, D_MLSTM:] * (MLSTM_HEAD_DIM ** -0.5)).astype(BF16)

    vT_m_ref[0] = _dot_nt(wvt_ref[...], xm_bf).astype(BF16)
    o_ref[0] = _dot(xm_bf, wo_ref[...])

    misc = _dot(xm_bf, wmisc_ref[...])
    cq = misc[:, :256]
    ckv = misc[:, 256:384]
    gp = misc[:, 384:512] + bg_ref[...]

    lane = lax.broadcasted_iota(I32, (tm, LANES), 1)
    is_f = ((lane >= 4) & (lane < 8)) | ((lane >= 12) & (lane < 16))
    logsig = jnp.minimum(gp, 0.0) - jnp.log1p(jnp.exp(-jnp.abs(gp)))
    lf = jnp.where(is_f, logsig, gp)
    row_id = lax.broadcasted_iota(I32, (tm, LANES), 0)

    def scan(x, op, ident, suffix):
        step = 1
        while step < tm:
            if step < SUBLANES:
                ok = (row_id + step < tm) if suffix else (row_id >= step)
                shifted = jnp.where(ok, pltpu.roll(x, tm - step if suffix else step, 0), ident)
            else:
                pad = jnp.full((step, LANES), ident, F32)
                shifted = jnp.concatenate([x[step:], pad] if suffix else [pad, x[:tm - step]], axis=0)
            x = op(x, shifted)
            step *= 2
        return x

    cum = jnp.where(lane < 8, scan(lf, jnp.add, 0.0, False), scan(lf, jnp.add, 0.0, True))
    a = lf - pltpu.roll(cum, LANES - 4, 1)
    gc = jnp.where(is_f, cum, a)
    run_max = jnp.where(lane < 8, scan(gc, jnp.maximum, -jnp.inf, False), scan(gc, jnp.maximum, -jnp.inf, True))
    table = jnp.where(lane < N_GATE, gc, pltpu.roll(run_max, N_GATE, 1))
    gc_ref[0] = table
    gt_ref[0] = table.T[:2 * N_GATE]

    ang_t = invfc_ref[...] * posr_ref[0]
    cos_t = jnp.cos(ang_t)
    sin_t = jnp.sin(ang_t)
    r0, r1, r2 = MLA_NOPE_DIM, MLA_NOPE_DIM + MLA_ROPE_DIM // 2, MLA_QK_DIM

    def rope_rows(blk):
        t1 = blk[r0:r1]
        t2 = blk[r1:r2]
        return jnp.concatenate([blk[:r0], t1 * cos_t - t2 * sin_t, t2 * cos_t + t1 * sin_t, blk[r2:]], axis=0)

    ckvn = _rms(ckv, gkva_ref[...]).astype(BF16)
    kn = _dot(ckvn, wk_ref[...])
    kr = rope_rows(_dot_nt(wkr_ref[...], xm_bf)).T
    for h in range(N_MLA_HEADS):
        sl = slice(h * LANES, (h + 1) * LANES)
        kcat_ref[0, :, sl] = (kn[:, sl] + kr).astype(BF16)

    cqn = _rms(cq, gqa_ref[...]).astype(BF16)
    qT = _dot_nt(wq_ref[...], cqn)
    for h in range(N_MLA_HEADS):
        qT_ref[0, h * LANES:(h + 1) * LANES, :] = rope_rows(qT[h * LANES:(h + 1) * LANES]).astype(BF16)
    vT = _dot_nt(wv_ref[...], ckvn)
    rid = lax.broadcasted_iota(I32, vT.shape, 0) & (LANES - 1)
    vT_ref[0] = jnp.where(rid == MLA_V_DIM, 1.0, vT).astype(BF16)


def _inproj(x, pos_row, g_mix, w_in, b_gates, conv_qk, g_q_a, w_q_b, g_kv_a, w_kv_b):
    B, S, D = x.shape
    tm = TM_INPROJ
    nt = S // tm
    hb = tm // SUBLANES
    off = [0, 512, 1024, 1536, 2048, 2064, 2320, 2448, 2480]
    wqk = w_in[:, off[0]:off[2]].astype(BF16)
    wvt = w_in[:, off[2]:off[3]].T.astype(BF16)
    wo = w_in[:, off[3]:off[4]].astype(BF16)
    w_gate = w_in[:, off[4]:off[5]]
    w_cq = w_in[:, off[5]:off[6]]
    w_ckv = w_in[:, off[6]:off[7]]
    w_kr = w_in[:, off[7]:off[8]]
    wkr = jnp.zeros((LANES, D), F32).at[MLA_NOPE_DIM:MLA_QK_DIM].set(w_kr.T).astype(BF16)
    gate_blk = jnp.zeros((D, LANES), F32).at[:, :N_GATE].set(w_gate)
    wmisc = jnp.concatenate([w_cq, w_ckv, gate_blk], axis=1).astype(BF16)
    bg = jnp.zeros((1, LANES), F32).at[0, :N_GATE].set(b_gates)
    pad_heads = lambda w3: jnp.pad(w3, ((0, 0), (0, 0), (0, LANES - w3.shape[2]))).reshape(w3.shape[0], -1)
    wq = pad_heads(w_q_b.reshape(-1, N_MLA_HEADS, MLA_QK_DIM)).T.astype(BF16)
    wkv3 = w_kv_b.reshape(-1, N_MLA_HEADS, MLA_NOPE_DIM + MLA_V_DIM)
    wk = pad_heads(wkv3[:, :, :MLA_NOPE_DIM]).astype(BF16)
    wv = pad_heads(wkv3[:, :, MLA_NOPE_DIM:]).T.astype(BF16)
    inv_freq = ROPE_THETA ** (-jnp.arange(0, MLA_ROPE_DIM, 2, dtype=F32) / MLA_ROPE_DIM)
    invf_col = inv_freq.reshape(-1, 1)

    row = lambda w: pl.BlockSpec((1, tm, w), lambda b, i: (b, i, 0))
    in_specs = [
        row(D),
        pl.BlockSpec((1, SUBLANES, D), lambda b, i: (b, jnp.maximum(i * hb - 1, 0), 0)),
        pl.BlockSpec((1, SUBLANES, D), lambda b, i: (b, jnp.minimum((i + 1) * hb, S // SUBLANES - 1), 0)),
        pl.BlockSpec((1, 1, tm), lambda b, i: (b, 0, i)),
        _full((1, D)), _full(wqk.shape), _full(wvt.shape), _full(wo.shape), _full(wmisc.shape), _full(wkr.shape),
        _full((1, LANES)), _full(conv_qk.shape), _full((1, 256)), _full(wq.shape), _full((1, 128)),
        _full(wk.shape), _full(wv.shape), _full(invf_col.shape),
    ]
    out_shape = [
        jax.ShapeDtypeStruct((B, D_MLSTM, S), BF16),
        jax.ShapeDtypeStruct((B, S, D_MLSTM), BF16),
        jax.ShapeDtypeStruct((B, D_MLSTM, S), BF16),
        jax.ShapeDtypeStruct((B, S, D_MLSTM), F32),
        jax.ShapeDtypeStruct((B, S, LANES), F32),
        jax.ShapeDtypeStruct((B, 2 * N_GATE, S), F32),
        jax.ShapeDtypeStruct((B, N_MLA_HEADS * LANES, S), BF16),
        jax.ShapeDtypeStruct((B, S, N_MLA_HEADS * LANES), BF16),
        jax.ShapeDtypeStruct((B, N_MLA_HEADS * LANES, S), BF16),
    ]
    out_specs = [
        pl.BlockSpec((1, D_MLSTM, tm), lambda b, i: (b, 0, i)),
        row(D_MLSTM),
        pl.BlockSpec((1, D_MLSTM, tm), lambda b, i: (b, 0, i)),
        row(D_MLSTM), row(LANES),
        pl.BlockSpec((1, 2 * N_GATE, tm), lambda b, i: (b, 0, i)),
        pl.BlockSpec((1, N_MLA_HEADS * LANES, tm), lambda b, i: (b, 0, i)),
        row(N_MLA_HEADS * LANES),
        pl.BlockSpec((1, N_MLA_HEADS * LANES, tm), lambda b, i: (b, 0, i)),
    ]
    return pl.pallas_call(
        _inproj_kernel, grid=(B, nt), in_specs=in_specs, out_specs=out_specs, out_shape=out_shape,
        compiler_params=_cparams(("parallel", "parallel")), name="inproj",
    )(x, x, x, pos_row, g_mix.reshape(1, D), wqk, wvt, wo, wmisc, wkr, bg, conv_qk, g_q_a.reshape(1, -1), wq,
      g_kv_a.reshape(1, -1), wk, wv, invf_col)


def _mlstm_kernel(kf_ref, qf_ref, vf_ref, gcf_ref, gtf_ref, kb_ref, qb_ref, vb_ref, gcb_ref, gtb_ref,
                  hf_ref, hb_ref, cn_ref, m_ref):
    T = MLSTM_CHUNK
    hd = MLSTM_HEAD_DIM

    @pl.when(pl.program_id(1) == 0)
    def _():
        cn_ref[...] = jnp.zeros_like(cn_ref)
        m_ref[...] = jnp.zeros_like(m_ref)

    s_idx = lax.broadcasted_iota(I32, (T, T), 0)
    t_idx = lax.broadcasted_iota(I32, (T, T), 1)
    ones_rows = (lax.broadcasted_iota(I32, (hd, T), 0) == 0).astype(BF16)
    neg_inf = F32(-jnp.inf)

    streams = []
    for d, (k_ref, qT_ref, vT_ref, gc_ref, gt_ref, h_ref) in enumerate(
            ((kf_ref, qf_ref, vf_ref, gcf_ref, gtf_ref, hf_ref),
             (kb_ref, qb_ref, vb_ref, gcb_ref, gtb_ref, hb_ref))):
        end = T - 1 if d == 0 else 0
        gc = gc_ref[0]
        gt = gt_ref[0]
        for h in range(N_MLSTM_HEADS):
            st = d * N_MLSTM_HEADS + h
            la = d * 8 + h
            a_row = gt[la:la + 1]
            b_row = gt[la + 4:la + 5]
            m_prev = m_ref[st][0:1, 0:1]
            kh = k_ref[0, :, h * hd:(h + 1) * hd]
            qTh = qT_ref[0, h * hd:(h + 1) * hd, :]
            v_ext = jnp.concatenate([vT_ref[0, h * hd:(h + 1) * hd, :], ones_rows], axis=0)
            cn = cn_ref[st]
            m_run = jnp.maximum(gt[N_GATE + la:N_GATE + la + 1], m_prev)
            m_end = m_run[:, end:end + 1]
            vw = (v_ext.astype(F32) * jnp.exp(a_row - m_end)).astype(BF16)
            streams.append(dict(
                st=st, d=d, h=h, h_ref=h_ref, a_col=gc[:, la:la + 1], b_row=b_row, m_prev=m_prev,
                m_run=m_run, m_end=m_end, g_sum=b_row[:, end:end + 1], cn=cn, v_ext=v_ext,
                s=_dot(kh, qTh),
                qcn=_dot(cn.astype(BF16), qTh),
                upd=_dot(vw, kh)))
    for x in streams:
        mask = (s_idx <= t_idx) if x["d"] == 0 else (s_idx >= t_idx)
        e = jnp.exp(jnp.where(mask, x["a_col"] - x["m_run"], neg_inf))
        x["pw"] = (e * x["s"]).astype(BF16)
    for x in streams:
        x["pv"] = _dot(x["v_ext"], x["pw"])
    for x in streams:
        st, h = x["st"], x["h"]
        iw = jnp.exp(x["m_prev"] - x["m_run"])
        num = iw * x["qcn"][:hd] + x["pv"][:hd]
        den = iw * x["qcn"][hd:hd + 1] + x["pv"][hd:hd + 1]
        floor = jnp.exp(-(x["b_row"] + x["m_run"]))
        x["h_ref"][0, h * hd:(h + 1) * hd, :] = num / jnp.maximum(jnp.abs(den), floor)
        cn_ref[st] = jnp.exp(x["m_prev"] - x["m_end"]) * x["cn"] + x["upd"]
        m_ref[st] = jnp.broadcast_to(x["g_sum"] + x["m_end"], (SUBLANES, LANES))


def _mlstm(k, qT, vT, gc, gt):
    B, S, _ = k.shape
    T = MLSTM_CHUNK
    nc = S // T
    fwd = lambda b, j: (b, j, 0)
    bwd = lambda b, j: (b, nc - 1 - j, 0)
    fwd_t = lambda b, j: (b, 0, j)
    bwd_t = lambda b, j: (b, 0, nc - 1 - j)

    def specs(row_map, col_map):
        return [pl.BlockSpec((1, T, D_MLSTM), row_map), pl.BlockSpec((1, D_MLSTM, T), col_map),
                pl.BlockSpec((1, D_MLSTM, T), col_map), pl.BlockSpec((1, T, LANES), row_map),
                pl.BlockSpec((1, 2 * N_GATE, T), col_map)]

    n_state = 2 * N_MLSTM_HEADS
    return pl.pallas_call(
        _mlstm_kernel, grid=(B, nc),
        in_specs=specs(fwd, fwd_t) + specs(bwd, bwd_t),
        out_specs=[pl.BlockSpec((1, D_MLSTM, T), fwd_t), pl.BlockSpec((1, D_MLSTM, T), bwd_t)],
        out_shape=[jax.ShapeDtypeStruct((B, D_MLSTM, S), F32)] * 2,
        scratch_shapes=[pltpu.VMEM((n_state, 2 * MLSTM_HEAD_DIM, MLSTM_HEAD_DIM), F32),
                        pltpu.VMEM((n_state, SUBLANES, LANES), F32)],
        compiler_params=_cparams(("parallel", "arbitrary")), name="mlstm",
    )(k, qT, vT, gc, gt, k, qT, vT, gc, gt)


def _mla_kernel(qT_ref, k_ref, vT_ref, g_ref, o_ref, sa_sc, sb_sc, ma_sc, mb_sc):
    t = pl.program_id(0)
    c = (MLA_QK_DIM ** -0.5) * math.log2(math.e)
    S = k_ref.shape[1]
    tq = qT_ref.shape[2]
    ck = S // MLA_CHUNKS

    @pl.when(t == 0)
    def _():
        sb_sc[...] = jnp.zeros(sb_sc.shape, F32)
        mb_sc[...] = jnp.zeros(mb_sc.shape, F32)

    def step(s_cur, m_cur, s_prev, m_prev_ref):
        m_prev = [m_prev_ref[h][0:1] for h in range(2)]
        m_run = [jnp.full((1, tq), -jnp.inf, F32) for _ in range(2)]
        accs = [None, None]
        for j in range(MLA_CHUNKS):
            rows = slice(j * ck, (j + 1) * ck)
            for h in range(2):
                s = _dot(k_ref[0, rows, h * LANES:(h + 1) * LANES], qT_ref[0, h * LANES:(h + 1) * LANES, :])
                m_run[h] = jnp.maximum(m_run[h], jnp.max(s, axis=0, keepdims=True))
                s_cur[h, rows, :] = s
            for h in range(2):
                p = jnp.exp2((s_prev[h, rows, :] - m_prev[h]) * c).astype(BF16)
                part = _dot(vT_ref[0, h * LANES:(h + 1) * LANES, rows], p)
                accs[h] = part if j == 0 else accs[h] + part
        for h in range(2):
            m_cur[h] = jnp.broadcast_to(m_run[h], (SUBLANES, tq))
        ys = []
        for h in range(2):
            acc = accs[h]
            o = acc[:MLA_V_DIM] / acc[MLA_V_DIM:MLA_V_DIM + 1]
            ms = jnp.mean(o * o, axis=0, keepdims=True)
            ys.append(o * lax.rsqrt(ms + EPS))
        o_ref[0] = jnp.concatenate(ys, axis=0).T * g_ref[...]

    @pl.when(t % 2 == 0)
    def _():
        step(sa_sc, ma_sc, sb_sc, mb_sc)

    @pl.when(t % 2 == 1)
    def _():
        step(sb_sc, mb_sc, sa_sc, ma_sc)


def _mla(qT, kcat, vT, g_head_mla):
    B, S, _ = kcat.shape
    tq = TQ_MLA
    npair = N_MLA_HEADS // 2
    nq = S // tq
    nt = B * npair * nq

    def tile(t):
        return t // (npair * nq), (t // nq) % npair, t % nq

    def a_map(f):
        return lambda t: f(*tile(jnp.minimum(t, nt - 1)))

    def b_map(f):
        return lambda t: f(*tile(jnp.maximum(t - 1, 0)))

    return pl.pallas_call(
        _mla_kernel, grid=(nt + 1,),
        in_specs=[pl.BlockSpec((1, 2 * LANES, tq), a_map(lambda b, p, i: (b, p, i))),
                  pl.BlockSpec((1, S, 2 * LANES), a_map(lambda b, p, i: (b, 0, p))),
                  pl.BlockSpec((1, 2 * LANES, S), b_map(lambda b, p, i: (b, p, 0))),
                  pl.BlockSpec((1, LANES), b_map(lambda b, p, i: (0, p)))],
        out_specs=pl.BlockSpec((1, tq, LANES), b_map(lambda b, p, i: (b, i, p))),
        out_shape=jax.ShapeDtypeStruct((B, S, N_MLA_HEADS * MLA_V_DIM), F32),
        scratch_shapes=[pltpu.VMEM((2, S, tq), F32), pltpu.VMEM((2, S, tq), F32),
                        pltpu.VMEM((2, SUBLANES, tq), F32), pltpu.VMEM((2, SUBLANES, tq), F32)],
        compiler_params=_cparams(("arbitrary",)), name="mla_attn",
    )(qT, kcat, vT, g_head_mla.reshape(1, -1))


def _memkv_kernel(mem_ref, g_ref, wk_ref, wv_ref, k_ref, v_ref):
    mn = _rms(mem_ref[0], g_ref[...]).astype(BF16)
    k_ref[0] = _dot(mn, wk_ref[...]).astype(BF16)
    v_ref[0] = _dot(mn, wv_ref[...]).astype(BF16)


def _memkv(mem, g_mem_kv, w_k, w_v):
    B, M, D = mem.shape
    blk = pl.BlockSpec((1, M, D), lambda b: (b, 0, 0))
    return pl.pallas_call(
        _memkv_kernel, grid=(B,),
        in_specs=[blk, _full((1, D)), _full(w_k.shape), _full(w_v.shape)],
        out_specs=[blk, blk], out_shape=[jax.ShapeDtypeStruct((B, M, D), BF16)] * 2,
        compiler_params=_cparams(("parallel",)), name="memkv",
    )(mem, g_mem_kv.reshape(1, D), w_k.astype(BF16), w_v.astype(BF16))


def _post_kernel(x_ref, hf_ref, hb_ref, om_ref, ya_ref, km_ref, vm_ref, ghm_ref, wout_ref, gmx_ref, wmq_ref,
                 wmo_ref, gffn_ref, wrh_ref, wrl_ref, x2_ref, hn_ref, aff_ref, afft_ref):
    tm = x_ref.shape[1]
    D = x_ref.shape[2]
    hr = tm // POST_SPLIT
    subs = [slice(i * hr, (i + 1) * hr) for i in range(POST_SPLIT)]
    ghm = ghm_ref[...]
    dm = D // N_MEM_HEADS
    heads = [slice(h * dm, (h + 1) * dm) for h in range(N_MEM_HEADS)]

    ycat = []
    for r in subs:
        hm = (hf_ref[0, :, r] + hb_ref[0, :, r]).T
        gate = _sigmoid(om_ref[0, r])
        parts = []
        for h in range(N_MLSTM_HEADS):
            sl = slice(h * MLSTM_HEAD_DIM, (h + 1) * MLSTM_HEAD_DIM)
            parts.append(gate[:, sl] * _rms(hm[:, sl], ghm[:, sl]))
        ycat.append(jnp.concatenate(parts + [ya_ref[0, r]], axis=1).astype(BF16))
    x1 = [x_ref[0, r] + _dot(y, wout_ref[...]) for r, y in zip(subs, ycat)]

    xn = [_rms(v, gmx_ref[...]).astype(BF16) for v in x1]
    qm = [_dot(v, wmq_ref[...]).astype(BF16) for v in xn]
    km = km_ref[0]
    vm = vm_ref[0]
    scores = [[_dot_nt(q[:, sl], km[:, sl]) * (dm ** -0.5) for sl in heads] for q in qm]
    probs = []
    for per_sub in scores:
        ps = []
        for s in per_sub:
            e = jnp.exp(s - jnp.max(s, axis=1, keepdims=True))
            ps.append((e / jnp.sum(e, axis=1, keepdims=True)).astype(BF16))
        probs.append(ps)
    om = [jnp.concatenate([_dot(p, vm[:, sl]) for p, sl in zip(ps, heads)], axis=1).astype(BF16) for ps in probs]
    x2 = [v + _dot(o, wmo_ref[...]) for v, o in zip(x1, om)]

    hns = []
    for r, v in zip(subs, x2):
        x2_ref[0, r] = v
        hn = _rms(v, gffn_ref[...])
        hn_ref[0, r] = hn.reshape(hr, SUBLANES, D // SUBLANES)
        hi = hn.astype(BF16)
        hns.append((hi, (hn - hi.astype(F32)).astype(BF16)))
    wrh = wrh_ref[...]
    logits = [_dot(hi, wrh) + (_dot(hi, wrl_ref[...]) + _dot(lo, wrh)) for hi, lo in hns]
    valid = lax.broadcasted_iota(I32, (hr, LANES), 1) < N_EXPERTS
    for r, lg in zip(subs, logits):
        lg = jnp.where(valid, lg, -jnp.inf)
        e = jnp.exp(lg - jnp.max(lg, axis=1, keepdims=True))
        aff = e / jnp.sum(e, axis=1, keepdims=True)
        aff_ref[0, r] = aff
        afft_ref[0, :, r] = aff.T[:N_EXPERTS]


def _post(x, hf, hb, om, ya, kmem, vmem, g_head_mlstm, w_out, g_mem_x, w_mem_q, w_mem_o, g_ffn, w_router):
    B, S, D = x.shape
    tm = TM_POST
    M = kmem.shape[1]
    wr = jnp.zeros((D, LANES), F32).at[:, :N_EXPERTS].set(w_router)
    wrh = wr.astype(BF16)
    wrl = (wr - wrh.astype(F32)).astype(BF16)
    row = lambda w: pl.BlockSpec((1, tm, w), lambda b, i: (b, i, 0))
    col = lambda w: pl.BlockSpec((1, w, tm), lambda b, i: (b, 0, i))
    mem_spec = pl.BlockSpec((1, M, D), lambda b, i: (b, 0, 0))
    return pl.pallas_call(
        _post_kernel, grid=(B, S // tm),
        in_specs=[row(D), col(D_MLSTM), col(D_MLSTM), row(D_MLSTM), row(D_MLSTM), mem_spec, mem_spec,
                  _full((1, D_MLSTM)), _full((D, D)), _full((1, D)), _full((D, D)), _full((D, D)),
                  _full((1, D)), _full((D, LANES)), _full((D, LANES))],
        out_specs=[row(D), pl.BlockSpec((1, tm, SUBLANES, D // SUBLANES), lambda b, i: (b, i, 0, 0)), row(LANES),
                   pl.BlockSpec((1, N_EXPERTS, tm), lambda b, i: (b, 0, i))],
        out_shape=[jax.ShapeDtypeStruct((B, S, D), F32), jax.ShapeDtypeStruct((B, S, SUBLANES, D // SUBLANES), F32),
                   jax.ShapeDtypeStruct((B, S, LANES), F32), jax.ShapeDtypeStruct((B, N_EXPERTS, S), F32)],
        compiler_params=_cparams(("parallel", "parallel")), name="post_mixer",
    )(x, hf, hb, om, ya, kmem, vmem, g_head_mlstm.reshape(1, -1), w_out.astype(BF16), g_mem_x.reshape(1, D),
      w_mem_q.astype(BF16), w_mem_o.astype(BF16), g_ffn.reshape(1, D), wrh, wrl)


def _topk_kernel(aff_ref, afft_ref, affb_ref, idx_ref, gate_ref, idx_sc, gate_sc):
    S = aff_ref.shape[1]
    cap = idx_ref.shape[2]
    nblk = S // CUM_BLOCK

    def open_interval(c):
        lo, hi = c
        mid = 0.5 * (lo + hi)
        return jnp.max(jnp.where((mid > lo) & (mid < hi), 1.0, 0.0)) > 0.0

    def bisect(c):
        lo, hi = c
        mid = 0.5 * (lo + hi)
        cnt = jnp.sum((afft_ref[0] >= mid).astype(F32), axis=1, keepdims=True)
        ge = cnt >= cap
        return jnp.where(ge, mid, lo), jnp.where(ge, hi, mid)

    lo_c, hi_c = lax.while_loop(open_interval, bisect,
                                (jnp.zeros((N_EXPERTS, 1), F32), jnp.full((N_EXPERTS, 1), 2.0, F32)))
    need_c = cap - jnp.sum((afft_ref[0] >= hi_c).astype(F32), axis=1, keepdims=True)
    eye = (lax.broadcasted_iota(I32, (N_EXPERTS, LANES), 0) == lax.broadcasted_iota(I32, (N_EXPERTS, LANES), 1))
    to_row = lambda col: jnp.sum(jnp.where(eye, col, 0.0), axis=0, keepdims=True)
    lo, hi, need = to_row(lo_c), to_row(hi_c), to_row(need_c)

    r = lax.broadcasted_iota(I32, (CUM_BLOCK, CUM_BLOCK), 0)
    c = lax.broadcasted_iota(I32, (CUM_BLOCK, CUM_BLOCK), 1)
    tri = (c <= r).astype(BF16)

    def cum_body(blk, carry):
        ce, cs = carry
        rows = pl.ds(pl.multiple_of(blk * CUM_BLOCK, CUM_BLOCK), CUM_BLOCK)
        a = aff_ref[0, rows, :]
        sure = a >= hi
        tie = (a >= lo) & jnp.logical_not(sure)
        eq = tie.astype(F32)
        eq_incl = _dot(tri, eq.astype(BF16)) + ce
        sel = jnp.where(sure | (tie & (eq_incl - eq < need)), 1.0, 0.0)
        local_incl = _dot(tri, sel.astype(BF16))
        rank_local = jnp.where(sel > 0.0, local_incl - sel, -2.0).astype(BF16)
        base = lax.convert_element_type(blk * CUM_BLOCK, F32)
        for e in range(N_EXPERTS):
            slot_local = jnp.clip(j_row - cs[:, e:e + 1], -1.0, CUM_BLOCK + 1.0).astype(BF16)
            onehot = jnp.where(rank_local[:, e:e + 1] == slot_local, one_bf, zero_bf)
            a_row = affb_ref[0, e, pl.ds(blk, 1), :]
            a_hi = a_row.astype(BF16).astype(F32)
            a_mid = (a_row - a_hi).astype(BF16).astype(F32)
            a_lo = a_row - a_hi - a_mid
            lhs = jnp.concatenate([tok_rows, a_hi, a_mid, a_lo, pad_rows], axis=0).astype(BF16)
            hit = _dot(lhs, onehot)
            idx_sc[e:e + 1, :] += hit[0:1] + base * hit[1:2]
            gate_sc[e:e + 1, :] += hit[2:3] + hit[3:4] + hit[4:5]
        return eq_incl[CUM_BLOCK - 1:CUM_BLOCK], cs + local_incl[CUM_BLOCK - 1:CUM_BLOCK]

    j_row = lax.broadcasted_iota(I32, (1, cap), 1).astype(F32)
    tok = lax.broadcasted_iota(I32, (1, CUM_BLOCK), 1).astype(F32)
    tok_rows = jnp.concatenate([tok, jnp.ones((1, CUM_BLOCK), F32)], axis=0)
    pad_rows = jnp.zeros((SUBLANES - 5, CUM_BLOCK), F32)
    one_bf = jnp.ones((CUM_BLOCK, cap), BF16)
    zero_bf = jnp.zeros((CUM_BLOCK, cap), BF16)
    idx_sc[...] = jnp.zeros_like(idx_sc)
    gate_sc[...] = jnp.zeros_like(gate_sc)
    zero = jnp.zeros((1, LANES), F32)
    lax.fori_loop(0, nblk, cum_body, (zero, zero))
    idx_ref[0] = idx_sc[...].astype(I32)
    gate_ref[0] = gate_sc[...]


def _topk(aff, aff_t, cap):
    B, S, _ = aff.shape
    nblk = S // CUM_BLOCK
    out = pl.BlockSpec((1, N_EXPERTS, cap), lambda b: (b, 0, 0))
    return pl.pallas_call(
        _topk_kernel, grid=(B,),
        in_specs=[pl.BlockSpec((1, S, LANES), lambda b: (b, 0, 0)),
                  pl.BlockSpec((1, N_EXPERTS, S), lambda b: (b, 0, 0)),
                  pl.BlockSpec((1, N_EXPERTS, nblk, CUM_BLOCK), lambda b: (b, 0, 0, 0))],
        out_specs=[out, out],
        out_shape=[jax.ShapeDtypeStruct((B, N_EXPERTS, cap), I32), jax.ShapeDtypeStruct((B, N_EXPERTS, cap), F32)],
        scratch_shapes=[pltpu.VMEM((N_EXPERTS, cap), F32), pltpu.VMEM((N_EXPERTS, cap), F32)],
        compiler_params=_cparams(("parallel",)), name="topk",
    )(aff, aff_t, aff_t.reshape(B, N_EXPERTS, nblk, CUM_BLOCK))


def _gather_kernel(idx_ref, hn_ref, xe_ref, rows_sc):
    cap = xe_ref.shape[2]

    def body(j, _):
        i = idx_ref[0, 0, j]
        rows_sc[pl.ds(j, 1)] = hn_ref[0, pl.ds(i, 1)]
        return 0

    lax.fori_loop(0, cap, body, 0, unroll=8)
    xe_ref[0, 0] = rows_sc[...].reshape(cap, xe_ref.shape[3]).astype(BF16)


def _gather(idx3, hn, cap):
    B, S, sub, lanes = hn.shape
    E = N_EXPERTS
    D = sub * lanes
    return pl.pallas_call(
        _gather_kernel, grid=(B, E),
        in_specs=[pl.BlockSpec((1, 1, cap), lambda b, e: (b * E + e, 0, 0), memory_space=pltpu.SMEM),
                  pl.BlockSpec((1, S, sub, lanes), lambda b, e: (b, 0, 0, 0))],
        out_specs=pl.BlockSpec((1, 1, cap, D), lambda b, e: (b, e, 0, 0)),
        out_shape=jax.ShapeDtypeStruct((B, E, cap, D), BF16),
        scratch_shapes=[pltpu.VMEM((cap, sub, lanes), F32)],
        compiler_params=_cparams(("parallel", "arbitrary")), name="moe_gather",
    )(idx3, hn)


def _ffn_kernel(xe_ref, wg_ref, wu_ref, wd_ref, ye_ref, wg_sc, wu_sc, wd_sc):
    f = pl.program_id(1)
    nb = xe_ref.shape[0]

    @pl.when(f == 0)
    def _():
        ye_ref[...] = jnp.zeros_like(ye_ref)

    wg_sc[...] = wg_ref[0].astype(BF16)
    wu_sc[...] = wu_ref[0].astype(BF16)
    wd_sc[...] = wd_ref[0].astype(BF16)

    def up(b):
        xb = xe_ref[b, 0]
        return _dot(xb, wg_sc[...]), _dot(xb, wu_sc[...])

    def down(b, h1, h2):
        hid = (h1 * _sigmoid(h1) * h2).astype(BF16)
        ye_ref[b, 0] += _dot(hid, wd_sc[...])

    hs = up(0)
    for b in range(nb):
        nxt = up(b + 1) if b + 1 < nb else None
        down(b, *hs)
        hs = nxt


def _ffn(xe, w_gate, w_up, w_down):
    B, E, cap, D = xe.shape
    F = w_gate.shape[2]
    tf = TF_FFN
    return pl.pallas_call(
        _ffn_kernel, grid=(E, F // tf),
        in_specs=[pl.BlockSpec((B, 1, cap, D), lambda e, f: (0, e, 0, 0)),
                  pl.BlockSpec((1, D, tf), lambda e, f: (e, 0, f)),
                  pl.BlockSpec((1, D, tf), lambda e, f: (e, 0, f)),
                  pl.BlockSpec((1, tf, D), lambda e, f: (e, f, 0))],
        out_specs=pl.BlockSpec((B, 1, cap, D), lambda e, f: (0, e, 0, 0)),
        out_shape=jax.ShapeDtypeStruct((B, E, cap, D), F32),
        scratch_shapes=[pltpu.VMEM((D, tf), BF16), pltpu.VMEM((D, tf), BF16), pltpu.VMEM((tf, D), BF16)],
        compiler_params=_cparams(("parallel", "arbitrary")), name="moe_ffn",
    )(xe, w_gate, w_up, w_down)


def _scatter_kernel(idx_ref, gate_ref, ye_ref, out_ref, ye_sc):
    cap = ye_ref.shape[2]

    @pl.when(pl.program_id(1) == 0)
    def _():
        out_ref[...] = jnp.zeros_like(out_ref)

    ye_sc[...] = ye_ref[0, 0].reshape(ye_sc.shape)

    def body(jb, _):
        base = pl.multiple_of(jb * SCATTER_GROUP, SCATTER_GROUP)
        ids = [idx_ref[0, 0, base + u] for u in range(SCATTER_GROUP)]
        new = [out_ref[0, pl.ds(ids[u], 1)] + ye_sc[pl.ds(base + u, 1)] * gate_ref[0, 0, base + u]
               for u in range(SCATTER_GROUP)]
        for u in range(SCATTER_GROUP):
            out_ref[0, pl.ds(ids[u], 1)] = new[u]
        return 0

    lax.fori_loop(0, cap // SCATTER_GROUP, body, 0)


def _scatter(idx3, gate3, ye, S):
    B, E, cap, D = ye.shape
    lanes = D // SUBLANES
    smem = lambda: pl.BlockSpec((1, 1, cap), lambda b, e: (b * E + e, 0, 0), memory_space=pltpu.SMEM)
    return pl.pallas_call(
        _scatter_kernel, grid=(B, E),
        in_specs=[smem(), smem(), pl.BlockSpec((1, 1, cap, D), lambda b, e: (b, e, 0, 0))],
        out_specs=pl.BlockSpec((1, S, SUBLANES, lanes), lambda b, e: (b, 0, 0, 0)),
        out_shape=jax.ShapeDtypeStruct((B, S, SUBLANES, lanes), F32),
        scratch_shapes=[pltpu.VMEM((cap, SUBLANES, lanes), F32)],
        compiler_params=_cparams(("parallel", "arbitrary")), name="moe_scatter",
    )(idx3, gate3, ye)


def _final_kernel(x_ref, y_ref, g_ref, o_ref):
    o_ref[0] = _rms(x_ref[0] + y_ref[0].reshape(x_ref.shape[1:]), g_ref[...])


def _final(x2, moe, g_final):
    B, S, D = x2.shape
    row = pl.BlockSpec((1, TM_FINAL, D), lambda b, i: (b, i, 0))
    tiles = pl.BlockSpec((1, TM_FINAL) + moe.shape[2:], lambda b, i: (b, i, 0, 0))
    return pl.pallas_call(
        _final_kernel, grid=(B, S // TM_FINAL), in_specs=[row, tiles, _full((1, D))], out_specs=row,
        out_shape=jax.ShapeDtypeStruct((B, S, D), F32),
        compiler_params=_cparams(("parallel", "parallel")), name="final_norm",
    )(x2, moe, g_final.reshape(1, D))


def kernel(x, mem, positions, g_mix, w_in, b_gates, conv_qk, g_q_a, w_q_b, g_kv_a, w_kv_b, g_head_mlstm,
           g_head_mla, w_out, g_mem_x, g_mem_kv, w_mem_q, w_mem_k, w_mem_v, w_mem_o, g_ffn, w_router,
           w_exp_gate, w_exp_up, w_exp_down, g_final):
    B, S, D = x.shape
    depth = g_mix.shape[0]
    assert depth == 1, "the MoE residual is folded into the final norm kernel, which assumes one layer"
    cap = EC_CAPACITY_FACTOR * S // N_EXPERTS
    pos_row = positions.astype(F32).reshape(B, 1, S)
    for l in range(depth):
        qT_m, k_m, vT_m, om, gc, gt, qT, kcat, vT = _inproj(
            x, pos_row, g_mix[l], w_in[l], b_gates[l], conv_qk[l], g_q_a[l], w_q_b[l], g_kv_a[l], w_kv_b[l])
        hf, hb = _mlstm(k_m, qT_m, vT_m, gc, gt)
        ya = _mla(qT, kcat, vT, g_head_mla[l])
        kmem, vmem = _memkv(mem, g_mem_kv[l], w_mem_k[l], w_mem_v[l])
        x2, hn, aff, aff_t = _post(x, hf, hb, om, ya, kmem, vmem, g_head_mlstm[l], w_out[l], g_mem_x[l],
                            w_mem_q[l], w_mem_o[l], g_ffn[l], w_router[l])
        idx, gate = _topk(aff, aff_t, cap)
        idx3 = idx.reshape(B * N_EXPERTS, 1, cap)
        xe = _gather(idx3, hn, cap)
        ye = _ffn(xe, w_exp_gate[l], w_exp_up[l], w_exp_down[l])
        moe = _scatter(idx3, gate.reshape(B * N_EXPERTS, 1, cap), ye, S)
        x = x2
    return _final(x, moe, g_final)
```

```python
import functools
import math

import jax
import jax.numpy as jnp
from jax import lax
from jax.experimental import pallas as pl
from jax.experimental.pallas import tpu as pltpu

F32 = jnp.float32
BF16 = jnp.bfloat16
I32 = jnp.int32

EPS = 1e-6
N_MLSTM_HEADS = 4
MLSTM_HEAD_DIM = 128
D_MLSTM = N_MLSTM_HEADS * MLSTM_HEAD_DIM
N_MLA_HEADS = 8
MLA_NOPE_DIM = 64
MLA_ROPE_DIM = 32
MLA_QK_DIM = MLA_NOPE_DIM + MLA_ROPE_DIM
MLA_V_DIM = 64
MLA_V_ROWS = 80
ROPE_THETA = 10000.0
N_GATE = 4 * N_MLSTM_HEADS
N_MEM_HEADS = 4
N_EXPERTS = 16
EC_CAPACITY_FACTOR = 2

LANES = 128
SUBLANES = 8
VMEM_LIMIT_BYTES = 56 * 1024 * 1024

MLSTM_CHUNK = 256
MLSTM_PAD_ROWS = 16
TM_INPROJ = 256
TQ_MLA = 256
MLA_CHUNKS = 16
TM_POST = 512
POST_SPLIT = 2
TF_FFN = 512
TM_FINAL = 512
CUM_BLOCK = 256
SCATTER_GROUP = 8


def _cparams(sem):
    return pltpu.CompilerParams(dimension_semantics=sem, vmem_limit_bytes=VMEM_LIMIT_BYTES)


def _dot(a, b):
    return jnp.dot(a, b, preferred_element_type=F32)


def _dot_nt(a, b):
    return lax.dot_general(a, b, (((1,), (1,)), ((), ())), preferred_element_type=F32)


def _rms(x, g):
    return x * lax.rsqrt(jnp.mean(x * x, axis=-1, keepdims=True) + EPS) * g


def _sigmoid(x):
    return 1.0 / (1.0 + jnp.exp(-x))


def _full(shape):
    return pl.BlockSpec(shape, lambda *_: (0,) * len(shape))


def _inproj_kernel(x_ref, xp_ref, xn_ref, posr_ref, gmix_ref, wqk_ref, wvt_ref, wo_ref, wmisc_ref, wkr_ref, bg_ref,
                   conv_ref, gqa_ref, wq_ref, gkva_ref, wk_ref, wv_ref, vones_ref, invfc_ref,
                   qT_m_ref, k_ref, vT_m_ref, o_ref, gc_ref, gt_ref, qT_ref, kcat_ref, vT_ref):
    tm = x_ref.shape[1]
    i = pl.program_id(1)
    last = pl.num_programs(1) - 1
    g = gmix_ref[...]
    xm = _rms(x_ref[0], g)
    xprev = _rms(xp_ref[0], g) * (i > 0).astype(F32)
    xnext = _rms(xn_ref[0], g) * (i < last).astype(F32)
    xm_bf = xm.astype(BF16)
    lhs = jnp.concatenate([xprev.astype(BF16), xm_bf, xnext.astype(BF16)], axis=0)

    pqk = _dot(lhs, wqk_ref[...])
    rows = tm + 2 * SUBLANES
    up = pltpu.roll(pqk, 1, 0)[SUBLANES:SUBLANES + tm]
    dn = pltpu.roll(pqk, rows - 1, 0)[SUBLANES:SUBLANES + tm]
    mid = pqk[SUBLANES:SUBLANES + tm]
    cw = conv_ref[...]
    conv = up * cw[0:1] + mid * cw[1:2] + dn * cw[2:3]
    act = conv * _sigmoid(conv)
    qT_m_ref[0] = act[:, :D_MLSTM].T.astype(BF16)
    k_ref[0] = (act[:, D_MLSTM:] * (MLSTM_HEAD_DIM ** -0.5)).astype(BF16)

    vT_m_ref[0] = _dot_nt(wvt_ref[...], xm_bf).astype(BF16)
    o_ref[0] = _dot(xm_bf, wo_ref[...])

    misc = _dot(xm_bf, wmisc_ref[...])
    cq = misc[:, :256]
    ckv = misc[:, 256:384]
    gp = misc[:, 384:512] + bg_ref[...]

    lane = lax.broadcasted_iota(I32, (tm, LANES), 1)
    is_f = ((lane >= 4) & (lane < 8)) | ((lane >= 12) & (lane < 16))
    logsig = jnp.minimum(gp, 0.0) - jnp.log1p(jnp.exp(-jnp.abs(gp)))
    lf = jnp.where(is_f, logsig, gp)
    row_id = lax.broadcasted_iota(I32, (tm, LANES), 0)

    def scan(x, op, ident, suffix):
        step = 1
        while step < tm:
            if step < SUBLANES:
                ok = (row_id + step < tm) if suffix else (row_id >= step)
                shifted = jnp.where(ok, pltpu.roll(x, tm - step if suffix else step, 0), ident)
            else:
                pad = jnp.full((step, LANES), ident, F32)
                shifted = jnp.concatenate([x[step:], pad] if suffix else [pad, x[:tm - step]], axis=0)
            x = op(x, shifted)
            step *= 2
        return x

    cum = jnp.where(lane < 8, scan(lf, jnp.add, 0.0, False), scan(lf, jnp.add, 0.0, True))
    a = lf - pltpu.roll(cum, LANES - 4, 1)
    gc = jnp.where(is_f, cum, a)
    run_max = jnp.where(lane < 8, scan(gc, jnp.maximum, -jnp.inf, False), scan(gc, jnp.maximum, -jnp.inf, True))
    table = jnp.where(lane < N_GATE, gc, pltpu.roll(run_max, N_GATE, 1))
    gc_ref[0] = table
    gt_ref[0] = table.T[:2 * N_GATE]

    ang_t = invfc_ref[...] * posr_ref[0]
    cos_t = jnp.cos(ang_t)
    sin_t = jnp.sin(ang_t)
    r0, r1, r2 = MLA_NOPE_DIM, MLA_NOPE_DIM + MLA_ROPE_DIM // 2, MLA_QK_DIM

    def rope_rows(blk):
        t1 = blk[r0:r1]
        t2 = blk[r1:r2]
        return jnp.concatenate([blk[:r0], t1 * cos_t - t2 * sin_t, t2 * cos_t + t1 * sin_t, blk[r2:]], axis=0)

    ckvn = _rms(ckv, gkva_ref[...]).astype(BF16)
    kn = _dot(ckvn, wk_ref[...])
    kr = rope_rows(_dot_nt(wkr_ref[...], xm_bf)).T
    for h in range(N_MLA_HEADS):
        sl = slice(h * LANES, (h + 1) * LANES)
        kcat_ref[0, :, sl] = (kn[:, sl] + kr).astype(BF16)

    cqn = _rms(cq, gqa_ref[...]).astype(BF16)
    qT = _dot_nt(wq_ref[...], cqn)
    for h in range(N_MLA_HEADS):
        qT_ref[0, h * LANES:(h + 1) * LANES, :] = rope_rows(qT[h * LANES:(h + 1) * LANES]).astype(BF16)
    vT_ref[0] = (_dot_nt(wv_ref[...], ckvn) + vones_ref[...]).astype(BF16)


def _inproj(x, pos_row, g_mix, w_in, b_gates, conv_qk, g_q_a, w_q_b, g_kv_a, w_kv_b):
    B, S, D = x.shape
    tm = TM_INPROJ
    nt = S // tm
    hb = tm // SUBLANES
    off = [0, 512, 1024, 1536, 2048, 2064, 2320, 2448, 2480]
    wqk = w_in[:, off[0]:off[2]].astype(BF16)
    wvt = w_in[:, off[2]:off[3]].T.astype(BF16)
    wo = w_in[:, off[3]:off[4]].astype(BF16)
    w_gate = w_in[:, off[4]:off[5]]
    w_cq = w_in[:, off[5]:off[6]]
    w_ckv = w_in[:, off[6]:off[7]]
    w_kr = w_in[:, off[7]:off[8]]
    wkr = jnp.zeros((LANES, D), F32).at[MLA_NOPE_DIM:MLA_QK_DIM].set(w_kr.T).astype(BF16)
    gate_blk = jnp.zeros((D, LANES), F32).at[:, :N_GATE].set(w_gate)
    wmisc = jnp.concatenate([w_cq, w_ckv, gate_blk], axis=1).astype(BF16)
    bg = jnp.zeros((1, LANES), F32).at[0, :N_GATE].set(b_gates)
    pad_heads = lambda w3: jnp.pad(w3, ((0, 0), (0, 0), (0, LANES - w3.shape[2]))).reshape(w3.shape[0], -1)
    wq = pad_heads(w_q_b.reshape(-1, N_MLA_HEADS, MLA_QK_DIM)).T.astype(BF16)
    wkv3 = w_kv_b.reshape(-1, N_MLA_HEADS, MLA_NOPE_DIM + MLA_V_DIM)
    wk = pad_heads(wkv3[:, :, :MLA_NOPE_DIM]).astype(BF16)
    wv = jnp.pad(wkv3[:, :, MLA_NOPE_DIM:], ((0, 0), (0, 0), (0, MLA_V_ROWS - MLA_V_DIM)))
    wv = wv.reshape(wv.shape[0], -1).T.astype(BF16)
    vones = jnp.zeros((N_MLA_HEADS, MLA_V_ROWS, tm), F32).at[:, MLA_V_DIM].set(1.0).reshape(-1, tm)
    inv_freq = ROPE_THETA ** (-jnp.arange(0, MLA_ROPE_DIM, 2, dtype=F32) / MLA_ROPE_DIM)
    invf_col = inv_freq.reshape(-1, 1)

    row = lambda w: pl.BlockSpec((1, tm, w), lambda b, i: (b, i, 0))
    in_specs = [
        row(D),
        pl.BlockSpec((1, SUBLANES, D), lambda b, i: (b, jnp.maximum(i * hb - 1, 0), 0)),
        pl.BlockSpec((1, SUBLANES, D), lambda b, i: (b, jnp.minimum((i + 1) * hb, S // SUBLANES - 1), 0)),
        pl.BlockSpec((1, 1, tm), lambda b, i: (b, 0, i)),
        _full((1, D)), _full(wqk.shape), _full(wvt.shape), _full(wo.shape), _full(wmisc.shape), _full(wkr.shape),
        _full((1, LANES)), _full(conv_qk.shape), _full((1, 256)), _full(wq.shape), _full((1, 128)),
        _full(wk.shape), _full(wv.shape), _full(vones.shape), _full(invf_col.shape),
    ]
    out_shape = [
        jax.ShapeDtypeStruct((B, D_MLSTM, S), BF16),
        jax.ShapeDtypeStruct((B, S, D_MLSTM), BF16),
        jax.ShapeDtypeStruct((B, D_MLSTM, S), BF16),
        jax.ShapeDtypeStruct((B, S, D_MLSTM), F32),
        jax.ShapeDtypeStruct((B, S, LANES), F32),
        jax.ShapeDtypeStruct((B, 2 * N_GATE, S), F32),
        jax.ShapeDtypeStruct((B, N_MLA_HEADS * LANES, S), BF16),
        jax.ShapeDtypeStruct((B, S, N_MLA_HEADS * LANES), BF16),
        jax.ShapeDtypeStruct((B, N_MLA_HEADS * MLA_V_ROWS, S), BF16),
    ]
    out_specs = [
        pl.BlockSpec((1, D_MLSTM, tm), lambda b, i: (b, 0, i)),
        row(D_MLSTM),
        pl.BlockSpec((1, D_MLSTM, tm), lambda b, i: (b, 0, i)),
        row(D_MLSTM), row(LANES),
        pl.BlockSpec((1, 2 * N_GATE, tm), lambda b, i: (b, 0, i)),
        pl.BlockSpec((1, N_MLA_HEADS * LANES, tm), lambda b, i: (b, 0, i)),
        row(N_MLA_HEADS * LANES),
        pl.BlockSpec((1, N_MLA_HEADS * MLA_V_ROWS, tm), lambda b, i: (b, 0, i)),
    ]
    return pl.pallas_call(
        _inproj_kernel, grid=(B, nt), in_specs=in_specs, out_specs=out_specs, out_shape=out_shape,
        compiler_params=_cparams(("parallel", "parallel")), name="inproj",
    )(x, x, x, pos_row, g_mix.reshape(1, D), wqk, wvt, wo, wmisc, wkr, bg, conv_qk, g_q_a.reshape(1, -1), wq,
      g_kv_a.reshape(1, -1), wk, wv, vones, invf_col)


def _mlstm_kernel(kf_ref, qf_ref, vf_ref, gcf_ref, gtf_ref, kb_ref, qb_ref, vb_ref, gcb_ref, gtb_ref,
                  hf_ref, hb_ref, cn_ref, m_ref):
    T = MLSTM_CHUNK
    hd = MLSTM_HEAD_DIM

    @pl.when(pl.program_id(1) == 0)
    def _():
        cn_ref[...] = jnp.zeros_like(cn_ref)
        m_ref[...] = jnp.zeros_like(m_ref)

    s_idx = lax.broadcasted_iota(I32, (T, T), 0)
    t_idx = lax.broadcasted_iota(I32, (T, T), 1)
    ones_rows = (lax.broadcasted_iota(I32, (MLSTM_PAD_ROWS, T), 0) == 0).astype(BF16)
    neg_inf = F32(-jnp.inf)

    streams = []
    for d, (k_ref, qT_ref, vT_ref, gc_ref, gt_ref, h_ref) in enumerate(
            ((kf_ref, qf_ref, vf_ref, gcf_ref, gtf_ref, hf_ref),
             (kb_ref, qb_ref, vb_ref, gcb_ref, gtb_ref, hb_ref))):
        end = T - 1 if d == 0 else 0
        gc = gc_ref[0]
        gt = gt_ref[0]
        for h in range(N_MLSTM_HEADS):
            st = d * N_MLSTM_HEADS + h
            la = d * 8 + h
            a_row = gt[la:la + 1]
            b_row = gt[la + 4:la + 5]
            m_prev = m_ref[st][0:1, 0:1]
            kh = k_ref[0, :, h * hd:(h + 1) * hd]
            qTh = qT_ref[0, h * hd:(h + 1) * hd, :]
            v_ext = jnp.concatenate([vT_ref[0, h * hd:(h + 1) * hd, :], ones_rows], axis=0)
            cn = cn_ref[st]
            m_run = jnp.maximum(gt[N_GATE + la:N_GATE + la + 1], m_prev)
            m_end = m_run[:, end:end + 1]
            vw = (v_ext.astype(F32) * jnp.exp(a_row - m_end)).astype(BF16)
            streams.append(dict(
                st=st, d=d, h=h, h_ref=h_ref, a_col=gc[:, la:la + 1], b_row=b_row, m_prev=m_prev,
                m_run=m_run, m_end=m_end, g_sum=b_row[:, end:end + 1], cn=cn, v_ext=v_ext,
                s=_dot(kh, qTh),
                qcn=_dot(cn.astype(BF16), qTh),
                upd=_dot(vw, kh)))
    for x in streams:
        mask = (s_idx <= t_idx) if x["d"] == 0 else (s_idx >= t_idx)
        e = jnp.exp(jnp.where(mask, x["a_col"] - x["m_run"], neg_inf))
        x["pw"] = (e * x["s"]).astype(BF16)
    for x in streams:
        x["pv"] = _dot(x["v_ext"], x["pw"])
    for x in streams:
        st, h = x["st"], x["h"]
        iw = jnp.exp(x["m_prev"] - x["m_run"])
        num = iw * x["qcn"][:hd] + x["pv"][:hd]
        den = iw * x["qcn"][hd:hd + 1] + x["pv"][hd:hd + 1]
        floor = jnp.exp(-(x["b_row"] + x["m_run"]))
        x["h_ref"][0, h * hd:(h + 1) * hd, :] = num / jnp.maximum(jnp.abs(den), floor)
        cn_ref[st] = jnp.exp(x["m_prev"] - x["m_end"]) * x["cn"] + x["upd"]
        m_ref[st] = jnp.broadcast_to(x["g_sum"] + x["m_end"], (SUBLANES, LANES))


def _mlstm(k, qT, vT, gc, gt):
    B, S, _ = k.shape
    T = MLSTM_CHUNK
    nc = S // T
    fwd = lambda b, j: (b, j, 0)
    bwd = lambda b, j: (b, nc - 1 - j, 0)
    fwd_t = lambda b, j: (b, 0, j)
    bwd_t = lambda b, j: (b, 0, nc - 1 - j)

    def specs(row_map, col_map):
        return [pl.BlockSpec((1, T, D_MLSTM), row_map), pl.BlockSpec((1, D_MLSTM, T), col_map),
                pl.BlockSpec((1, D_MLSTM, T), col_map), pl.BlockSpec((1, T, LANES), row_map),
                pl.BlockSpec((1, 2 * N_GATE, T), col_map)]

    n_state = 2 * N_MLSTM_HEADS
    return pl.pallas_call(
        _mlstm_kernel, grid=(B, nc),
        in_specs=specs(fwd, fwd_t) + specs(bwd, bwd_t),
        out_specs=[pl.BlockSpec((1, D_MLSTM, T), fwd_t), pl.BlockSpec((1, D_MLSTM, T), bwd_t)],
        out_shape=[jax.ShapeDtypeStruct((B, D_MLSTM, S), F32)] * 2,
        scratch_shapes=[pltpu.VMEM((n_state, MLSTM_HEAD_DIM + MLSTM_PAD_ROWS, MLSTM_HEAD_DIM), F32),
                        pltpu.VMEM((n_state, SUBLANES, LANES), F32)],
        compiler_params=_cparams(("parallel", "arbitrary")), name="mlstm",
    )(k, qT, vT, gc, gt, k, qT, vT, gc, gt)


def _mla_kernel(qT_ref, k_ref, vT_ref, g_ref, o_ref, sa_sc, sb_sc, ma_sc, mb_sc):
    t = pl.program_id(0)
    c = (MLA_QK_DIM ** -0.5) * math.log2(math.e)
    S = k_ref.shape[1]
    tq = qT_ref.shape[2]
    ck = S // MLA_CHUNKS

    @pl.when(t == 0)
    def _():
        sb_sc[...] = jnp.zeros(sb_sc.shape, F32)
        mb_sc[...] = jnp.zeros(mb_sc.shape, F32)

    def step(s_cur, m_cur, s_prev, m_prev_ref):
        m_prev = [m_prev_ref[h][0:1] for h in range(2)]
        m_run = [jnp.full((1, tq), -jnp.inf, F32) for _ in range(2)]
        accs = [None, None]
        for j in range(MLA_CHUNKS):
            rows = slice(j * ck, (j + 1) * ck)
            for h in range(2):
                s = _dot(k_ref[0, rows, h * LANES:(h + 1) * LANES], qT_ref[0, h * LANES:(h + 1) * LANES, :])
                m_run[h] = jnp.maximum(m_run[h], jnp.max(s, axis=0, keepdims=True))
                s_cur[h, rows, :] = s
            for h in range(2):
                p = jnp.exp2((s_prev[h, rows, :] - m_prev[h]) * c).astype(BF16)
                part = _dot(vT_ref[0, h * MLA_V_ROWS:(h + 1) * MLA_V_ROWS, rows], p)
                accs[h] = part if j == 0 else accs[h] + part
        for h in range(2):
            m_cur[h] = jnp.broadcast_to(m_run[h], (SUBLANES, tq))
        ys = []
        for h in range(2):
            acc = accs[h]
            o = acc[:MLA_V_DIM] / acc[MLA_V_DIM:MLA_V_DIM + 1]
            ms = jnp.mean(o * o, axis=0, keepdims=True)
            ys.append(o * lax.rsqrt(ms + EPS))
        o_ref[0] = jnp.concatenate(ys, axis=0).T * g_ref[...]

    @pl.when(t % 2 == 0)
    def _():
        step(sa_sc, ma_sc, sb_sc, mb_sc)

    @pl.when(t % 2 == 1)
    def _():
        step(sb_sc, mb_sc, sa_sc, ma_sc)


def _mla(qT, kcat, vT, g_head_mla):
    B, S, _ = kcat.shape
    tq = TQ_MLA
    npair = N_MLA_HEADS // 2
    nq = S // tq
    nt = B * npair * nq

    def tile(t):
        return t // (npair * nq), (t // nq) % npair, t % nq

    def a_map(f):
        return lambda t: f(*tile(jnp.minimum(t, nt - 1)))

    def b_map(f):
        return lambda t: f(*tile(jnp.maximum(t - 1, 0)))

    return pl.pallas_call(
        _mla_kernel, grid=(nt + 1,),
        in_specs=[pl.BlockSpec((1, 2 * LANES, tq), a_map(lambda b, p, i: (b, p, i))),
                  pl.BlockSpec((1, S, 2 * LANES), a_map(lambda b, p, i: (b, 0, p))),
                  pl.BlockSpec((1, 2 * MLA_V_ROWS, S), b_map(lambda b, p, i: (b, p, 0))),
                  pl.BlockSpec((1, LANES), b_map(lambda b, p, i: (0, p)))],
        out_specs=pl.BlockSpec((1, tq, LANES), b_map(lambda b, p, i: (b, i, p))),
        out_shape=jax.ShapeDtypeStruct((B, S, N_MLA_HEADS * MLA_V_DIM), F32),
        scratch_shapes=[pltpu.VMEM((2, S, tq), F32), pltpu.VMEM((2, S, tq), F32),
                        pltpu.VMEM((2, SUBLANES, tq), F32), pltpu.VMEM((2, SUBLANES, tq), F32)],
        compiler_params=_cparams(("arbitrary",)), name="mla_attn",
    )(qT, kcat, vT, g_head_mla.reshape(1, -1))


def _memkv_kernel(mem_ref, g_ref, wk_ref, wv_ref, k_ref, v_ref):
    mn = _rms(mem_ref[0], g_ref[...]).astype(BF16)
    k_ref[0] = _dot(mn, wk_ref[...]).astype(BF16)
    v_ref[0] = _dot(mn, wv_ref[...]).astype(BF16)


def _memkv(mem, g_mem_kv, w_k, w_v):
    B, M, D = mem.shape
    blk = pl.BlockSpec((1, M, D), lambda b: (b, 0, 0))
    return pl.pallas_call(
        _memkv_kernel, grid=(B,),
        in_specs=[blk, _full((1, D)), _full(w_k.shape), _full(w_v.shape)],
        out_specs=[blk, blk], out_shape=[jax.ShapeDtypeStruct((B, M, D), BF16)] * 2,
        compiler_params=_cparams(("parallel",)), name="memkv",
    )(mem, g_mem_kv.reshape(1, D), w_k.astype(BF16), w_v.astype(BF16))


def _post_kernel(x_ref, hf_ref, hb_ref, om_ref, ya_ref, km_ref, vm_ref, ghm_ref, wout_ref, gmx_ref, wmq_ref,
                 wmo_ref, gffn_ref, wrh_ref, wrl_ref, x2_ref, hn_ref, aff_ref, afft_ref):
    tm = x_ref.shape[1]
    D = x_ref.shape[2]
    hr = tm // POST_SPLIT
    subs = [slice(i * hr, (i + 1) * hr) for i in range(POST_SPLIT)]
    ghm = ghm_ref[...]
    dm = D // N_MEM_HEADS
    heads = [slice(h * dm, (h + 1) * dm) for h in range(N_MEM_HEADS)]

    ycat = []
    for r in subs:
        hm = (hf_ref[0, :, r] + hb_ref[0, :, r]).T
        gate = _sigmoid(om_ref[0, r])
        parts = []
        for h in range(N_MLSTM_HEADS):
            sl = slice(h * MLSTM_HEAD_DIM, (h + 1) * MLSTM_HEAD_DIM)
            parts.append(gate[:, sl] * _rms(hm[:, sl], ghm[:, sl]))
        ycat.append(jnp.concatenate(parts + [ya_ref[0, r]], axis=1).astype(BF16))
    x1 = [x_ref[0, r] + _dot(y, wout_ref[...]) for r, y in zip(subs, ycat)]

    xn = [_rms(v, gmx_ref[...]).astype(BF16) for v in x1]
    qm = [_dot(v, wmq_ref[...]).astype(BF16) for v in xn]
    km = km_ref[0]
    vm = vm_ref[0]
    scores = [[_dot_nt(q[:, sl], km[:, sl]) * (dm ** -0.5) for sl in heads] for q in qm]
    probs = []
    for per_sub in scores:
        ps = []
        for s in per_sub:
            e = jnp.exp(s - jnp.max(s, axis=1, keepdims=True))
            ps.append((e / jnp.sum(e, axis=1, keepdims=True)).astype(BF16))
        probs.append(ps)
    om = [jnp.concatenate([_dot(p, vm[:, sl]) for p, sl in zip(ps, heads)], axis=1).astype(BF16) for ps in probs]
    x2 = [v + _dot(o, wmo_ref[...]) for v, o in zip(x1, om)]

    hns = []
    for r, v in zip(subs, x2):
        x2_ref[0, r] = v
        hn = _rms(v, gffn_ref[...])
        hn_ref[0, r] = hn.reshape(hr, SUBLANES, D // SUBLANES)
        hi = hn.astype(BF16)
        hns.append((hi, (hn - hi.astype(F32)).astype(BF16)))
    wrh = wrh_ref[...]
    logits = [_dot(hi, wrh) + (_dot(hi, wrl_ref[...]) + _dot(lo, wrh)) for hi, lo in hns]
    valid = lax.broadcasted_iota(I32, (hr, LANES), 1) < N_EXPERTS
    for r, lg in zip(subs, logits):
        lg = jnp.where(valid, lg, -jnp.inf)
        e = jnp.exp(lg - jnp.max(lg, axis=1, keepdims=True))
        aff = e / jnp.sum(e, axis=1, keepdims=True)
        aff_ref[0, r] = aff
        afft_ref[0, :, r] = aff.T[:N_EXPERTS]


def _post(x, hf, hb, om, ya, kmem, vmem, g_head_mlstm, w_out, g_mem_x, w_mem_q, w_mem_o, g_ffn, w_router):
    B, S, D = x.shape
    tm = TM_POST
    M = kmem.shape[1]
    wr = jnp.zeros((D, LANES), F32).at[:, :N_EXPERTS].set(w_router)
    wrh = wr.astype(BF16)
    wrl = (wr - wrh.astype(F32)).astype(BF16)
    row = lambda w: pl.BlockSpec((1, tm, w), lambda b, i: (b, i, 0))
    col = lambda w: pl.BlockSpec((1, w, tm), lambda b, i: (b, 0, i))
    mem_spec = pl.BlockSpec((1, M, D), lambda b, i: (b, 0, 0))
    return pl.pallas_call(
        _post_kernel, grid=(B, S // tm),
        in_specs=[row(D), col(D_MLSTM), col(D_MLSTM), row(D_MLSTM), row(D_MLSTM), mem_spec, mem_spec,
                  _full((1, D_MLSTM)), _full((D, D)), _full((1, D)), _full((D, D)), _full((D, D)),
                  _full((1, D)), _full((D, LANES)), _full((D, LANES))],
        out_specs=[row(D), pl.BlockSpec((1, tm, SUBLANES, D // SUBLANES), lambda b, i: (b, i, 0, 0)), row(LANES),
                   pl.BlockSpec((1, N_EXPERTS, tm), lambda b, i: (b, 0, i))],
        out_shape=[jax.ShapeDtypeStruct((B, S, D), F32), jax.ShapeDtypeStruct((B, S, SUBLANES, D // SUBLANES), F32),
                   jax.ShapeDtypeStruct((B, S, LANES), F32), jax.ShapeDtypeStruct((B, N_EXPERTS, S), F32)],
        compiler_params=_cparams(("parallel", "parallel")), name="post_mixer",
    )(x, hf, hb, om, ya, kmem, vmem, g_head_mlstm.reshape(1, -1), w_out.astype(BF16), g_mem_x.reshape(1, D),
      w_mem_q.astype(BF16), w_mem_o.astype(BF16), g_ffn.reshape(1, D), wrh, wrl)


def _topk_kernel(aff_ref, afft_ref, affb_ref, idx_ref, gate_ref, idx_sc, gate_sc):
    S = aff_ref.shape[1]
    cap = idx_ref.shape[2]
    nblk = S // CUM_BLOCK

    def open_interval(c):
        lo, hi = c
        mid = 0.5 * (lo + hi)
        return jnp.max(jnp.where((mid > lo) & (mid < hi), 1.0, 0.0)) > 0.0

    def bisect(c):
        lo, hi = c
        mid = 0.5 * (lo + hi)
        cnt = jnp.sum((afft_ref[0] >= mid).astype(F32), axis=1, keepdims=True)
        ge = cnt >= cap
        return jnp.where(ge, mid, lo), jnp.where(ge, hi, mid)

    lo_c, hi_c = lax.while_loop(open_interval, bisect,
                                (jnp.zeros((N_EXPERTS, 1), F32), jnp.full((N_EXPERTS, 1), 2.0, F32)))
    need_c = cap - jnp.sum((afft_ref[0] >= hi_c).astype(F32), axis=1, keepdims=True)
    eye = (lax.broadcasted_iota(I32, (N_EXPERTS, LANES), 0) == lax.broadcasted_iota(I32, (N_EXPERTS, LANES), 1))
    to_row = lambda col: jnp.sum(jnp.where(eye, col, 0.0), axis=0, keepdims=True)
    lo, hi, need = to_row(lo_c), to_row(hi_c), to_row(need_c)

    r = lax.broadcasted_iota(I32, (CUM_BLOCK, CUM_BLOCK), 0)
    c = lax.broadcasted_iota(I32, (CUM_BLOCK, CUM_BLOCK), 1)
    tri = (c <= r).astype(BF16)

    def cum_body(blk, carry):
        ce, cs = carry
        rows = pl.ds(pl.multiple_of(blk * CUM_BLOCK, CUM_BLOCK), CUM_BLOCK)
        a = aff_ref[0, rows, :]
        sure = a >= hi
        tie = (a >= lo) & jnp.logical_not(sure)
        eq = tie.astype(F32)
        eq_incl = _dot(tri, eq.astype(BF16)) + ce
        sel = jnp.where(sure | (tie & (eq_incl - eq < need)), 1.0, 0.0)
        local_incl = _dot(tri, sel.astype(BF16))
        rank_local = jnp.where(sel > 0.0, local_incl - sel, -2.0).astype(BF16)
        base = lax.convert_element_type(blk * CUM_BLOCK, F32)
        for e in range(N_EXPERTS):
            slot_local = jnp.clip(j_row - cs[:, e:e + 1], -1.0, CUM_BLOCK + 1.0).astype(BF16)
            onehot = jnp.where(rank_local[:, e:e + 1] == slot_local, one_bf, zero_bf)
            a_row = affb_ref[0, e, pl.ds(blk, 1), :]
            a_hi = a_row.astype(BF16).astype(F32)
            a_mid = (a_row - a_hi).astype(BF16).astype(F32)
            a_lo = a_row - a_hi - a_mid
            lhs = jnp.concatenate([tok_rows, a_hi, a_mid, a_lo, pad_rows], axis=0).astype(BF16)
            hit = _dot(lhs, onehot)
            idx_sc[e:e + 1, :] += hit[0:1] + base * hit[1:2]
            gate_sc[e:e + 1, :] += hit[2:3] + hit[3:4] + hit[4:5]
        return eq_incl[CUM_BLOCK - 1:CUM_BLOCK], cs + local_incl[CUM_BLOCK - 1:CUM_BLOCK]

    j_row = lax.broadcasted_iota(I32, (1, cap), 1).astype(F32)
    tok = lax.broadcasted_iota(I32, (1, CUM_BLOCK), 1).astype(F32)
    tok_rows = jnp.concatenate([tok, jnp.ones((1, CUM_BLOCK), F32)], axis=0)
    pad_rows = jnp.zeros((SUBLANES - 5, CUM_BLOCK), F32)
    one_bf = jnp.ones((CUM_BLOCK, cap), BF16)
    zero_bf = jnp.zeros((CUM_BLOCK, cap), BF16)
    idx_sc[...] = jnp.zeros_like(idx_sc)
    gate_sc[...] = jnp.zeros_like(gate_sc)
    zero = jnp.zeros((1, LANES), F32)
    lax.fori_loop(0, nblk, cum_body, (zero, zero))
    idx_ref[0] = idx_sc[...].astype(I32)
    gate_ref[0] = gate_sc[...]


def _topk(aff, aff_t, cap):
    B, S, _ = aff.shape
    nblk = S // CUM_BLOCK
    out = pl.BlockSpec((1, N_EXPERTS, cap), lambda b: (b, 0, 0))
    return pl.pallas_call(
        _topk_kernel, grid=(B,),
        in_specs=[pl.BlockSpec((1, S, LANES), lambda b: (b, 0, 0)),
                  pl.BlockSpec((1, N_EXPERTS, S), lambda b: (b, 0, 0)),
                  pl.BlockSpec((1, N_EXPERTS, nblk, CUM_BLOCK), lambda b: (b, 0, 0, 0))],
        out_specs=[out, out],
        out_shape=[jax.ShapeDtypeStruct((B, N_EXPERTS, cap), I32), jax.ShapeDtypeStruct((B, N_EXPERTS, cap), F32)],
        scratch_shapes=[pltpu.VMEM((N_EXPERTS, cap), F32), pltpu.VMEM((N_EXPERTS, cap), F32)],
        compiler_params=_cparams(("parallel",)), name="topk",
    )(aff, aff_t, aff_t.reshape(B, N_EXPERTS, nblk, CUM_BLOCK))


def _gather_kernel(idx_ref, hn_ref, xe_ref, rows_sc):
    cap = xe_ref.shape[2]

    def body(j, _):
        i = idx_ref[0, 0, j]
        rows_sc[pl.ds(j, 1)] = hn_ref[0, pl.ds(i, 1)]
        return 0

    lax.fori_loop(0, cap, body, 0, unroll=8)
    xe_ref[0, 0] = rows_sc[...].reshape(cap, xe_ref.shape[3]).astype(BF16)


def _gather(idx3, hn, cap):
    B, S, sub, lanes = hn.shape
    E = N_EXPERTS
    D = sub * lanes
    return pl.pallas_call(
        _gather_kernel, grid=(B, E),
        in_specs=[pl.BlockSpec((1, 1, cap), lambda b, e: (b * E + e, 0, 0), memory_space=pltpu.SMEM),
                  pl.BlockSpec((1, S, sub, lanes), lambda b, e: (b, 0, 0, 0))],
        out_specs=pl.BlockSpec((1, 1, cap, D), lambda b, e: (b, e, 0, 0)),
        out_shape=jax.ShapeDtypeStruct((B, E, cap, D), BF16),
        scratch_shapes=[pltpu.VMEM((cap, sub, lanes), F32)],
        compiler_params=_cparams(("parallel", "arbitrary")), name="moe_gather",
    )(idx3, hn)


def _ffn_kernel(xe_ref, wg_ref, wu_ref, wd_ref, ye_ref, wg_sc, wu_sc, wd_sc):
    f = pl.program_id(1)
    nb = xe_ref.shape[0]

    @pl.when(f == 0)
    def _():
        ye_ref[...] = jnp.zeros_like(ye_ref)

    wg_sc[...] = wg_ref[0].astype(BF16)
    wu_sc[...] = wu_ref[0].astype(BF16)
    wd_sc[...] = wd_ref[0].astype(BF16)

    def up(b):
        xb = xe_ref[b, 0]
        return _dot(xb, wg_sc[...]), _dot(xb, wu_sc[...])

    def down(b, h1, h2):
        hid = (h1 * _sigmoid(h1) * h2).astype(BF16)
        ye_ref[b, 0] += _dot(hid, wd_sc[...])

    hs = up(0)
    for b in range(nb):
        nxt = up(b + 1) if b + 1 < nb else None
        down(b, *hs)
        hs = nxt


def _ffn(xe, w_gate, w_up, w_down):
    B, E, cap, D = xe.shape
    F = w_gate.shape[2]
    tf = TF_FFN
    return pl.pallas_call(
        _ffn_kernel, grid=(E, F // tf),
        in_specs=[pl.BlockSpec((B, 1, cap, D), lambda e, f: (0, e, 0, 0)),
                  pl.BlockSpec((1, D, tf), lambda e, f: (e, 0, f)),
                  pl.BlockSpec((1, D, tf), lambda e, f: (e, 0, f)),
                  pl.BlockSpec((1, tf, D), lambda e, f: (e, f, 0))],
        out_specs=pl.BlockSpec((B, 1, cap, D), lambda e, f: (0, e, 0, 0)),
        out_shape=jax.ShapeDtypeStruct((B, E, cap, D), F32),
        scratch_shapes=[pltpu.VMEM((D, tf), BF16), pltpu.VMEM((D, tf), BF16), pltpu.VMEM((tf, D), BF16)],
        compiler_params=_cparams(("parallel", "arbitrary")), name="moe_ffn",
    )(xe, w_gate, w_up, w_down)


def _scatter_kernel(idx_ref, gate_ref, ye_ref, out_ref, ye_sc):
    cap = ye_ref.shape[2]

    @pl.when(pl.program_id(1) == 0)
    def _():
        out_ref[...] = jnp.zeros_like(out_ref)

    ye_sc[...] = ye_ref[0, 0].reshape(ye_sc.shape)

    def body(jb, _):
        base = pl.multiple_of(jb * SCATTER_GROUP, SCATTER_GROUP)
        ids = [idx_ref[0, 0, base + u] for u in range(SCATTER_GROUP)]
        new = [out_ref[0, pl.ds(ids[u], 1)] + ye_sc[pl.ds(base + u, 1)] * gate_ref[0, 0, base + u]
               for u in range(SCATTER_GROUP)]
        for u in range(SCATTER_GROUP):
            out_ref[0, pl.ds(ids[u], 1)] = new[u]
        return 0

    lax.fori_loop(0, cap // SCATTER_GROUP, body, 0)


def _scatter(idx3, gate3, ye, S):
    B, E, cap, D = ye.shape
    lanes = D // SUBLANES
    smem = lambda: pl.BlockSpec((1, 1, cap), lambda b, e: (b * E + e, 0, 0), memory_space=pltpu.SMEM)
    return pl.pallas_call(
        _scatter_kernel, grid=(B, E),
        in_specs=[smem(), smem(), pl.BlockSpec((1, 1, cap, D), lambda b, e: (b, e, 0, 0))],
        out_specs=pl.BlockSpec((1, S, SUBLANES, lanes), lambda b, e: (b, 0, 0, 0)),
        out_shape=jax.ShapeDtypeStruct((B, S, SUBLANES, lanes), F32),
        scratch_shapes=[pltpu.VMEM((cap, SUBLANES, lanes), F32)],
        compiler_params=_cparams(("parallel", "arbitrary")), name="moe_scatter",
    )(idx3, gate3, ye)


def _final_kernel(x_ref, y_ref, g_ref, o_ref):
    o_ref[0] = _rms(x_ref[0] + y_ref[0].reshape(x_ref.shape[1:]), g_ref[...])


def _final(x2, moe, g_final):
    B, S, D = x2.shape
    row = pl.BlockSpec((1, TM_FINAL, D), lambda b, i: (b, i, 0))
    tiles = pl.BlockSpec((1, TM_FINAL) + moe.shape[2:], lambda b, i: (b, i, 0, 0))
    return pl.pallas_call(
        _final_kernel, grid=(B, S // TM_FINAL), in_specs=[row, tiles, _full((1, D))], out_specs=row,
        out_shape=jax.ShapeDtypeStruct((B, S, D), F32),
        compiler_params=_cparams(("parallel", "parallel")), name="final_norm",
    )(x2, moe, g_final.reshape(1, D))


def kernel(x, mem, positions, g_mix, w_in, b_gates, conv_qk, g_q_a, w_q_b, g_kv_a, w_kv_b, g_head_mlstm,
           g_head_mla, w_out, g_mem_x, g_mem_kv, w_mem_q, w_mem_k, w_mem_v, w_mem_o, g_ffn, w_router,
           w_exp_gate, w_exp_up, w_exp_down, g_final):
    B, S, D = x.shape
    depth = g_mix.shape[0]
    assert depth == 1, "the MoE residual is folded into the final norm kernel, which assumes one layer"
    cap = EC_CAPACITY_FACTOR * S // N_EXPERTS
    pos_row = positions.astype(F32).reshape(B, 1, S)
    for l in range(depth):
        qT_m, k_m, vT_m, om, gc, gt, qT, kcat, vT = _inproj(
            x, pos_row, g_mix[l], w_in[l], b_gates[l], conv_qk[l], g_q_a[l], w_q_b[l], g_kv_a[l], w_kv_b[l])
        hf, hb = _mlstm(k_m, qT_m, vT_m, gc, gt)
        ya = _mla(qT, kcat, vT, g_head_mla[l])
        kmem, vmem = _memkv(mem, g_mem_kv[l], w_mem_k[l], w_mem_v[l])
        x2, hn, aff, aff_t = _post(x, hf, hb, om, ya, kmem, vmem, g_head_mlstm[l], w_out[l], g_mem_x[l],
                            w_mem_q[l], w_mem_o[l], g_ffn[l], w_router[l])
        idx, gate = _topk(aff, aff_t, cap)
        idx3 = idx.reshape(B * N_EXPERTS, 1, cap)
        xe = _gather(idx3, hn, cap)
        ye = _ffn(xe, w_exp_gate[l], w_exp_up[l], w_exp_down[l])
        moe = _scatter(idx3, gate.reshape(B * N_EXPERTS, 1, cap), ye, S)
        x = x2
    return _final(x, moe, g_final)
```

```python
import functools
import math

import jax
import jax.numpy as jnp
from jax import lax
from jax.experimental import pallas as pl
from jax.experimental.pallas import tpu as pltpu

F32 = jnp.float32
BF16 = jnp.bfloat16
I32 = jnp.int32

EPS = 1e-6
N_MLSTM_HEADS = 4
MLSTM_HEAD_DIM = 128
D_MLSTM = N_MLSTM_HEADS * MLSTM_HEAD_DIM
N_MLA_HEADS = 8
MLA_NOPE_DIM = 64
MLA_ROPE_DIM = 32
MLA_QK_DIM = MLA_NOPE_DIM + MLA_ROPE_DIM
MLA_V_DIM = 64
MLA_V_ROWS = 80
ROPE_THETA = 10000.0
N_GATE = 4 * N_MLSTM_HEADS
N_MEM_HEADS = 4
N_EXPERTS = 16
EC_CAPACITY_FACTOR = 2

LANES = 128
SUBLANES = 8
VMEM_LIMIT_BYTES = 56 * 1024 * 1024

MLSTM_CHUNK = 256
MLSTM_PAD_ROWS = 16
TM_INPROJ = 256
TQ_MLA = 256
MLA_CHUNKS = 16
TM_POST = 512
POST_SPLIT = 2
TF_FFN = 512
FINAL_CHUNK = 256
CUM_BLOCK = 256
SCATTER_GROUP = 8


def _cparams(sem):
    return pltpu.CompilerParams(dimension_semantics=sem, vmem_limit_bytes=VMEM_LIMIT_BYTES)


def _dot(a, b):
    return jnp.dot(a, b, preferred_element_type=F32)


def _dot_nt(a, b):
    return lax.dot_general(a, b, (((1,), (1,)), ((), ())), preferred_element_type=F32)


def _rms(x, g):
    return x * lax.rsqrt(jnp.mean(x * x, axis=-1, keepdims=True) + EPS) * g


def _sigmoid(x):
    return 1.0 / (1.0 + jnp.exp(-x))


def _full(shape):
    return pl.BlockSpec(shape, lambda *_: (0,) * len(shape))


def _inproj_kernel(x_ref, xp_ref, xn_ref, posr_ref, gmix_ref, wqk_ref, wvt_ref, wo_ref, wmisc_ref, wkr_ref, bg_ref,
                   conv_ref, gqa_ref, wq_ref, gkva_ref, wk_ref, wv_ref, vones_ref, invfc_ref,
                   qT_m_ref, k_ref, vT_m_ref, o_ref, gc_ref, gt_ref, qT_ref, kcat_ref, vT_ref):
    tm = x_ref.shape[1]
    i = pl.program_id(1)
    last = pl.num_programs(1) - 1
    g = gmix_ref[...]
    xm = _rms(x_ref[0], g)
    xprev = _rms(xp_ref[0], g) * (i > 0).astype(F32)
    xnext = _rms(xn_ref[0], g) * (i < last).astype(F32)
    xm_bf = xm.astype(BF16)
    lhs = jnp.concatenate([xprev.astype(BF16), xm_bf, xnext.astype(BF16)], axis=0)

    pqk = _dot(lhs, wqk_ref[...])
    misc = _dot(xm_bf, wmisc_ref[...])
    vT_m = _dot_nt(wvt_ref[...], xm_bf)
    o_ref[0] = _dot(xm_bf, wo_ref[...])
    krT = _dot_nt(wkr_ref[...], xm_bf)
    cq = misc[:, :256]
    ckv = misc[:, 256:384]
    gp = misc[:, 384:512] + bg_ref[...]
    ckvn = _rms(ckv, gkva_ref[...]).astype(BF16)
    cqn = _rms(cq, gqa_ref[...]).astype(BF16)
    kn = _dot(ckvn, wk_ref[...])
    qT = _dot_nt(wq_ref[...], cqn)
    vT_a = _dot_nt(wv_ref[...], ckvn)
    vT_m_ref[0] = vT_m.astype(BF16)

    rows = tm + 2 * SUBLANES
    up = pltpu.roll(pqk, 1, 0)[SUBLANES:SUBLANES + tm]
    dn = pltpu.roll(pqk, rows - 1, 0)[SUBLANES:SUBLANES + tm]
    mid = pqk[SUBLANES:SUBLANES + tm]
    cw = conv_ref[...]
    conv = up * cw[0:1] + mid * cw[1:2] + dn * cw[2:3]
    act = conv * _sigmoid(conv)
    qT_m_ref[0] = act[:, :D_MLSTM].T.astype(BF16)
    k_ref[0] = (act[:, D_MLSTM:] * (MLSTM_HEAD_DIM ** -0.5)).astype(BF16)

    lane = lax.broadcasted_iota(I32, (tm, LANES), 1)
    is_f = ((lane >= 4) & (lane < 8)) | ((lane >= 12) & (lane < 16))
    logsig = jnp.minimum(gp, 0.0) - jnp.log1p(jnp.exp(-jnp.abs(gp)))
    lf = jnp.where(is_f, logsig, gp)
    row_id = lax.broadcasted_iota(I32, (tm, LANES), 0)

    def scan(x, op, ident, suffix):
        step = 1
        while step < tm:
            if step < SUBLANES:
                ok = (row_id + step < tm) if suffix else (row_id >= step)
                shifted = jnp.where(ok, pltpu.roll(x, tm - step if suffix else step, 0), ident)
            else:
                pad = jnp.full((step, LANES), ident, F32)
                shifted = jnp.concatenate([x[step:], pad] if suffix else [pad, x[:tm - step]], axis=0)
            x = op(x, shifted)
            step *= 2
        return x

    cum = jnp.where(lane < 8, scan(lf, jnp.add, 0.0, False), scan(lf, jnp.add, 0.0, True))
    a = lf - pltpu.roll(cum, LANES - 4, 1)
    gc = jnp.where(is_f, cum, a)
    run_max = jnp.where(lane < 8, scan(gc, jnp.maximum, -jnp.inf, False), scan(gc, jnp.maximum, -jnp.inf, True))
    table = jnp.where(lane < N_GATE, gc, pltpu.roll(run_max, N_GATE, 1))
    gc_ref[0] = table
    gt_ref[0] = table.T[:2 * N_GATE]

    ang_t = invfc_ref[...] * posr_ref[0]
    cos_t = jnp.cos(ang_t)
    sin_t = jnp.sin(ang_t)
    r0, r1, r2 = MLA_NOPE_DIM, MLA_NOPE_DIM + MLA_ROPE_DIM // 2, MLA_QK_DIM

    def rope_rows(blk):
        t1 = blk[r0:r1]
        t2 = blk[r1:r2]
        return jnp.concatenate([blk[:r0], t1 * cos_t - t2 * sin_t, t2 * cos_t + t1 * sin_t, blk[r2:]], axis=0)

    kr = rope_rows(krT).T
    for h in range(N_MLA_HEADS):
        sl = slice(h * LANES, (h + 1) * LANES)
        kcat_ref[0, :, sl] = (kn[:, sl] + kr).astype(BF16)
        qT_ref[0, sl, :] = rope_rows(qT[sl]).astype(BF16)
    vT_ref[0] = (vT_a + vones_ref[...]).astype(BF16)


def _inproj(x, pos_row, g_mix, w_in, b_gates, conv_qk, g_q_a, w_q_b, g_kv_a, w_kv_b):
    B, S, D = x.shape
    tm = TM_INPROJ
    nt = S // tm
    hb = tm // SUBLANES
    off = [0, 512, 1024, 1536, 2048, 2064, 2320, 2448, 2480]
    wqk = w_in[:, off[0]:off[2]].astype(BF16)
    wvt = w_in[:, off[2]:off[3]].T.astype(BF16)
    wo = w_in[:, off[3]:off[4]].astype(BF16)
    w_gate = w_in[:, off[4]:off[5]]
    w_cq = w_in[:, off[5]:off[6]]
    w_ckv = w_in[:, off[6]:off[7]]
    w_kr = w_in[:, off[7]:off[8]]
    wkr = jnp.zeros((LANES, D), F32).at[MLA_NOPE_DIM:MLA_QK_DIM].set(w_kr.T).astype(BF16)
    gate_blk = jnp.zeros((D, LANES), F32).at[:, :N_GATE].set(w_gate)
    wmisc = jnp.concatenate([w_cq, w_ckv, gate_blk], axis=1).astype(BF16)
    bg = jnp.zeros((1, LANES), F32).at[0, :N_GATE].set(b_gates)
    pad_heads = lambda w3: jnp.pad(w3, ((0, 0), (0, 0), (0, LANES - w3.shape[2]))).reshape(w3.shape[0], -1)
    wq = pad_heads(w_q_b.reshape(-1, N_MLA_HEADS, MLA_QK_DIM)).T.astype(BF16)
    wkv3 = w_kv_b.reshape(-1, N_MLA_HEADS, MLA_NOPE_DIM + MLA_V_DIM)
    wk = pad_heads(wkv3[:, :, :MLA_NOPE_DIM]).astype(BF16)
    wv = jnp.pad(wkv3[:, :, MLA_NOPE_DIM:], ((0, 0), (0, 0), (0, MLA_V_ROWS - MLA_V_DIM)))
    wv = wv.reshape(wv.shape[0], -1).T.astype(BF16)
    vones = jnp.zeros((N_MLA_HEADS, MLA_V_ROWS, tm), F32).at[:, MLA_V_DIM].set(1.0).reshape(-1, tm)
    inv_freq = ROPE_THETA ** (-jnp.arange(0, MLA_ROPE_DIM, 2, dtype=F32) / MLA_ROPE_DIM)
    invf_col = inv_freq.reshape(-1, 1)

    row = lambda w: pl.BlockSpec((1, tm, w), lambda b, i: (b, i, 0))
    in_specs = [
        row(D),
        pl.BlockSpec((1, SUBLANES, D), lambda b, i: (b, jnp.maximum(i * hb - 1, 0), 0)),
        pl.BlockSpec((1, SUBLANES, D), lambda b, i: (b, jnp.minimum((i + 1) * hb, S // SUBLANES - 1), 0)),
        pl.BlockSpec((1, 1, tm), lambda b, i: (b, 0, i)),
        _full((1, D)), _full(wqk.shape), _full(wvt.shape), _full(wo.shape), _full(wmisc.shape), _full(wkr.shape),
        _full((1, LANES)), _full(conv_qk.shape), _full((1, 256)), _full(wq.shape), _full((1, 128)),
        _full(wk.shape), _full(wv.shape), _full(vones.shape), _full(invf_col.shape),
    ]
    out_shape = [
        jax.ShapeDtypeStruct((B, D_MLSTM, S), BF16),
        jax.ShapeDtypeStruct((B, S, D_MLSTM), BF16),
        jax.ShapeDtypeStruct((B, D_MLSTM, S), BF16),
        jax.ShapeDtypeStruct((B, S, D_MLSTM), F32),
        jax.ShapeDtypeStruct((B, S, LANES), F32),
        jax.ShapeDtypeStruct((B, 2 * N_GATE, S), F32),
        jax.ShapeDtypeStruct((B, N_MLA_HEADS * LANES, S), BF16),
        jax.ShapeDtypeStruct((B, S, N_MLA_HEADS * LANES), BF16),
        jax.ShapeDtypeStruct((B, N_MLA_HEADS * MLA_V_ROWS, S), BF16),
    ]
    out_specs = [
        pl.BlockSpec((1, D_MLSTM, tm), lambda b, i: (b, 0, i)),
        row(D_MLSTM),
        pl.BlockSpec((1, D_MLSTM, tm), lambda b, i: (b, 0, i)),
        row(D_MLSTM), row(LANES),
        pl.BlockSpec((1, 2 * N_GATE, tm), lambda b, i: (b, 0, i)),
        pl.BlockSpec((1, N_MLA_HEADS * LANES, tm), lambda b, i: (b, 0, i)),
        row(N_MLA_HEADS * LANES),
        pl.BlockSpec((1, N_MLA_HEADS * MLA_V_ROWS, tm), lambda b, i: (b, 0, i)),
    ]
    return pl.pallas_call(
        _inproj_kernel, grid=(B, nt), in_specs=in_specs, out_specs=out_specs, out_shape=out_shape,
        compiler_params=_cparams(("parallel", "parallel")), name="inproj",
    )(x, x, x, pos_row, g_mix.reshape(1, D), wqk, wvt, wo, wmisc, wkr, bg, conv_qk, g_q_a.reshape(1, -1), wq,
      g_kv_a.reshape(1, -1), wk, wv, vones, invf_col)


def _mlstm_kernel(kf_ref, qf_ref, vf_ref, gcf_ref, gtf_ref, kb_ref, qb_ref, vb_ref, gcb_ref, gtb_ref,
                  hf_ref, hb_ref, cn_ref, m_ref):
    T = MLSTM_CHUNK
    hd = MLSTM_HEAD_DIM

    @pl.when(pl.program_id(1) == 0)
    def _():
        cn_ref[...] = jnp.zeros_like(cn_ref)
        m_ref[...] = jnp.zeros_like(m_ref)

    s_idx = lax.broadcasted_iota(I32, (T, T), 0)
    t_idx = lax.broadcasted_iota(I32, (T, T), 1)
    ones_rows = (lax.broadcasted_iota(I32, (MLSTM_PAD_ROWS, T), 0) == 0).astype(BF16)
    neg_inf = F32(-jnp.inf)

    streams = []
    for d, (k_ref, qT_ref, vT_ref, gc_ref, gt_ref, h_ref) in enumerate(
            ((kf_ref, qf_ref, vf_ref, gcf_ref, gtf_ref, hf_ref),
             (kb_ref, qb_ref, vb_ref, gcb_ref, gtb_ref, hb_ref))):
        end = T - 1 if d == 0 else 0
        gc = gc_ref[0]
        gt = gt_ref[0]
        for h in range(N_MLSTM_HEADS):
            st = d * N_MLSTM_HEADS + h
            la = d * 8 + h
            a_row = gt[la:la + 1]
            b_row = gt[la + 4:la + 5]
            m_prev = m_ref[st][0:1, 0:1]
            kh = k_ref[0, :, h * hd:(h + 1) * hd]
            qTh = qT_ref[0, h * hd:(h + 1) * hd, :]
            v_ext = jnp.concatenate([vT_ref[0, h * hd:(h + 1) * hd, :], ones_rows], axis=0)
            cn = cn_ref[st]
            m_run = jnp.maximum(gt[N_GATE + la:N_GATE + la + 1], m_prev)
            m_end = m_run[:, end:end + 1]
            vw = (v_ext.astype(F32) * jnp.exp(a_row - m_end)).astype(BF16)
            streams.append(dict(
                st=st, d=d, h=h, h_ref=h_ref, a_col=gc[:, la:la + 1], b_row=b_row, m_prev=m_prev,
                m_run=m_run, m_end=m_end, g_sum=b_row[:, end:end + 1], cn=cn, v_ext=v_ext,
                s=_dot(kh, qTh),
                qcn=_dot(cn.astype(BF16), qTh),
                upd=_dot(vw, kh)))
    for x in streams:
        mask = (s_idx <= t_idx) if x["d"] == 0 else (s_idx >= t_idx)
        e = jnp.exp(jnp.where(mask, x["a_col"] - x["m_run"], neg_inf))
        x["pw"] = (e * x["s"]).astype(BF16)
    for x in streams:
        x["pv"] = _dot(x["v_ext"], x["pw"])
    for x in streams:
        st, h = x["st"], x["h"]
        iw = jnp.exp(x["m_prev"] - x["m_run"])
        num = iw * x["qcn"][:hd] + x["pv"][:hd]
        den = iw * x["qcn"][hd:hd + 1] + x["pv"][hd:hd + 1]
        floor = jnp.exp(-(x["b_row"] + x["m_run"]))
        x["h_ref"][0, h * hd:(h + 1) * hd, :] = num / jnp.maximum(jnp.abs(den), floor)
        cn_ref[st] = jnp.exp(x["m_prev"] - x["m_end"]) * x["cn"] + x["upd"]
        m_ref[st] = jnp.broadcast_to(x["g_sum"] + x["m_end"], (SUBLANES, LANES))


def _mlstm(k, qT, vT, gc, gt):
    B, S, _ = k.shape
    T = MLSTM_CHUNK
    nc = S // T
    fwd = lambda b, j: (b, j, 0)
    bwd = lambda b, j: (b, nc - 1 - j, 0)
    fwd_t = lambda b, j: (b, 0, j)
    bwd_t = lambda b, j: (b, 0, nc - 1 - j)

    def specs(row_map, col_map):
        return [pl.BlockSpec((1, T, D_MLSTM), row_map), pl.BlockSpec((1, D_MLSTM, T), col_map),
                pl.BlockSpec((1, D_MLSTM, T), col_map), pl.BlockSpec((1, T, LANES), row_map),
                pl.BlockSpec((1, 2 * N_GATE, T), col_map)]

    n_state = 2 * N_MLSTM_HEADS
    return pl.pallas_call(
        _mlstm_kernel, grid=(B, nc),
        in_specs=specs(fwd, fwd_t) + specs(bwd, bwd_t),
        out_specs=[pl.BlockSpec((1, D_MLSTM, T), fwd_t), pl.BlockSpec((1, D_MLSTM, T), bwd_t)],
        out_shape=[jax.ShapeDtypeStruct((B, D_MLSTM, S), F32)] * 2,
        scratch_shapes=[pltpu.VMEM((n_state, MLSTM_HEAD_DIM + MLSTM_PAD_ROWS, MLSTM_HEAD_DIM), F32),
                        pltpu.VMEM((n_state, SUBLANES, LANES), F32)],
        compiler_params=_cparams(("parallel", "arbitrary")), name="mlstm",
    )(k, qT, vT, gc, gt, k, qT, vT, gc, gt)


def _mla_kernel(qT_ref, k_ref, vT_ref, g_ref, o_ref, sa_sc, sb_sc, ma_sc, mb_sc):
    t = pl.program_id(0)
    c = (MLA_QK_DIM ** -0.5) * math.log2(math.e)
    S = k_ref.shape[1]
    tq = qT_ref.shape[2]
    ck = S // MLA_CHUNKS

    @pl.when(t == 0)
    def _():
        sb_sc[...] = jnp.zeros(sb_sc.shape, F32)
        mb_sc[...] = jnp.zeros(mb_sc.shape, F32)

    def step(s_cur, m_cur, s_prev, m_prev_ref):
        m_prev = [m_prev_ref[h][0:1] for h in range(2)]
        m_run = [jnp.full((1, tq), -jnp.inf, F32) for _ in range(2)]
        accs = [None, None]
        for j in range(MLA_CHUNKS):
            rows = slice(j * ck, (j + 1) * ck)
            for h in range(2):
                s = _dot(k_ref[0, rows, h * LANES:(h + 1) * LANES], qT_ref[0, h * LANES:(h + 1) * LANES, :])
                m_run[h] = jnp.maximum(m_run[h], jnp.max(s, axis=0, keepdims=True))
                s_cur[h, rows, :] = s
            for h in range(2):
                p = jnp.exp2((s_prev[h, rows, :] - m_prev[h]) * c).astype(BF16)
                part = _dot(vT_ref[0, h * MLA_V_ROWS:(h + 1) * MLA_V_ROWS, rows], p)
                accs[h] = part if j == 0 else accs[h] + part
        for h in range(2):
            m_cur[h] = jnp.broadcast_to(m_run[h], (SUBLANES, tq))
        ys = []
        for h in range(2):
            acc = accs[h]
            o = acc[:MLA_V_DIM] / acc[MLA_V_DIM:MLA_V_DIM + 1]
            ms = jnp.mean(o * o, axis=0, keepdims=True)
            ys.append(o * lax.rsqrt(ms + EPS))
        o_ref[0] = jnp.concatenate(ys, axis=0).T * g_ref[...]

    @pl.when(t % 2 == 0)
    def _():
        step(sa_sc, ma_sc, sb_sc, mb_sc)

    @pl.when(t % 2 == 1)
    def _():
        step(sb_sc, mb_sc, sa_sc, ma_sc)


def _mla(qT, kcat, vT, g_head_mla):
    B, S, _ = kcat.shape
    tq = TQ_MLA
    npair = N_MLA_HEADS // 2
    nq = S // tq
    nt = B * npair * nq

    def tile(t):
        return t // (npair * nq), (t // nq) % npair, t % nq

    def a_map(f):
        return lambda t: f(*tile(jnp.minimum(t, nt - 1)))

    def b_map(f):
        return lambda t: f(*tile(jnp.maximum(t - 1, 0)))

    return pl.pallas_call(
        _mla_kernel, grid=(nt + 1,),
        in_specs=[pl.BlockSpec((1, 2 * LANES, tq), a_map(lambda b, p, i: (b, p, i))),
                  pl.BlockSpec((1, S, 2 * LANES), a_map(lambda b, p, i: (b, 0, p))),
                  pl.BlockSpec((1, 2 * MLA_V_ROWS, S), b_map(lambda b, p, i: (b, p, 0))),
                  pl.BlockSpec((1, LANES), b_map(lambda b, p, i: (0, p)))],
        out_specs=pl.BlockSpec((1, tq, LANES), b_map(lambda b, p, i: (b, i, p))),
        out_shape=jax.ShapeDtypeStruct((B, S, N_MLA_HEADS * MLA_V_DIM), F32),
        scratch_shapes=[pltpu.VMEM((2, S, tq), F32), pltpu.VMEM((2, S, tq), F32),
                        pltpu.VMEM((2, SUBLANES, tq), F32), pltpu.VMEM((2, SUBLANES, tq), F32)],
        compiler_params=_cparams(("arbitrary",)), name="mla_attn",
    )(qT, kcat, vT, g_head_mla.reshape(1, -1))


def _memkv_kernel(mem_ref, g_ref, wk_ref, wv_ref, k_ref, v_ref):
    mn = _rms(mem_ref[0], g_ref[...]).astype(BF16)
    k_ref[0] = _dot(mn, wk_ref[...]).astype(BF16)
    v_ref[0] = _dot(mn, wv_ref[...]).astype(BF16)


def _memkv(mem, g_mem_kv, w_k, w_v):
    B, M, D = mem.shape
    blk = pl.BlockSpec((1, M, D), lambda b: (b, 0, 0))
    return pl.pallas_call(
        _memkv_kernel, grid=(B,),
        in_specs=[blk, _full((1, D)), _full(w_k.shape), _full(w_v.shape)],
        out_specs=[blk, blk], out_shape=[jax.ShapeDtypeStruct((B, M, D), BF16)] * 2,
        compiler_params=_cparams(("parallel",)), name="memkv",
    )(mem, g_mem_kv.reshape(1, D), w_k.astype(BF16), w_v.astype(BF16))


def _post_kernel(x_ref, hf_ref, hb_ref, om_ref, ya_ref, km_ref, vm_ref, ghm_ref, wout_ref, gmx_ref, wmq_ref,
                 wmo_ref, gffn_ref, wrh_ref, wrl_ref, x2_ref, hn_ref, aff_ref, afft_ref):
    tm = x_ref.shape[1]
    D = x_ref.shape[2]
    hr = tm // POST_SPLIT
    subs = [slice(i * hr, (i + 1) * hr) for i in range(POST_SPLIT)]
    ghm = ghm_ref[...]
    dm = D // N_MEM_HEADS
    heads = [slice(h * dm, (h + 1) * dm) for h in range(N_MEM_HEADS)]

    ycat = []
    for r in subs:
        hm = (hf_ref[0, :, r] + hb_ref[0, :, r]).T
        gate = _sigmoid(om_ref[0, r])
        parts = []
        for h in range(N_MLSTM_HEADS):
            sl = slice(h * MLSTM_HEAD_DIM, (h + 1) * MLSTM_HEAD_DIM)
            parts.append(gate[:, sl] * _rms(hm[:, sl], ghm[:, sl]))
        ycat.append(jnp.concatenate(parts + [ya_ref[0, r]], axis=1).astype(BF16))
    x1 = [x_ref[0, r] + _dot(y, wout_ref[...]) for r, y in zip(subs, ycat)]

    xn = [_rms(v, gmx_ref[...]).astype(BF16) for v in x1]
    qm = [_dot(v, wmq_ref[...]).astype(BF16) for v in xn]
    km = km_ref[0]
    vm = vm_ref[0]
    scores = [[_dot_nt(q[:, sl], km[:, sl]) * (dm ** -0.5) for sl in heads] for q in qm]
    probs = []
    for per_sub in scores:
        ps = []
        for s in per_sub:
            e = jnp.exp(s - jnp.max(s, axis=1, keepdims=True))
            ps.append((e / jnp.sum(e, axis=1, keepdims=True)).astype(BF16))
        probs.append(ps)
    om = [jnp.concatenate([_dot(p, vm[:, sl]) for p, sl in zip(ps, heads)], axis=1).astype(BF16) for ps in probs]
    x2 = [v + _dot(o, wmo_ref[...]) for v, o in zip(x1, om)]

    hns = []
    for r, v in zip(subs, x2):
        x2_ref[0, r] = v
        hn = _rms(v, gffn_ref[...])
        hn_ref[0, r] = hn.reshape(hr, SUBLANES, D // SUBLANES)
        hi = hn.astype(BF16)
        hns.append((hi, (hn - hi.astype(F32)).astype(BF16)))
    wrh = wrh_ref[...]
    logits = [_dot(hi, wrh) + (_dot(hi, wrl_ref[...]) + _dot(lo, wrh)) for hi, lo in hns]
    valid = lax.broadcasted_iota(I32, (hr, LANES), 1) < N_EXPERTS
    for r, lg in zip(subs, logits):
        lg = jnp.where(valid, lg, -jnp.inf)
        e = jnp.exp(lg - jnp.max(lg, axis=1, keepdims=True))
        aff = e / jnp.sum(e, axis=1, keepdims=True)
        aff_ref[0, r] = aff
        afft_ref[0, :, r] = aff.T[:N_EXPERTS]


def _post(x, hf, hb, om, ya, kmem, vmem, g_head_mlstm, w_out, g_mem_x, w_mem_q, w_mem_o, g_ffn, w_router):
    B, S, D = x.shape
    tm = TM_POST
    M = kmem.shape[1]
    wr = jnp.zeros((D, LANES), F32).at[:, :N_EXPERTS].set(w_router)
    wrh = wr.astype(BF16)
    wrl = (wr - wrh.astype(F32)).astype(BF16)
    row = lambda w: pl.BlockSpec((1, tm, w), lambda b, i: (b, i, 0))
    col = lambda w: pl.BlockSpec((1, w, tm), lambda b, i: (b, 0, i))
    mem_spec = pl.BlockSpec((1, M, D), lambda b, i: (b, 0, 0))
    return pl.pallas_call(
        _post_kernel, grid=(B, S // tm),
        in_specs=[row(D), col(D_MLSTM), col(D_MLSTM), row(D_MLSTM), row(D_MLSTM), mem_spec, mem_spec,
                  _full((1, D_MLSTM)), _full((D, D)), _full((1, D)), _full((D, D)), _full((D, D)),
                  _full((1, D)), _full((D, LANES)), _full((D, LANES))],
        out_specs=[row(D), pl.BlockSpec((1, tm, SUBLANES, D // SUBLANES), lambda b, i: (b, i, 0, 0)), row(LANES),
                   pl.BlockSpec((1, N_EXPERTS, tm), lambda b, i: (b, 0, i))],
        out_shape=[jax.ShapeDtypeStruct((B, S, D), F32), jax.ShapeDtypeStruct((B, S, SUBLANES, D // SUBLANES), F32),
                   jax.ShapeDtypeStruct((B, S, LANES), F32), jax.ShapeDtypeStruct((B, N_EXPERTS, S), F32)],
        compiler_params=_cparams(("parallel", "parallel")), name="post_mixer",
    )(x, hf, hb, om, ya, kmem, vmem, g_head_mlstm.reshape(1, -1), w_out.astype(BF16), g_mem_x.reshape(1, D),
      w_mem_q.astype(BF16), w_mem_o.astype(BF16), g_ffn.reshape(1, D), wrh, wrl)


def _topk_kernel(aff_ref, afft_ref, affb_ref, idx_ref, gate_ref, idx_sc, gate_sc):
    S = aff_ref.shape[1]
    cap = idx_ref.shape[2]
    nblk = S // CUM_BLOCK

    def open_interval(c):
        lo, hi = c
        mid = 0.5 * (lo + hi)
        return jnp.max(jnp.where((mid > lo) & (mid < hi), 1.0, 0.0)) > 0.0

    def bisect(c):
        lo, hi = c
        mid = 0.5 * (lo + hi)
        cnt = jnp.sum((afft_ref[0] >= mid).astype(F32), axis=1, keepdims=True)
        ge = cnt >= cap
        return jnp.where(ge, mid, lo), jnp.where(ge, hi, mid)

    lo_c, hi_c = lax.while_loop(open_interval, bisect,
                                (jnp.zeros((N_EXPERTS, 1), F32), jnp.full((N_EXPERTS, 1), 2.0, F32)))
    need_c = cap - jnp.sum((afft_ref[0] >= hi_c).astype(F32), axis=1, keepdims=True)
    eye = (lax.broadcasted_iota(I32, (N_EXPERTS, LANES), 0) == lax.broadcasted_iota(I32, (N_EXPERTS, LANES), 1))
    to_row = lambda col: jnp.sum(jnp.where(eye, col, 0.0), axis=0, keepdims=True)
    lo, hi, need = to_row(lo_c), to_row(hi_c), to_row(need_c)

    r = lax.broadcasted_iota(I32, (CUM_BLOCK, CUM_BLOCK), 0)
    c = lax.broadcasted_iota(I32, (CUM_BLOCK, CUM_BLOCK), 1)
    tri = (c <= r).astype(BF16)

    def cum_body(blk, carry):
        ce, cs = carry
        rows = pl.ds(pl.multiple_of(blk * CUM_BLOCK, CUM_BLOCK), CUM_BLOCK)
        a = aff_ref[0, rows, :]
        sure = a >= hi
        tie = (a >= lo) & jnp.logical_not(sure)
        eq = tie.astype(F32)
        eq_incl = _dot(tri, eq.astype(BF16)) + ce
        sel = jnp.where(sure | (tie & (eq_incl - eq < need)), 1.0, 0.0)
        local_incl = _dot(tri, sel.astype(BF16))
        rank_local = jnp.where(sel > 0.0, local_incl - sel, -2.0).astype(BF16)
        base = lax.convert_element_type(blk * CUM_BLOCK, F32)
        for e in range(N_EXPERTS):
            slot_local = jnp.clip(j_row - cs[:, e:e + 1], -1.0, CUM_BLOCK + 1.0).astype(BF16)
            onehot = jnp.where(rank_local[:, e:e + 1] == slot_local, one_bf, zero_bf)
            a_row = affb_ref[0, e, pl.ds(blk, 1), :]
            a_hi = a_row.astype(BF16).astype(F32)
            a_mid = (a_row - a_hi).astype(BF16).astype(F32)
            a_lo = a_row - a_hi - a_mid
            lhs = jnp.concatenate([tok_rows, a_hi, a_mid, a_lo, pad_rows], axis=0).astype(BF16)
            hit = _dot(lhs, onehot)
            idx_sc[e:e + 1, :] += hit[0:1] + base * hit[1:2]
            gate_sc[e:e + 1, :] += hit[2:3] + hit[3:4] + hit[4:5]
        return eq_incl[CUM_BLOCK - 1:CUM_BLOCK], cs + local_incl[CUM_BLOCK - 1:CUM_BLOCK]

    j_row = lax.broadcasted_iota(I32, (1, cap), 1).astype(F32)
    tok = lax.broadcasted_iota(I32, (1, CUM_BLOCK), 1).astype(F32)
    tok_rows = jnp.concatenate([tok, jnp.ones((1, CUM_BLOCK), F32)], axis=0)
    pad_rows = jnp.zeros((SUBLANES - 5, CUM_BLOCK), F32)
    one_bf = jnp.ones((CUM_BLOCK, cap), BF16)
    zero_bf = jnp.zeros((CUM_BLOCK, cap), BF16)
    idx_sc[...] = jnp.zeros_like(idx_sc)
    gate_sc[...] = jnp.zeros_like(gate_sc)
    zero = jnp.zeros((1, LANES), F32)
    lax.fori_loop(0, nblk, cum_body, (zero, zero))
    idx_ref[0] = idx_sc[...].astype(I32)
    gate_ref[0] = gate_sc[...]


def _topk(aff, aff_t, cap):
    B, S, _ = aff.shape
    nblk = S // CUM_BLOCK
    out = pl.BlockSpec((1, N_EXPERTS, cap), lambda b: (b, 0, 0))
    return pl.pallas_call(
        _topk_kernel, grid=(B,),
        in_specs=[pl.BlockSpec((1, S, LANES), lambda b: (b, 0, 0)),
                  pl.BlockSpec((1, N_EXPERTS, S), lambda b: (b, 0, 0)),
                  pl.BlockSpec((1, N_EXPERTS, nblk, CUM_BLOCK), lambda b: (b, 0, 0, 0))],
        out_specs=[out, out],
        out_shape=[jax.ShapeDtypeStruct((B, N_EXPERTS, cap), I32), jax.ShapeDtypeStruct((B, N_EXPERTS, cap), F32)],
        scratch_shapes=[pltpu.VMEM((N_EXPERTS, cap), F32), pltpu.VMEM((N_EXPERTS, cap), F32)],
        compiler_params=_cparams(("parallel",)), name="topk",
    )(aff, aff_t, aff_t.reshape(B, N_EXPERTS, nblk, CUM_BLOCK))


def _gather_kernel(idx_ref, hn_ref, xe_ref, rows_sc):
    cap = xe_ref.shape[2]

    def body(j, _):
        i = idx_ref[0, 0, j]
        rows_sc[pl.ds(j, 1)] = hn_ref[0, pl.ds(i, 1)]
        return 0

    lax.fori_loop(0, cap, body, 0, unroll=8)
    xe_ref[0, 0] = rows_sc[...].reshape(cap, xe_ref.shape[3]).astype(BF16)


def _gather(idx3, hn, cap):
    B, S, sub, lanes = hn.shape
    E = N_EXPERTS
    D = sub * lanes
    return pl.pallas_call(
        _gather_kernel, grid=(B, E),
        in_specs=[pl.BlockSpec((1, 1, cap), lambda b, e: (b * E + e, 0, 0), memory_space=pltpu.SMEM),
                  pl.BlockSpec((1, S, sub, lanes), lambda b, e: (b, 0, 0, 0))],
        out_specs=pl.BlockSpec((1, 1, cap, D), lambda b, e: (b, e, 0, 0)),
        out_shape=jax.ShapeDtypeStruct((B, E, cap, D), BF16),
        scratch_shapes=[pltpu.VMEM((cap, sub, lanes), F32)],
        compiler_params=_cparams(("parallel", "arbitrary")), name="moe_gather",
    )(idx3, hn)


def _ffn_kernel(xe_ref, wg_ref, wu_ref, wd_ref, ye_ref, wg_sc, wu_sc, wd_sc):
    f = pl.program_id(1)
    nb = xe_ref.shape[0]

    @pl.when(f == 0)
    def _():
        ye_ref[...] = jnp.zeros_like(ye_ref)

    wg_sc[...] = wg_ref[0].astype(BF16)
    wu_sc[...] = wu_ref[0].astype(BF16)
    wd_sc[...] = wd_ref[0].astype(BF16)

    def up(b):
        xb = xe_ref[b, 0]
        return _dot(xb, wg_sc[...]), _dot(xb, wu_sc[...])

    def down(b, h1, h2):
        hid = (h1 * _sigmoid(h1) * h2).astype(BF16)
        ye_ref[b, 0] += _dot(hid, wd_sc[...])

    hs = up(0)
    for b in range(nb):
        nxt = up(b + 1) if b + 1 < nb else None
        down(b, *hs)
        hs = nxt


def _ffn(xe, w_gate, w_up, w_down):
    B, E, cap, D = xe.shape
    F = w_gate.shape[2]
    tf = TF_FFN
    return pl.pallas_call(
        _ffn_kernel, grid=(E, F // tf),
        in_specs=[pl.BlockSpec((B, 1, cap, D), lambda e, f: (0, e, 0, 0)),
                  pl.BlockSpec((1, D, tf), lambda e, f: (e, 0, f)),
                  pl.BlockSpec((1, D, tf), lambda e, f: (e, 0, f)),
                  pl.BlockSpec((1, tf, D), lambda e, f: (e, f, 0))],
        out_specs=pl.BlockSpec((B, 1, cap, D), lambda e, f: (0, e, 0, 0)),
        out_shape=jax.ShapeDtypeStruct((B, E, cap, D), F32),
        scratch_shapes=[pltpu.VMEM((D, tf), BF16), pltpu.VMEM((D, tf), BF16), pltpu.VMEM((tf, D), BF16)],
        compiler_params=_cparams(("parallel", "arbitrary")), name="moe_ffn",
    )(xe, w_gate, w_up, w_down)


def _scatter_final_kernel(idx_ref, gate_ref, ye_ref, g_ref, x2_hbm, out_hbm, acc_sc, ye_sc, x_buf, o_buf, in_sem,
                          out_sem):
    b = pl.program_id(0)
    e = pl.program_id(1)
    cap = ye_ref.shape[2]
    S = acc_sc.shape[0]
    ch = x_buf.shape[1]
    D = x_buf.shape[2]

    @pl.when(e == 0)
    def _():
        acc_sc[...] = jnp.zeros_like(acc_sc)

    ye_sc[...] = ye_ref[0, 0].reshape(ye_sc.shape)

    def body(jb, _):
        base = pl.multiple_of(jb * SCATTER_GROUP, SCATTER_GROUP)
        ids = [idx_ref[0, 0, base + u] for u in range(SCATTER_GROUP)]
        new = [acc_sc[pl.ds(ids[u], 1)] + ye_sc[pl.ds(base + u, 1)] * gate_ref[0, 0, base + u]
               for u in range(SCATTER_GROUP)]
        for u in range(SCATTER_GROUP):
            acc_sc[pl.ds(ids[u], 1)] = new[u]
        return 0

    lax.fori_loop(0, cap // SCATTER_GROUP, body, 0)

    def in_copy(c):
        return pltpu.make_async_copy(x2_hbm.at[b, pl.ds(c * ch, ch)], x_buf.at[c % 2], in_sem.at[c % 2])

    def out_copy(c):
        return pltpu.make_async_copy(o_buf.at[c % 2], out_hbm.at[b, pl.ds(c * ch, ch)], out_sem.at[c % 2])

    @pl.when(e == pl.num_programs(1) - 1)
    def _():
        nch = S // ch
        in_copy(0).start()
        for c in range(nch):
            in_copy(c).wait()
            if c + 1 < nch:
                in_copy(c + 1).start()
            if c >= 2:
                out_copy(c - 2).wait()
            moe = acc_sc[c * ch:(c + 1) * ch].reshape(ch, D)
            o_buf[c % 2] = _rms(x_buf[c % 2] + moe, g_ref[...])
            out_copy(c).start()
        for c in range(max(nch - 2, 0), nch):
            out_copy(c).wait()


def _scatter_final(idx3, gate3, ye, x2, g_final):
    B, E, cap, D = ye.shape
    S = x2.shape[1]
    lanes = D // SUBLANES
    smem = lambda: pl.BlockSpec((1, 1, cap), lambda b, e: (b * E + e, 0, 0), memory_space=pltpu.SMEM)
    return pl.pallas_call(
        _scatter_final_kernel, grid=(B, E),
        in_specs=[smem(), smem(), pl.BlockSpec((1, 1, cap, D), lambda b, e: (b, e, 0, 0)), _full((1, D)),
                  pl.BlockSpec(memory_space=pl.ANY)],
        out_specs=pl.BlockSpec(memory_space=pl.ANY),
        out_shape=jax.ShapeDtypeStruct((B, S, D), F32),
        scratch_shapes=[pltpu.VMEM((S, SUBLANES, lanes), F32), pltpu.VMEM((cap, SUBLANES, lanes), F32),
                        pltpu.VMEM((2, FINAL_CHUNK, D), F32), pltpu.VMEM((2, FINAL_CHUNK, D), F32),
                        pltpu.SemaphoreType.DMA((2,)), pltpu.SemaphoreType.DMA((2,))],
        compiler_params=_cparams(("arbitrary", "arbitrary")), name="moe_scatter_final",
    )(idx3, gate3, ye, g_final.reshape(1, D), x2)


def kernel(x, mem, positions, g_mix, w_in, b_gates, conv_qk, g_q_a, w_q_b, g_kv_a, w_kv_b, g_head_mlstm,
           g_head_mla, w_out, g_mem_x, g_mem_kv, w_mem_q, w_mem_k, w_mem_v, w_mem_o, g_ffn, w_router,
           w_exp_gate, w_exp_up, w_exp_down, g_final):
    B, S, D = x.shape
    depth = g_mix.shape[0]
    assert depth == 1, "the MoE residual is folded into the final norm kernel, which assumes one layer"
    cap = EC_CAPACITY_FACTOR * S // N_EXPERTS
    pos_row = positions.astype(F32).reshape(B, 1, S)
    for l in range(depth):
        qT_m, k_m, vT_m, om, gc, gt, qT, kcat, vT = _inproj(
            x, pos_row, g_mix[l], w_in[l], b_gates[l], conv_qk[l], g_q_a[l], w_q_b[l], g_kv_a[l], w_kv_b[l])
        hf, hb = _mlstm(k_m, qT_m, vT_m, gc, gt)
        ya = _mla(qT, kcat, vT, g_head_mla[l])
        kmem, vmem = _memkv(mem, g_mem_kv[l], w_mem_k[l], w_mem_v[l])
        x2, hn, aff, aff_t = _post(x, hf, hb, om, ya, kmem, vmem, g_head_mlstm[l], w_out[l], g_mem_x[l],
                            w_mem_q[l], w_mem_o[l], g_ffn[l], w_router[l])
        idx, gate = _topk(aff, aff_t, cap)
        idx3 = idx.reshape(B * N_EXPERTS, 1, cap)
        xe = _gather(idx3, hn, cap)
        ye = _ffn(xe, w_exp_gate[l], w_exp_up[l], w_exp_down[l])
        x = _scatter_final(idx3, gate.reshape(B * N_EXPERTS, 1, cap), ye, x2, g_final)
    return x
```

```python
import functools
import math

import jax
import jax.numpy as jnp
from jax import lax
from jax.experimental import pallas as pl
from jax.experimental.pallas import tpu as pltpu

F32 = jnp.float32
BF16 = jnp.bfloat16
I32 = jnp.int32

EPS = 1e-6
N_MLSTM_HEADS = 4
MLSTM_HEAD_DIM = 128
D_MLSTM = N_MLSTM_HEADS * MLSTM_HEAD_DIM
N_MLA_HEADS = 8
MLA_NOPE_DIM = 64
MLA_ROPE_DIM = 32
MLA_QK_DIM = MLA_NOPE_DIM + MLA_ROPE_DIM
MLA_V_DIM = 64
MLA_V_ROWS = 80
ROPE_THETA = 10000.0
N_GATE = 4 * N_MLSTM_HEADS
N_MEM_HEADS = 4
N_EXPERTS = 16
EC_CAPACITY_FACTOR = 2

LANES = 128
SUBLANES = 8
VMEM_LIMIT_BYTES = 56 * 1024 * 1024

MLSTM_CHUNK = 256
MLSTM_PAD_ROWS = 16
TM_INPROJ = 256
TQ_MLA = 512
MLA_CHUNKS = 16
TM_POST = 512
POST_SPLIT = 2
TF_FFN = 512
FINAL_CHUNK = 256
CUM_BLOCK = 256
SCATTER_GROUP = 8
EXPERT_GROUP = 4


def _cparams(sem):
    return pltpu.CompilerParams(dimension_semantics=sem, vmem_limit_bytes=VMEM_LIMIT_BYTES)


def _dot(a, b):
    return jnp.dot(a, b, preferred_element_type=F32)


def _dot_nt(a, b):
    return lax.dot_general(a, b, (((1,), (1,)), ((), ())), preferred_element_type=F32)


def _rms(x, g):
    return x * lax.rsqrt(jnp.mean(x * x, axis=-1, keepdims=True) + EPS) * g


def _sigmoid(x):
    return 1.0 / (1.0 + jnp.exp(-x))


def _full(shape):
    return pl.BlockSpec(shape, lambda *_: (0,) * len(shape))


def _inproj_kernel(x_ref, xp_ref, xn_ref, posr_ref, gmix_ref, wqk_ref, wvt_ref, wo_ref, wmisc_ref, wkr_ref, bg_ref,
                   conv_ref, gqa_ref, wq_ref, gkva_ref, wk_ref, wv_ref, vones_ref, invfc_ref,
                   qT_m_ref, k_ref, vT_m_ref, o_ref, gc_ref, gt_ref, qT_ref, kcat_ref, vT_ref):
    tm = x_ref.shape[1]
    i = pl.program_id(1)
    last = pl.num_programs(1) - 1
    g = gmix_ref[...]
    xm = _rms(x_ref[0], g)
    xprev = _rms(xp_ref[0], g) * (i > 0).astype(F32)
    xnext = _rms(xn_ref[0], g) * (i < last).astype(F32)
    xm_bf = xm.astype(BF16)
    lhs = jnp.concatenate([xprev.astype(BF16), xm_bf, xnext.astype(BF16)], axis=0)

    pqk = _dot(lhs, wqk_ref[...])
    misc = _dot(xm_bf, wmisc_ref[...])
    vT_m = _dot_nt(wvt_ref[...], xm_bf)
    o_ref[0] = _dot(xm_bf, wo_ref[...])
    krT = _dot_nt(wkr_ref[...], xm_bf)
    cq = misc[:, :256]
    ckv = misc[:, 256:384]
    gp = misc[:, 384:512] + bg_ref[...]
    ckvn = _rms(ckv, gkva_ref[...]).astype(BF16)
    cqn = _rms(cq, gqa_ref[...]).astype(BF16)
    kn = _dot(ckvn, wk_ref[...])
    qT = _dot_nt(wq_ref[...], cqn)
    vT_a = _dot_nt(wv_ref[...], ckvn)
    vT_m_ref[0] = vT_m.astype(BF16)

    rows = tm + 2 * SUBLANES
    up = pltpu.roll(pqk, 1, 0)[SUBLANES:SUBLANES + tm]
    dn = pltpu.roll(pqk, rows - 1, 0)[SUBLANES:SUBLANES + tm]
    mid = pqk[SUBLANES:SUBLANES + tm]
    cw = conv_ref[...]
    conv = up * cw[0:1] + mid * cw[1:2] + dn * cw[2:3]
    act = conv * _sigmoid(conv)
    qT_m_ref[0] = act[:, :D_MLSTM].T.astype(BF16)
    k_ref[0] = (act[:, D_MLSTM:] * (MLSTM_HEAD_DIM ** -0.5)).astype(BF16)

    lane = lax.broadcasted_iota(I32, (tm, LANES), 1)
    is_f = ((lane >= 4) & (lane < 8)) | ((lane >= 12) & (lane < 16))
    logsig = jnp.minimum(gp, 0.0) - jnp.log1p(jnp.exp(-jnp.abs(gp)))
    lf = jnp.where(is_f, logsig, gp)
    row_id = lax.broadcasted_iota(I32, (tm, LANES), 0)

    def scan(x, op, ident, suffix):
        step = 1
        while step < tm:
            if step < SUBLANES:
                ok = (row_id + step < tm) if suffix else (row_id >= step)
                shifted = jnp.where(ok, pltpu.roll(x, tm - step if suffix else step, 0), ident)
            else:
                pad = jnp.full((step, LANES), ident, F32)
                shifted = jnp.concatenate([x[step:], pad] if suffix else [pad, x[:tm - step]], axis=0)
            x = op(x, shifted)
            step *= 2
        return x

    cum = jnp.where(lane < 8, scan(lf, jnp.add, 0.0, False), scan(lf, jnp.add, 0.0, True))
    a = lf - pltpu.roll(cum, LANES - 4, 1)
    gc = jnp.where(is_f, cum, a)
    run_max = jnp.where(lane < 8, scan(gc, jnp.maximum, -jnp.inf, False), scan(gc, jnp.maximum, -jnp.inf, True))
    table = jnp.where(lane < N_GATE, gc, pltpu.roll(run_max, N_GATE, 1))
    gc_ref[0] = table
    gt_ref[0] = table.T[:2 * N_GATE]

    ang_t = invfc_ref[...] * posr_ref[0]
    cos_t = jnp.cos(ang_t)
    sin_t = jnp.sin(ang_t)
    r0, r1, r2 = MLA_NOPE_DIM, MLA_NOPE_DIM + MLA_ROPE_DIM // 2, MLA_QK_DIM

    def rope_rows(blk):
        t1 = blk[r0:r1]
        t2 = blk[r1:r2]
        return jnp.concatenate([blk[:r0], t1 * cos_t - t2 * sin_t, t2 * cos_t + t1 * sin_t, blk[r2:]], axis=0)

    kr = rope_rows(krT).T
    for h in range(N_MLA_HEADS):
        sl = slice(h * LANES, (h + 1) * LANES)
        kcat_ref[0, :, sl] = (kn[:, sl] + kr).astype(BF16)
        qT_ref[0, sl, :] = rope_rows(qT[sl]).astype(BF16)
    vT_ref[0] = (vT_a + vones_ref[...]).astype(BF16)


def _inproj(x, pos_row, g_mix, w_in, b_gates, conv_qk, g_q_a, w_q_b, g_kv_a, w_kv_b):
    B, S, D = x.shape
    tm = TM_INPROJ
    nt = S // tm
    hb = tm // SUBLANES
    off = [0, 512, 1024, 1536, 2048, 2064, 2320, 2448, 2480]
    wqk = w_in[:, off[0]:off[2]].astype(BF16)
    wvt = w_in[:, off[2]:off[3]].T.astype(BF16)
    wo = w_in[:, off[3]:off[4]].astype(BF16)
    w_gate = w_in[:, off[4]:off[5]]
    w_cq = w_in[:, off[5]:off[6]]
    w_ckv = w_in[:, off[6]:off[7]]
    w_kr = w_in[:, off[7]:off[8]]
    wkr = jnp.zeros((LANES, D), F32).at[MLA_NOPE_DIM:MLA_QK_DIM].set(w_kr.T).astype(BF16)
    gate_blk = jnp.zeros((D, LANES), F32).at[:, :N_GATE].set(w_gate)
    wmisc = jnp.concatenate([w_cq, w_ckv, gate_blk], axis=1).astype(BF16)
    bg = jnp.zeros((1, LANES), F32).at[0, :N_GATE].set(b_gates)
    pad_heads = lambda w3: jnp.pad(w3, ((0, 0), (0, 0), (0, LANES - w3.shape[2]))).reshape(w3.shape[0], -1)
    wq = pad_heads(w_q_b.reshape(-1, N_MLA_HEADS, MLA_QK_DIM)).T.astype(BF16)
    wkv3 = w_kv_b.reshape(-1, N_MLA_HEADS, MLA_NOPE_DIM + MLA_V_DIM)
    wk = pad_heads(wkv3[:, :, :MLA_NOPE_DIM]).astype(BF16)
    wv = jnp.pad(wkv3[:, :, MLA_NOPE_DIM:], ((0, 0), (0, 0), (0, MLA_V_ROWS - MLA_V_DIM)))
    wv = wv.reshape(wv.shape[0], -1).T.astype(BF16)
    vones = jnp.zeros((N_MLA_HEADS, MLA_V_ROWS, tm), F32).at[:, MLA_V_DIM].set(1.0).reshape(-1, tm)
    inv_freq = ROPE_THETA ** (-jnp.arange(0, MLA_ROPE_DIM, 2, dtype=F32) / MLA_ROPE_DIM)
    invf_col = inv_freq.reshape(-1, 1)

    row = lambda w: pl.BlockSpec((1, tm, w), lambda b, i: (b, i, 0))
    in_specs = [
        row(D),
        pl.BlockSpec((1, SUBLANES, D), lambda b, i: (b, jnp.maximum(i * hb - 1, 0), 0)),
        pl.BlockSpec((1, SUBLANES, D), lambda b, i: (b, jnp.minimum((i + 1) * hb, S // SUBLANES - 1), 0)),
        pl.BlockSpec((1, 1, tm), lambda b, i: (b, 0, i)),
        _full((1, D)), _full(wqk.shape), _full(wvt.shape), _full(wo.shape), _full(wmisc.shape), _full(wkr.shape),
        _full((1, LANES)), _full(conv_qk.shape), _full((1, 256)), _full(wq.shape), _full((1, 128)),
        _full(wk.shape), _full(wv.shape), _full(vones.shape), _full(invf_col.shape),
    ]
    out_shape = [
        jax.ShapeDtypeStruct((B, D_MLSTM, S), BF16),
        jax.ShapeDtypeStruct((B, S, D_MLSTM), BF16),
        jax.ShapeDtypeStruct((B, D_MLSTM, S), BF16),
        jax.ShapeDtypeStruct((B, S, D_MLSTM), F32),
        jax.ShapeDtypeStruct((B, S, LANES), F32),
        jax.ShapeDtypeStruct((B, 2 * N_GATE, S), F32),
        jax.ShapeDtypeStruct((B, N_MLA_HEADS * LANES, S), BF16),
        jax.ShapeDtypeStruct((B, S, N_MLA_HEADS * LANES), BF16),
        jax.ShapeDtypeStruct((B, N_MLA_HEADS * MLA_V_ROWS, S), BF16),
    ]
    out_specs = [
        pl.BlockSpec((1, D_MLSTM, tm), lambda b, i: (b, 0, i)),
        row(D_MLSTM),
        pl.BlockSpec((1, D_MLSTM, tm), lambda b, i: (b, 0, i)),
        row(D_MLSTM), row(LANES),
        pl.BlockSpec((1, 2 * N_GATE, tm), lambda b, i: (b, 0, i)),
        pl.BlockSpec((1, N_MLA_HEADS * LANES, tm), lambda b, i: (b, 0, i)),
        row(N_MLA_HEADS * LANES),
        pl.BlockSpec((1, N_MLA_HEADS * MLA_V_ROWS, tm), lambda b, i: (b, 0, i)),
    ]
    return pl.pallas_call(
        _inproj_kernel, grid=(B, nt), in_specs=in_specs, out_specs=out_specs, out_shape=out_shape,
        compiler_params=_cparams(("parallel", "parallel")), name="inproj",
    )(x, x, x, pos_row, g_mix.reshape(1, D), wqk, wvt, wo, wmisc, wkr, bg, conv_qk, g_q_a.reshape(1, -1), wq,
      g_kv_a.reshape(1, -1), wk, wv, vones, invf_col)


def _mlstm_kernel(kf_ref, qf_ref, vf_ref, gcf_ref, gtf_ref, kb_ref, qb_ref, vb_ref, gcb_ref, gtb_ref,
                  hf_ref, hb_ref, cn_ref, m_ref):
    T = MLSTM_CHUNK
    hd = MLSTM_HEAD_DIM

    @pl.when(pl.program_id(1) == 0)
    def _():
        cn_ref[...] = jnp.zeros_like(cn_ref)
        m_ref[...] = jnp.zeros_like(m_ref)

    s_idx = lax.broadcasted_iota(I32, (T, T), 0)
    t_idx = lax.broadcasted_iota(I32, (T, T), 1)
    ones_rows = (lax.broadcasted_iota(I32, (MLSTM_PAD_ROWS, T), 0) == 0).astype(BF16)
    neg_inf = F32(-jnp.inf)

    streams = []
    for d, (k_ref, qT_ref, vT_ref, gc_ref, gt_ref, h_ref) in enumerate(
            ((kf_ref, qf_ref, vf_ref, gcf_ref, gtf_ref, hf_ref),
             (kb_ref, qb_ref, vb_ref, gcb_ref, gtb_ref, hb_ref))):
        end = T - 1 if d == 0 else 0
        gc = gc_ref[0]
        gt = gt_ref[0]
        for h in range(N_MLSTM_HEADS):
            st = d * N_MLSTM_HEADS + h
            la = d * 8 + h
            a_row = gt[la:la + 1]
            b_row = gt[la + 4:la + 5]
            m_prev = m_ref[st][0:1, 0:1]
            kh = k_ref[0, :, h * hd:(h + 1) * hd]
            qTh = qT_ref[0, h * hd:(h + 1) * hd, :]
            v_ext = jnp.concatenate([vT_ref[0, h * hd:(h + 1) * hd, :], ones_rows], axis=0)
            cn = cn_ref[st]
            m_run = jnp.maximum(gt[N_GATE + la:N_GATE + la + 1], m_prev)
            m_end = m_run[:, end:end + 1]
            vw = (v_ext.astype(F32) * jnp.exp(a_row - m_end)).astype(BF16)
            streams.append(dict(
                st=st, d=d, h=h, h_ref=h_ref, a_col=gc[:, la:la + 1], b_row=b_row, m_prev=m_prev,
                m_run=m_run, m_end=m_end, g_sum=b_row[:, end:end + 1], cn=cn, v_ext=v_ext,
                s=_dot(kh, qTh),
                qcn=_dot(cn.astype(BF16), qTh),
                upd=_dot(vw, kh)))
    for x in streams:
        mask = (s_idx <= t_idx) if x["d"] == 0 else (s_idx >= t_idx)
        e = jnp.exp(jnp.where(mask, x["a_col"] - x["m_run"], neg_inf))
        x["pw"] = (e * x["s"]).astype(BF16)
    for x in streams:
        x["pv"] = _dot(x["v_ext"], x["pw"])
    for x in streams:
        st, h = x["st"], x["h"]
        iw = jnp.exp(x["m_prev"] - x["m_run"])
        num = iw * x["qcn"][:hd] + x["pv"][:hd]
        den = iw * x["qcn"][hd:hd + 1] + x["pv"][hd:hd + 1]
        floor = jnp.exp(-(x["b_row"] + x["m_run"]))
        x["h_ref"][0, h * hd:(h + 1) * hd, :] = num / jnp.maximum(jnp.abs(den), floor)
        cn_ref[st] = jnp.exp(x["m_prev"] - x["m_end"]) * x["cn"] + x["upd"]
        m_ref[st] = jnp.broadcast_to(x["g_sum"] + x["m_end"], (SUBLANES, LANES))


def _mlstm(k, qT, vT, gc, gt):
    B, S, _ = k.shape
    T = MLSTM_CHUNK
    nc = S // T
    fwd = lambda b, j: (b, j, 0)
    bwd = lambda b, j: (b, nc - 1 - j, 0)
    fwd_t = lambda b, j: (b, 0, j)
    bwd_t = lambda b, j: (b, 0, nc - 1 - j)

    def specs(row_map, col_map):
        return [pl.BlockSpec((1, T, D_MLSTM), row_map), pl.BlockSpec((1, D_MLSTM, T), col_map),
                pl.BlockSpec((1, D_MLSTM, T), col_map), pl.BlockSpec((1, T, LANES), row_map),
                pl.BlockSpec((1, 2 * N_GATE, T), col_map)]

    n_state = 2 * N_MLSTM_HEADS
    return pl.pallas_call(
        _mlstm_kernel, grid=(B, nc),
        in_specs=specs(fwd, fwd_t) + specs(bwd, bwd_t),
        out_specs=[pl.BlockSpec((1, D_MLSTM, T), fwd_t), pl.BlockSpec((1, D_MLSTM, T), bwd_t)],
        out_shape=[jax.ShapeDtypeStruct((B, D_MLSTM, S), F32)] * 2,
        scratch_shapes=[pltpu.VMEM((n_state, MLSTM_HEAD_DIM + MLSTM_PAD_ROWS, MLSTM_HEAD_DIM), F32),
                        pltpu.VMEM((n_state, SUBLANES, LANES), F32)],
        compiler_params=_cparams(("parallel", "arbitrary")), name="mlstm",
    )(k, qT, vT, gc, gt, k, qT, vT, gc, gt)


def _mla_kernel(qT_ref, k_ref, vT_ref, g_ref, o_ref, sa_sc, sb_sc, ma_sc, mb_sc):
    t = pl.program_id(0)
    c = (MLA_QK_DIM ** -0.5) * math.log2(math.e)
    S = k_ref.shape[1]
    tq = qT_ref.shape[2]
    ck = S // MLA_CHUNKS

    @pl.when(t == 0)
    def _():
        sb_sc[...] = jnp.zeros(sb_sc.shape, F32)
        mb_sc[...] = jnp.zeros(mb_sc.shape, F32)

    def step(s_cur, m_cur, s_prev, m_prev_ref):
        m_prev = [m_prev_ref[h][0:1] for h in range(2)]
        m_run = [jnp.full((1, tq), -jnp.inf, F32) for _ in range(2)]
        accs = [None, None]
        for j in range(MLA_CHUNKS):
            rows = slice(j * ck, (j + 1) * ck)
            for h in range(2):
                s = _dot(k_ref[0, rows, h * LANES:(h + 1) * LANES], qT_ref[0, h * LANES:(h + 1) * LANES, :])
                m_run[h] = jnp.maximum(m_run[h], jnp.max(s, axis=0, keepdims=True))
                s_cur[h, rows, :] = s
            for h in range(2):
                p = jnp.exp2((s_prev[h, rows, :] - m_prev[h]) * c).astype(BF16)
                part = _dot(vT_ref[0, h * MLA_V_ROWS:(h + 1) * MLA_V_ROWS, rows], p)
                accs[h] = part if j == 0 else accs[h] + part
        for h in range(2):
            m_cur[h] = jnp.broadcast_to(m_run[h], (SUBLANES, tq))
        ys = []
        for h in range(2):
            acc = accs[h]
            o = acc[:MLA_V_DIM] / acc[MLA_V_DIM:MLA_V_DIM + 1]
            ms = jnp.mean(o * o, axis=0, keepdims=True)
            ys.append(o * lax.rsqrt(ms + EPS))
        o_ref[0] = jnp.concatenate(ys, axis=0).T * g_ref[...]

    @pl.when(t % 2 == 0)
    def _():
        step(sa_sc, ma_sc, sb_sc, mb_sc)

    @pl.when(t % 2 == 1)
    def _():
        step(sb_sc, mb_sc, sa_sc, ma_sc)


def _mla(qT, kcat, vT, g_head_mla):
    B, S, _ = kcat.shape
    tq = TQ_MLA
    npair = N_MLA_HEADS // 2
    nq = S // tq
    nt = B * npair * nq

    def tile(t):
        return t // (npair * nq), (t // nq) % npair, t % nq

    def a_map(f):
        return lambda t: f(*tile(jnp.minimum(t, nt - 1)))

    def b_map(f):
        return lambda t: f(*tile(jnp.maximum(t - 1, 0)))

    return pl.pallas_call(
        _mla_kernel, grid=(nt + 1,),
        in_specs=[pl.BlockSpec((1, 2 * LANES, tq), a_map(lambda b, p, i: (b, p, i))),
                  pl.BlockSpec((1, S, 2 * LANES), a_map(lambda b, p, i: (b, 0, p))),
                  pl.BlockSpec((1, 2 * MLA_V_ROWS, S), b_map(lambda b, p, i: (b, p, 0))),
                  pl.BlockSpec((1, LANES), b_map(lambda b, p, i: (0, p)))],
        out_specs=pl.BlockSpec((1, tq, LANES), b_map(lambda b, p, i: (b, i, p))),
        out_shape=jax.ShapeDtypeStruct((B, S, N_MLA_HEADS * MLA_V_DIM), F32),
        scratch_shapes=[pltpu.VMEM((2, S, tq), F32), pltpu.VMEM((2, S, tq), F32),
                        pltpu.VMEM((2, SUBLANES, tq), F32), pltpu.VMEM((2, SUBLANES, tq), F32)],
        compiler_params=_cparams(("arbitrary",)), name="mla_attn",
    )(qT, kcat, vT, g_head_mla.reshape(1, -1))


def _memkv_kernel(mem_ref, g_ref, wk_ref, wv_ref, k_ref, v_ref):
    mn = _rms(mem_ref[0], g_ref[...]).astype(BF16)
    k_ref[0] = _dot(mn, wk_ref[...]).astype(BF16)
    v_ref[0] = _dot(mn, wv_ref[...]).astype(BF16)


def _memkv(mem, g_mem_kv, w_k, w_v):
    B, M, D = mem.shape
    blk = pl.BlockSpec((1, M, D), lambda b: (b, 0, 0))
    return pl.pallas_call(
        _memkv_kernel, grid=(B,),
        in_specs=[blk, _full((1, D)), _full(w_k.shape), _full(w_v.shape)],
        out_specs=[blk, blk], out_shape=[jax.ShapeDtypeStruct((B, M, D), BF16)] * 2,
        compiler_params=_cparams(("parallel",)), name="memkv",
    )(mem, g_mem_kv.reshape(1, D), w_k.astype(BF16), w_v.astype(BF16))


def _post_kernel(x_ref, hf_ref, hb_ref, om_ref, ya_ref, km_ref, vm_ref, ghm_ref, wout_ref, gmx_ref, wmq_ref,
                 wmo_ref, gffn_ref, wrh_ref, wrl_ref, x2_ref, hn_ref, aff_ref, afft_ref):
    tm = x_ref.shape[1]
    D = x_ref.shape[2]
    hr = tm // POST_SPLIT
    subs = [slice(i * hr, (i + 1) * hr) for i in range(POST_SPLIT)]
    ghm = ghm_ref[...]
    dm = D // N_MEM_HEADS
    heads = [slice(h * dm, (h + 1) * dm) for h in range(N_MEM_HEADS)]

    ycat = []
    for r in subs:
        hm = (hf_ref[0, :, r] + hb_ref[0, :, r]).T
        gate = _sigmoid(om_ref[0, r])
        parts = []
        for h in range(N_MLSTM_HEADS):
            sl = slice(h * MLSTM_HEAD_DIM, (h + 1) * MLSTM_HEAD_DIM)
            parts.append(gate[:, sl] * _rms(hm[:, sl], ghm[:, sl]))
        ycat.append(jnp.concatenate(parts + [ya_ref[0, r]], axis=1).astype(BF16))
    x1 = [x_ref[0, r] + _dot(y, wout_ref[...]) for r, y in zip(subs, ycat)]

    xn = [_rms(v, gmx_ref[...]).astype(BF16) for v in x1]
    qm = [_dot(v, wmq_ref[...]).astype(BF16) for v in xn]
    km = km_ref[0]
    vm = vm_ref[0]
    scores = [[_dot_nt(q[:, sl], km[:, sl]) * (dm ** -0.5) for sl in heads] for q in qm]
    probs = []
    for per_sub in scores:
        ps = []
        for s in per_sub:
            e = jnp.exp(s - jnp.max(s, axis=1, keepdims=True))
            ps.append((e / jnp.sum(e, axis=1, keepdims=True)).astype(BF16))
        probs.append(ps)
    om = [jnp.concatenate([_dot(p, vm[:, sl]) for p, sl in zip(ps, heads)], axis=1).astype(BF16) for ps in probs]
    x2 = [v + _dot(o, wmo_ref[...]) for v, o in zip(x1, om)]

    hns = []
    for r, v in zip(subs, x2):
        x2_ref[0, r] = v
        hn = _rms(v, gffn_ref[...])
        hn_ref[0, r] = hn.reshape(hr, SUBLANES, D // SUBLANES)
        hi = hn.astype(BF16)
        hns.append((hi, (hn - hi.astype(F32)).astype(BF16)))
    wrh = wrh_ref[...]
    logits = [_dot(hi, wrh) + (_dot(hi, wrl_ref[...]) + _dot(lo, wrh)) for hi, lo in hns]
    valid = lax.broadcasted_iota(I32, (hr, LANES), 1) < N_EXPERTS
    for r, lg in zip(subs, logits):
        lg = jnp.where(valid, lg, -jnp.inf)
        e = jnp.exp(lg - jnp.max(lg, axis=1, keepdims=True))
        aff = e / jnp.sum(e, axis=1, keepdims=True)
        aff_ref[0, r] = aff
        afft_ref[0, :, r] = aff.T[:N_EXPERTS]


def _post(x, hf, hb, om, ya, kmem, vmem, g_head_mlstm, w_out, g_mem_x, w_mem_q, w_mem_o, g_ffn, w_router):
    B, S, D = x.shape
    tm = TM_POST
    M = kmem.shape[1]
    wr = jnp.zeros((D, LANES), F32).at[:, :N_EXPERTS].set(w_router)
    wrh = wr.astype(BF16)
    wrl = (wr - wrh.astype(F32)).astype(BF16)
    row = lambda w: pl.BlockSpec((1, tm, w), lambda b, i: (b, i, 0))
    col = lambda w: pl.BlockSpec((1, w, tm), lambda b, i: (b, 0, i))
    mem_spec = pl.BlockSpec((1, M, D), lambda b, i: (b, 0, 0))
    return pl.pallas_call(
        _post_kernel, grid=(B, S // tm),
        in_specs=[row(D), col(D_MLSTM), col(D_MLSTM), row(D_MLSTM), row(D_MLSTM), mem_spec, mem_spec,
                  _full((1, D_MLSTM)), _full((D, D)), _full((1, D)), _full((D, D)), _full((D, D)),
                  _full((1, D)), _full((D, LANES)), _full((D, LANES))],
        out_specs=[row(D), pl.BlockSpec((1, tm, SUBLANES, D // SUBLANES), lambda b, i: (b, i, 0, 0)), row(LANES),
                   pl.BlockSpec((1, N_EXPERTS, tm), lambda b, i: (b, 0, i))],
        out_shape=[jax.ShapeDtypeStruct((B, S, D), F32), jax.ShapeDtypeStruct((B, S, SUBLANES, D // SUBLANES), F32),
                   jax.ShapeDtypeStruct((B, S, LANES), F32), jax.ShapeDtypeStruct((B, N_EXPERTS, S), F32)],
        compiler_params=_cparams(("parallel", "parallel")), name="post_mixer",
    )(x, hf, hb, om, ya, kmem, vmem, g_head_mlstm.reshape(1, -1), w_out.astype(BF16), g_mem_x.reshape(1, D),
      w_mem_q.astype(BF16), w_mem_o.astype(BF16), g_ffn.reshape(1, D), wrh, wrl)


def _topk_kernel(aff_ref, afft_ref, affb_ref, idx_ref, gate_ref, idx_sc, gate_sc):
    S = aff_ref.shape[1]
    cap = idx_ref.shape[2]
    nblk = S // CUM_BLOCK

    def open_interval(c):
        lo, hi = c
        mid = 0.5 * (lo + hi)
        return jnp.max(jnp.where((mid > lo) & (mid < hi), 1.0, 0.0)) > 0.0

    def bisect(c):
        lo, hi = c
        mid = 0.5 * (lo + hi)
        cnt = jnp.sum((afft_ref[0] >= mid).astype(F32), axis=1, keepdims=True)
        ge = cnt >= cap
        return jnp.where(ge, mid, lo), jnp.where(ge, hi, mid)

    lo_c, hi_c = lax.while_loop(open_interval, bisect,
                                (jnp.zeros((N_EXPERTS, 1), F32), jnp.full((N_EXPERTS, 1), 2.0, F32)))
    need_c = cap - jnp.sum((afft_ref[0] >= hi_c).astype(F32), axis=1, keepdims=True)
    eye = (lax.broadcasted_iota(I32, (N_EXPERTS, LANES), 0) == lax.broadcasted_iota(I32, (N_EXPERTS, LANES), 1))
    to_row = lambda col: jnp.sum(jnp.where(eye, col, 0.0), axis=0, keepdims=True)
    lo, hi, need = to_row(lo_c), to_row(hi_c), to_row(need_c)

    r = lax.broadcasted_iota(I32, (CUM_BLOCK, CUM_BLOCK), 0)
    c = lax.broadcasted_iota(I32, (CUM_BLOCK, CUM_BLOCK), 1)
    tri = (c <= r).astype(BF16)

    def cum_body(blk, carry):
        ce, cs = carry
        rows = pl.ds(pl.multiple_of(blk * CUM_BLOCK, CUM_BLOCK), CUM_BLOCK)
        a = aff_ref[0, rows, :]
        sure = a >= hi
        tie = (a >= lo) & jnp.logical_not(sure)
        eq = tie.astype(F32)
        eq_incl = _dot(tri, eq.astype(BF16)) + ce
        sel = jnp.where(sure | (tie & (eq_incl - eq < need)), 1.0, 0.0)
        local_incl = _dot(tri, sel.astype(BF16))
        rank_local = jnp.where(sel > 0.0, local_incl - sel, -2.0).astype(BF16)
        base = lax.convert_element_type(blk * CUM_BLOCK, F32)
        for e in range(N_EXPERTS):
            slot_local = jnp.clip(j_row - cs[:, e:e + 1], -1.0, CUM_BLOCK + 1.0).astype(BF16)
            onehot = jnp.where(rank_local[:, e:e + 1] == slot_local, one_bf, zero_bf)
            a_row = affb_ref[0, e, pl.ds(blk, 1), :]
            a_hi = a_row.astype(BF16).astype(F32)
            a_mid = (a_row - a_hi).astype(BF16).astype(F32)
            a_lo = a_row - a_hi - a_mid
            lhs = jnp.concatenate([tok_rows, a_hi, a_mid, a_lo, pad_rows], axis=0).astype(BF16)
            hit = _dot(lhs, onehot)
            idx_sc[e:e + 1, :] += hit[0:1] + base * hit[1:2]
            gate_sc[e:e + 1, :] += hit[2:3] + hit[3:4] + hit[4:5]
        return eq_incl[CUM_BLOCK - 1:CUM_BLOCK], cs + local_incl[CUM_BLOCK - 1:CUM_BLOCK]

    j_row = lax.broadcasted_iota(I32, (1, cap), 1).astype(F32)
    tok = lax.broadcasted_iota(I32, (1, CUM_BLOCK), 1).astype(F32)
    tok_rows = jnp.concatenate([tok, jnp.ones((1, CUM_BLOCK), F32)], axis=0)
    pad_rows = jnp.zeros((SUBLANES - 5, CUM_BLOCK), F32)
    one_bf = jnp.ones((CUM_BLOCK, cap), BF16)
    zero_bf = jnp.zeros((CUM_BLOCK, cap), BF16)
    idx_sc[...] = jnp.zeros_like(idx_sc)
    gate_sc[...] = jnp.zeros_like(gate_sc)
    zero = jnp.zeros((1, LANES), F32)
    lax.fori_loop(0, nblk, cum_body, (zero, zero))
    idx_ref[0] = idx_sc[...].astype(I32)
    gate_ref[0] = gate_sc[...]


def _topk(aff, aff_t, cap):
    B, S, _ = aff.shape
    nblk = S // CUM_BLOCK
    out = pl.BlockSpec((1, N_EXPERTS, cap), lambda b: (b, 0, 0))
    return pl.pallas_call(
        _topk_kernel, grid=(B,),
        in_specs=[pl.BlockSpec((1, S, LANES), lambda b: (b, 0, 0)),
                  pl.BlockSpec((1, N_EXPERTS, S), lambda b: (b, 0, 0)),
                  pl.BlockSpec((1, N_EXPERTS, nblk, CUM_BLOCK), lambda b: (b, 0, 0, 0))],
        out_specs=[out, out],
        out_shape=[jax.ShapeDtypeStruct((B, N_EXPERTS, cap), I32), jax.ShapeDtypeStruct((B, N_EXPERTS, cap), F32)],
        scratch_shapes=[pltpu.VMEM((N_EXPERTS, cap), F32), pltpu.VMEM((N_EXPERTS, cap), F32)],
        compiler_params=_cparams(("parallel",)), name="topk",
    )(aff, aff_t, aff_t.reshape(B, N_EXPERTS, nblk, CUM_BLOCK))


def _gather_kernel(idx_ref, hn_ref, xe_ref, rows_sc):
    cap = xe_ref.shape[2]
    for g in range(EXPERT_GROUP):
        def body(j, _):
            i = idx_ref[g, 0, j]
            rows_sc[pl.ds(j, 1)] = hn_ref[0, pl.ds(i, 1)]
            return 0

        lax.fori_loop(0, cap, body, 0, unroll=8)
        xe_ref[0, g] = rows_sc[...].reshape(cap, xe_ref.shape[3]).astype(BF16)


def _gather(idx3, hn, cap):
    B, S, sub, lanes = hn.shape
    E = N_EXPERTS
    G = EXPERT_GROUP
    D = sub * lanes
    return pl.pallas_call(
        _gather_kernel, grid=(B, E // G),
        in_specs=[pl.BlockSpec((G, 1, cap), lambda b, e: (b * (E // G) + e, 0, 0), memory_space=pltpu.SMEM),
                  pl.BlockSpec((1, S, sub, lanes), lambda b, e: (b, 0, 0, 0))],
        out_specs=pl.BlockSpec((1, G, cap, D), lambda b, e: (b, e, 0, 0)),
        out_shape=jax.ShapeDtypeStruct((B, E, cap, D), BF16),
        scratch_shapes=[pltpu.VMEM((cap, sub, lanes), F32)],
        compiler_params=_cparams(("parallel", "arbitrary")), name="moe_gather",
    )(idx3, hn)


def _ffn_kernel(xe_ref, wg_ref, wu_ref, wd_ref, ye_ref, wg_sc, wu_sc, wd_sc):
    f = pl.program_id(1)
    nb = xe_ref.shape[0]

    @pl.when(f == 0)
    def _():
        ye_ref[...] = jnp.zeros_like(ye_ref)

    wg_sc[...] = wg_ref[0].astype(BF16)
    wu_sc[...] = wu_ref[0].astype(BF16)
    wd_sc[...] = wd_ref[0].astype(BF16)

    def up(b):
        xb = xe_ref[b, 0]
        return _dot(xb, wg_sc[...]), _dot(xb, wu_sc[...])

    def down(b, h1, h2):
        hid = (h1 * _sigmoid(h1) * h2).astype(BF16)
        ye_ref[b, 0] += _dot(hid, wd_sc[...])

    hs = up(0)
    for b in range(nb):
        nxt = up(b + 1) if b + 1 < nb else None
        down(b, *hs)
        hs = nxt


def _ffn(xe, w_gate, w_up, w_down):
    B, E, cap, D = xe.shape
    F = w_gate.shape[2]
    tf = TF_FFN
    return pl.pallas_call(
        _ffn_kernel, grid=(E, F // tf),
        in_specs=[pl.BlockSpec((B, 1, cap, D), lambda e, f: (0, e, 0, 0)),
                  pl.BlockSpec((1, D, tf), lambda e, f: (e, 0, f)),
                  pl.BlockSpec((1, D, tf), lambda e, f: (e, 0, f)),
                  pl.BlockSpec((1, tf, D), lambda e, f: (e, f, 0))],
        out_specs=pl.BlockSpec((B, 1, cap, D), lambda e, f: (0, e, 0, 0)),
        out_shape=jax.ShapeDtypeStruct((B, E, cap, D), F32),
        scratch_shapes=[pltpu.VMEM((D, tf), BF16), pltpu.VMEM((D, tf), BF16), pltpu.VMEM((tf, D), BF16)],
        compiler_params=_cparams(("parallel", "arbitrary")), name="moe_ffn",
    )(xe, w_gate, w_up, w_down)


def _scatter_final_kernel(idx_ref, gate_ref, ye_ref, g_ref, x2_hbm, out_hbm, acc_sc, ye_sc, x_buf, o_buf, in_sem,
                          out_sem):
    b = pl.program_id(0)
    e = pl.program_id(1)
    cap = ye_ref.shape[2]
    S = acc_sc.shape[0]
    ch = x_buf.shape[1]
    D = x_buf.shape[2]

    @pl.when(e == 0)
    def _():
        acc_sc[...] = jnp.zeros_like(acc_sc)

    for g in range(EXPERT_GROUP):
        ye_sc[...] = ye_ref[0, g].reshape(ye_sc.shape)

        def body(jb, _):
            base = pl.multiple_of(jb * SCATTER_GROUP, SCATTER_GROUP)
            ids = [idx_ref[g, 0, base + u] for u in range(SCATTER_GROUP)]
            new = [acc_sc[pl.ds(ids[u], 1)] + ye_sc[pl.ds(base + u, 1)] * gate_ref[g, 0, base + u]
                   for u in range(SCATTER_GROUP)]
            for u in range(SCATTER_GROUP):
                acc_sc[pl.ds(ids[u], 1)] = new[u]
            return 0

        lax.fori_loop(0, cap // SCATTER_GROUP, body, 0)

    def in_copy(c):
        return pltpu.make_async_copy(x2_hbm.at[b, pl.ds(c * ch, ch)], x_buf.at[c % 2], in_sem.at[c % 2])

    def out_copy(c):
        return pltpu.make_async_copy(o_buf.at[c % 2], out_hbm.at[b, pl.ds(c * ch, ch)], out_sem.at[c % 2])

    @pl.when(e == pl.num_programs(1) - 1)
    def _():
        nch = S // ch
        in_copy(0).start()
        for c in range(nch):
            in_copy(c).wait()
            if c + 1 < nch:
                in_copy(c + 1).start()
            if c >= 2:
                out_copy(c - 2).wait()
            moe = acc_sc[c * ch:(c + 1) * ch].reshape(ch, D)
            o_buf[c % 2] = _rms(x_buf[c % 2] + moe, g_ref[...])
            out_copy(c).start()
        for c in range(max(nch - 2, 0), nch):
            out_copy(c).wait()


def _scatter_final(idx3, gate3, ye, x2, g_final):
    B, E, cap, D = ye.shape
    S = x2.shape[1]
    lanes = D // SUBLANES
    G = EXPERT_GROUP
    smem = lambda: pl.BlockSpec((G, 1, cap), lambda b, e: (b * (E // G) + e, 0, 0), memory_space=pltpu.SMEM)
    return pl.pallas_call(
        _scatter_final_kernel, grid=(B, E // G),
        in_specs=[smem(), smem(), pl.BlockSpec((1, G, cap, D), lambda b, e: (b, e, 0, 0)), _full((1, D)),
                  pl.BlockSpec(memory_space=pl.ANY)],
        out_specs=pl.BlockSpec(memory_space=pl.ANY),
        out_shape=jax.ShapeDtypeStruct((B, S, D), F32),
        scratch_shapes=[pltpu.VMEM((S, SUBLANES, lanes), F32), pltpu.VMEM((cap, SUBLANES, lanes), F32),
                        pltpu.VMEM((2, FINAL_CHUNK, D), F32), pltpu.VMEM((2, FINAL_CHUNK, D), F32),
                        pltpu.SemaphoreType.DMA((2,)), pltpu.SemaphoreType.DMA((2,))],
        compiler_params=_cparams(("arbitrary", "arbitrary")), name="moe_scatter_final",
    )(idx3, gate3, ye, g_final.reshape(1, D), x2)


def kernel(x, mem, positions, g_mix, w_in, b_gates, conv_qk, g_q_a, w_q_b, g_kv_a, w_kv_b, g_head_mlstm,
           g_head_mla, w_out, g_mem_x, g_mem_kv, w_mem_q, w_mem_k, w_mem_v, w_mem_o, g_ffn, w_router,
           w_exp_gate, w_exp_up, w_exp_down, g_final):
    B, S, D = x.shape
    depth = g_mix.shape[0]
    assert depth == 1, "the MoE residual is folded into the final norm kernel, which assumes one layer"
    cap = EC_CAPACITY_FACTOR * S // N_EXPERTS
    pos_row = positions.astype(F32).reshape(B, 1, S)
    for l in range(depth):
        qT_m, k_m, vT_m, om, gc, gt, qT, kcat, vT = _inproj(
            x, pos_row, g_mix[l], w_in[l], b_gates[l], conv_qk[l], g_q_a[l], w_q_b[l], g_kv_a[l], w_kv_b[l])
        hf, hb = _mlstm(k_m, qT_m, vT_m, gc, gt)
        ya = _mla(qT, kcat, vT, g_head_mla[l])
        kmem, vmem = _memkv(mem, g_mem_kv[l], w_mem_k[l], w_mem_v[l])
        x2, hn, aff, aff_t = _post(x, hf, hb, om, ya, kmem, vmem, g_head_mlstm[l], w_out[l], g_mem_x[l],
                            w_mem_q[l], w_mem_o[l], g_ffn[l], w_router[l])
        idx, gate = _topk(aff, aff_t, cap)
        idx3 = idx.reshape(B * N_EXPERTS, 1, cap)
        xe = _gather(idx3, hn, cap)
        ye = _ffn(xe, w_exp_gate[l], w_exp_up[l], w_exp_down[l])
        x = _scatter_final(idx3, gate.reshape(B * N_EXPERTS, 1, cap), ye, x2, g_final)
    return x
```

```python
import math

import jax
import jax.numpy as jnp
from jax import lax
from jax.experimental import pallas as pl
from jax.experimental.pallas import tpu as pltpu

F32 = jnp.float32
BF16 = jnp.bfloat16
I32 = jnp.int32

EPS = 1e-6
N_MLSTM_HEADS = 4
MLSTM_HEAD_DIM = 128
D_MLSTM = N_MLSTM_HEADS * MLSTM_HEAD_DIM
N_MLA_HEADS = 8
MLA_NOPE_DIM = 64
MLA_ROPE_DIM = 32
MLA_QK_DIM = MLA_NOPE_DIM + MLA_ROPE_DIM
MLA_V_DIM = 64
MLA_V_ROWS = 80
Q_LORA_RANK = 256
KV_LORA_RANK = 128
ROPE_THETA = 10000.0
N_GATE = 4 * N_MLSTM_HEADS
N_MEM_HEADS = 4
N_EXPERTS = 16
EC_CAPACITY_FACTOR = 2

LANES = 128
SUBLANES = 8
VMEM_LIMIT_BYTES = 56 * 1024 * 1024

MLSTM_CHUNK = 256
MLSTM_PAD_ROWS = 16
TM_INPROJ = 256
TQ_MLA = 256
MLA_CHUNKS = 16
TM_POST = 512
POST_SPLIT = 2
TF_FFN = 512
FINAL_CHUNK = 512
FINAL_IN_BUFS = 3
FINAL_OUT_BUFS = 2
CUM_BLOCK = 256
SCATTER_GROUP = 8


def _cparams(sem):
    return pltpu.CompilerParams(dimension_semantics=sem, vmem_limit_bytes=VMEM_LIMIT_BYTES)


def _dot(a, b):
    return jnp.dot(a, b, preferred_element_type=F32)


def _dot_nt(a, b):
    return lax.dot_general(a, b, (((1,), (1,)), ((), ())), preferred_element_type=F32)


def _rms(x, g):
    return x * lax.rsqrt(jnp.mean(x * x, axis=-1, keepdims=True) + EPS) * g


def _sigmoid(x):
    return 1.0 / (1.0 + jnp.exp(-x))


def _full(shape):
    return pl.BlockSpec(shape, lambda *_: (0,) * len(shape))


def _inproj_kernel(x_ref, xp_ref, xn_ref, posr_ref, gmix_ref, wqk_ref, wvt_ref, wo_ref, wmisc_ref, wkr_ref, bg_ref,
                   conv_ref, gqa_ref, wq_ref, gkva_ref, wk_ref, wv_ref, vones_ref, invfc_ref,
                   qT_m_ref, k_ref, vT_m_ref, o_ref, gc_ref, gt_ref, qT_ref, kcat_ref, vT_ref):
    tm = x_ref.shape[1]
    i = pl.program_id(1)
    last = pl.num_programs(1) - 1
    g = gmix_ref[...]
    xm = _rms(x_ref[0], g)
    xprev = _rms(xp_ref[0], g) * (i > 0).astype(F32)
    xnext = _rms(xn_ref[0], g) * (i < last).astype(F32)
    xm_bf = xm.astype(BF16)
    lhs = jnp.concatenate([xprev.astype(BF16), xm_bf, xnext.astype(BF16)], axis=0)

    pqk = _dot(lhs, wqk_ref[...])
    misc = _dot(xm_bf, wmisc_ref[...])
    vT_m = _dot_nt(wvt_ref[...], xm_bf)
    o_ref[0] = _dot(xm_bf, wo_ref[...])
    krT = _dot_nt(wkr_ref[...], xm_bf)
    cq = misc[:, :Q_LORA_RANK]
    ckv = misc[:, Q_LORA_RANK:Q_LORA_RANK + KV_LORA_RANK]
    gp = misc[:, Q_LORA_RANK + KV_LORA_RANK:] + bg_ref[...]
    ckvn = _rms(ckv, gkva_ref[...]).astype(BF16)
    cqn = _rms(cq, gqa_ref[...]).astype(BF16)
    kn = _dot(ckvn, wk_ref[...])
    qT = _dot_nt(wq_ref[...], cqn)
    vT_a = _dot_nt(wv_ref[...], ckvn)
    vT_m_ref[0] = vT_m.astype(BF16)

    rows = tm + 2 * SUBLANES
    up = pltpu.roll(pqk, 1, 0)[SUBLANES:SUBLANES + tm]
    dn = pltpu.roll(pqk, rows - 1, 0)[SUBLANES:SUBLANES + tm]
    mid = pqk[SUBLANES:SUBLANES + tm]
    cw = conv_ref[...]
    conv = up * cw[0:1] + mid * cw[1:2] + dn * cw[2:3]
    act = conv * _sigmoid(conv)
    qT_m_ref[0] = act[:, :D_MLSTM].T.astype(BF16)
    k_ref[0] = (act[:, D_MLSTM:] * (MLSTM_HEAD_DIM ** -0.5)).astype(BF16)

    lane = lax.broadcasted_iota(I32, (tm, LANES), 1)
    is_f = ((lane >= 4) & (lane < 8)) | ((lane >= 12) & (lane < 16))
    logsig = jnp.minimum(gp, 0.0) - jnp.log1p(jnp.exp(-jnp.abs(gp)))
    lf = jnp.where(is_f, logsig, gp)
    row_id = lax.broadcasted_iota(I32, (tm, LANES), 0)

    def scan(x, op, ident, suffix):
        step = 1
        while step < tm:
            if step < SUBLANES:
                ok = (row_id + step < tm) if suffix else (row_id >= step)
                shifted = jnp.where(ok, pltpu.roll(x, tm - step if suffix else step, 0), ident)
            else:
                pad = jnp.full((step, LANES), ident, F32)
                shifted = jnp.concatenate([x[step:], pad] if suffix else [pad, x[:tm - step]], axis=0)
            x = op(x, shifted)
            step *= 2
        return x

    cum = jnp.where(lane < 8, scan(lf, jnp.add, 0.0, False), scan(lf, jnp.add, 0.0, True))
    a = lf - pltpu.roll(cum, LANES - 4, 1)
    gc = jnp.where(is_f, cum, a)
    run_max = jnp.where(lane < 8, scan(gc, jnp.maximum, -jnp.inf, False), scan(gc, jnp.maximum, -jnp.inf, True))
    table = jnp.where(lane < N_GATE, gc, pltpu.roll(run_max, N_GATE, 1))
    gc_ref[0] = table
    gt_ref[0] = table.T[:2 * N_GATE]

    ang_t = invfc_ref[...] * posr_ref[0]
    cos_t = jnp.cos(ang_t)
    sin_t = jnp.sin(ang_t)
    r0, r1, r2 = MLA_NOPE_DIM, MLA_NOPE_DIM + MLA_ROPE_DIM // 2, MLA_QK_DIM

    def rope_rows(blk):
        t1 = blk[r0:r1]
        t2 = blk[r1:r2]
        return jnp.concatenate([blk[:r0], t1 * cos_t - t2 * sin_t, t2 * cos_t + t1 * sin_t, blk[r2:]], axis=0)

    kr = rope_rows(krT).T
    for h in range(N_MLA_HEADS):
        sl = slice(h * LANES, (h + 1) * LANES)
        kcat_ref[0, :, sl] = (kn[:, sl] + kr).astype(BF16)
        qT_ref[0, sl, :] = rope_rows(qT[sl]).astype(BF16)
    vT_ref[0] = (vT_a + vones_ref[...]).astype(BF16)


def _inproj(x, pos_row, g_mix, w_in, b_gates, conv_qk, g_q_a, w_q_b, g_kv_a, w_kv_b):
    B, S, D = x.shape
    tm = TM_INPROJ
    assert tm == MLSTM_CHUNK and S % tm == 0, "the gate scans treat one row tile as one mLSTM chunk"
    nt = S // tm
    hb = tm // SUBLANES
    sizes = [D_MLSTM, D_MLSTM, D_MLSTM, D_MLSTM, N_GATE, Q_LORA_RANK, KV_LORA_RANK, MLA_ROPE_DIM]
    off = [sum(sizes[:n]) for n in range(len(sizes) + 1)]
    assert w_in.shape[1] == off[-1]
    wqk = w_in[:, off[0]:off[2]].astype(BF16)
    wvt = w_in[:, off[2]:off[3]].T.astype(BF16)
    wo = w_in[:, off[3]:off[4]].astype(BF16)
    w_gate = w_in[:, off[4]:off[5]]
    w_cq = w_in[:, off[5]:off[6]]
    w_ckv = w_in[:, off[6]:off[7]]
    w_kr = w_in[:, off[7]:off[8]]
    wkr = jnp.zeros((LANES, D), F32).at[MLA_NOPE_DIM:MLA_QK_DIM].set(w_kr.T).astype(BF16)
    gate_blk = jnp.zeros((D, LANES), F32).at[:, :N_GATE].set(w_gate)
    wmisc = jnp.concatenate([w_cq, w_ckv, gate_blk], axis=1).astype(BF16)
    bg = jnp.zeros((1, LANES), F32).at[0, :N_GATE].set(b_gates)
    pad_heads = lambda w3: jnp.pad(w3, ((0, 0), (0, 0), (0, LANES - w3.shape[2]))).reshape(w3.shape[0], -1)
    wq = pad_heads(w_q_b.reshape(-1, N_MLA_HEADS, MLA_QK_DIM)).T.astype(BF16)
    wkv3 = w_kv_b.reshape(-1, N_MLA_HEADS, MLA_NOPE_DIM + MLA_V_DIM)
    wk = pad_heads(wkv3[:, :, :MLA_NOPE_DIM]).astype(BF16)
    wv = jnp.pad(wkv3[:, :, MLA_NOPE_DIM:], ((0, 0), (0, 0), (0, MLA_V_ROWS - MLA_V_DIM)))
    wv = wv.reshape(wv.shape[0], -1).T.astype(BF16)
    vones = jnp.zeros((N_MLA_HEADS, MLA_V_ROWS, tm), F32).at[:, MLA_V_DIM].set(1.0).reshape(-1, tm)
    inv_freq = ROPE_THETA ** (-jnp.arange(0, MLA_ROPE_DIM, 2, dtype=F32) / MLA_ROPE_DIM)
    invf_col = inv_freq.reshape(-1, 1)

    row = lambda w: pl.BlockSpec((1, tm, w), lambda b, i: (b, i, 0))
    in_specs = [
        row(D),
        pl.BlockSpec((1, SUBLANES, D), lambda b, i: (b, jnp.maximum(i * hb - 1, 0), 0)),
        pl.BlockSpec((1, SUBLANES, D), lambda b, i: (b, jnp.minimum((i + 1) * hb, S // SUBLANES - 1), 0)),
        pl.BlockSpec((1, 1, tm), lambda b, i: (b, 0, i)),
        _full((1, D)), _full(wqk.shape), _full(wvt.shape), _full(wo.shape), _full(wmisc.shape), _full(wkr.shape),
        _full((1, LANES)), _full(conv_qk.shape), _full((1, Q_LORA_RANK)), _full(wq.shape), _full((1, KV_LORA_RANK)),
        _full(wk.shape), _full(wv.shape), _full(vones.shape), _full(invf_col.shape),
    ]
    out_shape = [
        jax.ShapeDtypeStruct((B, D_MLSTM, S), BF16),
        jax.ShapeDtypeStruct((B, S, D_MLSTM), BF16),
        jax.ShapeDtypeStruct((B, D_MLSTM, S), BF16),
        jax.ShapeDtypeStruct((B, S, D_MLSTM), F32),
        jax.ShapeDtypeStruct((B, S, LANES), F32),
        jax.ShapeDtypeStruct((B, 2 * N_GATE, S), F32),
        jax.ShapeDtypeStruct((B, N_MLA_HEADS * LANES, S), BF16),
        jax.ShapeDtypeStruct((B, S, N_MLA_HEADS * LANES), BF16),
        jax.ShapeDtypeStruct((B, N_MLA_HEADS * MLA_V_ROWS, S), BF16),
    ]
    out_specs = [
        pl.BlockSpec((1, D_MLSTM, tm), lambda b, i: (b, 0, i)),
        row(D_MLSTM),
        pl.BlockSpec((1, D_MLSTM, tm), lambda b, i: (b, 0, i)),
        row(D_MLSTM), row(LANES),
        pl.BlockSpec((1, 2 * N_GATE, tm), lambda b, i: (b, 0, i)),
        pl.BlockSpec((1, N_MLA_HEADS * LANES, tm), lambda b, i: (b, 0, i)),
        row(N_MLA_HEADS * LANES),
        pl.BlockSpec((1, N_MLA_HEADS * MLA_V_ROWS, tm), lambda b, i: (b, 0, i)),
    ]
    return pl.pallas_call(
        _inproj_kernel, grid=(B, nt), in_specs=in_specs, out_specs=out_specs, out_shape=out_shape,
        compiler_params=_cparams(("parallel", "parallel")), name="inproj",
    )(x, x, x, pos_row, g_mix.reshape(1, D), wqk, wvt, wo, wmisc, wkr, bg, conv_qk, g_q_a.reshape(1, -1), wq,
      g_kv_a.reshape(1, -1), wk, wv, vones, invf_col)


def _mlstm_kernel(kf_ref, qf_ref, vf_ref, gcf_ref, gtf_ref, kb_ref, qb_ref, vb_ref, gcb_ref, gtb_ref,
                  hf_ref, hb_ref, cn_ref, m_ref):
    T = MLSTM_CHUNK
    hd = MLSTM_HEAD_DIM

    @pl.when(pl.program_id(1) == 0)
    def _():
        cn_ref[...] = jnp.zeros_like(cn_ref)
        m_ref[...] = jnp.zeros_like(m_ref)

    s_idx = lax.broadcasted_iota(I32, (T, T), 0)
    t_idx = lax.broadcasted_iota(I32, (T, T), 1)
    ones_rows = (lax.broadcasted_iota(I32, (MLSTM_PAD_ROWS, T), 0) == 0).astype(BF16)
    neg_inf = F32(-jnp.inf)

    streams = []
    for d, (k_ref, qT_ref, vT_ref, gc_ref, gt_ref, h_ref) in enumerate(
            ((kf_ref, qf_ref, vf_ref, gcf_ref, gtf_ref, hf_ref),
             (kb_ref, qb_ref, vb_ref, gcb_ref, gtb_ref, hb_ref))):
        end = T - 1 if d == 0 else 0
        gc = gc_ref[0]
        gt = gt_ref[0]
        for h in range(N_MLSTM_HEADS):
            st = d * N_MLSTM_HEADS + h
            la = d * 8 + h
            a_row = gt[la:la + 1]
            b_row = gt[la + 4:la + 5]
            m_prev = m_ref[st][0:1, 0:1]
            kh = k_ref[0, :, h * hd:(h + 1) * hd]
            qTh = qT_ref[0, h * hd:(h + 1) * hd, :]
            v_ext = jnp.concatenate([vT_ref[0, h * hd:(h + 1) * hd, :], ones_rows], axis=0)
            cn = cn_ref[st]
            m_run = jnp.maximum(gt[N_GATE + la:N_GATE + la + 1], m_prev)
            m_end = m_run[:, end:end + 1]
            vw = (v_ext.astype(F32) * jnp.exp(a_row - m_end)).astype(BF16)
            streams.append(dict(
                st=st, d=d, h=h, h_ref=h_ref, a_col=gc[:, la:la + 1], b_row=b_row, m_prev=m_prev,
                m_run=m_run, m_end=m_end, g_sum=b_row[:, end:end + 1], cn=cn, v_ext=v_ext,
                s=_dot(kh, qTh),
                qcn=_dot(cn.astype(BF16), qTh),
                upd=_dot(vw, kh)))
    for x in streams:
        mask = (s_idx <= t_idx) if x["d"] == 0 else (s_idx >= t_idx)
        e = jnp.exp(jnp.where(mask, x["a_col"] - x["m_run"], neg_inf))
        x["pw"] = (e * x["s"]).astype(BF16)
    for x in streams:
        x["pv"] = _dot(x["v_ext"], x["pw"])
    for x in streams:
        st, h = x["st"], x["h"]
        iw = jnp.exp(x["m_prev"] - x["m_run"])
        num = iw * x["qcn"][:hd] + x["pv"][:hd]
        den = iw * x["qcn"][hd:hd + 1] + x["pv"][hd:hd + 1]
        floor = jnp.exp(-(x["b_row"] + x["m_run"]))
        x["h_ref"][0, h * hd:(h + 1) * hd, :] = num / jnp.maximum(jnp.abs(den), floor)
        cn_ref[st] = jnp.exp(x["m_prev"] - x["m_end"]) * x["cn"] + x["upd"]
        m_ref[st] = jnp.broadcast_to(x["g_sum"] + x["m_end"], (SUBLANES, LANES))


def _mlstm(k, qT, vT, gc, gt):
    B, S, _ = k.shape
    T = MLSTM_CHUNK
    nc = S // T
    fwd = lambda b, j: (b, j, 0)
    bwd = lambda b, j: (b, nc - 1 - j, 0)
    fwd_t = lambda b, j: (b, 0, j)
    bwd_t = lambda b, j: (b, 0, nc - 1 - j)

    def specs(row_map, col_map):
        return [pl.BlockSpec((1, T, D_MLSTM), row_map), pl.BlockSpec((1, D_MLSTM, T), col_map),
                pl.BlockSpec((1, D_MLSTM, T), col_map), pl.BlockSpec((1, T, LANES), row_map),
                pl.BlockSpec((1, 2 * N_GATE, T), col_map)]

    n_state = 2 * N_MLSTM_HEADS
    return pl.pallas_call(
        _mlstm_kernel, grid=(B, nc),
        in_specs=specs(fwd, fwd_t) + specs(bwd, bwd_t),
        out_specs=[pl.BlockSpec((1, D_MLSTM, T), fwd_t), pl.BlockSpec((1, D_MLSTM, T), bwd_t)],
        out_shape=[jax.ShapeDtypeStruct((B, D_MLSTM, S), F32)] * 2,
        scratch_shapes=[pltpu.VMEM((n_state, MLSTM_HEAD_DIM + MLSTM_PAD_ROWS, MLSTM_HEAD_DIM), F32),
                        pltpu.VMEM((n_state, SUBLANES, LANES), F32)],
        compiler_params=_cparams(("parallel", "arbitrary")), name="mlstm",
    )(k, qT, vT, gc, gt, k, qT, vT, gc, gt)


def _mla_kernel(qT_ref, k_ref, vT_ref, g_ref, o_ref, sa_sc, sb_sc, ma_sc, mb_sc):
    t = pl.program_id(0)
    c = (MLA_QK_DIM ** -0.5) * math.log2(math.e)
    S = k_ref.shape[1]
    tq = qT_ref.shape[2]
    ck = S // MLA_CHUNKS

    @pl.when(t == 0)
    def _():
        sb_sc[...] = jnp.zeros(sb_sc.shape, F32)
        mb_sc[...] = jnp.zeros(mb_sc.shape, F32)

    def step(s_cur, m_cur, s_prev, m_prev_ref):
        m_prev = [m_prev_ref[h][0:1] for h in range(2)]
        m_run = [jnp.full((1, tq), -jnp.inf, F32) for _ in range(2)]
        accs = [None, None]
        for j in range(MLA_CHUNKS):
            rows = slice(j * ck, (j + 1) * ck)
            for h in range(2):
                s = _dot(k_ref[0, rows, h * LANES:(h + 1) * LANES], qT_ref[0, h * LANES:(h + 1) * LANES, :])
                m_run[h] = jnp.maximum(m_run[h], jnp.max(s, axis=0, keepdims=True))
                s_cur[h, rows, :] = s
            for h in range(2):
                p = jnp.exp2((s_prev[h, rows, :] - m_prev[h]) * c).astype(BF16)
                part = _dot(vT_ref[0, h * MLA_V_ROWS:(h + 1) * MLA_V_ROWS, rows], p)
                accs[h] = part if j == 0 else accs[h] + part
        for h in range(2):
            m_cur[h] = jnp.broadcast_to(m_run[h], (SUBLANES, tq))
        ys = []
        for h in range(2):
            acc = accs[h]
            o = acc[:MLA_V_DIM] / acc[MLA_V_DIM:MLA_V_DIM + 1]
            ms = jnp.mean(o * o, axis=0, keepdims=True)
            ys.append(o * lax.rsqrt(ms + EPS))
        o_ref[0] = jnp.concatenate(ys, axis=0).T * g_ref[...]

    @pl.when(t % 2 == 0)
    def _():
        step(sa_sc, ma_sc, sb_sc, mb_sc)

    @pl.when(t % 2 == 1)
    def _():
        step(sb_sc, mb_sc, sa_sc, ma_sc)


def _mla(qT, kcat, vT, g_head_mla):
    B, S, _ = kcat.shape
    tq = TQ_MLA
    npair = N_MLA_HEADS // 2
    nq = S // tq
    nt = B * npair * nq

    def tile(t):
        return t // (npair * nq), (t // nq) % npair, t % nq

    def a_map(f):
        return lambda t: f(*tile(jnp.minimum(t, nt - 1)))

    def b_map(f):
        return lambda t: f(*tile(jnp.maximum(t - 1, 0)))

    return pl.pallas_call(
        _mla_kernel, grid=(nt + 1,),
        in_specs=[pl.BlockSpec((1, 2 * LANES, tq), a_map(lambda b, p, i: (b, p, i))),
                  pl.BlockSpec((1, S, 2 * LANES), a_map(lambda b, p, i: (b, 0, p))),
                  pl.BlockSpec((1, 2 * MLA_V_ROWS, S), b_map(lambda b, p, i: (b, p, 0))),
                  pl.BlockSpec((1, LANES), b_map(lambda b, p, i: (0, p)))],
        out_specs=pl.BlockSpec((1, tq, LANES), b_map(lambda b, p, i: (b, i, p))),
        out_shape=jax.ShapeDtypeStruct((B, S, N_MLA_HEADS * MLA_V_DIM), F32),
        scratch_shapes=[pltpu.VMEM((2, S, tq), F32), pltpu.VMEM((2, S, tq), F32),
                        pltpu.VMEM((2, SUBLANES, tq), F32), pltpu.VMEM((2, SUBLANES, tq), F32)],
        compiler_params=_cparams(("arbitrary",)), name="mla_attn",
    )(qT, kcat, vT, g_head_mla.reshape(1, -1))


def _memkv_kernel(mem_ref, g_ref, wk_ref, wv_ref, k_ref, v_ref):
    mn = _rms(mem_ref[0], g_ref[...]).astype(BF16)
    k_ref[0] = _dot(mn, wk_ref[...]).astype(BF16)
    v_ref[0] = _dot(mn, wv_ref[...]).astype(BF16)


def _memkv(mem, g_mem_kv, w_k, w_v):
    B, M, D = mem.shape
    blk = pl.BlockSpec((1, M, D), lambda b: (b, 0, 0))
    return pl.pallas_call(
        _memkv_kernel, grid=(B,),
        in_specs=[blk, _full((1, D)), _full(w_k.shape), _full(w_v.shape)],
        out_specs=[blk, blk], out_shape=[jax.ShapeDtypeStruct((B, M, D), BF16)] * 2,
        compiler_params=_cparams(("parallel",)), name="memkv",
    )(mem, g_mem_kv.reshape(1, D), w_k.astype(BF16), w_v.astype(BF16))


def _post_kernel(x_ref, hf_ref, hb_ref, om_ref, ya_ref, km_ref, vm_ref, ghm_ref, wout_ref, gmx_ref, wmq_ref,
                 wmo_ref, gffn_ref, wrh_ref, wrl_ref, x2_ref, hn_ref, aff_ref, afft_ref):
    tm = x_ref.shape[1]
    D = x_ref.shape[2]
    hr = tm // POST_SPLIT
    subs = [slice(i * hr, (i + 1) * hr) for i in range(POST_SPLIT)]
    ghm = ghm_ref[...]
    dm = D // N_MEM_HEADS
    heads = [slice(h * dm, (h + 1) * dm) for h in range(N_MEM_HEADS)]

    ycat = []
    for r in subs:
        hm = (hf_ref[0, :, r] + hb_ref[0, :, r]).T
        gate = _sigmoid(om_ref[0, r])
        parts = []
        for h in range(N_MLSTM_HEADS):
            sl = slice(h * MLSTM_HEAD_DIM, (h + 1) * MLSTM_HEAD_DIM)
            parts.append(gate[:, sl] * _rms(hm[:, sl], ghm[:, sl]))
        ycat.append(jnp.concatenate(parts + [ya_ref[0, r]], axis=1).astype(BF16))
    x1 = [x_ref[0, r] + _dot(y, wout_ref[...]) for r, y in zip(subs, ycat)]

    xn = [_rms(v, gmx_ref[...]).astype(BF16) for v in x1]
    qm = [_dot(v, wmq_ref[...]).astype(BF16) for v in xn]
    km = km_ref[0]
    vm = vm_ref[0]
    scores = [[_dot_nt(q[:, sl], km[:, sl]) * (dm ** -0.5) for sl in heads] for q in qm]
    probs = []
    for per_sub in scores:
        ps = []
        for s in per_sub:
            e = jnp.exp(s - jnp.max(s, axis=1, keepdims=True))
            ps.append((e / jnp.sum(e, axis=1, keepdims=True)).astype(BF16))
        probs.append(ps)
    om = [jnp.concatenate([_dot(p, vm[:, sl]) for p, sl in zip(ps, heads)], axis=1).astype(BF16) for ps in probs]
    x2 = [v + _dot(o, wmo_ref[...]) for v, o in zip(x1, om)]

    hns = []
    for r, v in zip(subs, x2):
        x2_ref[0, r] = v
        hn = _rms(v, gffn_ref[...])
        hn_ref[0, r] = hn.reshape(hr, SUBLANES, D // SUBLANES)
        hi = hn.astype(BF16)
        hns.append((hi, (hn - hi.astype(F32)).astype(BF16)))
    wrh = wrh_ref[...]
    logits = [_dot(hi, wrh) + (_dot(hi, wrl_ref[...]) + _dot(lo, wrh)) for hi, lo in hns]
    valid = lax.broadcasted_iota(I32, (hr, LANES), 1) < N_EXPERTS
    for r, lg in zip(subs, logits):
        lg = jnp.where(valid, lg, -jnp.inf)
        e = jnp.exp(lg - jnp.max(lg, axis=1, keepdims=True))
        aff = e / jnp.sum(e, axis=1, keepdims=True)
        aff_ref[0, r] = aff
        afft_ref[0, :, r] = aff.T[:N_EXPERTS]


def _post(x, hf, hb, om, ya, kmem, vmem, g_head_mlstm, w_out, g_mem_x, w_mem_q, w_mem_o, g_ffn, w_router):
    B, S, D = x.shape
    tm = TM_POST
    M = kmem.shape[1]
    wr = jnp.zeros((D, LANES), F32).at[:, :N_EXPERTS].set(w_router)
    wrh = wr.astype(BF16)
    wrl = (wr - wrh.astype(F32)).astype(BF16)
    row = lambda w: pl.BlockSpec((1, tm, w), lambda b, i: (b, i, 0))
    col = lambda w: pl.BlockSpec((1, w, tm), lambda b, i: (b, 0, i))
    mem_spec = pl.BlockSpec((1, M, D), lambda b, i: (b, 0, 0))
    return pl.pallas_call(
        _post_kernel, grid=(B, S // tm),
        in_specs=[row(D), col(D_MLSTM), col(D_MLSTM), row(D_MLSTM), row(D_MLSTM), mem_spec, mem_spec,
                  _full((1, D_MLSTM)), _full((D, D)), _full((1, D)), _full((D, D)), _full((D, D)),
                  _full((1, D)), _full((D, LANES)), _full((D, LANES))],
        out_specs=[row(D), pl.BlockSpec((1, tm, SUBLANES, D // SUBLANES), lambda b, i: (b, i, 0, 0)), row(LANES),
                   pl.BlockSpec((1, N_EXPERTS, tm), lambda b, i: (b, 0, i))],
        out_shape=[jax.ShapeDtypeStruct((B, S, D), F32), jax.ShapeDtypeStruct((B, S, SUBLANES, D // SUBLANES), F32),
                   jax.ShapeDtypeStruct((B, S, LANES), F32), jax.ShapeDtypeStruct((B, N_EXPERTS, S), F32)],
        compiler_params=_cparams(("parallel", "parallel")), name="post_mixer",
    )(x, hf, hb, om, ya, kmem, vmem, g_head_mlstm.reshape(1, -1), w_out.astype(BF16), g_mem_x.reshape(1, D),
      w_mem_q.astype(BF16), w_mem_o.astype(BF16), g_ffn.reshape(1, D), wrh, wrl)


def _topk_kernel(aff_ref, afft_ref, affb_ref, idx_ref, gate_ref, idx_sc, gate_sc):
    S = aff_ref.shape[1]
    cap = idx_ref.shape[2]
    nblk = S // CUM_BLOCK

    def open_interval(c):
        lo, hi = c
        mid = 0.5 * (lo + hi)
        return jnp.max(jnp.where((mid > lo) & (mid < hi), 1.0, 0.0)) > 0.0

    def bisect(c):
        lo, hi = c
        mid = 0.5 * (lo + hi)
        cnt = jnp.sum((afft_ref[0] >= mid).astype(F32), axis=1, keepdims=True)
        ge = cnt >= cap
        return jnp.where(ge, mid, lo), jnp.where(ge, hi, mid)

    lo_c, hi_c = lax.while_loop(open_interval, bisect,
                                (jnp.zeros((N_EXPERTS, 1), F32), jnp.full((N_EXPERTS, 1), 2.0, F32)))
    need_c = cap - jnp.sum((afft_ref[0] >= hi_c).astype(F32), axis=1, keepdims=True)
    eye = (lax.broadcasted_iota(I32, (N_EXPERTS, LANES), 0) == lax.broadcasted_iota(I32, (N_EXPERTS, LANES), 1))
    to_row = lambda col: jnp.sum(jnp.where(eye, col, 0.0), axis=0, keepdims=True)
    lo, hi, need = to_row(lo_c), to_row(hi_c), to_row(need_c)

    r = lax.broadcasted_iota(I32, (CUM_BLOCK, CUM_BLOCK), 0)
    c = lax.broadcasted_iota(I32, (CUM_BLOCK, CUM_BLOCK), 1)
    tri = (c <= r).astype(BF16)

    def cum_body(blk, carry):
        ce, cs = carry
        rows = pl.ds(pl.multiple_of(blk * CUM_BLOCK, CUM_BLOCK), CUM_BLOCK)
        a = aff_ref[0, rows, :]
        sure = a >= hi
        tie = (a >= lo) & jnp.logical_not(sure)
        eq = tie.astype(F32)
        eq_incl = _dot(tri, eq.astype(BF16)) + ce
        sel = jnp.where(sure | (tie & (eq_incl - eq < need)), 1.0, 0.0)
        local_incl = _dot(tri, sel.astype(BF16))
        rank_local = jnp.where(sel > 0.0, local_incl - sel, -2.0).astype(BF16)
        base = lax.convert_element_type(blk * CUM_BLOCK, F32)
        for e in range(N_EXPERTS):
            slot_local = jnp.clip(j_row - cs[:, e:e + 1], -1.0, CUM_BLOCK + 1.0).astype(BF16)
            onehot = jnp.where(rank_local[:, e:e + 1] == slot_local, one_bf, zero_bf)
            a_row = affb_ref[0, e, pl.ds(blk, 1), :]
            a_hi = a_row.astype(BF16).astype(F32)
            a_mid = (a_row - a_hi).astype(BF16).astype(F32)
            a_lo = a_row - a_hi - a_mid
            lhs = jnp.concatenate([tok_rows, a_hi, a_mid, a_lo, pad_rows], axis=0).astype(BF16)
            hit = _dot(lhs, onehot)
            idx_sc[e:e + 1, :] += hit[0:1] + base * hit[1:2]
            gate_sc[e:e + 1, :] += hit[2:3] + hit[3:4] + hit[4:5]
        return eq_incl[CUM_BLOCK - 1:CUM_BLOCK], cs + local_incl[CUM_BLOCK - 1:CUM_BLOCK]

    j_row = lax.broadcasted_iota(I32, (1, cap), 1).astype(F32)
    tok = lax.broadcasted_iota(I32, (1, CUM_BLOCK), 1).astype(F32)
    tok_rows = jnp.concatenate([tok, jnp.ones((1, CUM_BLOCK), F32)], axis=0)
    pad_rows = jnp.zeros((SUBLANES - 5, CUM_BLOCK), F32)
    one_bf = jnp.ones((CUM_BLOCK, cap), BF16)
    zero_bf = jnp.zeros((CUM_BLOCK, cap), BF16)
    idx_sc[...] = jnp.zeros_like(idx_sc)
    gate_sc[...] = jnp.zeros_like(gate_sc)
    zero = jnp.zeros((1, LANES), F32)
    lax.fori_loop(0, nblk, cum_body, (zero, zero))
    idx_ref[0] = idx_sc[...].astype(I32)
    gate_ref[0] = gate_sc[...]


def _topk(aff, aff_t, cap):
    B, S, _ = aff.shape
    nblk = S // CUM_BLOCK
    out = pl.BlockSpec((1, N_EXPERTS, cap), lambda b: (b, 0, 0))
    return pl.pallas_call(
        _topk_kernel, grid=(B,),
        in_specs=[pl.BlockSpec((1, S, LANES), lambda b: (b, 0, 0)),
                  pl.BlockSpec((1, N_EXPERTS, S), lambda b: (b, 0, 0)),
                  pl.BlockSpec((1, N_EXPERTS, nblk, CUM_BLOCK), lambda b: (b, 0, 0, 0))],
        out_specs=[out, out],
        out_shape=[jax.ShapeDtypeStruct((B, N_EXPERTS, cap), I32), jax.ShapeDtypeStruct((B, N_EXPERTS, cap), F32)],
        scratch_shapes=[pltpu.VMEM((N_EXPERTS, cap), F32), pltpu.VMEM((N_EXPERTS, cap), F32)],
        compiler_params=_cparams(("parallel",)), name="topk",
    )(aff, aff_t, aff_t.reshape(B, N_EXPERTS, nblk, CUM_BLOCK))


def _gather_kernel(idx_ref, hn_ref, xe_ref, rows_sc):
    cap = xe_ref.shape[2]

    def body(j, _):
        i = idx_ref[0, 0, j]
        rows_sc[pl.ds(j, 1)] = hn_ref[0, pl.ds(i, 1)]
        return 0

    lax.fori_loop(0, cap, body, 0, unroll=8)
    xe_ref[0, 0] = rows_sc[...].reshape(cap, xe_ref.shape[3]).astype(BF16)


def _gather(idx3, hn, cap):
    B, S, sub, lanes = hn.shape
    E = N_EXPERTS
    D = sub * lanes
    return pl.pallas_call(
        _gather_kernel, grid=(B, E),
        in_specs=[pl.BlockSpec((1, 1, cap), lambda b, e: (b * E + e, 0, 0), memory_space=pltpu.SMEM),
                  pl.BlockSpec((1, S, sub, lanes), lambda b, e: (b, 0, 0, 0))],
        out_specs=pl.BlockSpec((1, 1, cap, D), lambda b, e: (b, e, 0, 0)),
        out_shape=jax.ShapeDtypeStruct((B, E, cap, D), BF16),
        scratch_shapes=[pltpu.VMEM((cap, sub, lanes), F32)],
        compiler_params=_cparams(("parallel", "arbitrary")), name="moe_gather",
    )(idx3, hn)


def _ffn_kernel(xe_ref, wg_ref, wu_ref, wd_ref, ye_ref, wg_sc, wu_sc, wd_sc):
    f = pl.program_id(1)
    nb = xe_ref.shape[0]

    @pl.when(f == 0)
    def _():
        ye_ref[...] = jnp.zeros_like(ye_ref)

    wg_sc[...] = wg_ref[0].astype(BF16)
    wu_sc[...] = wu_ref[0].astype(BF16)
    wd_sc[...] = wd_ref[0].astype(BF16)

    def up(b):
        xb = xe_ref[b, 0]
        return _dot(xb, wg_sc[...]), _dot(xb, wu_sc[...])

    def down(b, h1, h2):
        hid = (h1 * _sigmoid(h1) * h2).astype(BF16)
        ye_ref[b, 0] += _dot(hid, wd_sc[...])

    hs = up(0)
    for b in range(nb):
        nxt = up(b + 1) if b + 1 < nb else None
        down(b, *hs)
        hs = nxt


def _ffn(xe, w_gate, w_up, w_down):
    B, E, cap, D = xe.shape
    F = w_gate.shape[2]
    tf = TF_FFN
    return pl.pallas_call(
        _ffn_kernel, grid=(E, F // tf),
        in_specs=[pl.BlockSpec((B, 1, cap, D), lambda e, f: (0, e, 0, 0)),
                  pl.BlockSpec((1, D, tf), lambda e, f: (e, 0, f)),
                  pl.BlockSpec((1, D, tf), lambda e, f: (e, 0, f)),
                  pl.BlockSpec((1, tf, D), lambda e, f: (e, f, 0))],
        out_specs=pl.BlockSpec((B, 1, cap, D), lambda e, f: (0, e, 0, 0)),
        out_shape=jax.ShapeDtypeStruct((B, E, cap, D), F32),
        scratch_shapes=[pltpu.VMEM((D, tf), BF16), pltpu.VMEM((D, tf), BF16), pltpu.VMEM((tf, D), BF16)],
        compiler_params=_cparams(("parallel", "arbitrary")), name="moe_ffn",
    )(xe, w_gate, w_up, w_down)


def _scatter_final_kernel(idx_ref, gate_ref, ye_ref, g_ref, x2_hbm, out_hbm, acc_sc, ye_sc, x_buf, o_buf, in_sem,
                          out_sem):
    b = pl.program_id(0)
    e = pl.program_id(1)
    cap = ye_ref.shape[2]
    S = acc_sc.shape[0]
    ch = x_buf.shape[1]
    D = x_buf.shape[2]

    @pl.when(e == 0)
    def _():
        acc_sc[...] = jnp.zeros_like(acc_sc)

    ye_sc[...] = ye_ref[0, 0].reshape(ye_sc.shape)

    def body(jb, _):
        base = pl.multiple_of(jb * SCATTER_GROUP, SCATTER_GROUP)
        ids = [idx_ref[0, 0, base + u] for u in range(SCATTER_GROUP)]
        new = [acc_sc[pl.ds(ids[u], 1)] + ye_sc[pl.ds(base + u, 1)] * gate_ref[0, 0, base + u]
               for u in range(SCATTER_GROUP)]
        for u in range(SCATTER_GROUP):
            acc_sc[pl.ds(ids[u], 1)] = new[u]
        return 0

    lax.fori_loop(0, cap // SCATTER_GROUP, body, 0)

    n_in = x_buf.shape[0]
    n_out = o_buf.shape[0]

    def in_copy(c):
        return pltpu.make_async_copy(x2_hbm.at[b, pl.ds(c * ch, ch)], x_buf.at[c % n_in], in_sem.at[c % n_in])

    def out_copy(c):
        return pltpu.make_async_copy(o_buf.at[c % n_out], out_hbm.at[b, pl.ds(c * ch, ch)], out_sem.at[c % n_out])

    @pl.when(e == pl.num_programs(1) - 1)
    def _():
        nch = S // ch
        ahead = n_in - 1
        for c in range(min(ahead, nch)):
            in_copy(c).start()
        for c in range(nch):
            in_copy(c).wait()
            if c + ahead < nch:
                in_copy(c + ahead).start()
            if c >= n_out:
                out_copy(c - n_out).wait()
            moe = acc_sc[c * ch:(c + 1) * ch].reshape(ch, D)
            o_buf[c % n_out] = _rms(x_buf[c % n_in] + moe, g_ref[...])
            out_copy(c).start()
        for c in range(max(nch - n_out, 0), nch):
            out_copy(c).wait()


def _scatter_final(idx3, gate3, ye, x2, g_final):
    B, E, cap, D = ye.shape
    S = x2.shape[1]
    lanes = D // SUBLANES
    smem = lambda: pl.BlockSpec((1, 1, cap), lambda b, e: (b * E + e, 0, 0), memory_space=pltpu.SMEM)
    return pl.pallas_call(
        _scatter_final_kernel, grid=(B, E),
        in_specs=[smem(), smem(), pl.BlockSpec((1, 1, cap, D), lambda b, e: (b, e, 0, 0)), _full((1, D)),
                  pl.BlockSpec(memory_space=pl.ANY)],
        out_specs=pl.BlockSpec(memory_space=pl.ANY),
        out_shape=jax.ShapeDtypeStruct((B, S, D), F32),
        scratch_shapes=[pltpu.VMEM((S, SUBLANES, lanes), F32), pltpu.VMEM((cap, SUBLANES, lanes), F32),
                        pltpu.VMEM((FINAL_IN_BUFS, FINAL_CHUNK, D), F32),
                        pltpu.VMEM((FINAL_OUT_BUFS, FINAL_CHUNK, D), F32),
                        pltpu.SemaphoreType.DMA((FINAL_IN_BUFS,)), pltpu.SemaphoreType.DMA((FINAL_OUT_BUFS,))],
        compiler_params=_cparams(("arbitrary", "arbitrary")), name="moe_scatter_final",
    )(idx3, gate3, ye, g_final.reshape(1, D), x2)


def kernel(x, mem, positions, g_mix, w_in, b_gates, conv_qk, g_q_a, w_q_b, g_kv_a, w_kv_b, g_head_mlstm,
           g_head_mla, w_out, g_mem_x, g_mem_kv, w_mem_q, w_mem_k, w_mem_v, w_mem_o, g_ffn, w_router,
           w_exp_gate, w_exp_up, w_exp_down, g_final):
    B, S, D = x.shape
    depth = g_mix.shape[0]
    assert depth == 1, "the MoE residual is folded into the final norm kernel, which assumes one layer"
    cap = EC_CAPACITY_FACTOR * S // N_EXPERTS
    pos_row = positions.astype(F32).reshape(B, 1, S)
    for l in range(depth):
        qT_m, k_m, vT_m, om, gc, gt, qT, kcat, vT = _inproj(
            x, pos_row, g_mix[l], w_in[l], b_gates[l], conv_qk[l], g_q_a[l], w_q_b[l], g_kv_a[l], w_kv_b[l])
        hf, hb = _mlstm(k_m, qT_m, vT_m, gc, gt)
        ya = _mla(qT, kcat, vT, g_head_mla[l])
        kmem, vmem = _memkv(mem, g_mem_kv[l], w_mem_k[l], w_mem_v[l])
        x2, hn, aff, aff_t = _post(x, hf, hb, om, ya, kmem, vmem, g_head_mlstm[l], w_out[l], g_mem_x[l],
                            w_mem_q[l], w_mem_o[l], g_ffn[l], w_router[l])
        idx, gate = _topk(aff, aff_t, cap)
        idx3 = idx.reshape(B * N_EXPERTS, 1, cap)
        xe = _gather(idx3, hn, cap)
        ye = _ffn(xe, w_exp_gate[l], w_exp_up[l], w_exp_down[l])
        x = _scatter_final(idx3, gate.reshape(B * N_EXPERTS, 1, cap), ye, x2, g_final)
    return x
```

```python
import math

import jax
import jax.numpy as jnp
from jax import lax
from jax.experimental import pallas as pl
from jax.experimental.pallas import tpu as pltpu

F32 = jnp.float32
BF16 = jnp.bfloat16
I32 = jnp.int32

EPS = 1e-6
N_MLSTM_HEADS = 4
MLSTM_HEAD_DIM = 128
D_MLSTM = N_MLSTM_HEADS * MLSTM_HEAD_DIM
N_MLA_HEADS = 8
MLA_NOPE_DIM = 64
MLA_ROPE_DIM = 32
MLA_QK_DIM = MLA_NOPE_DIM + MLA_ROPE_DIM
MLA_V_DIM = 64
MLA_V_ROWS = 80
Q_LORA_RANK = 256
KV_LORA_RANK = 128
ROPE_THETA = 10000.0
N_GATE = 4 * N_MLSTM_HEADS
N_MEM_HEADS = 4
N_EXPERTS = 16
EC_CAPACITY_FACTOR = 2

LANES = 128
SUBLANES = 8
VMEM_LIMIT_BYTES = 56 * 1024 * 1024

MLSTM_CHUNK = 256
MLSTM_PAD_ROWS = 16
TM_INPROJ = 256
TQ_MLA = 256
MLA_CHUNKS = 16
TM_POST = 512
POST_SPLIT = 2
TF_FFN = 512
FINAL_CHUNK = 512
FINAL_IN_BUFS = 4
FINAL_OUT_BUFS = 3
CUM_BLOCK = 256
SCATTER_GROUP = 8


def _cparams(sem):
    return pltpu.CompilerParams(dimension_semantics=sem, vmem_limit_bytes=VMEM_LIMIT_BYTES)


def _dot(a, b):
    return jnp.dot(a, b, preferred_element_type=F32)


def _dot_nt(a, b):
    return lax.dot_general(a, b, (((1,), (1,)), ((), ())), preferred_element_type=F32)


def _rms(x, g):
    return x * lax.rsqrt(jnp.mean(x * x, axis=-1, keepdims=True) + EPS) * g


def _sigmoid(x):
    return 1.0 / (1.0 + jnp.exp(-x))


def _full(shape):
    return pl.BlockSpec(shape, lambda *_: (0,) * len(shape))


def _inproj_kernel(x_ref, xp_ref, xn_ref, posr_ref, gmix_ref, wqk_ref, wvt_ref, wo_ref, wmisc_ref, wkr_ref, bg_ref,
                   conv_ref, gqa_ref, wq_ref, gkva_ref, wk_ref, wv_ref, vones_ref, invfc_ref,
                   qT_m_ref, k_ref, vT_m_ref, o_ref, gc_ref, gt_ref, qT_ref, kcat_ref, vT_ref):
    tm = x_ref.shape[1]
    i = pl.program_id(1)
    last = pl.num_programs(1) - 1
    g = gmix_ref[...]
    xm = _rms(x_ref[0], g)
    xprev = _rms(xp_ref[0], g) * (i > 0).astype(F32)
    xnext = _rms(xn_ref[0], g) * (i < last).astype(F32)
    xm_bf = xm.astype(BF16)
    lhs = jnp.concatenate([xprev.astype(BF16), xm_bf, xnext.astype(BF16)], axis=0)

    pqk = _dot(lhs, wqk_ref[...])
    misc = _dot(xm_bf, wmisc_ref[...])
    vT_m = _dot_nt(wvt_ref[...], xm_bf)
    o_ref[0] = _dot(xm_bf, wo_ref[...])
    krT = _dot_nt(wkr_ref[...], xm_bf)
    cq = misc[:, :Q_LORA_RANK]
    ckv = misc[:, Q_LORA_RANK:Q_LORA_RANK + KV_LORA_RANK]
    gp = misc[:, Q_LORA_RANK + KV_LORA_RANK:] + bg_ref[...]
    ckvn = _rms(ckv, gkva_ref[...]).astype(BF16)
    cqn = _rms(cq, gqa_ref[...]).astype(BF16)
    kn = _dot(ckvn, wk_ref[...])
    qT = _dot_nt(wq_ref[...], cqn)
    vT_a = _dot_nt(wv_ref[...], ckvn)
    vT_m_ref[0] = vT_m.astype(BF16)

    rows = tm + 2 * SUBLANES
    up = pltpu.roll(pqk, 1, 0)[SUBLANES:SUBLANES + tm]
    dn = pltpu.roll(pqk, rows - 1, 0)[SUBLANES:SUBLANES + tm]
    mid = pqk[SUBLANES:SUBLANES + tm]
    cw = conv_ref[...]
    conv = up * cw[0:1] + mid * cw[1:2] + dn * cw[2:3]
    act = conv * _sigmoid(conv)
    qT_m_ref[0] = act[:, :D_MLSTM].T.astype(BF16)
    k_ref[0] = (act[:, D_MLSTM:] * (MLSTM_HEAD_DIM ** -0.5)).astype(BF16)

    lane = lax.broadcasted_iota(I32, (tm, LANES), 1)
    is_f = ((lane >= 4) & (lane < 8)) | ((lane >= 12) & (lane < 16))
    logsig = jnp.minimum(gp, 0.0) - jnp.log1p(jnp.exp(-jnp.abs(gp)))
    lf = jnp.where(is_f, logsig, gp)
    row_id = lax.broadcasted_iota(I32, (tm, LANES), 0)

    def scan(x, op, ident, suffix):
        step = 1
        while step < tm:
            if step < SUBLANES:
                ok = (row_id + step < tm) if suffix else (row_id >= step)
                shifted = jnp.where(ok, pltpu.roll(x, tm - step if suffix else step, 0), ident)
            else:
                pad = jnp.full((step, LANES), ident, F32)
                shifted = jnp.concatenate([x[step:], pad] if suffix else [pad, x[:tm - step]], axis=0)
            x = op(x, shifted)
            step *= 2
        return x

    cum = jnp.where(lane < 8, scan(lf, jnp.add, 0.0, False), scan(lf, jnp.add, 0.0, True))
    a = lf - pltpu.roll(cum, LANES - 4, 1)
    gc = jnp.where(is_f, cum, a)
    run_max = jnp.where(lane < 8, scan(gc, jnp.maximum, -jnp.inf, False), scan(gc, jnp.maximum, -jnp.inf, True))
    table = jnp.where(lane < N_GATE, gc, pltpu.roll(run_max, N_GATE, 1))
    gc_ref[0] = table
    gt_ref[0] = table.T[:2 * N_GATE]

    ang_t = invfc_ref[...] * posr_ref[0]
    cos_t = jnp.cos(ang_t)
    sin_t = jnp.sin(ang_t)
    r0, r1, r2 = MLA_NOPE_DIM, MLA_NOPE_DIM + MLA_ROPE_DIM // 2, MLA_QK_DIM

    def rope_rows(blk):
        t1 = blk[r0:r1]
        t2 = blk[r1:r2]
        return jnp.concatenate([blk[:r0], t1 * cos_t - t2 * sin_t, t2 * cos_t + t1 * sin_t, blk[r2:]], axis=0)

    kr = rope_rows(krT).T
    for h in range(N_MLA_HEADS):
        sl = slice(h * LANES, (h + 1) * LANES)
        kcat_ref[0, :, sl] = (kn[:, sl] + kr).astype(BF16)
        qT_ref[0, sl, :] = rope_rows(qT[sl]).astype(BF16)
    vT_ref[0] = (vT_a + vones_ref[...]).astype(BF16)


def _inproj(x, pos_row, g_mix, w_in, b_gates, conv_qk, g_q_a, w_q_b, g_kv_a, w_kv_b):
    B, S, D = x.shape
    tm = TM_INPROJ
    assert tm == MLSTM_CHUNK and S % tm == 0, "the gate scans treat one row tile as one mLSTM chunk"
    nt = S // tm
    hb = tm // SUBLANES
    sizes = [D_MLSTM, D_MLSTM, D_MLSTM, D_MLSTM, N_GATE, Q_LORA_RANK, KV_LORA_RANK, MLA_ROPE_DIM]
    off = [sum(sizes[:n]) for n in range(len(sizes) + 1)]
    assert w_in.shape[1] == off[-1]
    wqk = w_in[:, off[0]:off[2]].astype(BF16)
    wvt = w_in[:, off[2]:off[3]].T.astype(BF16)
    wo = w_in[:, off[3]:off[4]].astype(BF16)
    w_gate = w_in[:, off[4]:off[5]]
    w_cq = w_in[:, off[5]:off[6]]
    w_ckv = w_in[:, off[6]:off[7]]
    w_kr = w_in[:, off[7]:off[8]]
    wkr = jnp.zeros((LANES, D), F32).at[MLA_NOPE_DIM:MLA_QK_DIM].set(w_kr.T).astype(BF16)
    gate_blk = jnp.zeros((D, LANES), F32).at[:, :N_GATE].set(w_gate)
    wmisc = jnp.concatenate([w_cq, w_ckv, gate_blk], axis=1).astype(BF16)
    bg = jnp.zeros((1, LANES), F32).at[0, :N_GATE].set(b_gates)
    pad_heads = lambda w3: jnp.pad(w3, ((0, 0), (0, 0), (0, LANES - w3.shape[2]))).reshape(w3.shape[0], -1)
    wq = pad_heads(w_q_b.reshape(-1, N_MLA_HEADS, MLA_QK_DIM)).T.astype(BF16)
    wkv3 = w_kv_b.reshape(-1, N_MLA_HEADS, MLA_NOPE_DIM + MLA_V_DIM)
    wk = pad_heads(wkv3[:, :, :MLA_NOPE_DIM]).astype(BF16)
    wv = jnp.pad(wkv3[:, :, MLA_NOPE_DIM:], ((0, 0), (0, 0), (0, MLA_V_ROWS - MLA_V_DIM)))
    wv = wv.reshape(wv.shape[0], -1).T.astype(BF16)
    vones = jnp.zeros((N_MLA_HEADS, MLA_V_ROWS, tm), F32).at[:, MLA_V_DIM].set(1.0).reshape(-1, tm)
    inv_freq = ROPE_THETA ** (-jnp.arange(0, MLA_ROPE_DIM, 2, dtype=F32) / MLA_ROPE_DIM)
    invf_col = inv_freq.reshape(-1, 1)

    row = lambda w: pl.BlockSpec((1, tm, w), lambda b, i: (b, i, 0))
    in_specs = [
        row(D),
        pl.BlockSpec((1, SUBLANES, D), lambda b, i: (b, jnp.maximum(i * hb - 1, 0), 0)),
        pl.BlockSpec((1, SUBLANES, D), lambda b, i: (b, jnp.minimum((i + 1) * hb, S // SUBLANES - 1), 0)),
        pl.BlockSpec((1, 1, tm), lambda b, i: (b, 0, i)),
        _full((1, D)), _full(wqk.shape), _full(wvt.shape), _full(wo.shape), _full(wmisc.shape), _full(wkr.shape),
        _full((1, LANES)), _full(conv_qk.shape), _full((1, Q_LORA_RANK)), _full(wq.shape), _full((1, KV_LORA_RANK)),
        _full(wk.shape), _full(wv.shape), _full(vones.shape), _full(invf_col.shape),
    ]
    out_shape = [
        jax.ShapeDtypeStruct((B, D_MLSTM, S), BF16),
        jax.ShapeDtypeStruct((B, S, D_MLSTM), BF16),
        jax.ShapeDtypeStruct((B, D_MLSTM, S), BF16),
        jax.ShapeDtypeStruct((B, S, D_MLSTM), F32),
        jax.ShapeDtypeStruct((B, S, LANES), F32),
        jax.ShapeDtypeStruct((B, 2 * N_GATE, S), F32),
        jax.ShapeDtypeStruct((B, N_MLA_HEADS * LANES, S), BF16),
        jax.ShapeDtypeStruct((B, S, N_MLA_HEADS * LANES), BF16),
        jax.ShapeDtypeStruct((B, N_MLA_HEADS * MLA_V_ROWS, S), BF16),
    ]
    out_specs = [
        pl.BlockSpec((1, D_MLSTM, tm), lambda b, i: (b, 0, i)),
        row(D_MLSTM),
        pl.BlockSpec((1, D_MLSTM, tm), lambda b, i: (b, 0, i)),
        row(D_MLSTM), row(LANES),
        pl.BlockSpec((1, 2 * N_GATE, tm), lambda b, i: (b, 0, i)),
        pl.BlockSpec((1, N_MLA_HEADS * LANES, tm), lambda b, i: (b, 0, i)),
        row(N_MLA_HEADS * LANES),
        pl.BlockSpec((1, N_MLA_HEADS * MLA_V_ROWS, tm), lambda b, i: (b, 0, i)),
    ]
    return pl.pallas_call(
        _inproj_kernel, grid=(B, nt), in_specs=in_specs, out_specs=out_specs, out_shape=out_shape,
        compiler_params=_cparams(("parallel", "parallel")), name="inproj",
    )(x, x, x, pos_row, g_mix.reshape(1, D), wqk, wvt, wo, wmisc, wkr, bg, conv_qk, g_q_a.reshape(1, -1), wq,
      g_kv_a.reshape(1, -1), wk, wv, vones, invf_col)


def _mlstm_kernel(kf_ref, qf_ref, vf_ref, gcf_ref, gtf_ref, kb_ref, qb_ref, vb_ref, gcb_ref, gtb_ref,
                  hf_ref, hb_ref, cn_ref, m_ref):
    T = MLSTM_CHUNK
    hd = MLSTM_HEAD_DIM

    @pl.when(pl.program_id(1) == 0)
    def _():
        cn_ref[...] = jnp.zeros_like(cn_ref)
        m_ref[...] = jnp.zeros_like(m_ref)

    s_idx = lax.broadcasted_iota(I32, (T, T), 0)
    t_idx = lax.broadcasted_iota(I32, (T, T), 1)
    ones_rows = (lax.broadcasted_iota(I32, (MLSTM_PAD_ROWS, T), 0) == 0).astype(BF16)
    neg_inf = F32(-jnp.inf)

    streams = []
    for d, (k_ref, qT_ref, vT_ref, gc_ref, gt_ref, h_ref) in enumerate(
            ((kf_ref, qf_ref, vf_ref, gcf_ref, gtf_ref, hf_ref),
             (kb_ref, qb_ref, vb_ref, gcb_ref, gtb_ref, hb_ref))):
        end = T - 1 if d == 0 else 0
        gc = gc_ref[0]
        gt = gt_ref[0]
        for h in range(N_MLSTM_HEADS):
            st = d * N_MLSTM_HEADS + h
            la = d * 8 + h
            a_row = gt[la:la + 1]
            b_row = gt[la + 4:la + 5]
            m_prev = m_ref[st][0:1, 0:1]
            kh = k_ref[0, :, h * hd:(h + 1) * hd]
            qTh = qT_ref[0, h * hd:(h + 1) * hd, :]
            v_ext = jnp.concatenate([vT_ref[0, h * hd:(h + 1) * hd, :], ones_rows], axis=0)
            cn = cn_ref[st]
            m_run = jnp.maximum(gt[N_GATE + la:N_GATE + la + 1], m_prev)
            m_end = m_run[:, end:end + 1]
            vw = (v_ext.astype(F32) * jnp.exp(a_row - m_end)).astype(BF16)
            streams.append(dict(
                st=st, d=d, h=h, h_ref=h_ref, a_col=gc[:, la:la + 1], b_row=b_row, m_prev=m_prev,
                m_run=m_run, m_end=m_end, g_sum=b_row[:, end:end + 1], cn=cn, v_ext=v_ext,
                s=_dot(kh, qTh),
                qcn=_dot(cn.astype(BF16), qTh),
                upd=_dot(vw, kh)))
    for x in streams:
        mask = (s_idx <= t_idx) if x["d"] == 0 else (s_idx >= t_idx)
        e = jnp.exp(jnp.where(mask, x["a_col"] - x["m_run"], neg_inf))
        x["pw"] = (e * x["s"]).astype(BF16)
    for x in streams:
        x["pv"] = _dot(x["v_ext"], x["pw"])
    for x in streams:
        st, h = x["st"], x["h"]
        iw = jnp.exp(x["m_prev"] - x["m_run"])
        num = iw * x["qcn"][:hd] + x["pv"][:hd]
        den = iw * x["qcn"][hd:hd + 1] + x["pv"][hd:hd + 1]
        floor = jnp.exp(-(x["b_row"] + x["m_run"]))
        x["h_ref"][0, h * hd:(h + 1) * hd, :] = num / jnp.maximum(jnp.abs(den), floor)
        cn_ref[st] = jnp.exp(x["m_prev"] - x["m_end"]) * x["cn"] + x["upd"]
        m_ref[st] = jnp.broadcast_to(x["g_sum"] + x["m_end"], (SUBLANES, LANES))


def _mlstm(k, qT, vT, gc, gt):
    B, S, _ = k.shape
    T = MLSTM_CHUNK
    nc = S // T
    fwd = lambda b, j: (b, j, 0)
    bwd = lambda b, j: (b, nc - 1 - j, 0)
    fwd_t = lambda b, j: (b, 0, j)
    bwd_t = lambda b, j: (b, 0, nc - 1 - j)

    def specs(row_map, col_map):
        return [pl.BlockSpec((1, T, D_MLSTM), row_map), pl.BlockSpec((1, D_MLSTM, T), col_map),
                pl.BlockSpec((1, D_MLSTM, T), col_map), pl.BlockSpec((1, T, LANES), row_map),
                pl.BlockSpec((1, 2 * N_GATE, T), col_map)]

    n_state = 2 * N_MLSTM_HEADS
    return pl.pallas_call(
        _mlstm_kernel, grid=(B, nc),
        in_specs=specs(fwd, fwd_t) + specs(bwd, bwd_t),
        out_specs=[pl.BlockSpec((1, D_MLSTM, T), fwd_t), pl.BlockSpec((1, D_MLSTM, T), bwd_t)],
        out_shape=[jax.ShapeDtypeStruct((B, D_MLSTM, S), F32)] * 2,
        scratch_shapes=[pltpu.VMEM((n_state, MLSTM_HEAD_DIM + MLSTM_PAD_ROWS, MLSTM_HEAD_DIM), F32),
                        pltpu.VMEM((n_state, SUBLANES, LANES), F32)],
        compiler_params=_cparams(("parallel", "arbitrary")), name="mlstm",
    )(k, qT, vT, gc, gt, k, qT, vT, gc, gt)


def _mla_kernel(qT_ref, k_ref, vT_ref, g_ref, o_ref, sa_sc, sb_sc, ma_sc, mb_sc):
    t = pl.program_id(0)
    c = (MLA_QK_DIM ** -0.5) * math.log2(math.e)
    S = k_ref.shape[1]
    tq = qT_ref.shape[2]
    ck = S // MLA_CHUNKS

    @pl.when(t == 0)
    def _():
        sb_sc[...] = jnp.zeros(sb_sc.shape, F32)
        mb_sc[...] = jnp.zeros(mb_sc.shape, F32)

    def step(s_cur, m_cur, s_prev, m_prev_ref):
        m_prev = [m_prev_ref[h][0:1] for h in range(2)]
        m_run = [jnp.full((1, tq), -jnp.inf, F32) for _ in range(2)]
        accs = [None, None]
        for j in range(MLA_CHUNKS):
            rows = slice(j * ck, (j + 1) * ck)
            for h in range(2):
                s = _dot(k_ref[0, rows, h * LANES:(h + 1) * LANES], qT_ref[0, h * LANES:(h + 1) * LANES, :])
                m_run[h] = jnp.maximum(m_run[h], jnp.max(s, axis=0, keepdims=True))
                s_cur[h, rows, :] = s
            for h in range(2):
                p = jnp.exp2((s_prev[h, rows, :] - m_prev[h]) * c).astype(BF16)
                part = _dot(vT_ref[0, h * MLA_V_ROWS:(h + 1) * MLA_V_ROWS, rows], p)
                accs[h] = part if j == 0 else accs[h] + part
        for h in range(2):
            m_cur[h] = jnp.broadcast_to(m_run[h], (SUBLANES, tq))
        ys = []
        for h in range(2):
            acc = accs[h]
            o = acc[:MLA_V_DIM] / acc[MLA_V_DIM:MLA_V_DIM + 1]
            ms = jnp.mean(o * o, axis=0, keepdims=True)
            ys.append(o * lax.rsqrt(ms + EPS))
        o_ref[0] = jnp.concatenate(ys, axis=0).T * g_ref[...]

    @pl.when(t % 2 == 0)
    def _():
        step(sa_sc, ma_sc, sb_sc, mb_sc)

    @pl.when(t % 2 == 1)
    def _():
        step(sb_sc, mb_sc, sa_sc, ma_sc)


def _mla(qT, kcat, vT, g_head_mla):
    B, S, _ = kcat.shape
    tq = TQ_MLA
    npair = N_MLA_HEADS // 2
    nq = S // tq
    nt = B * npair * nq

    def tile(t):
        return t // (npair * nq), (t // nq) % npair, t % nq

    def a_map(f):
        return lambda t: f(*tile(jnp.minimum(t, nt - 1)))

    def b_map(f):
        return lambda t: f(*tile(jnp.maximum(t - 1, 0)))

    return pl.pallas_call(
        _mla_kernel, grid=(nt + 1,),
        in_specs=[pl.BlockSpec((1, 2 * LANES, tq), a_map(lambda b, p, i: (b, p, i))),
                  pl.BlockSpec((1, S, 2 * LANES), a_map(lambda b, p, i: (b, 0, p))),
                  pl.BlockSpec((1, 2 * MLA_V_ROWS, S), b_map(lambda b, p, i: (b, p, 0))),
                  pl.BlockSpec((1, LANES), b_map(lambda b, p, i: (0, p)))],
        out_specs=pl.BlockSpec((1, tq, LANES), b_map(lambda b, p, i: (b, i, p))),
        out_shape=jax.ShapeDtypeStruct((B, S, N_MLA_HEADS * MLA_V_DIM), F32),
        scratch_shapes=[pltpu.VMEM((2, S, tq), F32), pltpu.VMEM((2, S, tq), F32),
                        pltpu.VMEM((2, SUBLANES, tq), F32), pltpu.VMEM((2, SUBLANES, tq), F32)],
        compiler_params=_cparams(("arbitrary",)), name="mla_attn",
    )(qT, kcat, vT, g_head_mla.reshape(1, -1))


def _memkv_kernel(mem_ref, g_ref, wk_ref, wv_ref, k_ref, v_ref):
    mn = _rms(mem_ref[0], g_ref[...]).astype(BF16)
    k_ref[0] = _dot(mn, wk_ref[...]).astype(BF16)
    v_ref[0] = _dot(mn, wv_ref[...]).astype(BF16)


def _memkv(mem, g_mem_kv, w_k, w_v):
    B, M, D = mem.shape
    blk = pl.BlockSpec((1, M, D), lambda b: (b, 0, 0))
    return pl.pallas_call(
        _memkv_kernel, grid=(B,),
        in_specs=[blk, _full((1, D)), _full(w_k.shape), _full(w_v.shape)],
        out_specs=[blk, blk], out_shape=[jax.ShapeDtypeStruct((B, M, D), BF16)] * 2,
        compiler_params=_cparams(("parallel",)), name="memkv",
    )(mem, g_mem_kv.reshape(1, D), w_k.astype(BF16), w_v.astype(BF16))


def _post_kernel(x_ref, hf_ref, hb_ref, om_ref, ya_ref, km_ref, vm_ref, ghm_ref, wout_ref, gmx_ref, wmq_ref,
                 wmo_ref, gffn_ref, wrh_ref, wrl_ref, x2_ref, hn_ref, aff_ref, afft_ref):
    tm = x_ref.shape[1]
    D = x_ref.shape[2]
    hr = tm // POST_SPLIT
    subs = [slice(i * hr, (i + 1) * hr) for i in range(POST_SPLIT)]
    ghm = ghm_ref[...]
    dm = D // N_MEM_HEADS
    heads = [slice(h * dm, (h + 1) * dm) for h in range(N_MEM_HEADS)]

    ycat = []
    for r in subs:
        hm = (hf_ref[0, :, r] + hb_ref[0, :, r]).T
        gate = _sigmoid(om_ref[0, r])
        parts = []
        for h in range(N_MLSTM_HEADS):
            sl = slice(h * MLSTM_HEAD_DIM, (h + 1) * MLSTM_HEAD_DIM)
            parts.append(gate[:, sl] * _rms(hm[:, sl], ghm[:, sl]))
        ycat.append(jnp.concatenate(parts + [ya_ref[0, r]], axis=1).astype(BF16))
    x1 = [x_ref[0, r] + _dot(y, wout_ref[...]) for r, y in zip(subs, ycat)]

    xn = [_rms(v, gmx_ref[...]).astype(BF16) for v in x1]
    qm = [_dot(v, wmq_ref[...]).astype(BF16) for v in xn]
    km = km_ref[0]
    vm = vm_ref[0]
    scores = [[_dot_nt(q[:, sl], km[:, sl]) * (dm ** -0.5) for sl in heads] for q in qm]
    probs = []
    for per_sub in scores:
        ps = []
        for s in per_sub:
            e = jnp.exp(s - jnp.max(s, axis=1, keepdims=True))
            ps.append((e / jnp.sum(e, axis=1, keepdims=True)).astype(BF16))
        probs.append(ps)
    om = [jnp.concatenate([_dot(p, vm[:, sl]) for p, sl in zip(ps, heads)], axis=1).astype(BF16) for ps in probs]
    x2 = [v + _dot(o, wmo_ref[...]) for v, o in zip(x1, om)]

    hns = []
    for r, v in zip(subs, x2):
        x2_ref[0, r] = v
        hn = _rms(v, gffn_ref[...])
        hn_ref[0, r] = hn.reshape(hr, SUBLANES, D // SUBLANES)
        hi = hn.astype(BF16)
        hns.append((hi, (hn - hi.astype(F32)).astype(BF16)))
    wrh = wrh_ref[...]
    logits = [_dot(hi, wrh) + (_dot(hi, wrl_ref[...]) + _dot(lo, wrh)) for hi, lo in hns]
    valid = lax.broadcasted_iota(I32, (hr, LANES), 1) < N_EXPERTS
    for r, lg in zip(subs, logits):
        lg = jnp.where(valid, lg, -jnp.inf)
        e = jnp.exp(lg - jnp.max(lg, axis=1, keepdims=True))
        aff = e / jnp.sum(e, axis=1, keepdims=True)
        aff_ref[0, r] = aff
        afft_ref[0, :, r] = aff.T[:N_EXPERTS]


def _post(x, hf, hb, om, ya, kmem, vmem, g_head_mlstm, w_out, g_mem_x, w_mem_q, w_mem_o, g_ffn, w_router):
    B, S, D = x.shape
    tm = TM_POST
    M = kmem.shape[1]
    wr = jnp.zeros((D, LANES), F32).at[:, :N_EXPERTS].set(w_router)
    wrh = wr.astype(BF16)
    wrl = (wr - wrh.astype(F32)).astype(BF16)
    row = lambda w: pl.BlockSpec((1, tm, w), lambda b, i: (b, i, 0))
    col = lambda w: pl.BlockSpec((1, w, tm), lambda b, i: (b, 0, i))
    mem_spec = pl.BlockSpec((1, M, D), lambda b, i: (b, 0, 0))
    return pl.pallas_call(
        _post_kernel, grid=(B, S // tm),
        in_specs=[row(D), col(D_MLSTM), col(D_MLSTM), row(D_MLSTM), row(D_MLSTM), mem_spec, mem_spec,
                  _full((1, D_MLSTM)), _full((D, D)), _full((1, D)), _full((D, D)), _full((D, D)),
                  _full((1, D)), _full((D, LANES)), _full((D, LANES))],
        out_specs=[row(D), pl.BlockSpec((1, tm, SUBLANES, D // SUBLANES), lambda b, i: (b, i, 0, 0)), row(LANES),
                   pl.BlockSpec((1, N_EXPERTS, tm), lambda b, i: (b, 0, i))],
        out_shape=[jax.ShapeDtypeStruct((B, S, D), F32), jax.ShapeDtypeStruct((B, S, SUBLANES, D // SUBLANES), F32),
                   jax.ShapeDtypeStruct((B, S, LANES), F32), jax.ShapeDtypeStruct((B, N_EXPERTS, S), F32)],
        compiler_params=_cparams(("parallel", "parallel")), name="post_mixer",
    )(x, hf, hb, om, ya, kmem, vmem, g_head_mlstm.reshape(1, -1), w_out.astype(BF16), g_mem_x.reshape(1, D),
      w_mem_q.astype(BF16), w_mem_o.astype(BF16), g_ffn.reshape(1, D), wrh, wrl)


def _topk_kernel(aff_ref, afft_ref, affb_ref, idx_ref, gate_ref, idx_sc, gate_sc):
    S = aff_ref.shape[1]
    cap = idx_ref.shape[2]
    nblk = S // CUM_BLOCK

    def open_interval(c):
        lo, hi = c
        mid = 0.5 * (lo + hi)
        return jnp.max(jnp.where((mid > lo) & (mid < hi), 1.0, 0.0)) > 0.0

    def bisect(c):
        lo, hi = c
        mid = 0.5 * (lo + hi)
        cnt = jnp.sum((afft_ref[0] >= mid).astype(F32), axis=1, keepdims=True)
        ge = cnt >= cap
        return jnp.where(ge, mid, lo), jnp.where(ge, hi, mid)

    lo_c, hi_c = lax.while_loop(open_interval, bisect,
                                (jnp.zeros((N_EXPERTS, 1), F32), jnp.full((N_EXPERTS, 1), 2.0, F32)))
    need_c = cap - jnp.sum((afft_ref[0] >= hi_c).astype(F32), axis=1, keepdims=True)
    eye = (lax.broadcasted_iota(I32, (N_EXPERTS, LANES), 0) == lax.broadcasted_iota(I32, (N_EXPERTS, LANES), 1))
    to_row = lambda col: jnp.sum(jnp.where(eye, col, 0.0), axis=0, keepdims=True)
    lo, hi, need = to_row(lo_c), to_row(hi_c), to_row(need_c)

    r = lax.broadcasted_iota(I32, (CUM_BLOCK, CUM_BLOCK), 0)
    c = lax.broadcasted_iota(I32, (CUM_BLOCK, CUM_BLOCK), 1)
    tri = (c <= r).astype(BF16)

    def cum_body(blk, carry):
        ce, cs = carry
        rows = pl.ds(pl.multiple_of(blk * CUM_BLOCK, CUM_BLOCK), CUM_BLOCK)
        a = aff_ref[0, rows, :]
        sure = a >= hi
        tie = (a >= lo) & jnp.logical_not(sure)
        eq = tie.astype(F32)
        eq_incl = _dot(tri, eq.astype(BF16)) + ce
        sel = jnp.where(sure | (tie & (eq_incl - eq < need)), 1.0, 0.0)
        local_incl = _dot(tri, sel.astype(BF16))
        rank_local = jnp.where(sel > 0.0, local_incl - sel, -2.0).astype(BF16)
        base = lax.convert_element_type(blk * CUM_BLOCK, F32)
        for e in range(N_EXPERTS):
            slot_local = jnp.clip(j_row - cs[:, e:e + 1], -1.0, CUM_BLOCK + 1.0).astype(BF16)
            onehot = jnp.where(rank_local[:, e:e + 1] == slot_local, one_bf, zero_bf)
            a_row = affb_ref[0, e, pl.ds(blk, 1), :]
            a_hi = a_row.astype(BF16).astype(F32)
            a_mid = (a_row - a_hi).astype(BF16).astype(F32)
            a_lo = a_row - a_hi - a_mid
            lhs = jnp.concatenate([tok_rows, a_hi, a_mid, a_lo, pad_rows], axis=0).astype(BF16)
            hit = _dot(lhs, onehot)
            idx_sc[e:e + 1, :] += hit[0:1] + base * hit[1:2]
            gate_sc[e:e + 1, :] += hit[2:3] + hit[3:4] + hit[4:5]
        return eq_incl[CUM_BLOCK - 1:CUM_BLOCK], cs + local_incl[CUM_BLOCK - 1:CUM_BLOCK]

    j_row = lax.broadcasted_iota(I32, (1, cap), 1).astype(F32)
    tok = lax.broadcasted_iota(I32, (1, CUM_BLOCK), 1).astype(F32)
    tok_rows = jnp.concatenate([tok, jnp.ones((1, CUM_BLOCK), F32)], axis=0)
    pad_rows = jnp.zeros((SUBLANES - 5, CUM_BLOCK), F32)
    one_bf = jnp.ones((CUM_BLOCK, cap), BF16)
    zero_bf = jnp.zeros((CUM_BLOCK, cap), BF16)
    idx_sc[...] = jnp.zeros_like(idx_sc)
    gate_sc[...] = jnp.zeros_like(gate_sc)
    zero = jnp.zeros((1, LANES), F32)
    lax.fori_loop(0, nblk, cum_body, (zero, zero))
    idx_ref[0] = idx_sc[...].astype(I32)
    gate_ref[0] = gate_sc[...]


def _topk(aff, aff_t, cap):
    B, S, _ = aff.shape
    nblk = S // CUM_BLOCK
    out = pl.BlockSpec((1, N_EXPERTS, cap), lambda b: (b, 0, 0))
    return pl.pallas_call(
        _topk_kernel, grid=(B,),
        in_specs=[pl.BlockSpec((1, S, LANES), lambda b: (b, 0, 0)),
                  pl.BlockSpec((1, N_EXPERTS, S), lambda b: (b, 0, 0)),
                  pl.BlockSpec((1, N_EXPERTS, nblk, CUM_BLOCK), lambda b: (b, 0, 0, 0))],
        out_specs=[out, out],
        out_shape=[jax.ShapeDtypeStruct((B, N_EXPERTS, cap), I32), jax.ShapeDtypeStruct((B, N_EXPERTS, cap), F32)],
        scratch_shapes=[pltpu.VMEM((N_EXPERTS, cap), F32), pltpu.VMEM((N_EXPERTS, cap), F32)],
        compiler_params=_cparams(("parallel",)), name="topk",
    )(aff, aff_t, aff_t.reshape(B, N_EXPERTS, nblk, CUM_BLOCK))


def _gather_kernel(idx_ref, hn_hbm, xe_ref, buf_a, buf_b, rows_sc, sem):
    b = pl.program_id(0)
    e = pl.program_id(1)
    cap = xe_ref.shape[2]

    def fetch(seq, buf, slot):
        return pltpu.make_async_copy(hn_hbm.at[seq], buf, sem.at[slot])

    @pl.when((b == 0) & (e == 0))
    def _():
        fetch(0, buf_a, 0).start()

    def run(cur, cur_slot, nxt, nxt_slot):
        @pl.when(e == 0)
        def _():
            fetch(b, cur, cur_slot).wait()

            @pl.when(b + 1 < pl.num_programs(0))
            def _():
                fetch(b + 1, nxt, nxt_slot).start()

        def body(j, _):
            i = idx_ref[0, 0, j]
            rows_sc[pl.ds(j, 1)] = cur[pl.ds(i, 1)]
            return 0

        lax.fori_loop(0, cap, body, 0, unroll=8)
        xe_ref[0, 0] = rows_sc[...].reshape(cap, xe_ref.shape[3]).astype(BF16)

    @pl.when(b % 2 == 0)
    def _():
        run(buf_a, 0, buf_b, 1)

    @pl.when(b % 2 == 1)
    def _():
        run(buf_b, 1, buf_a, 0)


def _gather(idx3, hn, cap):
    B, S, sub, lanes = hn.shape
    E = N_EXPERTS
    D = sub * lanes
    seq_buf = pltpu.VMEM((S, sub, lanes), F32)
    return pl.pallas_call(
        _gather_kernel, grid=(B, E),
        in_specs=[pl.BlockSpec((1, 1, cap), lambda b, e: (b * E + e, 0, 0), memory_space=pltpu.SMEM),
                  pl.BlockSpec(memory_space=pl.ANY)],
        out_specs=pl.BlockSpec((1, 1, cap, D), lambda b, e: (b, e, 0, 0)),
        out_shape=jax.ShapeDtypeStruct((B, E, cap, D), BF16),
        scratch_shapes=[seq_buf, seq_buf, pltpu.VMEM((cap, sub, lanes), F32), pltpu.SemaphoreType.DMA((2,))],
        compiler_params=_cparams(("arbitrary", "arbitrary")), name="moe_gather",
    )(idx3, hn)


def _ffn_kernel(xe_ref, wg_ref, wu_ref, wd_ref, ye_ref, wg_sc, wu_sc, wd_sc):
    f = pl.program_id(1)
    nb = xe_ref.shape[0]

    @pl.when(f == 0)
    def _():
        ye_ref[...] = jnp.zeros_like(ye_ref)

    wg_sc[...] = wg_ref[0].astype(BF16)
    wu_sc[...] = wu_ref[0].astype(BF16)
    wd_sc[...] = wd_ref[0].astype(BF16)

    def up(b):
        xb = xe_ref[b, 0]
        return _dot(xb, wg_sc[...]), _dot(xb, wu_sc[...])

    def down(b, h1, h2):
        hid = (h1 * _sigmoid(h1) * h2).astype(BF16)
        ye_ref[b, 0] += _dot(hid, wd_sc[...])

    hs = up(0)
    for b in range(nb):
        nxt = up(b + 1) if b + 1 < nb else None
        down(b, *hs)
        hs = nxt


def _ffn(xe, w_gate, w_up, w_down):
    B, E, cap, D = xe.shape
    F = w_gate.shape[2]
    tf = TF_FFN
    return pl.pallas_call(
        _ffn_kernel, grid=(E, F // tf),
        in_specs=[pl.BlockSpec((B, 1, cap, D), lambda e, f: (0, e, 0, 0)),
                  pl.BlockSpec((1, D, tf), lambda e, f: (e, 0, f)),
                  pl.BlockSpec((1, D, tf), lambda e, f: (e, 0, f)),
                  pl.BlockSpec((1, tf, D), lambda e, f: (e, f, 0))],
        out_specs=pl.BlockSpec((B, 1, cap, D), lambda e, f: (0, e, 0, 0)),
        out_shape=jax.ShapeDtypeStruct((B, E, cap, D), F32),
        scratch_shapes=[pltpu.VMEM((D, tf), BF16), pltpu.VMEM((D, tf), BF16), pltpu.VMEM((tf, D), BF16)],
        compiler_params=_cparams(("parallel", "arbitrary")), name="moe_ffn",
    )(xe, w_gate, w_up, w_down)


def _scatter_final_kernel(idx_ref, gate_ref, ye_ref, g_ref, x2_hbm, out_hbm, acc_sc, ye_sc, x_buf, o_buf, in_sem,
                          out_sem):
    b = pl.program_id(0)
    e = pl.program_id(1)
    cap = ye_ref.shape[2]
    S = acc_sc.shape[0]
    ch = x_buf.shape[1]
    D = x_buf.shape[2]

    @pl.when(e == 0)
    def _():
        acc_sc[...] = jnp.zeros_like(acc_sc)

    ye_sc[...] = ye_ref[0, 0].reshape(ye_sc.shape)

    def body(jb, _):
        base = pl.multiple_of(jb * SCATTER_GROUP, SCATTER_GROUP)
        ids = [idx_ref[0, 0, base + u] for u in range(SCATTER_GROUP)]
        new = [acc_sc[pl.ds(ids[u], 1)] + ye_sc[pl.ds(base + u, 1)] * gate_ref[0, 0, base + u]
               for u in range(SCATTER_GROUP)]
        for u in range(SCATTER_GROUP):
            acc_sc[pl.ds(ids[u], 1)] = new[u]
        return 0

    lax.fori_loop(0, cap // SCATTER_GROUP, body, 0)

    n_in = x_buf.shape[0]
    n_out = o_buf.shape[0]

    def in_copy(c):
        return pltpu.make_async_copy(x2_hbm.at[b, pl.ds(c * ch, ch)], x_buf.at[c % n_in], in_sem.at[c % n_in])

    def out_copy(c):
        return pltpu.make_async_copy(o_buf.at[c % n_out], out_hbm.at[b, pl.ds(c * ch, ch)], out_sem.at[c % n_out])

    @pl.when(e == pl.num_programs(1) - 1)
    def _():
        nch = S // ch
        ahead = n_in - 1
        for c in range(min(ahead, nch)):
            in_copy(c).start()
        for c in range(nch):
            in_copy(c).wait()
            if c + ahead < nch:
                in_copy(c + ahead).start()
            if c >= n_out:
                out_copy(c - n_out).wait()
            moe = acc_sc[c * ch:(c + 1) * ch].reshape(ch, D)
            o_buf[c % n_out] = _rms(x_buf[c % n_in] + moe, g_ref[...])
            out_copy(c).start()
        for c in range(max(nch - n_out, 0), nch):
            out_copy(c).wait()


def _scatter_final(idx3, gate3, ye, x2, g_final):
    B, E, cap, D = ye.shape
    S = x2.shape[1]
    lanes = D // SUBLANES
    smem = lambda: pl.BlockSpec((1, 1, cap), lambda b, e: (b * E + e, 0, 0), memory_space=pltpu.SMEM)
    return pl.pallas_call(
        _scatter_final_kernel, grid=(B, E),
        in_specs=[smem(), smem(), pl.BlockSpec((1, 1, cap, D), lambda b, e: (b, e, 0, 0)), _full((1, D)),
                  pl.BlockSpec(memory_space=pl.ANY)],
        out_specs=pl.BlockSpec(memory_space=pl.ANY),
        out_shape=jax.ShapeDtypeStruct((B, S, D), F32),
        scratch_shapes=[pltpu.VMEM((S, SUBLANES, lanes), F32), pltpu.VMEM((cap, SUBLANES, lanes), F32),
                        pltpu.VMEM((FINAL_IN_BUFS, FINAL_CHUNK, D), F32),
                        pltpu.VMEM((FINAL_OUT_BUFS, FINAL_CHUNK, D), F32),
                        pltpu.SemaphoreType.DMA((FINAL_IN_BUFS,)), pltpu.SemaphoreType.DMA((FINAL_OUT_BUFS,))],
        compiler_params=_cparams(("arbitrary", "arbitrary")), name="moe_scatter_final",
    )(idx3, gate3, ye, g_final.reshape(1, D), x2)


def kernel(x, mem, positions, g_mix, w_in, b_gates, conv_qk, g_q_a, w_q_b, g_kv_a, w_kv_b, g_head_mlstm,
           g_head_mla, w_out, g_mem_x, g_mem_kv, w_mem_q, w_mem_k, w_mem_v, w_mem_o, g_ffn, w_router,
           w_exp_gate, w_exp_up, w_exp_down, g_final):
    B, S, D = x.shape
    depth = g_mix.shape[0]
    assert depth == 1, "the MoE residual is folded into the final norm kernel, which assumes one layer"
    cap = EC_CAPACITY_FACTOR * S // N_EXPERTS
    pos_row = positions.astype(F32).reshape(B, 1, S)
    for l in range(depth):
        qT_m, k_m, vT_m, om, gc, gt, qT, kcat, vT = _inproj(
            x, pos_row, g_mix[l], w_in[l], b_gates[l], conv_qk[l], g_q_a[l], w_q_b[l], g_kv_a[l], w_kv_b[l])
        hf, hb = _mlstm(k_m, qT_m, vT_m, gc, gt)
        ya = _mla(qT, kcat, vT, g_head_mla[l])
        kmem, vmem = _memkv(mem, g_mem_kv[l], w_mem_k[l], w_mem_v[l])
        x2, hn, aff, aff_t = _post(x, hf, hb, om, ya, kmem, vmem, g_head_mlstm[l], w_out[l], g_mem_x[l],
                            w_mem_q[l], w_mem_o[l], g_ffn[l], w_router[l])
        idx, gate = _topk(aff, aff_t, cap)
        idx3 = idx.reshape(B * N_EXPERTS, 1, cap)
        xe = _gather(idx3, hn, cap)
        ye = _ffn(xe, w_exp_gate[l], w_exp_up[l], w_exp_down[l])
        x = _scatter_final(idx3, gate.reshape(B * N_EXPERTS, 1, cap), ye, x2, g_final)
    return x
```

```python
import math

import jax
import jax.numpy as jnp
from jax import lax
from jax.experimental import pallas as pl
from jax.experimental.pallas import tpu as pltpu

F32 = jnp.float32
BF16 = jnp.bfloat16
I32 = jnp.int32

EPS = 1e-6
N_MLSTM_HEADS = 4
MLSTM_HEAD_DIM = 128
D_MLSTM = N_MLSTM_HEADS * MLSTM_HEAD_DIM
N_MLA_HEADS = 8
MLA_NOPE_DIM = 64
MLA_ROPE_DIM = 32
MLA_QK_DIM = MLA_NOPE_DIM + MLA_ROPE_DIM
MLA_V_DIM = 64
MLA_V_ROWS = 80
Q_LORA_RANK = 256
KV_LORA_RANK = 128
ROPE_THETA = 10000.0
N_GATE = 4 * N_MLSTM_HEADS
N_MEM_HEADS = 4
N_EXPERTS = 16
EC_CAPACITY_FACTOR = 2

LANES = 128
SUBLANES = 8
VMEM_LIMIT_BYTES = 56 * 1024 * 1024

MLSTM_CHUNK = 256
MLSTM_PAD_ROWS = 16
MLSTM_CHUNKS_PER_STEP = 2
TM_INPROJ = 256
TQ_MLA = 256
MLA_CHUNKS = 16
TM_POST = 512
POST_SPLIT = 2
TF_FFN = 512
FINAL_CHUNK = 512
FINAL_IN_BUFS = 4
FINAL_OUT_BUFS = 3
CUM_BLOCK = 256
SCATTER_GROUP = 8


def _cparams(sem):
    return pltpu.CompilerParams(dimension_semantics=sem, vmem_limit_bytes=VMEM_LIMIT_BYTES)


def _dot(a, b):
    return jnp.dot(a, b, preferred_element_type=F32)


def _dot_nt(a, b):
    return lax.dot_general(a, b, (((1,), (1,)), ((), ())), preferred_element_type=F32)


def _rms(x, g):
    return x * lax.rsqrt(jnp.mean(x * x, axis=-1, keepdims=True) + EPS) * g


def _sigmoid(x):
    return 1.0 / (1.0 + jnp.exp(-x))


def _full(shape):
    return pl.BlockSpec(shape, lambda *_: (0,) * len(shape))


def _inproj_kernel(x_ref, xp_ref, xn_ref, posr_ref, gmix_ref, wqk_ref, wvt_ref, wo_ref, wmisc_ref, wkr_ref, bg_ref,
                   conv_ref, gqa_ref, wq_ref, gkva_ref, wk_ref, wv_ref, vones_ref, invfc_ref,
                   qT_m_ref, k_ref, vT_m_ref, o_ref, gc_ref, gt_ref, qT_ref, kcat_ref, vT_ref):
    tm = x_ref.shape[1]
    i = pl.program_id(1)
    last = pl.num_programs(1) - 1
    g = gmix_ref[...]
    xm = _rms(x_ref[0], g)
    xprev = _rms(xp_ref[0], g) * (i > 0).astype(F32)
    xnext = _rms(xn_ref[0], g) * (i < last).astype(F32)
    xm_bf = xm.astype(BF16)
    lhs = jnp.concatenate([xprev.astype(BF16), xm_bf, xnext.astype(BF16)], axis=0)

    pqk = _dot(lhs, wqk_ref[...])
    misc = _dot(xm_bf, wmisc_ref[...])
    vT_m = _dot_nt(wvt_ref[...], xm_bf)
    o_ref[0] = _dot(xm_bf, wo_ref[...])
    krT = _dot_nt(wkr_ref[...], xm_bf)
    cq = misc[:, :Q_LORA_RANK]
    ckv = misc[:, Q_LORA_RANK:Q_LORA_RANK + KV_LORA_RANK]
    gp = misc[:, Q_LORA_RANK + KV_LORA_RANK:] + bg_ref[...]
    ckvn = _rms(ckv, gkva_ref[...]).astype(BF16)
    cqn = _rms(cq, gqa_ref[...]).astype(BF16)
    kn = _dot(ckvn, wk_ref[...])
    qT = _dot_nt(wq_ref[...], cqn)
    vT_a = _dot_nt(wv_ref[...], ckvn)
    vT_m_ref[0] = vT_m.astype(BF16)

    rows = tm + 2 * SUBLANES
    up = pltpu.roll(pqk, 1, 0)[SUBLANES:SUBLANES + tm]
    dn = pltpu.roll(pqk, rows - 1, 0)[SUBLANES:SUBLANES + tm]
    mid = pqk[SUBLANES:SUBLANES + tm]
    cw = conv_ref[...]
    conv = up * cw[0:1] + mid * cw[1:2] + dn * cw[2:3]
    act = conv * _sigmoid(conv)
    qT_m_ref[0] = act[:, :D_MLSTM].T.astype(BF16)
    k_ref[0] = (act[:, D_MLSTM:] * (MLSTM_HEAD_DIM ** -0.5)).astype(BF16)

    lane = lax.broadcasted_iota(I32, (tm, LANES), 1)
    is_f = ((lane >= 4) & (lane < 8)) | ((lane >= 12) & (lane < 16))
    logsig = jnp.minimum(gp, 0.0) - jnp.log1p(jnp.exp(-jnp.abs(gp)))
    lf = jnp.where(is_f, logsig, gp)
    row_id = lax.broadcasted_iota(I32, (tm, LANES), 0)

    def scan(x, op, ident, suffix):
        step = 1
        while step < tm:
            if step < SUBLANES:
                ok = (row_id + step < tm) if suffix else (row_id >= step)
                shifted = jnp.where(ok, pltpu.roll(x, tm - step if suffix else step, 0), ident)
            else:
                pad = jnp.full((step, LANES), ident, F32)
                shifted = jnp.concatenate([x[step:], pad] if suffix else [pad, x[:tm - step]], axis=0)
            x = op(x, shifted)
            step *= 2
        return x

    cum = jnp.where(lane < 8, scan(lf, jnp.add, 0.0, False), scan(lf, jnp.add, 0.0, True))
    a = lf - pltpu.roll(cum, LANES - 4, 1)
    gc = jnp.where(is_f, cum, a)
    run_max = jnp.where(lane < 8, scan(gc, jnp.maximum, -jnp.inf, False), scan(gc, jnp.maximum, -jnp.inf, True))
    table = jnp.where(lane < N_GATE, gc, pltpu.roll(run_max, N_GATE, 1))
    gc_ref[0] = table
    gt_ref[0] = table.T[:2 * N_GATE]

    ang_t = invfc_ref[...] * posr_ref[0]
    cos_t = jnp.cos(ang_t)
    sin_t = jnp.sin(ang_t)
    r0, r1, r2 = MLA_NOPE_DIM, MLA_NOPE_DIM + MLA_ROPE_DIM // 2, MLA_QK_DIM

    def rope_rows(blk):
        t1 = blk[r0:r1]
        t2 = blk[r1:r2]
        return jnp.concatenate([blk[:r0], t1 * cos_t - t2 * sin_t, t2 * cos_t + t1 * sin_t, blk[r2:]], axis=0)

    kr = rope_rows(krT).T
    for h in range(N_MLA_HEADS):
        sl = slice(h * LANES, (h + 1) * LANES)
        kcat_ref[0, :, sl] = (kn[:, sl] + kr).astype(BF16)
        qT_ref[0, sl, :] = rope_rows(qT[sl]).astype(BF16)
    vT_ref[0] = (vT_a + vones_ref[...]).astype(BF16)


def _inproj(x, pos_row, g_mix, w_in, b_gates, conv_qk, g_q_a, w_q_b, g_kv_a, w_kv_b):
    B, S, D = x.shape
    tm = TM_INPROJ
    assert tm == MLSTM_CHUNK and S % tm == 0, "the gate scans treat one row tile as one mLSTM chunk"
    nt = S // tm
    hb = tm // SUBLANES
    sizes = [D_MLSTM, D_MLSTM, D_MLSTM, D_MLSTM, N_GATE, Q_LORA_RANK, KV_LORA_RANK, MLA_ROPE_DIM]
    off = [sum(sizes[:n]) for n in range(len(sizes) + 1)]
    assert w_in.shape[1] == off[-1]
    wqk = w_in[:, off[0]:off[2]].astype(BF16)
    wvt = w_in[:, off[2]:off[3]].T.astype(BF16)
    wo = w_in[:, off[3]:off[4]].astype(BF16)
    w_gate = w_in[:, off[4]:off[5]]
    w_cq = w_in[:, off[5]:off[6]]
    w_ckv = w_in[:, off[6]:off[7]]
    w_kr = w_in[:, off[7]:off[8]]
    wkr = jnp.zeros((LANES, D), F32).at[MLA_NOPE_DIM:MLA_QK_DIM].set(w_kr.T).astype(BF16)
    gate_blk = jnp.zeros((D, LANES), F32).at[:, :N_GATE].set(w_gate)
    wmisc = jnp.concatenate([w_cq, w_ckv, gate_blk], axis=1).astype(BF16)
    bg = jnp.zeros((1, LANES), F32).at[0, :N_GATE].set(b_gates)
    pad_heads = lambda w3: jnp.pad(w3, ((0, 0), (0, 0), (0, LANES - w3.shape[2]))).reshape(w3.shape[0], -1)
    wq = pad_heads(w_q_b.reshape(-1, N_MLA_HEADS, MLA_QK_DIM)).T.astype(BF16)
    wkv3 = w_kv_b.reshape(-1, N_MLA_HEADS, MLA_NOPE_DIM + MLA_V_DIM)
    wk = pad_heads(wkv3[:, :, :MLA_NOPE_DIM]).astype(BF16)
    wv = jnp.pad(wkv3[:, :, MLA_NOPE_DIM:], ((0, 0), (0, 0), (0, MLA_V_ROWS - MLA_V_DIM)))
    wv = wv.reshape(wv.shape[0], -1).T.astype(BF16)
    vones = jnp.zeros((N_MLA_HEADS, MLA_V_ROWS, tm), F32).at[:, MLA_V_DIM].set(1.0).reshape(-1, tm)
    inv_freq = ROPE_THETA ** (-jnp.arange(0, MLA_ROPE_DIM, 2, dtype=F32) / MLA_ROPE_DIM)
    invf_col = inv_freq.reshape(-1, 1)

    row = lambda w: pl.BlockSpec((1, tm, w), lambda b, i: (b, i, 0))
    in_specs = [
        row(D),
        pl.BlockSpec((1, SUBLANES, D), lambda b, i: (b, jnp.maximum(i * hb - 1, 0), 0)),
        pl.BlockSpec((1, SUBLANES, D), lambda b, i: (b, jnp.minimum((i + 1) * hb, S // SUBLANES - 1), 0)),
        pl.BlockSpec((1, 1, tm), lambda b, i: (b, 0, i)),
        _full((1, D)), _full(wqk.shape), _full(wvt.shape), _full(wo.shape), _full(wmisc.shape), _full(wkr.shape),
        _full((1, LANES)), _full(conv_qk.shape), _full((1, Q_LORA_RANK)), _full(wq.shape), _full((1, KV_LORA_RANK)),
        _full(wk.shape), _full(wv.shape), _full(vones.shape), _full(invf_col.shape),
    ]
    out_shape = [
        jax.ShapeDtypeStruct((B, D_MLSTM, S), BF16),
        jax.ShapeDtypeStruct((B, S, D_MLSTM), BF16),
        jax.ShapeDtypeStruct((B, D_MLSTM, S), BF16),
        jax.ShapeDtypeStruct((B, S, D_MLSTM), F32),
        jax.ShapeDtypeStruct((B, S, LANES), F32),
        jax.ShapeDtypeStruct((B, 2 * N_GATE, S), F32),
        jax.ShapeDtypeStruct((B, N_MLA_HEADS * LANES, S), BF16),
        jax.ShapeDtypeStruct((B, S, N_MLA_HEADS * LANES), BF16),
        jax.ShapeDtypeStruct((B, N_MLA_HEADS * MLA_V_ROWS, S), BF16),
    ]
    out_specs = [
        pl.BlockSpec((1, D_MLSTM, tm), lambda b, i: (b, 0, i)),
        row(D_MLSTM),
        pl.BlockSpec((1, D_MLSTM, tm), lambda b, i: (b, 0, i)),
        row(D_MLSTM), row(LANES),
        pl.BlockSpec((1, 2 * N_GATE, tm), lambda b, i: (b, 0, i)),
        pl.BlockSpec((1, N_MLA_HEADS * LANES, tm), lambda b, i: (b, 0, i)),
        row(N_MLA_HEADS * LANES),
        pl.BlockSpec((1, N_MLA_HEADS * MLA_V_ROWS, tm), lambda b, i: (b, 0, i)),
    ]
    return pl.pallas_call(
        _inproj_kernel, grid=(B, nt), in_specs=in_specs, out_specs=out_specs, out_shape=out_shape,
        compiler_params=_cparams(("parallel", "parallel")), name="inproj",
    )(x, x, x, pos_row, g_mix.reshape(1, D), wqk, wvt, wo, wmisc, wkr, bg, conv_qk, g_q_a.reshape(1, -1), wq,
      g_kv_a.reshape(1, -1), wk, wv, vones, invf_col)


def _mlstm_kernel(kf_ref, qf_ref, vf_ref, gcf_ref, gtf_ref, kb_ref, qb_ref, vb_ref, gcb_ref, gtb_ref,
                  hf_ref, hb_ref, cn_ref, m_ref):
    T = MLSTM_CHUNK
    hd = MLSTM_HEAD_DIM

    @pl.when(pl.program_id(1) == 0)
    def _():
        cn_ref[...] = jnp.zeros_like(cn_ref)
        m_ref[...] = jnp.zeros_like(m_ref)

    s_idx = lax.broadcasted_iota(I32, (T, T), 0)
    t_idx = lax.broadcasted_iota(I32, (T, T), 1)
    ones_rows = (lax.broadcasted_iota(I32, (MLSTM_PAD_ROWS, T), 0) == 0).astype(BF16)
    neg_inf = F32(-jnp.inf)

    def chunk(c):
        streams = []
        for d, (k_ref, qT_ref, vT_ref, gc_ref, gt_ref, h_ref) in enumerate(
                ((kf_ref, qf_ref, vf_ref, gcf_ref, gtf_ref, hf_ref),
                 (kb_ref, qb_ref, vb_ref, gcb_ref, gtb_ref, hb_ref))):
            end = T - 1 if d == 0 else 0
            pos = c if d == 0 else MLSTM_CHUNKS_PER_STEP - 1 - c
            tok = slice(pos * T, (pos + 1) * T)
            gc = gc_ref[0, tok]
            gt = gt_ref[0, :, tok]
            for h in range(N_MLSTM_HEADS):
                st = d * N_MLSTM_HEADS + h
                la = d * 8 + h
                a_row = gt[la:la + 1]
                b_row = gt[la + 4:la + 5]
                m_prev = m_ref[st][0:1, 0:1]
                kh = k_ref[0, tok, h * hd:(h + 1) * hd]
                qTh = qT_ref[0, h * hd:(h + 1) * hd, tok]
                v_ext = jnp.concatenate([vT_ref[0, h * hd:(h + 1) * hd, tok], ones_rows], axis=0)
                cn = cn_ref[st]
                m_run = jnp.maximum(gt[N_GATE + la:N_GATE + la + 1], m_prev)
                m_end = m_run[:, end:end + 1]
                vw = (v_ext.astype(F32) * jnp.exp(a_row - m_end)).astype(BF16)
                streams.append(dict(
                    st=st, d=d, h=h, h_ref=h_ref, tok=tok, a_col=gc[:, la:la + 1], b_row=b_row, m_prev=m_prev,
                    m_run=m_run, m_end=m_end, g_sum=b_row[:, end:end + 1], cn=cn, v_ext=v_ext,
                    s=_dot(kh, qTh),
                    qcn=_dot(cn.astype(BF16), qTh),
                    upd=_dot(vw, kh)))
        for x in streams:
            mask = (s_idx <= t_idx) if x["d"] == 0 else (s_idx >= t_idx)
            e = jnp.exp(jnp.where(mask, x["a_col"] - x["m_run"], neg_inf))
            x["pw"] = (e * x["s"]).astype(BF16)
        for x in streams:
            x["pv"] = _dot(x["v_ext"], x["pw"])
        for x in streams:
            st, h = x["st"], x["h"]
            iw = jnp.exp(x["m_prev"] - x["m_run"])
            num = iw * x["qcn"][:hd] + x["pv"][:hd]
            den = iw * x["qcn"][hd:hd + 1] + x["pv"][hd:hd + 1]
            floor = jnp.exp(-(x["b_row"] + x["m_run"]))
            x["h_ref"][0, h * hd:(h + 1) * hd, x["tok"]] = num / jnp.maximum(jnp.abs(den), floor)
            cn_ref[st] = jnp.exp(x["m_prev"] - x["m_end"]) * x["cn"] + x["upd"]
            m_ref[st] = jnp.broadcast_to(x["g_sum"] + x["m_end"], (SUBLANES, LANES))

    for c in range(MLSTM_CHUNKS_PER_STEP):
        chunk(c)


def _mlstm(k, qT, vT, gc, gt):
    B, S, _ = k.shape
    T = MLSTM_CHUNK * MLSTM_CHUNKS_PER_STEP
    nc = S // T
    fwd = lambda b, j: (b, j, 0)
    bwd = lambda b, j: (b, nc - 1 - j, 0)
    fwd_t = lambda b, j: (b, 0, j)
    bwd_t = lambda b, j: (b, 0, nc - 1 - j)

    def specs(row_map, col_map):
        return [pl.BlockSpec((1, T, D_MLSTM), row_map), pl.BlockSpec((1, D_MLSTM, T), col_map),
                pl.BlockSpec((1, D_MLSTM, T), col_map), pl.BlockSpec((1, T, LANES), row_map),
                pl.BlockSpec((1, 2 * N_GATE, T), col_map)]

    n_state = 2 * N_MLSTM_HEADS
    return pl.pallas_call(
        _mlstm_kernel, grid=(B, nc),
        in_specs=specs(fwd, fwd_t) + specs(bwd, bwd_t),
        out_specs=[pl.BlockSpec((1, D_MLSTM, T), fwd_t), pl.BlockSpec((1, D_MLSTM, T), bwd_t)],
        out_shape=[jax.ShapeDtypeStruct((B, D_MLSTM, S), F32)] * 2,
        scratch_shapes=[pltpu.VMEM((n_state, MLSTM_HEAD_DIM + MLSTM_PAD_ROWS, MLSTM_HEAD_DIM), F32),
                        pltpu.VMEM((n_state, SUBLANES, LANES), F32)],
        compiler_params=_cparams(("parallel", "arbitrary")), name="mlstm",
    )(k, qT, vT, gc, gt, k, qT, vT, gc, gt)


def _mla_kernel(qT_ref, k_ref, vT_ref, g_ref, o_ref, sa_sc, sb_sc, ma_sc, mb_sc):
    t = pl.program_id(0)
    c = (MLA_QK_DIM ** -0.5) * math.log2(math.e)
    S = k_ref.shape[1]
    tq = qT_ref.shape[2]
    ck = S // MLA_CHUNKS

    @pl.when(t == 0)
    def _():
        sb_sc[...] = jnp.zeros(sb_sc.shape, F32)
        mb_sc[...] = jnp.zeros(mb_sc.shape, F32)

    def step(s_cur, m_cur, s_prev, m_prev_ref):
        m_prev = [m_prev_ref[h][0:1] for h in range(2)]
        m_run = [jnp.full((1, tq), -jnp.inf, F32) for _ in range(2)]
        accs = [None, None]
        for j in range(MLA_CHUNKS):
            rows = slice(j * ck, (j + 1) * ck)
            for h in range(2):
                s = _dot(k_ref[0, rows, h * LANES:(h + 1) * LANES], qT_ref[0, h * LANES:(h + 1) * LANES, :])
                m_run[h] = jnp.maximum(m_run[h], jnp.max(s, axis=0, keepdims=True))
                s_cur[h, rows, :] = s
            for h in range(2):
                p = jnp.exp2((s_prev[h, rows, :] - m_prev[h]) * c).astype(BF16)
                part = _dot(vT_ref[0, h * MLA_V_ROWS:(h + 1) * MLA_V_ROWS, rows], p)
                accs[h] = part if j == 0 else accs[h] + part
        for h in range(2):
            m_cur[h] = jnp.broadcast_to(m_run[h], (SUBLANES, tq))
        ys = []
        for h in range(2):
            acc = accs[h]
            o = acc[:MLA_V_DIM] / acc[MLA_V_DIM:MLA_V_DIM + 1]
            ms = jnp.mean(o * o, axis=0, keepdims=True)
            ys.append(o * lax.rsqrt(ms + EPS))
        o_ref[0] = jnp.concatenate(ys, axis=0).T * g_ref[...]

    @pl.when(t % 2 == 0)
    def _():
        step(sa_sc, ma_sc, sb_sc, mb_sc)

    @pl.when(t % 2 == 1)
    def _():
        step(sb_sc, mb_sc, sa_sc, ma_sc)


def _mla(qT, kcat, vT, g_head_mla):
    B, S, _ = kcat.shape
    tq = TQ_MLA
    npair = N_MLA_HEADS // 2
    nq = S // tq
    nt = B * npair * nq

    def tile(t):
        return t // (npair * nq), (t // nq) % npair, t % nq

    def a_map(f):
        return lambda t: f(*tile(jnp.minimum(t, nt - 1)))

    def b_map(f):
        return lambda t: f(*tile(jnp.maximum(t - 1, 0)))

    return pl.pallas_call(
        _mla_kernel, grid=(nt + 1,),
        in_specs=[pl.BlockSpec((1, 2 * LANES, tq), a_map(lambda b, p, i: (b, p, i))),
                  pl.BlockSpec((1, S, 2 * LANES), a_map(lambda b, p, i: (b, 0, p))),
                  pl.BlockSpec((1, 2 * MLA_V_ROWS, S), b_map(lambda b, p, i: (b, p, 0))),
                  pl.BlockSpec((1, LANES), b_map(lambda b, p, i: (0, p)))],
        out_specs=pl.BlockSpec((1, tq, LANES), b_map(lambda b, p, i: (b, i, p))),
        out_shape=jax.ShapeDtypeStruct((B, S, N_MLA_HEADS * MLA_V_DIM), F32),
        scratch_shapes=[pltpu.VMEM((2, S, tq), F32), pltpu.VMEM((2, S, tq), F32),
                        pltpu.VMEM((2, SUBLANES, tq), F32), pltpu.VMEM((2, SUBLANES, tq), F32)],
        compiler_params=_cparams(("arbitrary",)), name="mla_attn",
    )(qT, kcat, vT, g_head_mla.reshape(1, -1))


def _memkv_kernel(mem_ref, g_ref, wk_ref, wv_ref, k_ref, v_ref):
    mn = _rms(mem_ref[0], g_ref[...]).astype(BF16)
    k_ref[0] = _dot(mn, wk_ref[...]).astype(BF16)
    v_ref[0] = _dot(mn, wv_ref[...]).astype(BF16)


def _memkv(mem, g_mem_kv, w_k, w_v):
    B, M, D = mem.shape
    blk = pl.BlockSpec((1, M, D), lambda b: (b, 0, 0))
    return pl.pallas_call(
        _memkv_kernel, grid=(B,),
        in_specs=[blk, _full((1, D)), _full(w_k.shape), _full(w_v.shape)],
        out_specs=[blk, blk], out_shape=[jax.ShapeDtypeStruct((B, M, D), BF16)] * 2,
        compiler_params=_cparams(("parallel",)), name="memkv",
    )(mem, g_mem_kv.reshape(1, D), w_k.astype(BF16), w_v.astype(BF16))


def _post_kernel(x_ref, hf_ref, hb_ref, om_ref, ya_ref, km_ref, vm_ref, ghm_ref, wout_ref, gmx_ref, wmq_ref,
                 wmo_ref, gffn_ref, wrh_ref, wrl_ref, x2_ref, hn_ref, aff_ref, afft_ref):
    tm = x_ref.shape[1]
    D = x_ref.shape[2]
    hr = tm // POST_SPLIT
    subs = [slice(i * hr, (i + 1) * hr) for i in range(POST_SPLIT)]
    ghm = ghm_ref[...]
    dm = D // N_MEM_HEADS
    heads = [slice(h * dm, (h + 1) * dm) for h in range(N_MEM_HEADS)]

    ycat = []
    for r in subs:
        hm = (hf_ref[0, :, r] + hb_ref[0, :, r]).T
        gate = _sigmoid(om_ref[0, r])
        parts = []
        for h in range(N_MLSTM_HEADS):
            sl = slice(h * MLSTM_HEAD_DIM, (h + 1) * MLSTM_HEAD_DIM)
            parts.append(gate[:, sl] * _rms(hm[:, sl], ghm[:, sl]))
        ycat.append(jnp.concatenate(parts + [ya_ref[0, r]], axis=1).astype(BF16))
    x1 = [x_ref[0, r] + _dot(y, wout_ref[...]) for r, y in zip(subs, ycat)]

    xn = [_rms(v, gmx_ref[...]).astype(BF16) for v in x1]
    qm = [_dot(v, wmq_ref[...]).astype(BF16) for v in xn]
    km = km_ref[0]
    vm = vm_ref[0]
    scores = [[_dot_nt(q[:, sl], km[:, sl]) * (dm ** -0.5) for sl in heads] for q in qm]
    probs = []
    for per_sub in scores:
        ps = []
        for s in per_sub:
            e = jnp.exp(s - jnp.max(s, axis=1, keepdims=True))
            ps.append((e / jnp.sum(e, axis=1, keepdims=True)).astype(BF16))
        probs.append(ps)
    om = [jnp.concatenate([_dot(p, vm[:, sl]) for p, sl in zip(ps, heads)], axis=1).astype(BF16) for ps in probs]
    x2 = [v + _dot(o, wmo_ref[...]) for v, o in zip(x1, om)]

    hns = []
    for r, v in zip(subs, x2):
        x2_ref[0, r] = v
        hn = _rms(v, gffn_ref[...])
        hn_ref[0, r] = hn.reshape(hr, SUBLANES, D // SUBLANES)
        hi = hn.astype(BF16)
        hns.append((hi, (hn - hi.astype(F32)).astype(BF16)))
    wrh = wrh_ref[...]
    logits = [_dot(hi, wrh) + (_dot(hi, wrl_ref[...]) + _dot(lo, wrh)) for hi, lo in hns]
    valid = lax.broadcasted_iota(I32, (hr, LANES), 1) < N_EXPERTS
    for r, lg in zip(subs, logits):
        lg = jnp.where(valid, lg, -jnp.inf)
        e = jnp.exp(lg - jnp.max(lg, axis=1, keepdims=True))
        aff = e / jnp.sum(e, axis=1, keepdims=True)
        aff_ref[0, r] = aff
        afft_ref[0, :, r] = aff.T[:N_EXPERTS]


def _post(x, hf, hb, om, ya, kmem, vmem, g_head_mlstm, w_out, g_mem_x, w_mem_q, w_mem_o, g_ffn, w_router):
    B, S, D = x.shape
    tm = TM_POST
    M = kmem.shape[1]
    wr = jnp.zeros((D, LANES), F32).at[:, :N_EXPERTS].set(w_router)
    wrh = wr.astype(BF16)
    wrl = (wr - wrh.astype(F32)).astype(BF16)
    row = lambda w: pl.BlockSpec((1, tm, w), lambda b, i: (b, i, 0))
    col = lambda w: pl.BlockSpec((1, w, tm), lambda b, i: (b, 0, i))
    mem_spec = pl.BlockSpec((1, M, D), lambda b, i: (b, 0, 0))
    return pl.pallas_call(
        _post_kernel, grid=(B, S // tm),
        in_specs=[row(D), col(D_MLSTM), col(D_MLSTM), row(D_MLSTM), row(D_MLSTM), mem_spec, mem_spec,
                  _full((1, D_MLSTM)), _full((D, D)), _full((1, D)), _full((D, D)), _full((D, D)),
                  _full((1, D)), _full((D, LANES)), _full((D, LANES))],
        out_specs=[row(D), pl.BlockSpec((1, tm, SUBLANES, D // SUBLANES), lambda b, i: (b, i, 0, 0)), row(LANES),
                   pl.BlockSpec((1, N_EXPERTS, tm), lambda b, i: (b, 0, i))],
        out_shape=[jax.ShapeDtypeStruct((B, S, D), F32), jax.ShapeDtypeStruct((B, S, SUBLANES, D // SUBLANES), F32),
                   jax.ShapeDtypeStruct((B, S, LANES), F32), jax.ShapeDtypeStruct((B, N_EXPERTS, S), F32)],
        compiler_params=_cparams(("parallel", "parallel")), name="post_mixer",
    )(x, hf, hb, om, ya, kmem, vmem, g_head_mlstm.reshape(1, -1), w_out.astype(BF16), g_mem_x.reshape(1, D),
      w_mem_q.astype(BF16), w_mem_o.astype(BF16), g_ffn.reshape(1, D), wrh, wrl)


def _topk_kernel(aff_ref, afft_ref, affb_ref, idx_ref, gate_ref, idx_sc, gate_sc):
    S = aff_ref.shape[1]
    cap = idx_ref.shape[2]
    nblk = S // CUM_BLOCK

    def open_interval(c):
        lo, hi = c
        mid = 0.5 * (lo + hi)
        return jnp.max(jnp.where((mid > lo) & (mid < hi), 1.0, 0.0)) > 0.0

    def bisect(c):
        lo, hi = c
        mid = 0.5 * (lo + hi)
        cnt = jnp.sum((afft_ref[0] >= mid).astype(F32), axis=1, keepdims=True)
        ge = cnt >= cap
        return jnp.where(ge, mid, lo), jnp.where(ge, hi, mid)

    lo_c, hi_c = lax.while_loop(open_interval, bisect,
                                (jnp.zeros((N_EXPERTS, 1), F32), jnp.full((N_EXPERTS, 1), 2.0, F32)))
    need_c = cap - jnp.sum((afft_ref[0] >= hi_c).astype(F32), axis=1, keepdims=True)
    eye = (lax.broadcasted_iota(I32, (N_EXPERTS, LANES), 0) == lax.broadcasted_iota(I32, (N_EXPERTS, LANES), 1))
    to_row = lambda col: jnp.sum(jnp.where(eye, col, 0.0), axis=0, keepdims=True)
    lo, hi, need = to_row(lo_c), to_row(hi_c), to_row(need_c)

    r = lax.broadcasted_iota(I32, (CUM_BLOCK, CUM_BLOCK), 0)
    c = lax.broadcasted_iota(I32, (CUM_BLOCK, CUM_BLOCK), 1)
    tri = (c <= r).astype(BF16)

    def cum_body(blk, carry):
        ce, cs = carry
        rows = pl.ds(pl.multiple_of(blk * CUM_BLOCK, CUM_BLOCK), CUM_BLOCK)
        a = aff_ref[0, rows, :]
        sure = a >= hi
        tie = (a >= lo) & jnp.logical_not(sure)
        eq = tie.astype(F32)
        eq_incl = _dot(tri, eq.astype(BF16)) + ce
        sel = jnp.where(sure | (tie & (eq_incl - eq < need)), 1.0, 0.0)
        local_incl = _dot(tri, sel.astype(BF16))
        rank_local = jnp.where(sel > 0.0, local_incl - sel, -2.0).astype(BF16)
        base = lax.convert_element_type(blk * CUM_BLOCK, F32)
        for e in range(N_EXPERTS):
            slot_local = jnp.clip(j_row - cs[:, e:e + 1], -1.0, CUM_BLOCK + 1.0).astype(BF16)
            onehot = jnp.where(rank_local[:, e:e + 1] == slot_local, one_bf, zero_bf)
            a_row = affb_ref[0, e, pl.ds(blk, 1), :]
            a_hi = a_row.astype(BF16).astype(F32)
            a_mid = (a_row - a_hi).astype(BF16).astype(F32)
            a_lo = a_row - a_hi - a_mid
            lhs = jnp.concatenate([tok_rows, a_hi, a_mid, a_lo, pad_rows], axis=0).astype(BF16)
            hit = _dot(lhs, onehot)
            idx_sc[e:e + 1, :] += hit[0:1] + base * hit[1:2]
            gate_sc[e:e + 1, :] += hit[2:3] + hit[3:4] + hit[4:5]
        return eq_incl[CUM_BLOCK - 1:CUM_BLOCK], cs + local_incl[CUM_BLOCK - 1:CUM_BLOCK]

    j_row = lax.broadcasted_iota(I32, (1, cap), 1).astype(F32)
    tok = lax.broadcasted_iota(I32, (1, CUM_BLOCK), 1).astype(F32)
    tok_rows = jnp.concatenate([tok, jnp.ones((1, CUM_BLOCK), F32)], axis=0)
    pad_rows = jnp.zeros((SUBLANES - 5, CUM_BLOCK), F32)
    one_bf = jnp.ones((CUM_BLOCK, cap), BF16)
    zero_bf = jnp.zeros((CUM_BLOCK, cap), BF16)
    idx_sc[...] = jnp.zeros_like(idx_sc)
    gate_sc[...] = jnp.zeros_like(gate_sc)
    zero = jnp.zeros((1, LANES), F32)
    lax.fori_loop(0, nblk, cum_body, (zero, zero))
    idx_ref[0] = idx_sc[...].astype(I32)
    gate_ref[0] = gate_sc[...]


def _topk(aff, aff_t, cap):
    B, S, _ = aff.shape
    nblk = S // CUM_BLOCK
    out = pl.BlockSpec((1, N_EXPERTS, cap), lambda b: (b, 0, 0))
    return pl.pallas_call(
        _topk_kernel, grid=(B,),
        in_specs=[pl.BlockSpec((1, S, LANES), lambda b: (b, 0, 0)),
                  pl.BlockSpec((1, N_EXPERTS, S), lambda b: (b, 0, 0)),
                  pl.BlockSpec((1, N_EXPERTS, nblk, CUM_BLOCK), lambda b: (b, 0, 0, 0))],
        out_specs=[out, out],
        out_shape=[jax.ShapeDtypeStruct((B, N_EXPERTS, cap), I32), jax.ShapeDtypeStruct((B, N_EXPERTS, cap), F32)],
        scratch_shapes=[pltpu.VMEM((N_EXPERTS, cap), F32), pltpu.VMEM((N_EXPERTS, cap), F32)],
        compiler_params=_cparams(("parallel",)), name="topk",
    )(aff, aff_t, aff_t.reshape(B, N_EXPERTS, nblk, CUM_BLOCK))


def _gather_kernel(idx_ref, hn_hbm, xe_ref, buf_a, buf_b, rows_sc, sem):
    b = pl.program_id(0)
    e = pl.program_id(1)
    cap = xe_ref.shape[2]

    def fetch(seq, buf, slot):
        return pltpu.make_async_copy(hn_hbm.at[seq], buf, sem.at[slot])

    @pl.when((b == 0) & (e == 0))
    def _():
        fetch(0, buf_a, 0).start()

    def run(cur, cur_slot, nxt, nxt_slot):
        @pl.when(e == 0)
        def _():
            fetch(b, cur, cur_slot).wait()

            @pl.when(b + 1 < pl.num_programs(0))
            def _():
                fetch(b + 1, nxt, nxt_slot).start()

        def body(j, _):
            i = idx_ref[0, 0, j]
            rows_sc[pl.ds(j, 1)] = cur[pl.ds(i, 1)]
            return 0

        lax.fori_loop(0, cap, body, 0, unroll=8)
        xe_ref[0, 0] = rows_sc[...].reshape(cap, xe_ref.shape[3]).astype(BF16)

    @pl.when(b % 2 == 0)
    def _():
        run(buf_a, 0, buf_b, 1)

    @pl.when(b % 2 == 1)
    def _():
        run(buf_b, 1, buf_a, 0)


def _gather(idx3, hn, cap):
    B, S, sub, lanes = hn.shape
    E = N_EXPERTS
    D = sub * lanes
    seq_buf = pltpu.VMEM((S, sub, lanes), F32)
    return pl.pallas_call(
        _gather_kernel, grid=(B, E),
        in_specs=[pl.BlockSpec((1, 1, cap), lambda b, e: (b * E + e, 0, 0), memory_space=pltpu.SMEM),
                  pl.BlockSpec(memory_space=pl.ANY)],
        out_specs=pl.BlockSpec((1, 1, cap, D), lambda b, e: (b, e, 0, 0)),
        out_shape=jax.ShapeDtypeStruct((B, E, cap, D), BF16),
        scratch_shapes=[seq_buf, seq_buf, pltpu.VMEM((cap, sub, lanes), F32), pltpu.SemaphoreType.DMA((2,))],
        compiler_params=_cparams(("arbitrary", "arbitrary")), name="moe_gather",
    )(idx3, hn)


def _ffn_kernel(xe_ref, wg_ref, wu_ref, wd_ref, ye_ref, wg_sc, wu_sc, wd_sc):
    f = pl.program_id(1)
    nb = xe_ref.shape[0]

    @pl.when(f == 0)
    def _():
        ye_ref[...] = jnp.zeros_like(ye_ref)

    wg_sc[...] = wg_ref[0].astype(BF16)
    wu_sc[...] = wu_ref[0].astype(BF16)
    wd_sc[...] = wd_ref[0].astype(BF16)

    def up(b):
        xb = xe_ref[b, 0]
        return _dot(xb, wg_sc[...]), _dot(xb, wu_sc[...])

    def down(b, h1, h2):
        hid = (h1 * _sigmoid(h1) * h2).astype(BF16)
        ye_ref[b, 0] += _dot(hid, wd_sc[...])

    hs = up(0)
    for b in range(nb):
        nxt = up(b + 1) if b + 1 < nb else None
        down(b, *hs)
        hs = nxt


def _ffn(xe, w_gate, w_up, w_down):
    B, E, cap, D = xe.shape
    F = w_gate.shape[2]
    tf = TF_FFN
    return pl.pallas_call(
        _ffn_kernel, grid=(E, F // tf),
        in_specs=[pl.BlockSpec((B, 1, cap, D), lambda e, f: (0, e, 0, 0)),
                  pl.BlockSpec((1, D, tf), lambda e, f: (e, 0, f)),
                  pl.BlockSpec((1, D, tf), lambda e, f: (e, 0, f)),
                  pl.BlockSpec((1, tf, D), lambda e, f: (e, f, 0))],
        out_specs=pl.BlockSpec((B, 1, cap, D), lambda e, f: (0, e, 0, 0)),
        out_shape=jax.ShapeDtypeStruct((B, E, cap, D), F32),
        scratch_shapes=[pltpu.VMEM((D, tf), BF16), pltpu.VMEM((D, tf), BF16), pltpu.VMEM((tf, D), BF16)],
        compiler_params=_cparams(("parallel", "arbitrary")), name="moe_ffn",
    )(xe, w_gate, w_up, w_down)


def _scatter_final_kernel(idx_ref, gate_ref, ye_ref, g_ref, x2_hbm, out_hbm, acc_sc, ye_sc, x_buf, o_buf, in_sem,
                          out_sem):
    b = pl.program_id(0)
    e = pl.program_id(1)
    cap = ye_ref.shape[2]
    S = acc_sc.shape[0]
    ch = x_buf.shape[1]
    D = x_buf.shape[2]

    @pl.when(e == 0)
    def _():
        acc_sc[...] = jnp.zeros_like(acc_sc)

    ye_sc[...] = ye_ref[0, 0].reshape(ye_sc.shape)

    def body(jb, _):
        base = pl.multiple_of(jb * SCATTER_GROUP, SCATTER_GROUP)
        ids = [idx_ref[0, 0, base + u] for u in range(SCATTER_GROUP)]
        new = [acc_sc[pl.ds(ids[u], 1)] + ye_sc[pl.ds(base + u, 1)] * gate_ref[0, 0, base + u]
               for u in range(SCATTER_GROUP)]
        for u in range(SCATTER_GROUP):
            acc_sc[pl.ds(ids[u], 1)] = new[u]
        return 0

    lax.fori_loop(0, cap // SCATTER_GROUP, body, 0)

    n_in = x_buf.shape[0]
    n_out = o_buf.shape[0]

    def in_copy(c):
        return pltpu.make_async_copy(x2_hbm.at[b, pl.ds(c * ch, ch)], x_buf.at[c % n_in], in_sem.at[c % n_in])

    def out_copy(c):
        return pltpu.make_async_copy(o_buf.at[c % n_out], out_hbm.at[b, pl.ds(c * ch, ch)], out_sem.at[c % n_out])

    @pl.when(e == pl.num_programs(1) - 1)
    def _():
        nch = S // ch
        ahead = n_in - 1
        for c in range(min(ahead, nch)):
            in_copy(c).start()
        for c in range(nch):
            in_copy(c).wait()
            if c + ahead < nch:
                in_copy(c + ahead).start()
            if c >= n_out:
                out_copy(c - n_out).wait()
            moe = acc_sc[c * ch:(c + 1) * ch].reshape(ch, D)
            o_buf[c % n_out] = _rms(x_buf[c % n_in] + moe, g_ref[...])
            out_copy(c).start()
        for c in range(max(nch - n_out, 0), nch):
            out_copy(c).wait()


def _scatter_final(idx3, gate3, ye, x2, g_final):
    B, E, cap, D = ye.shape
    S = x2.shape[1]
    lanes = D // SUBLANES
    smem = lambda: pl.BlockSpec((1, 1, cap), lambda b, e: (b * E + e, 0, 0), memory_space=pltpu.SMEM)
    return pl.pallas_call(
        _scatter_final_kernel, grid=(B, E),
        in_specs=[smem(), smem(), pl.BlockSpec((1, 1, cap, D), lambda b, e: (b, e, 0, 0)), _full((1, D)),
                  pl.BlockSpec(memory_space=pl.ANY)],
        out_specs=pl.BlockSpec(memory_space=pl.ANY),
        out_shape=jax.ShapeDtypeStruct((B, S, D), F32),
        scratch_shapes=[pltpu.VMEM((S, SUBLANES, lanes), F32), pltpu.VMEM((cap, SUBLANES, lanes), F32),
                        pltpu.VMEM((FINAL_IN_BUFS, FINAL_CHUNK, D), F32),
                        pltpu.VMEM((FINAL_OUT_BUFS, FINAL_CHUNK, D), F32),
                        pltpu.SemaphoreType.DMA((FINAL_IN_BUFS,)), pltpu.SemaphoreType.DMA((FINAL_OUT_BUFS,))],
        compiler_params=_cparams(("arbitrary", "arbitrary")), name="moe_scatter_final",
    )(idx3, gate3, ye, g_final.reshape(1, D), x2)


def kernel(x, mem, positions, g_mix, w_in, b_gates, conv_qk, g_q_a, w_q_b, g_kv_a, w_kv_b, g_head_mlstm,
           g_head_mla, w_out, g_mem_x, g_mem_kv, w_mem_q, w_mem_k, w_mem_v, w_mem_o, g_ffn, w_router,
           w_exp_gate, w_exp_up, w_exp_down, g_final):
    B, S, D = x.shape
    depth = g_mix.shape[0]
    assert depth == 1, "the MoE residual is folded into the final norm kernel, which assumes one layer"
    cap = EC_CAPACITY_FACTOR * S // N_EXPERTS
    pos_row = positions.astype(F32).reshape(B, 1, S)
    for l in range(depth):
        qT_m, k_m, vT_m, om, gc, gt, qT, kcat, vT = _inproj(
            x, pos_row, g_mix[l], w_in[l], b_gates[l], conv_qk[l], g_q_a[l], w_q_b[l], g_kv_a[l], w_kv_b[l])
        hf, hb = _mlstm(k_m, qT_m, vT_m, gc, gt)
        ya = _mla(qT, kcat, vT, g_head_mla[l])
        kmem, vmem = _memkv(mem, g_mem_kv[l], w_mem_k[l], w_mem_v[l])
        x2, hn, aff, aff_t = _post(x, hf, hb, om, ya, kmem, vmem, g_head_mlstm[l], w_out[l], g_mem_x[l],
                            w_mem_q[l], w_mem_o[l], g_ffn[l], w_router[l])
        idx, gate = _topk(aff, aff_t, cap)
        idx3 = idx.reshape(B * N_EXPERTS, 1, cap)
        xe = _gather(idx3, hn, cap)
        ye = _ffn(xe, w_exp_gate[l], w_exp_up[l], w_exp_down[l])
        x = _scatter_final(idx3, gate.reshape(B * N_EXPERTS, 1, cap), ye, x2, g_final)
    return x
```

```python
import math

import jax
import jax.numpy as jnp
from jax import lax
from jax.experimental import pallas as pl
from jax.experimental.pallas import tpu as pltpu

F32 = jnp.float32
BF16 = jnp.bfloat16
I32 = jnp.int32

EPS = 1e-6
N_MLSTM_HEADS = 4
MLSTM_HEAD_DIM = 128
D_MLSTM = N_MLSTM_HEADS * MLSTM_HEAD_DIM
N_MLA_HEADS = 8
MLA_NOPE_DIM = 64
MLA_ROPE_DIM = 32
MLA_QK_DIM = MLA_NOPE_DIM + MLA_ROPE_DIM
MLA_V_DIM = 64
MLA_V_ROWS = 80
Q_LORA_RANK = 256
KV_LORA_RANK = 128
ROPE_THETA = 10000.0
N_GATE = 4 * N_MLSTM_HEADS
N_MEM_HEADS = 4
N_EXPERTS = 16
EC_CAPACITY_FACTOR = 2

LANES = 128
SUBLANES = 8
VMEM_LIMIT_BYTES = 56 * 1024 * 1024

MLSTM_CHUNK = 256
MLSTM_PAD_ROWS = 16
MLSTM_CHUNKS_PER_STEP = 2
INPROJ_SPLIT = 2
TM_INPROJ = MLSTM_CHUNK * INPROJ_SPLIT
TQ_MLA = 256
MLA_CHUNKS = 16
TM_POST = 512
POST_SPLIT = 2
TF_FFN = 512
FINAL_CHUNK = 512
FINAL_IN_BUFS = 4
FINAL_OUT_BUFS = 3
CUM_BLOCK = 256
SCATTER_GROUP = 8


def _cparams(sem):
    return pltpu.CompilerParams(dimension_semantics=sem, vmem_limit_bytes=VMEM_LIMIT_BYTES)


def _dot(a, b):
    return jnp.dot(a, b, preferred_element_type=F32)


def _dot_nt(a, b):
    return lax.dot_general(a, b, (((1,), (1,)), ((), ())), preferred_element_type=F32)


def _rms(x, g):
    return x * lax.rsqrt(jnp.mean(x * x, axis=-1, keepdims=True) + EPS) * g


def _sigmoid(x):
    return 1.0 / (1.0 + jnp.exp(-x))


def _full(shape):
    return pl.BlockSpec(shape, lambda *_: (0,) * len(shape))


def _inproj_kernel(x_ref, xp_ref, xn_ref, posr_ref, gmix_ref, wqk_ref, wvt_ref, wo_ref, wmisc_ref, wkr_ref, bg_ref,
                   conv_ref, gqa_ref, wq_ref, gkva_ref, wk_ref, wv_ref, vones_ref, invfc_ref,
                   qT_m_ref, k_ref, vT_m_ref, o_ref, gc_ref, gt_ref, qT_ref, kcat_ref, vT_ref):
    tm = x_ref.shape[1]
    hr = tm // INPROJ_SPLIT
    i = pl.program_id(1)
    last = pl.num_programs(1) - 1
    g = gmix_ref[...]
    xm_bf = _rms(x_ref[0], g).astype(BF16)
    halo = [(_rms(xp_ref[0], g) * (i > 0).astype(F32)).astype(BF16),
            (_rms(xn_ref[0], g) * (i < last).astype(F32)).astype(BF16)]

    def matmuls(u):
        r = slice(u * hr, (u + 1) * hr)
        xs = xm_bf[r]
        before = halo[0] if u == 0 else xm_bf[u * hr - SUBLANES:u * hr]
        after = halo[1] if u == INPROJ_SPLIT - 1 else xm_bf[(u + 1) * hr:(u + 1) * hr + SUBLANES]
        lhs = jnp.concatenate([before, xs, after], axis=0)
        pqk = _dot(lhs, wqk_ref[...])
        misc = _dot(xs, wmisc_ref[...])
        vT_m = _dot_nt(wvt_ref[...], xs)
        o_ref[0, r] = _dot(xs, wo_ref[...])
        krT = _dot_nt(wkr_ref[...], xs)
        ckvn = _rms(misc[:, Q_LORA_RANK:Q_LORA_RANK + KV_LORA_RANK], gkva_ref[...]).astype(BF16)
        cqn = _rms(misc[:, :Q_LORA_RANK], gqa_ref[...]).astype(BF16)
        return dict(r=r, pqk=pqk, vT_m=vT_m, krT=krT, gp=misc[:, Q_LORA_RANK + KV_LORA_RANK:] + bg_ref[...],
                    kn=_dot(ckvn, wk_ref[...]), qT=_dot_nt(wq_ref[...], cqn), vT_a=_dot_nt(wv_ref[...], ckvn))

    lane = lax.broadcasted_iota(I32, (hr, LANES), 1)
    is_f = ((lane >= 4) & (lane < 8)) | ((lane >= 12) & (lane < 16))
    row_id = lax.broadcasted_iota(I32, (hr, LANES), 0)
    cw = conv_ref[...]
    r0, r1, r2 = MLA_NOPE_DIM, MLA_NOPE_DIM + MLA_ROPE_DIM // 2, MLA_QK_DIM

    def scan(x, op, ident, suffix):
        step = 1
        while step < hr:
            if step < SUBLANES:
                ok = (row_id + step < hr) if suffix else (row_id >= step)
                shifted = jnp.where(ok, pltpu.roll(x, hr - step if suffix else step, 0), ident)
            else:
                pad = jnp.full((step, LANES), ident, F32)
                shifted = jnp.concatenate([x[step:], pad] if suffix else [pad, x[:hr - step]], axis=0)
            x = op(x, shifted)
            step *= 2
        return x

    def elementwise(m):
        r = m["r"]
        vT_m_ref[0, :, r] = m["vT_m"].astype(BF16)

        pqk = m["pqk"]
        up = pltpu.roll(pqk, 1, 0)[SUBLANES:SUBLANES + hr]
        dn = pltpu.roll(pqk, hr + 2 * SUBLANES - 1, 0)[SUBLANES:SUBLANES + hr]
        mid = pqk[SUBLANES:SUBLANES + hr]
        conv = up * cw[0:1] + mid * cw[1:2] + dn * cw[2:3]
        act = conv * _sigmoid(conv)
        qT_m_ref[0, :, r] = act[:, :D_MLSTM].T.astype(BF16)
        k_ref[0, r] = (act[:, D_MLSTM:] * (MLSTM_HEAD_DIM ** -0.5)).astype(BF16)

        gp = m["gp"]
        logsig = jnp.minimum(gp, 0.0) - jnp.log1p(jnp.exp(-jnp.abs(gp)))
        lf = jnp.where(is_f, logsig, gp)
        cum = jnp.where(lane < 8, scan(lf, jnp.add, 0.0, False), scan(lf, jnp.add, 0.0, True))
        a = lf - pltpu.roll(cum, LANES - 4, 1)
        gc = jnp.where(is_f, cum, a)
        run_max = jnp.where(lane < 8, scan(gc, jnp.maximum, -jnp.inf, False),
                            scan(gc, jnp.maximum, -jnp.inf, True))
        table = jnp.where(lane < N_GATE, gc, pltpu.roll(run_max, N_GATE, 1))
        gc_ref[0, r] = table
        gt_ref[0, :, r] = table.T[:2 * N_GATE]

        ang_t = invfc_ref[...] * posr_ref[0, :, r]
        cos_t = jnp.cos(ang_t)
        sin_t = jnp.sin(ang_t)

        def rope_rows(blk):
            t1 = blk[r0:r1]
            t2 = blk[r1:r2]
            return jnp.concatenate([blk[:r0], t1 * cos_t - t2 * sin_t, t2 * cos_t + t1 * sin_t, blk[r2:]], axis=0)

        kr = rope_rows(m["krT"]).T
        for h in range(N_MLA_HEADS):
            sl = slice(h * LANES, (h + 1) * LANES)
            kcat_ref[0, r, sl] = (m["kn"][:, sl] + kr).astype(BF16)
            qT_ref[0, sl, r] = rope_rows(m["qT"][sl]).astype(BF16)
        vT_ref[0, :, r] = (m["vT_a"] + vones_ref[...]).astype(BF16)

    parts = [matmuls(u) for u in range(INPROJ_SPLIT)]
    for m in parts:
        elementwise(m)


def _inproj(x, pos_row, g_mix, w_in, b_gates, conv_qk, g_q_a, w_q_b, g_kv_a, w_kv_b):
    B, S, D = x.shape
    tm = TM_INPROJ
    assert S % tm == 0
    nt = S // tm
    hb = tm // SUBLANES
    sizes = [D_MLSTM, D_MLSTM, D_MLSTM, D_MLSTM, N_GATE, Q_LORA_RANK, KV_LORA_RANK, MLA_ROPE_DIM]
    off = [sum(sizes[:n]) for n in range(len(sizes) + 1)]
    assert w_in.shape[1] == off[-1]
    wqk = w_in[:, off[0]:off[2]].astype(BF16)
    wvt = w_in[:, off[2]:off[3]].T.astype(BF16)
    wo = w_in[:, off[3]:off[4]].astype(BF16)
    w_gate = w_in[:, off[4]:off[5]]
    w_cq = w_in[:, off[5]:off[6]]
    w_ckv = w_in[:, off[6]:off[7]]
    w_kr = w_in[:, off[7]:off[8]]
    wkr = jnp.zeros((LANES, D), F32).at[MLA_NOPE_DIM:MLA_QK_DIM].set(w_kr.T).astype(BF16)
    gate_blk = jnp.zeros((D, LANES), F32).at[:, :N_GATE].set(w_gate)
    wmisc = jnp.concatenate([w_cq, w_ckv, gate_blk], axis=1).astype(BF16)
    bg = jnp.zeros((1, LANES), F32).at[0, :N_GATE].set(b_gates)
    pad_heads = lambda w3: jnp.pad(w3, ((0, 0), (0, 0), (0, LANES - w3.shape[2]))).reshape(w3.shape[0], -1)
    wq = pad_heads(w_q_b.reshape(-1, N_MLA_HEADS, MLA_QK_DIM)).T.astype(BF16)
    wkv3 = w_kv_b.reshape(-1, N_MLA_HEADS, MLA_NOPE_DIM + MLA_V_DIM)
    wk = pad_heads(wkv3[:, :, :MLA_NOPE_DIM]).astype(BF16)
    wv = jnp.pad(wkv3[:, :, MLA_NOPE_DIM:], ((0, 0), (0, 0), (0, MLA_V_ROWS - MLA_V_DIM)))
    wv = wv.reshape(wv.shape[0], -1).T.astype(BF16)
    vones = jnp.zeros((N_MLA_HEADS, MLA_V_ROWS, MLSTM_CHUNK), F32).at[:, MLA_V_DIM].set(1.0)
    vones = vones.reshape(-1, MLSTM_CHUNK)
    inv_freq = ROPE_THETA ** (-jnp.arange(0, MLA_ROPE_DIM, 2, dtype=F32) / MLA_ROPE_DIM)
    invf_col = inv_freq.reshape(-1, 1)

    row = lambda w: pl.BlockSpec((1, tm, w), lambda b, i: (b, i, 0))
    in_specs = [
        row(D),
        pl.BlockSpec((1, SUBLANES, D), lambda b, i: (b, jnp.maximum(i * hb - 1, 0), 0)),
        pl.BlockSpec((1, SUBLANES, D), lambda b, i: (b, jnp.minimum((i + 1) * hb, S // SUBLANES - 1), 0)),
        pl.BlockSpec((1, 1, tm), lambda b, i: (b, 0, i)),
        _full((1, D)), _full(wqk.shape), _full(wvt.shape), _full(wo.shape), _full(wmisc.shape), _full(wkr.shape),
        _full((1, LANES)), _full(conv_qk.shape), _full((1, Q_LORA_RANK)), _full(wq.shape), _full((1, KV_LORA_RANK)),
        _full(wk.shape), _full(wv.shape), _full(vones.shape), _full(invf_col.shape),
    ]
    out_shape = [
        jax.ShapeDtypeStruct((B, D_MLSTM, S), BF16),
        jax.ShapeDtypeStruct((B, S, D_MLSTM), BF16),
        jax.ShapeDtypeStruct((B, D_MLSTM, S), BF16),
        jax.ShapeDtypeStruct((B, S, D_MLSTM), F32),
        jax.ShapeDtypeStruct((B, S, LANES), F32),
        jax.ShapeDtypeStruct((B, 2 * N_GATE, S), F32),
        jax.ShapeDtypeStruct((B, N_MLA_HEADS * LANES, S), BF16),
        jax.ShapeDtypeStruct((B, S, N_MLA_HEADS * LANES), BF16),
        jax.ShapeDtypeStruct((B, N_MLA_HEADS * MLA_V_ROWS, S), BF16),
    ]
    out_specs = [
        pl.BlockSpec((1, D_MLSTM, tm), lambda b, i: (b, 0, i)),
        row(D_MLSTM),
        pl.BlockSpec((1, D_MLSTM, tm), lambda b, i: (b, 0, i)),
        row(D_MLSTM), row(LANES),
        pl.BlockSpec((1, 2 * N_GATE, tm), lambda b, i: (b, 0, i)),
        pl.BlockSpec((1, N_MLA_HEADS * LANES, tm), lambda b, i: (b, 0, i)),
        row(N_MLA_HEADS * LANES),
        pl.BlockSpec((1, N_MLA_HEADS * MLA_V_ROWS, tm), lambda b, i: (b, 0, i)),
    ]
    return pl.pallas_call(
        _inproj_kernel, grid=(B, nt), in_specs=in_specs, out_specs=out_specs, out_shape=out_shape,
        compiler_params=_cparams(("parallel", "parallel")), name="inproj",
    )(x, x, x, pos_row, g_mix.reshape(1, D), wqk, wvt, wo, wmisc, wkr, bg, conv_qk, g_q_a.reshape(1, -1), wq,
      g_kv_a.reshape(1, -1), wk, wv, vones, invf_col)


def _mlstm_kernel(kf_ref, qf_ref, vf_ref, gcf_ref, gtf_ref, kb_ref, qb_ref, vb_ref, gcb_ref, gtb_ref,
                  hf_ref, hb_ref, cn_ref, m_ref):
    T = MLSTM_CHUNK
    hd = MLSTM_HEAD_DIM

    @pl.when(pl.program_id(1) == 0)
    def _():
        cn_ref[...] = jnp.zeros_like(cn_ref)
        m_ref[...] = jnp.zeros_like(m_ref)

    s_idx = lax.broadcasted_iota(I32, (T, T), 0)
    t_idx = lax.broadcasted_iota(I32, (T, T), 1)
    ones_rows = (lax.broadcasted_iota(I32, (MLSTM_PAD_ROWS, T), 0) == 0).astype(BF16)
    neg_inf = F32(-jnp.inf)

    def chunk(c):
        streams = []
        for d, (k_ref, qT_ref, vT_ref, gc_ref, gt_ref, h_ref) in enumerate(
                ((kf_ref, qf_ref, vf_ref, gcf_ref, gtf_ref, hf_ref),
                 (kb_ref, qb_ref, vb_ref, gcb_ref, gtb_ref, hb_ref))):
            end = T - 1 if d == 0 else 0
            pos = c if d == 0 else MLSTM_CHUNKS_PER_STEP - 1 - c
            tok = slice(pos * T, (pos + 1) * T)
            gc = gc_ref[0, tok]
            gt = gt_ref[0, :, tok]
            for h in range(N_MLSTM_HEADS):
                st = d * N_MLSTM_HEADS + h
                la = d * 8 + h
                a_row = gt[la:la + 1]
                b_row = gt[la + 4:la + 5]
                m_prev = m_ref[st][0:1, 0:1]
                kh = k_ref[0, tok, h * hd:(h + 1) * hd]
                qTh = qT_ref[0, h * hd:(h + 1) * hd, tok]
                v_ext = jnp.concatenate([vT_ref[0, h * hd:(h + 1) * hd, tok], ones_rows], axis=0)
                cn = cn_ref[st]
                m_run = jnp.maximum(gt[N_GATE + la:N_GATE + la + 1], m_prev)
                m_end = m_run[:, end:end + 1]
                vw = (v_ext.astype(F32) * jnp.exp(a_row - m_end)).astype(BF16)
                streams.append(dict(
                    st=st, d=d, h=h, h_ref=h_ref, tok=tok, a_col=gc[:, la:la + 1], b_row=b_row, m_prev=m_prev,
                    m_run=m_run, m_end=m_end, g_sum=b_row[:, end:end + 1], cn=cn, v_ext=v_ext,
                    s=_dot(kh, qTh),
                    qcn=_dot(cn.astype(BF16), qTh),
                    upd=_dot(vw, kh)))
        for x in streams:
            mask = (s_idx <= t_idx) if x["d"] == 0 else (s_idx >= t_idx)
            e = jnp.exp(jnp.where(mask, x["a_col"] - x["m_run"], neg_inf))
            x["pw"] = (e * x["s"]).astype(BF16)
        for x in streams:
            x["pv"] = _dot(x["v_ext"], x["pw"])
        for x in streams:
            st, h = x["st"], x["h"]
            iw = jnp.exp(x["m_prev"] - x["m_run"])
            num = iw * x["qcn"][:hd] + x["pv"][:hd]
            den = iw * x["qcn"][hd:hd + 1] + x["pv"][hd:hd + 1]
            floor = jnp.exp(-(x["b_row"] + x["m_run"]))
            x["h_ref"][0, h * hd:(h + 1) * hd, x["tok"]] = num / jnp.maximum(jnp.abs(den), floor)
            cn_ref[st] = jnp.exp(x["m_prev"] - x["m_end"]) * x["cn"] + x["upd"]
            m_ref[st] = jnp.broadcast_to(x["g_sum"] + x["m_end"], (SUBLANES, LANES))

    for c in range(MLSTM_CHUNKS_PER_STEP):
        chunk(c)


def _mlstm(k, qT, vT, gc, gt):
    B, S, _ = k.shape
    T = MLSTM_CHUNK * MLSTM_CHUNKS_PER_STEP
    nc = S // T
    fwd = lambda b, j: (b, j, 0)
    bwd = lambda b, j: (b, nc - 1 - j, 0)
    fwd_t = lambda b, j: (b, 0, j)
    bwd_t = lambda b, j: (b, 0, nc - 1 - j)

    def specs(row_map, col_map):
        return [pl.BlockSpec((1, T, D_MLSTM), row_map), pl.BlockSpec((1, D_MLSTM, T), col_map),
                pl.BlockSpec((1, D_MLSTM, T), col_map), pl.BlockSpec((1, T, LANES), row_map),
                pl.BlockSpec((1, 2 * N_GATE, T), col_map)]

    n_state = 2 * N_MLSTM_HEADS
    return pl.pallas_call(
        _mlstm_kernel, grid=(B, nc),
        in_specs=specs(fwd, fwd_t) + specs(bwd, bwd_t),
        out_specs=[pl.BlockSpec((1, D_MLSTM, T), fwd_t), pl.BlockSpec((1, D_MLSTM, T), bwd_t)],
        out_shape=[jax.ShapeDtypeStruct((B, D_MLSTM, S), F32)] * 2,
        scratch_shapes=[pltpu.VMEM((n_state, MLSTM_HEAD_DIM + MLSTM_PAD_ROWS, MLSTM_HEAD_DIM), F32),
                        pltpu.VMEM((n_state, SUBLANES, LANES), F32)],
        compiler_params=_cparams(("parallel", "arbitrary")), name="mlstm",
    )(k, qT, vT, gc, gt, k, qT, vT, gc, gt)


def _mla_kernel(qT_ref, k_ref, vT_ref, g_ref, o_ref, sa_sc, sb_sc, ma_sc, mb_sc):
    t = pl.program_id(0)
    c = (MLA_QK_DIM ** -0.5) * math.log2(math.e)
    S = k_ref.shape[1]
    tq = qT_ref.shape[2]
    ck = S // MLA_CHUNKS

    @pl.when(t == 0)
    def _():
        sb_sc[...] = jnp.zeros(sb_sc.shape, F32)
        mb_sc[...] = jnp.zeros(mb_sc.shape, F32)

    def step(s_cur, m_cur, s_prev, m_prev_ref):
        m_prev = [m_prev_ref[h][0:1] for h in range(2)]
        m_run = [jnp.full((1, tq), -jnp.inf, F32) for _ in range(2)]
        accs = [None, None]
        for j in range(MLA_CHUNKS):
            rows = slice(j * ck, (j + 1) * ck)
            for h in range(2):
                s = _dot(k_ref[0, rows, h * LANES:(h + 1) * LANES], qT_ref[0, h * LANES:(h + 1) * LANES, :])
                m_run[h] = jnp.maximum(m_run[h], jnp.max(s, axis=0, keepdims=True))
                s_cur[h, rows, :] = s
            for h in range(2):
                p = jnp.exp2((s_prev[h, rows, :] - m_prev[h]) * c).astype(BF16)
                part = _dot(vT_ref[0, h * MLA_V_ROWS:(h + 1) * MLA_V_ROWS, rows], p)
                accs[h] = part if j == 0 else accs[h] + part
        for h in range(2):
            m_cur[h] = jnp.broadcast_to(m_run[h], (SUBLANES, tq))
        ys = []
        for h in range(2):
            acc = accs[h]
            o = acc[:MLA_V_DIM] / acc[MLA_V_DIM:MLA_V_DIM + 1]
            ms = jnp.mean(o * o, axis=0, keepdims=True)
            ys.append(o * lax.rsqrt(ms + EPS))
        o_ref[0] = jnp.concatenate(ys, axis=0).T * g_ref[...]

    @pl.when(t % 2 == 0)
    def _():
        step(sa_sc, ma_sc, sb_sc, mb_sc)

    @pl.when(t % 2 == 1)
    def _():
        step(sb_sc, mb_sc, sa_sc, ma_sc)


def _mla(qT, kcat, vT, g_head_mla):
    B, S, _ = kcat.shape
    tq = TQ_MLA
    npair = N_MLA_HEADS // 2
    nq = S // tq
    nt = B * npair * nq

    def tile(t):
        return t // (npair * nq), (t // nq) % npair, t % nq

    def a_map(f):
        return lambda t: f(*tile(jnp.minimum(t, nt - 1)))

    def b_map(f):
        return lambda t: f(*tile(jnp.maximum(t - 1, 0)))

    return pl.pallas_call(
        _mla_kernel, grid=(nt + 1,),
        in_specs=[pl.BlockSpec((1, 2 * LANES, tq), a_map(lambda b, p, i: (b, p, i))),
                  pl.BlockSpec((1, S, 2 * LANES), a_map(lambda b, p, i: (b, 0, p))),
                  pl.BlockSpec((1, 2 * MLA_V_ROWS, S), b_map(lambda b, p, i: (b, p, 0))),
                  pl.BlockSpec((1, LANES), b_map(lambda b, p, i: (0, p)))],
        out_specs=pl.BlockSpec((1, tq, LANES), b_map(lambda b, p, i: (b, i, p))),
        out_shape=jax.ShapeDtypeStruct((B, S, N_MLA_HEADS * MLA_V_DIM), F32),
        scratch_shapes=[pltpu.VMEM((2, S, tq), F32), pltpu.VMEM((2, S, tq), F32),
                        pltpu.VMEM((2, SUBLANES, tq), F32), pltpu.VMEM((2, SUBLANES, tq), F32)],
        compiler_params=_cparams(("arbitrary",)), name="mla_attn",
    )(qT, kcat, vT, g_head_mla.reshape(1, -1))


def _memkv_kernel(mem_ref, g_ref, wk_ref, wv_ref, k_ref, v_ref):
    mn = _rms(mem_ref[0], g_ref[...]).astype(BF16)
    k_ref[0] = _dot(mn, wk_ref[...]).astype(BF16)
    v_ref[0] = _dot(mn, wv_ref[...]).astype(BF16)


def _memkv(mem, g_mem_kv, w_k, w_v):
    B, M, D = mem.shape
    blk = pl.BlockSpec((1, M, D), lambda b: (b, 0, 0))
    return pl.pallas_call(
        _memkv_kernel, grid=(B,),
        in_specs=[blk, _full((1, D)), _full(w_k.shape), _full(w_v.shape)],
        out_specs=[blk, blk], out_shape=[jax.ShapeDtypeStruct((B, M, D), BF16)] * 2,
        compiler_params=_cparams(("parallel",)), name="memkv",
    )(mem, g_mem_kv.reshape(1, D), w_k.astype(BF16), w_v.astype(BF16))


def _post_kernel(x_ref, hf_ref, hb_ref, om_ref, ya_ref, km_ref, vm_ref, ghm_ref, wout_ref, gmx_ref, wmq_ref,
                 wmo_ref, gffn_ref, wrh_ref, wrl_ref, x2_ref, hn_ref, aff_ref, afft_ref):
    tm = x_ref.shape[1]
    D = x_ref.shape[2]
    hr = tm // POST_SPLIT
    subs = [slice(i * hr, (i + 1) * hr) for i in range(POST_SPLIT)]
    ghm = ghm_ref[...]
    dm = D // N_MEM_HEADS
    heads = [slice(h * dm, (h + 1) * dm) for h in range(N_MEM_HEADS)]

    ycat = []
    for r in subs:
        hm = (hf_ref[0, :, r] + hb_ref[0, :, r]).T
        gate = _sigmoid(om_ref[0, r])
        parts = []
        for h in range(N_MLSTM_HEADS):
            sl = slice(h * MLSTM_HEAD_DIM, (h + 1) * MLSTM_HEAD_DIM)
            parts.append(gate[:, sl] * _rms(hm[:, sl], ghm[:, sl]))
        ycat.append(jnp.concatenate(parts + [ya_ref[0, r]], axis=1).astype(BF16))
    x1 = [x_ref[0, r] + _dot(y, wout_ref[...]) for r, y in zip(subs, ycat)]

    xn = [_rms(v, gmx_ref[...]).astype(BF16) for v in x1]
    qm = [_dot(v, wmq_ref[...]).astype(BF16) for v in xn]
    km = km_ref[0]
    vm = vm_ref[0]
    scores = [[_dot_nt(q[:, sl], km[:, sl]) * (dm ** -0.5) for sl in heads] for q in qm]
    probs = []
    for per_sub in scores:
        ps = []
        for s in per_sub:
            e = jnp.exp(s - jnp.max(s, axis=1, keepdims=True))
            ps.append((e / jnp.sum(e, axis=1, keepdims=True)).astype(BF16))
        probs.append(ps)
    om = [jnp.concatenate([_dot(p, vm[:, sl]) for p, sl in zip(ps, heads)], axis=1).astype(BF16) for ps in probs]
    x2 = [v + _dot(o, wmo_ref[...]) for v, o in zip(x1, om)]

    hns = []
    for r, v in zip(subs, x2):
        x2_ref[0, r] = v
        hn = _rms(v, gffn_ref[...])
        hn_ref[0, r] = hn.reshape(hr, SUBLANES, D // SUBLANES)
        hi = hn.astype(BF16)
        hns.append((hi, (hn - hi.astype(F32)).astype(BF16)))
    wrh = wrh_ref[...]
    logits = [_dot(hi, wrh) + (_dot(hi, wrl_ref[...]) + _dot(lo, wrh)) for hi, lo in hns]
    valid = lax.broadcasted_iota(I32, (hr, LANES), 1) < N_EXPERTS
    for r, lg in zip(subs, logits):
        lg = jnp.where(valid, lg, -jnp.inf)
        e = jnp.exp(lg - jnp.max(lg, axis=1, keepdims=True))
        aff = e / jnp.sum(e, axis=1, keepdims=True)
        aff_ref[0, r] = aff
        afft_ref[0, :, r] = aff.T[:N_EXPERTS]


def _post(x, hf, hb, om, ya, kmem, vmem, g_head_mlstm, w_out, g_mem_x, w_mem_q, w_mem_o, g_ffn, w_router):
    B, S, D = x.shape
    tm = TM_POST
    M = kmem.shape[1]
    wr = jnp.zeros((D, LANES), F32).at[:, :N_EXPERTS].set(w_router)
    wrh = wr.astype(BF16)
    wrl = (wr - wrh.astype(F32)).astype(BF16)
    row = lambda w: pl.BlockSpec((1, tm, w), lambda b, i: (b, i, 0))
    col = lambda w: pl.BlockSpec((1, w, tm), lambda b, i: (b, 0, i))
    mem_spec = pl.BlockSpec((1, M, D), lambda b, i: (b, 0, 0))
    return pl.pallas_call(
        _post_kernel, grid=(B, S // tm),
        in_specs=[row(D), col(D_MLSTM), col(D_MLSTM), row(D_MLSTM), row(D_MLSTM), mem_spec, mem_spec,
                  _full((1, D_MLSTM)), _full((D, D)), _full((1, D)), _full((D, D)), _full((D, D)),
                  _full((1, D)), _full((D, LANES)), _full((D, LANES))],
        out_specs=[row(D), pl.BlockSpec((1, tm, SUBLANES, D // SUBLANES), lambda b, i: (b, i, 0, 0)), row(LANES),
                   pl.BlockSpec((1, N_EXPERTS, tm), lambda b, i: (b, 0, i))],
        out_shape=[jax.ShapeDtypeStruct((B, S, D), F32), jax.ShapeDtypeStruct((B, S, SUBLANES, D // SUBLANES), F32),
                   jax.ShapeDtypeStruct((B, S, LANES), F32), jax.ShapeDtypeStruct((B, N_EXPERTS, S), F32)],
        compiler_params=_cparams(("parallel", "parallel")), name="post_mixer",
    )(x, hf, hb, om, ya, kmem, vmem, g_head_mlstm.reshape(1, -1), w_out.astype(BF16), g_mem_x.reshape(1, D),
      w_mem_q.astype(BF16), w_mem_o.astype(BF16), g_ffn.reshape(1, D), wrh, wrl)


def _topk_kernel(aff_ref, afft_ref, affb_ref, idx_ref, gate_ref, idx_sc, gate_sc):
    S = aff_ref.shape[1]
    cap = idx_ref.shape[2]
    nblk = S // CUM_BLOCK

    def open_interval(c):
        lo, hi = c
        mid = 0.5 * (lo + hi)
        return jnp.max(jnp.where((mid > lo) & (mid < hi), 1.0, 0.0)) > 0.0

    def bisect(c):
        lo, hi = c
        mid = 0.5 * (lo + hi)
        cnt = jnp.sum((afft_ref[0] >= mid).astype(F32), axis=1, keepdims=True)
        ge = cnt >= cap
        return jnp.where(ge, mid, lo), jnp.where(ge, hi, mid)

    lo_c, hi_c = lax.while_loop(open_interval, bisect,
                                (jnp.zeros((N_EXPERTS, 1), F32), jnp.full((N_EXPERTS, 1), 2.0, F32)))
    need_c = cap - jnp.sum((afft_ref[0] >= hi_c).astype(F32), axis=1, keepdims=True)
    eye = (lax.broadcasted_iota(I32, (N_EXPERTS, LANES), 0) == lax.broadcasted_iota(I32, (N_EXPERTS, LANES), 1))
    to_row = lambda col: jnp.sum(jnp.where(eye, col, 0.0), axis=0, keepdims=True)
    lo, hi, need = to_row(lo_c), to_row(hi_c), to_row(need_c)

    r = lax.broadcasted_iota(I32, (CUM_BLOCK, CUM_BLOCK), 0)
    c = lax.broadcasted_iota(I32, (CUM_BLOCK, CUM_BLOCK), 1)
    tri = (c <= r).astype(BF16)

    def cum_body(blk, carry):
        ce, cs = carry
        rows = pl.ds(pl.multiple_of(blk * CUM_BLOCK, CUM_BLOCK), CUM_BLOCK)
        a = aff_ref[0, rows, :]
        sure = a >= hi
        tie = (a >= lo) & jnp.logical_not(sure)
        eq = tie.astype(F32)
        eq_incl = _dot(tri, eq.astype(BF16)) + ce
        sel = jnp.where(sure | (tie & (eq_incl - eq < need)), 1.0, 0.0)
        local_incl = _dot(tri, sel.astype(BF16))
        rank_local = jnp.where(sel > 0.0, local_incl - sel, -2.0).astype(BF16)
        base = lax.convert_element_type(blk * CUM_BLOCK, F32)
        for e in range(N_EXPERTS):
            slot_local = jnp.clip(j_row - cs[:, e:e + 1], -1.0, CUM_BLOCK + 1.0).astype(BF16)
            onehot = jnp.where(rank_local[:, e:e + 1] == slot_local, one_bf, zero_bf)
            a_row = affb_ref[0, e, pl.ds(blk, 1), :]
            a_hi = a_row.astype(BF16).astype(F32)
            a_mid = (a_row - a_hi).astype(BF16).astype(F32)
            a_lo = a_row - a_hi - a_mid
            lhs = jnp.concatenate([tok_rows, a_hi, a_mid, a_lo, pad_rows], axis=0).astype(BF16)
            hit = _dot(lhs, onehot)
            idx_sc[e:e + 1, :] += hit[0:1] + base * hit[1:2]
            gate_sc[e:e + 1, :] += hit[2:3] + hit[3:4] + hit[4:5]
        return eq_incl[CUM_BLOCK - 1:CUM_BLOCK], cs + local_incl[CUM_BLOCK - 1:CUM_BLOCK]

    j_row = lax.broadcasted_iota(I32, (1, cap), 1).astype(F32)
    tok = lax.broadcasted_iota(I32, (1, CUM_BLOCK), 1).astype(F32)
    tok_rows = jnp.concatenate([tok, jnp.ones((1, CUM_BLOCK), F32)], axis=0)
    pad_rows = jnp.zeros((SUBLANES - 5, CUM_BLOCK), F32)
    one_bf = jnp.ones((CUM_BLOCK, cap), BF16)
    zero_bf = jnp.zeros((CUM_BLOCK, cap), BF16)
    idx_sc[...] = jnp.zeros_like(idx_sc)
    gate_sc[...] = jnp.zeros_like(gate_sc)
    zero = jnp.zeros((1, LANES), F32)
    lax.fori_loop(0, nblk, cum_body, (zero, zero))
    idx_ref[0] = idx_sc[...].astype(I32)
    gate_ref[0] = gate_sc[...]


def _topk(aff, aff_t, cap):
    B, S, _ = aff.shape
    nblk = S // CUM_BLOCK
    out = pl.BlockSpec((1, N_EXPERTS, cap), lambda b: (b, 0, 0))
    return pl.pallas_call(
        _topk_kernel, grid=(B,),
        in_specs=[pl.BlockSpec((1, S, LANES), lambda b: (b, 0, 0)),
                  pl.BlockSpec((1, N_EXPERTS, S), lambda b: (b, 0, 0)),
                  pl.BlockSpec((1, N_EXPERTS, nblk, CUM_BLOCK), lambda b: (b, 0, 0, 0))],
        out_specs=[out, out],
        out_shape=[jax.ShapeDtypeStruct((B, N_EXPERTS, cap), I32), jax.ShapeDtypeStruct((B, N_EXPERTS, cap), F32)],
        scratch_shapes=[pltpu.VMEM((N_EXPERTS, cap), F32), pltpu.VMEM((N_EXPERTS, cap), F32)],
        compiler_params=_cparams(("parallel",)), name="topk",
    )(aff, aff_t, aff_t.reshape(B, N_EXPERTS, nblk, CUM_BLOCK))


def _gather_kernel(idx_ref, hn_hbm, xe_ref, buf_a, buf_b, rows_sc, sem):
    b = pl.program_id(0)
    e = pl.program_id(1)
    cap = xe_ref.shape[2]

    def fetch(seq, buf, slot):
        return pltpu.make_async_copy(hn_hbm.at[seq], buf, sem.at[slot])

    @pl.when((b == 0) & (e == 0))
    def _():
        fetch(0, buf_a, 0).start()

    def run(cur, cur_slot, nxt, nxt_slot):
        @pl.when(e == 0)
        def _():
            fetch(b, cur, cur_slot).wait()

            @pl.when(b + 1 < pl.num_programs(0))
            def _():
                fetch(b + 1, nxt, nxt_slot).start()

        def body(j, _):
            i = idx_ref[0, 0, j]
            rows_sc[pl.ds(j, 1)] = cur[pl.ds(i, 1)]
            return 0

        lax.fori_loop(0, cap, body, 0, unroll=8)
        xe_ref[0, 0] = rows_sc[...].reshape(cap, xe_ref.shape[3]).astype(BF16)

    @pl.when(b % 2 == 0)
    def _():
        run(buf_a, 0, buf_b, 1)

    @pl.when(b % 2 == 1)
    def _():
        run(buf_b, 1, buf_a, 0)


def _gather(idx3, hn, cap):
    B, S, sub, lanes = hn.shape
    E = N_EXPERTS
    D = sub * lanes
    seq_buf = pltpu.VMEM((S, sub, lanes), F32)
    return pl.pallas_call(
        _gather_kernel, grid=(B, E),
        in_specs=[pl.BlockSpec((1, 1, cap), lambda b, e: (b * E + e, 0, 0), memory_space=pltpu.SMEM),
                  pl.BlockSpec(memory_space=pl.ANY)],
        out_specs=pl.BlockSpec((1, 1, cap, D), lambda b, e: (b, e, 0, 0)),
        out_shape=jax.ShapeDtypeStruct((B, E, cap, D), BF16),
        scratch_shapes=[seq_buf, seq_buf, pltpu.VMEM((cap, sub, lanes), F32), pltpu.SemaphoreType.DMA((2,))],
        compiler_params=_cparams(("arbitrary", "arbitrary")), name="moe_gather",
    )(idx3, hn)


def _ffn_kernel(xe_ref, wg_ref, wu_ref, wd_ref, ye_ref, wg_sc, wu_sc, wd_sc):
    f = pl.program_id(1)
    nb = xe_ref.shape[0]

    @pl.when(f == 0)
    def _():
        ye_ref[...] = jnp.zeros_like(ye_ref)

    wg_sc[...] = wg_ref[0].astype(BF16)
    wu_sc[...] = wu_ref[0].astype(BF16)
    wd_sc[...] = wd_ref[0].astype(BF16)

    def up(b):
        xb = xe_ref[b, 0]
        return _dot(xb, wg_sc[...]), _dot(xb, wu_sc[...])

    def down(b, h1, h2):
        hid = (h1 * _sigmoid(h1) * h2).astype(BF16)
        ye_ref[b, 0] += _dot(hid, wd_sc[...])

    hs = up(0)
    for b in range(nb):
        nxt = up(b + 1) if b + 1 < nb else None
        down(b, *hs)
        hs = nxt


def _ffn(xe, w_gate, w_up, w_down):
    B, E, cap, D = xe.shape
    F = w_gate.shape[2]
    tf = TF_FFN
    return pl.pallas_call(
        _ffn_kernel, grid=(E, F // tf),
        in_specs=[pl.BlockSpec((B, 1, cap, D), lambda e, f: (0, e, 0, 0)),
                  pl.BlockSpec((1, D, tf), lambda e, f: (e, 0, f)),
                  pl.BlockSpec((1, D, tf), lambda e, f: (e, 0, f)),
                  pl.BlockSpec((1, tf, D), lambda e, f: (e, f, 0))],
        out_specs=pl.BlockSpec((B, 1, cap, D), lambda e, f: (0, e, 0, 0)),
        out_shape=jax.ShapeDtypeStruct((B, E, cap, D), F32),
        scratch_shapes=[pltpu.VMEM((D, tf), BF16), pltpu.VMEM((D, tf), BF16), pltpu.VMEM((tf, D), BF16)],
        compiler_params=_cparams(("parallel", "arbitrary")), name="moe_ffn",
    )(xe, w_gate, w_up, w_down)


def _scatter_final_kernel(idx_ref, gate_ref, ye_ref, g_ref, x2_hbm, out_hbm, acc_sc, ye_sc, x_buf, o_buf, in_sem,
                          out_sem):
    b = pl.program_id(0)
    e = pl.program_id(1)
    cap = ye_ref.shape[2]
    S = acc_sc.shape[0]
    ch = x_buf.shape[1]
    D = x_buf.shape[2]

    @pl.when(e == 0)
    def _():
        acc_sc[...] = jnp.zeros_like(acc_sc)

    ye_sc[...] = ye_ref[0, 0].reshape(ye_sc.shape)

    def body(jb, _):
        base = pl.multiple_of(jb * SCATTER_GROUP, SCATTER_GROUP)
        ids = [idx_ref[0, 0, base + u] for u in range(SCATTER_GROUP)]
        new = [acc_sc[pl.ds(ids[u], 1)] + ye_sc[pl.ds(base + u, 1)] * gate_ref[0, 0, base + u]
               for u in range(SCATTER_GROUP)]
        for u in range(SCATTER_GROUP):
            acc_sc[pl.ds(ids[u], 1)] = new[u]
        return 0

    lax.fori_loop(0, cap // SCATTER_GROUP, body, 0)

    n_in = x_buf.shape[0]
    n_out = o_buf.shape[0]

    def in_copy(c):
        return pltpu.make_async_copy(x2_hbm.at[b, pl.ds(c * ch, ch)], x_buf.at[c % n_in], in_sem.at[c % n_in])

    def out_copy(c):
        return pltpu.make_async_copy(o_buf.at[c % n_out], out_hbm.at[b, pl.ds(c * ch, ch)], out_sem.at[c % n_out])

    @pl.when(e == pl.num_programs(1) - 1)
    def _():
        nch = S // ch
        ahead = n_in - 1
        for c in range(min(ahead, nch)):
            in_copy(c).start()
        for c in range(nch):
            in_copy(c).wait()
            if c + ahead < nch:
                in_copy(c + ahead).start()
            if c >= n_out:
                out_copy(c - n_out).wait()
            moe = acc_sc[c * ch:(c + 1) * ch].reshape(ch, D)
            o_buf[c % n_out] = _rms(x_buf[c % n_in] + moe, g_ref[...])
            out_copy(c).start()
        for c in range(max(nch - n_out, 0), nch):
            out_copy(c).wait()


def _scatter_final(idx3, gate3, ye, x2, g_final):
    B, E, cap, D = ye.shape
    S = x2.shape[1]
    lanes = D // SUBLANES
    smem = lambda: pl.BlockSpec((1, 1, cap), lambda b, e: (b * E + e, 0, 0), memory_space=pltpu.SMEM)
    return pl.pallas_call(
        _scatter_final_kernel, grid=(B, E),
        in_specs=[smem(), smem(), pl.BlockSpec((1, 1, cap, D), lambda b, e: (b, e, 0, 0)), _full((1, D)),
                  pl.BlockSpec(memory_space=pl.ANY)],
        out_specs=pl.BlockSpec(memory_space=pl.ANY),
        out_shape=jax.ShapeDtypeStruct((B, S, D), F32),
        scratch_shapes=[pltpu.VMEM((S, SUBLANES, lanes), F32), pltpu.VMEM((cap, SUBLANES, lanes), F32),
                        pltpu.VMEM((FINAL_IN_BUFS, FINAL_CHUNK, D), F32),
                        pltpu.VMEM((FINAL_OUT_BUFS, FINAL_CHUNK, D), F32),
                        pltpu.SemaphoreType.DMA((FINAL_IN_BUFS,)), pltpu.SemaphoreType.DMA((FINAL_OUT_BUFS,))],
        compiler_params=_cparams(("arbitrary", "arbitrary")), name="moe_scatter_final",
    )(idx3, gate3, ye, g_final.reshape(1, D), x2)


def kernel(x, mem, positions, g_mix, w_in, b_gates, conv_qk, g_q_a, w_q_b, g_kv_a, w_kv_b, g_head_mlstm,
           g_head_mla, w_out, g_mem_x, g_mem_kv, w_mem_q, w_mem_k, w_mem_v, w_mem_o, g_ffn, w_router,
           w_exp_gate, w_exp_up, w_exp_down, g_final):
    B, S, D = x.shape
    depth = g_mix.shape[0]
    assert depth == 1, "the MoE residual is folded into the final norm kernel, which assumes one layer"
    cap = EC_CAPACITY_FACTOR * S // N_EXPERTS
    pos_row = positions.astype(F32).reshape(B, 1, S)
    for l in range(depth):
        qT_m, k_m, vT_m, om, gc, gt, qT, kcat, vT = _inproj(
            x, pos_row, g_mix[l], w_in[l], b_gates[l], conv_qk[l], g_q_a[l], w_q_b[l], g_kv_a[l], w_kv_b[l])
        hf, hb = _mlstm(k_m, qT_m, vT_m, gc, gt)
        ya = _mla(qT, kcat, vT, g_head_mla[l])
        kmem, vmem = _memkv(mem, g_mem_kv[l], w_mem_k[l], w_mem_v[l])
        x2, hn, aff, aff_t = _post(x, hf, hb, om, ya, kmem, vmem, g_head_mlstm[l], w_out[l], g_mem_x[l],
                            w_mem_q[l], w_mem_o[l], g_ffn[l], w_router[l])
        idx, gate = _topk(aff, aff_t, cap)
        idx3 = idx.reshape(B * N_EXPERTS, 1, cap)
        xe = _gather(idx3, hn, cap)
        ye = _ffn(xe, w_exp_gate[l], w_exp_up[l], w_exp_down[l])
        x = _scatter_final(idx3, gate.reshape(B * N_EXPERTS, 1, cap), ye, x2, g_final)
    return x
```

```python
import math

import jax
import jax.numpy as jnp
from jax import lax
from jax.experimental import pallas as pl
from jax.experimental.pallas import tpu as pltpu

F32 = jnp.float32
BF16 = jnp.bfloat16
I32 = jnp.int32

EPS = 1e-6
N_MLSTM_HEADS = 4
MLSTM_HEAD_DIM = 128
D_MLSTM = N_MLSTM_HEADS * MLSTM_HEAD_DIM
N_MLA_HEADS = 8
MLA_NOPE_DIM = 64
MLA_ROPE_DIM = 32
MLA_QK_DIM = MLA_NOPE_DIM + MLA_ROPE_DIM
MLA_V_DIM = 64
MLA_V_ROWS = 80
Q_LORA_RANK = 256
KV_LORA_RANK = 128
ROPE_THETA = 10000.0
N_GATE = 4 * N_MLSTM_HEADS
N_MEM_HEADS = 4
N_EXPERTS = 16
EC_CAPACITY_FACTOR = 2

LANES = 128
SUBLANES = 8
VMEM_LIMIT_BYTES = 56 * 1024 * 1024

MLSTM_CHUNK = 256
MLSTM_PAD_ROWS = 16
MLSTM_CHUNKS_PER_STEP = 4
INPROJ_SPLIT = 2
TM_INPROJ = MLSTM_CHUNK * INPROJ_SPLIT
TQ_MLA = 256
MLA_CHUNKS = 16
TM_POST = 512
POST_SPLIT = 2
TF_FFN = 512
FINAL_CHUNK = 512
FINAL_IN_BUFS = 4
FINAL_OUT_BUFS = 3
CUM_BLOCK = 256
SCATTER_GROUP = 8


def _cparams(sem):
    return pltpu.CompilerParams(dimension_semantics=sem, vmem_limit_bytes=VMEM_LIMIT_BYTES)


def _dot(a, b):
    return jnp.dot(a, b, preferred_element_type=F32)


def _dot_nt(a, b):
    return lax.dot_general(a, b, (((1,), (1,)), ((), ())), preferred_element_type=F32)


def _rms(x, g):
    return x * lax.rsqrt(jnp.mean(x * x, axis=-1, keepdims=True) + EPS) * g


def _sigmoid(x):
    return 1.0 / (1.0 + jnp.exp(-x))


def _full(shape):
    return pl.BlockSpec(shape, lambda *_: (0,) * len(shape))


def _inproj_kernel(x_ref, xp_ref, xn_ref, posr_ref, gmix_ref, wqk_ref, wvt_ref, wo_ref, wmisc_ref, wkr_ref, bg_ref,
                   conv_ref, gqa_ref, wq_ref, gkva_ref, wk_ref, wv_ref, vones_ref, invfc_ref,
                   qT_m_ref, k_ref, vT_m_ref, o_ref, gc_ref, gt_ref, qT_ref, kcat_ref, vT_ref):
    tm = x_ref.shape[1]
    hr = tm // INPROJ_SPLIT
    i = pl.program_id(1)
    last = pl.num_programs(1) - 1
    g = gmix_ref[...]
    xm_bf = _rms(x_ref[0], g).astype(BF16)
    halo = [(_rms(xp_ref[0], g) * (i > 0).astype(F32)).astype(BF16),
            (_rms(xn_ref[0], g) * (i < last).astype(F32)).astype(BF16)]

    def matmuls(u):
        r = slice(u * hr, (u + 1) * hr)
        xs = xm_bf[r]
        before = halo[0] if u == 0 else xm_bf[u * hr - SUBLANES:u * hr]
        after = halo[1] if u == INPROJ_SPLIT - 1 else xm_bf[(u + 1) * hr:(u + 1) * hr + SUBLANES]
        lhs = jnp.concatenate([before, xs, after], axis=0)
        pqk = _dot(lhs, wqk_ref[...])
        misc = _dot(xs, wmisc_ref[...])
        vT_m = _dot_nt(wvt_ref[...], xs)
        o_ref[0, r] = _dot(xs, wo_ref[...])
        krT = _dot_nt(wkr_ref[...], xs)
        ckvn = _rms(misc[:, Q_LORA_RANK:Q_LORA_RANK + KV_LORA_RANK], gkva_ref[...]).astype(BF16)
        cqn = _rms(misc[:, :Q_LORA_RANK], gqa_ref[...]).astype(BF16)
        return dict(r=r, pqk=pqk, vT_m=vT_m, krT=krT, gp=misc[:, Q_LORA_RANK + KV_LORA_RANK:] + bg_ref[...],
                    kn=_dot(ckvn, wk_ref[...]), qT=_dot_nt(wq_ref[...], cqn), vT_a=_dot_nt(wv_ref[...], ckvn))

    lane = lax.broadcasted_iota(I32, (hr, LANES), 1)
    is_f = ((lane >= 4) & (lane < 8)) | ((lane >= 12) & (lane < 16))
    row_id = lax.broadcasted_iota(I32, (hr, LANES), 0)
    cw = conv_ref[...]
    r0, r1, r2 = MLA_NOPE_DIM, MLA_NOPE_DIM + MLA_ROPE_DIM // 2, MLA_QK_DIM

    def scan(x, op, ident, suffix):
        step = 1
        while step < hr:
            if step < SUBLANES:
                ok = (row_id + step < hr) if suffix else (row_id >= step)
                shifted = jnp.where(ok, pltpu.roll(x, hr - step if suffix else step, 0), ident)
            else:
                pad = jnp.full((step, LANES), ident, F32)
                shifted = jnp.concatenate([x[step:], pad] if suffix else [pad, x[:hr - step]], axis=0)
            x = op(x, shifted)
            step *= 2
        return x

    def elementwise(m):
        r = m["r"]
        vT_m_ref[0, :, r] = m["vT_m"].astype(BF16)

        pqk = m["pqk"]
        up = pltpu.roll(pqk, 1, 0)[SUBLANES:SUBLANES + hr]
        dn = pltpu.roll(pqk, hr + 2 * SUBLANES - 1, 0)[SUBLANES:SUBLANES + hr]
        mid = pqk[SUBLANES:SUBLANES + hr]
        conv = up * cw[0:1] + mid * cw[1:2] + dn * cw[2:3]
        act = conv * _sigmoid(conv)
        qT_m_ref[0, :, r] = act[:, :D_MLSTM].T.astype(BF16)
        k_ref[0, r] = (act[:, D_MLSTM:] * (MLSTM_HEAD_DIM ** -0.5)).astype(BF16)

        gp = m["gp"]
        logsig = jnp.minimum(gp, 0.0) - jnp.log1p(jnp.exp(-jnp.abs(gp)))
        lf = jnp.where(is_f, logsig, gp)
        cum = jnp.where(lane < 8, scan(lf, jnp.add, 0.0, False), scan(lf, jnp.add, 0.0, True))
        a = lf - pltpu.roll(cum, LANES - 4, 1)
        gc = jnp.where(is_f, cum, a)
        run_max = jnp.where(lane < 8, scan(gc, jnp.maximum, -jnp.inf, False),
                            scan(gc, jnp.maximum, -jnp.inf, True))
        table = jnp.where(lane < N_GATE, gc, pltpu.roll(run_max, N_GATE, 1))
        gc_ref[0, r] = table
        gt_ref[0, :, r] = table.T[:2 * N_GATE]

        ang_t = invfc_ref[...] * posr_ref[0, :, r]
        cos_t = jnp.cos(ang_t)
        sin_t = jnp.sin(ang_t)

        def rope_rows(blk):
            t1 = blk[r0:r1]
            t2 = blk[r1:r2]
            return jnp.concatenate([blk[:r0], t1 * cos_t - t2 * sin_t, t2 * cos_t + t1 * sin_t, blk[r2:]], axis=0)

        kr = rope_rows(m["krT"]).T
        for h in range(N_MLA_HEADS):
            sl = slice(h * LANES, (h + 1) * LANES)
            kcat_ref[0, r, sl] = (m["kn"][:, sl] + kr).astype(BF16)
            qT_ref[0, sl, r] = rope_rows(m["qT"][sl]).astype(BF16)
        vT_ref[0, :, r] = (m["vT_a"] + vones_ref[...]).astype(BF16)

    parts = [matmuls(u) for u in range(INPROJ_SPLIT)]
    for m in parts:
        elementwise(m)


def _inproj(x, pos_row, g_mix, w_in, b_gates, conv_qk, g_q_a, w_q_b, g_kv_a, w_kv_b):
    B, S, D = x.shape
    tm = TM_INPROJ
    assert S % tm == 0
    nt = S // tm
    hb = tm // SUBLANES
    sizes = [D_MLSTM, D_MLSTM, D_MLSTM, D_MLSTM, N_GATE, Q_LORA_RANK, KV_LORA_RANK, MLA_ROPE_DIM]
    off = [sum(sizes[:n]) for n in range(len(sizes) + 1)]
    assert w_in.shape[1] == off[-1]
    wqk = w_in[:, off[0]:off[2]].astype(BF16)
    wvt = w_in[:, off[2]:off[3]].T.astype(BF16)
    wo = w_in[:, off[3]:off[4]].astype(BF16)
    w_gate = w_in[:, off[4]:off[5]]
    w_cq = w_in[:, off[5]:off[6]]
    w_ckv = w_in[:, off[6]:off[7]]
    w_kr = w_in[:, off[7]:off[8]]
    wkr = jnp.zeros((LANES, D), F32).at[MLA_NOPE_DIM:MLA_QK_DIM].set(w_kr.T).astype(BF16)
    gate_blk = jnp.zeros((D, LANES), F32).at[:, :N_GATE].set(w_gate)
    wmisc = jnp.concatenate([w_cq, w_ckv, gate_blk], axis=1).astype(BF16)
    bg = jnp.zeros((1, LANES), F32).at[0, :N_GATE].set(b_gates)
    pad_heads = lambda w3: jnp.pad(w3, ((0, 0), (0, 0), (0, LANES - w3.shape[2]))).reshape(w3.shape[0], -1)
    wq = pad_heads(w_q_b.reshape(-1, N_MLA_HEADS, MLA_QK_DIM)).T.astype(BF16)
    wkv3 = w_kv_b.reshape(-1, N_MLA_HEADS, MLA_NOPE_DIM + MLA_V_DIM)
    wk = pad_heads(wkv3[:, :, :MLA_NOPE_DIM]).astype(BF16)
    wv = jnp.pad(wkv3[:, :, MLA_NOPE_DIM:], ((0, 0), (0, 0), (0, MLA_V_ROWS - MLA_V_DIM)))
    wv = wv.reshape(wv.shape[0], -1).T.astype(BF16)
    vones = jnp.zeros((N_MLA_HEADS, MLA_V_ROWS, MLSTM_CHUNK), F32).at[:, MLA_V_DIM].set(1.0)
    vones = vones.reshape(-1, MLSTM_CHUNK)
    inv_freq = ROPE_THETA ** (-jnp.arange(0, MLA_ROPE_DIM, 2, dtype=F32) / MLA_ROPE_DIM)
    invf_col = inv_freq.reshape(-1, 1)

    row = lambda w: pl.BlockSpec((1, tm, w), lambda b, i: (b, i, 0))
    in_specs = [
        row(D),
        pl.BlockSpec((1, SUBLANES, D), lambda b, i: (b, jnp.maximum(i * hb - 1, 0), 0)),
        pl.BlockSpec((1, SUBLANES, D), lambda b, i: (b, jnp.minimum((i + 1) * hb, S // SUBLANES - 1), 0)),
        pl.BlockSpec((1, 1, tm), lambda b, i: (b, 0, i)),
        _full((1, D)), _full(wqk.shape), _full(wvt.shape), _full(wo.shape), _full(wmisc.shape), _full(wkr.shape),
        _full((1, LANES)), _full(conv_qk.shape), _full((1, Q_LORA_RANK)), _full(wq.shape), _full((1, KV_LORA_RANK)),
        _full(wk.shape), _full(wv.shape), _full(vones.shape), _full(invf_col.shape),
    ]
    out_shape = [
        jax.ShapeDtypeStruct((B, D_MLSTM, S), BF16),
        jax.ShapeDtypeStruct((B, S, D_MLSTM), BF16),
        jax.ShapeDtypeStruct((B, D_MLSTM, S), BF16),
        jax.ShapeDtypeStruct((B, S, D_MLSTM), F32),
        jax.ShapeDtypeStruct((B, S, LANES), F32),
        jax.ShapeDtypeStruct((B, 2 * N_GATE, S), F32),
        jax.ShapeDtypeStruct((B, N_MLA_HEADS * LANES, S), BF16),
        jax.ShapeDtypeStruct((B, S, N_MLA_HEADS * LANES), BF16),
        jax.ShapeDtypeStruct((B, N_MLA_HEADS * MLA_V_ROWS, S), BF16),
    ]
    out_specs = [
        pl.BlockSpec((1, D_MLSTM, tm), lambda b, i: (b, 0, i)),
        row(D_MLSTM),
        pl.BlockSpec((1, D_MLSTM, tm), lambda b, i: (b, 0, i)),
        row(D_MLSTM), row(LANES),
        pl.BlockSpec((1, 2 * N_GATE, tm), lambda b, i: (b, 0, i)),
        pl.BlockSpec((1, N_MLA_HEADS * LANES, tm), lambda b, i: (b, 0, i)),
        row(N_MLA_HEADS * LANES),
        pl.BlockSpec((1, N_MLA_HEADS * MLA_V_ROWS, tm), lambda b, i: (b, 0, i)),
    ]
    return pl.pallas_call(
        _inproj_kernel, grid=(B, nt), in_specs=in_specs, out_specs=out_specs, out_shape=out_shape,
        compiler_params=_cparams(("parallel", "parallel")), name="inproj",
    )(x, x, x, pos_row, g_mix.reshape(1, D), wqk, wvt, wo, wmisc, wkr, bg, conv_qk, g_q_a.reshape(1, -1), wq,
      g_kv_a.reshape(1, -1), wk, wv, vones, invf_col)


def _mlstm_kernel(kf_ref, qf_ref, vf_ref, gcf_ref, gtf_ref, kb_ref, qb_ref, vb_ref, gcb_ref, gtb_ref,
                  hf_ref, hb_ref, cn_ref, m_ref):
    T = MLSTM_CHUNK
    hd = MLSTM_HEAD_DIM

    @pl.when(pl.program_id(1) == 0)
    def _():
        cn_ref[...] = jnp.zeros_like(cn_ref)
        m_ref[...] = jnp.zeros_like(m_ref)

    s_idx = lax.broadcasted_iota(I32, (T, T), 0)
    t_idx = lax.broadcasted_iota(I32, (T, T), 1)
    ones_rows = (lax.broadcasted_iota(I32, (MLSTM_PAD_ROWS, T), 0) == 0).astype(BF16)
    neg_inf = F32(-jnp.inf)

    def chunk(c):
        streams = []
        for d, (k_ref, qT_ref, vT_ref, gc_ref, gt_ref, h_ref) in enumerate(
                ((kf_ref, qf_ref, vf_ref, gcf_ref, gtf_ref, hf_ref),
                 (kb_ref, qb_ref, vb_ref, gcb_ref, gtb_ref, hb_ref))):
            end = T - 1 if d == 0 else 0
            pos = c if d == 0 else MLSTM_CHUNKS_PER_STEP - 1 - c
            tok = slice(pos * T, (pos + 1) * T)
            gc = gc_ref[0, tok]
            gt = gt_ref[0, :, tok]
            for h in range(N_MLSTM_HEADS):
                st = d * N_MLSTM_HEADS + h
                la = d * 8 + h
                a_row = gt[la:la + 1]
                b_row = gt[la + 4:la + 5]
                m_prev = m_ref[st][0:1, 0:1]
                kh = k_ref[0, tok, h * hd:(h + 1) * hd]
                qTh = qT_ref[0, h * hd:(h + 1) * hd, tok]
                v_ext = jnp.concatenate([vT_ref[0, h * hd:(h + 1) * hd, tok], ones_rows], axis=0)
                cn = cn_ref[st]
                m_run = jnp.maximum(gt[N_GATE + la:N_GATE + la + 1], m_prev)
                m_end = m_run[:, end:end + 1]
                vw = (v_ext.astype(F32) * jnp.exp(a_row - m_end)).astype(BF16)
                streams.append(dict(
                    st=st, d=d, h=h, h_ref=h_ref, tok=tok, a_col=gc[:, la:la + 1], b_row=b_row, m_prev=m_prev,
                    m_run=m_run, m_end=m_end, g_sum=b_row[:, end:end + 1], cn=cn, v_ext=v_ext,
                    s=_dot(kh, qTh),
                    qcn=_dot(cn.astype(BF16), qTh),
                    upd=_dot(vw, kh)))
        for x in streams:
            mask = (s_idx <= t_idx) if x["d"] == 0 else (s_idx >= t_idx)
            e = jnp.exp(jnp.where(mask, x["a_col"] - x["m_run"], neg_inf))
            x["pw"] = (e * x["s"]).astype(BF16)
        for x in streams:
            x["pv"] = _dot(x["v_ext"], x["pw"])
        for x in streams:
            st, h = x["st"], x["h"]
            iw = jnp.exp(x["m_prev"] - x["m_run"])
            num = iw * x["qcn"][:hd] + x["pv"][:hd]
            den = iw * x["qcn"][hd:hd + 1] + x["pv"][hd:hd + 1]
            floor = jnp.exp(-(x["b_row"] + x["m_run"]))
            x["h_ref"][0, h * hd:(h + 1) * hd, x["tok"]] = num / jnp.maximum(jnp.abs(den), floor)
            cn_ref[st] = jnp.exp(x["m_prev"] - x["m_end"]) * x["cn"] + x["upd"]
            m_ref[st] = jnp.broadcast_to(x["g_sum"] + x["m_end"], (SUBLANES, LANES))

    for c in range(MLSTM_CHUNKS_PER_STEP):
        chunk(c)


def _mlstm(k, qT, vT, gc, gt):
    B, S, _ = k.shape
    T = MLSTM_CHUNK * MLSTM_CHUNKS_PER_STEP
    nc = S // T
    fwd = lambda b, j: (b, j, 0)
    bwd = lambda b, j: (b, nc - 1 - j, 0)
    fwd_t = lambda b, j: (b, 0, j)
    bwd_t = lambda b, j: (b, 0, nc - 1 - j)

    def specs(row_map, col_map):
        return [pl.BlockSpec((1, T, D_MLSTM), row_map), pl.BlockSpec((1, D_MLSTM, T), col_map),
                pl.BlockSpec((1, D_MLSTM, T), col_map), pl.BlockSpec((1, T, LANES), row_map),
                pl.BlockSpec((1, 2 * N_GATE, T), col_map)]

    n_state = 2 * N_MLSTM_HEADS
    return pl.pallas_call(
        _mlstm_kernel, grid=(B, nc),
        in_specs=specs(fwd, fwd_t) + specs(bwd, bwd_t),
        out_specs=[pl.BlockSpec((1, D_MLSTM, T), fwd_t), pl.BlockSpec((1, D_MLSTM, T), bwd_t)],
        out_shape=[jax.ShapeDtypeStruct((B, D_MLSTM, S), F32)] * 2,
        scratch_shapes=[pltpu.VMEM((n_state, MLSTM_HEAD_DIM + MLSTM_PAD_ROWS, MLSTM_HEAD_DIM), F32),
                        pltpu.VMEM((n_state, SUBLANES, LANES), F32)],
        compiler_params=_cparams(("parallel", "arbitrary")), name="mlstm",
    )(k, qT, vT, gc, gt, k, qT, vT, gc, gt)


def _mla_kernel(qT_ref, k_ref, vT_ref, g_ref, o_ref, sa_sc, sb_sc, ma_sc, mb_sc):
    t = pl.program_id(0)
    c = (MLA_QK_DIM ** -0.5) * math.log2(math.e)
    S = k_ref.shape[1]
    tq = qT_ref.shape[2]
    ck = S // MLA_CHUNKS

    @pl.when(t == 0)
    def _():
        sb_sc[...] = jnp.zeros(sb_sc.shape, F32)
        mb_sc[...] = jnp.zeros(mb_sc.shape, F32)

    def step(s_cur, m_cur, s_prev, m_prev_ref):
        m_prev = [m_prev_ref[h][0:1] for h in range(2)]
        m_run = [jnp.full((1, tq), -jnp.inf, F32) for _ in range(2)]
        accs = [None, None]
        for j in range(MLA_CHUNKS):
            rows = slice(j * ck, (j + 1) * ck)
            for h in range(2):
                s = _dot(k_ref[0, rows, h * LANES:(h + 1) * LANES], qT_ref[0, h * LANES:(h + 1) * LANES, :])
                m_run[h] = jnp.maximum(m_run[h], jnp.max(s, axis=0, keepdims=True))
                s_cur[h, rows, :] = s
            for h in range(2):
                p = jnp.exp2((s_prev[h, rows, :] - m_prev[h]) * c).astype(BF16)
                part = _dot(vT_ref[0, h * MLA_V_ROWS:(h + 1) * MLA_V_ROWS, rows], p)
                accs[h] = part if j == 0 else accs[h] + part
        for h in range(2):
            m_cur[h] = jnp.broadcast_to(m_run[h], (SUBLANES, tq))
        ys = []
        for h in range(2):
            acc = accs[h]
            o = acc[:MLA_V_DIM] / acc[MLA_V_DIM:MLA_V_DIM + 1]
            ms = jnp.mean(o * o, axis=0, keepdims=True)
            ys.append(o * lax.rsqrt(ms + EPS))
        o_ref[0] = jnp.concatenate(ys, axis=0).T * g_ref[...]

    @pl.when(t % 2 == 0)
    def _():
        step(sa_sc, ma_sc, sb_sc, mb_sc)

    @pl.when(t % 2 == 1)
    def _():
        step(sb_sc, mb_sc, sa_sc, ma_sc)


def _mla(qT, kcat, vT, g_head_mla):
    B, S, _ = kcat.shape
    tq = TQ_MLA
    npair = N_MLA_HEADS // 2
    nq = S // tq
    nt = B * npair * nq

    def tile(t):
        return t // (npair * nq), (t // nq) % npair, t % nq

    def a_map(f):
        return lambda t: f(*tile(jnp.minimum(t, nt - 1)))

    def b_map(f):
        return lambda t: f(*tile(jnp.maximum(t - 1, 0)))

    return pl.pallas_call(
        _mla_kernel, grid=(nt + 1,),
        in_specs=[pl.BlockSpec((1, 2 * LANES, tq), a_map(lambda b, p, i: (b, p, i))),
                  pl.BlockSpec((1, S, 2 * LANES), a_map(lambda b, p, i: (b, 0, p))),
                  pl.BlockSpec((1, 2 * MLA_V_ROWS, S), b_map(lambda b, p, i: (b, p, 0))),
                  pl.BlockSpec((1, LANES), b_map(lambda b, p, i: (0, p)))],
        out_specs=pl.BlockSpec((1, tq, LANES), b_map(lambda b, p, i: (b, i, p))),
        out_shape=jax.ShapeDtypeStruct((B, S, N_MLA_HEADS * MLA_V_DIM), F32),
        scratch_shapes=[pltpu.VMEM((2, S, tq), F32), pltpu.VMEM((2, S, tq), F32),
                        pltpu.VMEM((2, SUBLANES, tq), F32), pltpu.VMEM((2, SUBLANES, tq), F32)],
        compiler_params=_cparams(("arbitrary",)), name="mla_attn",
    )(qT, kcat, vT, g_head_mla.reshape(1, -1))


def _memkv_kernel(mem_ref, g_ref, wk_ref, wv_ref, k_ref, v_ref):
    mn = _rms(mem_ref[0], g_ref[...]).astype(BF16)
    k_ref[0] = _dot(mn, wk_ref[...]).astype(BF16)
    v_ref[0] = _dot(mn, wv_ref[...]).astype(BF16)


def _memkv(mem, g_mem_kv, w_k, w_v):
    B, M, D = mem.shape
    blk = pl.BlockSpec((1, M, D), lambda b: (b, 0, 0))
    return pl.pallas_call(
        _memkv_kernel, grid=(B,),
        in_specs=[blk, _full((1, D)), _full(w_k.shape), _full(w_v.shape)],
        out_specs=[blk, blk], out_shape=[jax.ShapeDtypeStruct((B, M, D), BF16)] * 2,
        compiler_params=_cparams(("parallel",)), name="memkv",
    )(mem, g_mem_kv.reshape(1, D), w_k.astype(BF16), w_v.astype(BF16))


def _post_kernel(x_ref, hf_ref, hb_ref, om_ref, ya_ref, km_ref, vm_ref, ghm_ref, wout_ref, gmx_ref, wmq_ref,
                 wmo_ref, gffn_ref, wrh_ref, wrl_ref, x2_ref, hn_ref, aff_ref, afft_ref):
    tm = x_ref.shape[1]
    D = x_ref.shape[2]
    hr = tm // POST_SPLIT
    subs = [slice(i * hr, (i + 1) * hr) for i in range(POST_SPLIT)]
    ghm = ghm_ref[...]
    dm = D // N_MEM_HEADS
    heads = [slice(h * dm, (h + 1) * dm) for h in range(N_MEM_HEADS)]

    ycat = []
    for r in subs:
        hm = (hf_ref[0, :, r] + hb_ref[0, :, r]).T
        gate = _sigmoid(om_ref[0, r])
        parts = []
        for h in range(N_MLSTM_HEADS):
            sl = slice(h * MLSTM_HEAD_DIM, (h + 1) * MLSTM_HEAD_DIM)
            parts.append(gate[:, sl] * _rms(hm[:, sl], ghm[:, sl]))
        ycat.append(jnp.concatenate(parts + [ya_ref[0, r]], axis=1).astype(BF16))
    x1 = [x_ref[0, r] + _dot(y, wout_ref[...]) for r, y in zip(subs, ycat)]

    xn = [_rms(v, gmx_ref[...]).astype(BF16) for v in x1]
    qm = [_dot(v, wmq_ref[...]).astype(BF16) for v in xn]
    km = km_ref[0]
    vm = vm_ref[0]
    scores = [[_dot_nt(q[:, sl], km[:, sl]) * (dm ** -0.5) for sl in heads] for q in qm]
    probs = []
    for per_sub in scores:
        ps = []
        for s in per_sub:
            e = jnp.exp(s - jnp.max(s, axis=1, keepdims=True))
            ps.append((e / jnp.sum(e, axis=1, keepdims=True)).astype(BF16))
        probs.append(ps)
    om = [jnp.concatenate([_dot(p, vm[:, sl]) for p, sl in zip(ps, heads)], axis=1).astype(BF16) for ps in probs]
    x2 = [v + _dot(o, wmo_ref[...]) for v, o in zip(x1, om)]

    hns = []
    for r, v in zip(subs, x2):
        x2_ref[0, r] = v
        hn = _rms(v, gffn_ref[...])
        hn_ref[0, r] = hn.reshape(hr, SUBLANES, D // SUBLANES)
        hi = hn.astype(BF16)
        hns.append((hi, (hn - hi.astype(F32)).astype(BF16)))
    wrh = wrh_ref[...]
    logits = [_dot(hi, wrh) + (_dot(hi, wrl_ref[...]) + _dot(lo, wrh)) for hi, lo in hns]
    valid = lax.broadcasted_iota(I32, (hr, LANES), 1) < N_EXPERTS
    for r, lg in zip(subs, logits):
        lg = jnp.where(valid, lg, -jnp.inf)
        e = jnp.exp(lg - jnp.max(lg, axis=1, keepdims=True))
        aff = e / jnp.sum(e, axis=1, keepdims=True)
        aff_ref[0, r] = aff
        afft_ref[0, :, r] = aff.T[:N_EXPERTS]


def _post(x, hf, hb, om, ya, kmem, vmem, g_head_mlstm, w_out, g_mem_x, w_mem_q, w_mem_o, g_ffn, w_router):
    B, S, D = x.shape
    tm = TM_POST
    M = kmem.shape[1]
    wr = jnp.zeros((D, LANES), F32).at[:, :N_EXPERTS].set(w_router)
    wrh = wr.astype(BF16)
    wrl = (wr - wrh.astype(F32)).astype(BF16)
    row = lambda w: pl.BlockSpec((1, tm, w), lambda b, i: (b, i, 0))
    col = lambda w: pl.BlockSpec((1, w, tm), lambda b, i: (b, 0, i))
    mem_spec = pl.BlockSpec((1, M, D), lambda b, i: (b, 0, 0))
    return pl.pallas_call(
        _post_kernel, grid=(B, S // tm),
        in_specs=[row(D), col(D_MLSTM), col(D_MLSTM), row(D_MLSTM), row(D_MLSTM), mem_spec, mem_spec,
                  _full((1, D_MLSTM)), _full((D, D)), _full((1, D)), _full((D, D)), _full((D, D)),
                  _full((1, D)), _full((D, LANES)), _full((D, LANES))],
        out_specs=[row(D), pl.BlockSpec((1, tm, SUBLANES, D // SUBLANES), lambda b, i: (b, i, 0, 0)), row(LANES),
                   pl.BlockSpec((1, N_EXPERTS, tm), lambda b, i: (b, 0, i))],
        out_shape=[jax.ShapeDtypeStruct((B, S, D), F32), jax.ShapeDtypeStruct((B, S, SUBLANES, D // SUBLANES), F32),
                   jax.ShapeDtypeStruct((B, S, LANES), F32), jax.ShapeDtypeStruct((B, N_EXPERTS, S), F32)],
        compiler_params=_cparams(("parallel", "parallel")), name="post_mixer",
    )(x, hf, hb, om, ya, kmem, vmem, g_head_mlstm.reshape(1, -1), w_out.astype(BF16), g_mem_x.reshape(1, D),
      w_mem_q.astype(BF16), w_mem_o.astype(BF16), g_ffn.reshape(1, D), wrh, wrl)


def _topk_kernel(aff_ref, afft_ref, affb_ref, idx_ref, gate_ref, idx_sc, gate_sc):
    S = aff_ref.shape[1]
    cap = idx_ref.shape[2]
    nblk = S // CUM_BLOCK

    def open_interval(c):
        lo, hi = c
        mid = 0.5 * (lo + hi)
        return jnp.max(jnp.where((mid > lo) & (mid < hi), 1.0, 0.0)) > 0.0

    def bisect(c):
        lo, hi = c
        mid = 0.5 * (lo + hi)
        cnt = jnp.sum((afft_ref[0] >= mid).astype(F32), axis=1, keepdims=True)
        ge = cnt >= cap
        return jnp.where(ge, mid, lo), jnp.where(ge, hi, mid)

    lo_c, hi_c = lax.while_loop(open_interval, bisect,
                                (jnp.zeros((N_EXPERTS, 1), F32), jnp.full((N_EXPERTS, 1), 2.0, F32)))
    need_c = cap - jnp.sum((afft_ref[0] >= hi_c).astype(F32), axis=1, keepdims=True)
    eye = (lax.broadcasted_iota(I32, (N_EXPERTS, LANES), 0) == lax.broadcasted_iota(I32, (N_EXPERTS, LANES), 1))
    to_row = lambda col: jnp.sum(jnp.where(eye, col, 0.0), axis=0, keepdims=True)
    lo, hi, need = to_row(lo_c), to_row(hi_c), to_row(need_c)

    r = lax.broadcasted_iota(I32, (CUM_BLOCK, CUM_BLOCK), 0)
    c = lax.broadcasted_iota(I32, (CUM_BLOCK, CUM_BLOCK), 1)
    tri = (c <= r).astype(BF16)

    def cum_body(blk, carry):
        ce, cs = carry
        rows = pl.ds(pl.multiple_of(blk * CUM_BLOCK, CUM_BLOCK), CUM_BLOCK)
        a = aff_ref[0, rows, :]
        sure = a >= hi
        tie = (a >= lo) & jnp.logical_not(sure)
        eq = tie.astype(F32)
        eq_incl = _dot(tri, eq.astype(BF16)) + ce
        sel = jnp.where(sure | (tie & (eq_incl - eq < need)), 1.0, 0.0)
        local_incl = _dot(tri, sel.astype(BF16))
        rank_local = jnp.where(sel > 0.0, local_incl - sel, -2.0).astype(BF16)
        base = lax.convert_element_type(blk * CUM_BLOCK, F32)
        for e in range(N_EXPERTS):
            slot_local = jnp.clip(j_row - cs[:, e:e + 1], -1.0, CUM_BLOCK + 1.0).astype(BF16)
            onehot = jnp.where(rank_local[:, e:e + 1] == slot_local, one_bf, zero_bf)
            a_row = affb_ref[0, e, pl.ds(blk, 1), :]
            a_hi = a_row.astype(BF16).astype(F32)
            a_mid = (a_row - a_hi).astype(BF16).astype(F32)
            a_lo = a_row - a_hi - a_mid
            lhs = jnp.concatenate([tok_rows, a_hi, a_mid, a_lo, pad_rows], axis=0).astype(BF16)
            hit = _dot(lhs, onehot)
            idx_sc[e:e + 1, :] += hit[0:1] + base * hit[1:2]
            gate_sc[e:e + 1, :] += hit[2:3] + hit[3:4] + hit[4:5]
        return eq_incl[CUM_BLOCK - 1:CUM_BLOCK], cs + local_incl[CUM_BLOCK - 1:CUM_BLOCK]

    j_row = lax.broadcasted_iota(I32, (1, cap), 1).astype(F32)
    tok = lax.broadcasted_iota(I32, (1, CUM_BLOCK), 1).astype(F32)
    tok_rows = jnp.concatenate([tok, jnp.ones((1, CUM_BLOCK), F32)], axis=0)
    pad_rows = jnp.zeros((SUBLANES - 5, CUM_BLOCK), F32)
    one_bf = jnp.ones((CUM_BLOCK, cap), BF16)
    zero_bf = jnp.zeros((CUM_BLOCK, cap), BF16)
    idx_sc[...] = jnp.zeros_like(idx_sc)
    gate_sc[...] = jnp.zeros_like(gate_sc)
    zero = jnp.zeros((1, LANES), F32)
    lax.fori_loop(0, nblk, cum_body, (zero, zero))
    idx_ref[0] = idx_sc[...].astype(I32)
    gate_ref[0] = gate_sc[...]


def _topk(aff, aff_t, cap):
    B, S, _ = aff.shape
    nblk = S // CUM_BLOCK
    out = pl.BlockSpec((1, N_EXPERTS, cap), lambda b: (b, 0, 0))
    return pl.pallas_call(
        _topk_kernel, grid=(B,),
        in_specs=[pl.BlockSpec((1, S, LANES), lambda b: (b, 0, 0)),
                  pl.BlockSpec((1, N_EXPERTS, S), lambda b: (b, 0, 0)),
                  pl.BlockSpec((1, N_EXPERTS, nblk, CUM_BLOCK), lambda b: (b, 0, 0, 0))],
        out_specs=[out, out],
        out_shape=[jax.ShapeDtypeStruct((B, N_EXPERTS, cap), I32), jax.ShapeDtypeStruct((B, N_EXPERTS, cap), F32)],
        scratch_shapes=[pltpu.VMEM((N_EXPERTS, cap), F32), pltpu.VMEM((N_EXPERTS, cap), F32)],
        compiler_params=_cparams(("parallel",)), name="topk",
    )(aff, aff_t, aff_t.reshape(B, N_EXPERTS, nblk, CUM_BLOCK))


def _gather_kernel(idx_ref, hn_hbm, xe_ref, buf_a, buf_b, rows_sc, sem):
    b = pl.program_id(0)
    e = pl.program_id(1)
    cap = xe_ref.shape[2]

    def fetch(seq, buf, slot):
        return pltpu.make_async_copy(hn_hbm.at[seq], buf, sem.at[slot])

    @pl.when((b == 0) & (e == 0))
    def _():
        fetch(0, buf_a, 0).start()

    def run(cur, cur_slot, nxt, nxt_slot):
        @pl.when(e == 0)
        def _():
            fetch(b, cur, cur_slot).wait()

            @pl.when(b + 1 < pl.num_programs(0))
            def _():
                fetch(b + 1, nxt, nxt_slot).start()

        def body(j, _):
            i = idx_ref[0, 0, j]
            rows_sc[pl.ds(j, 1)] = cur[pl.ds(i, 1)]
            return 0

        lax.fori_loop(0, cap, body, 0, unroll=8)
        xe_ref[0, 0] = rows_sc[...].reshape(cap, xe_ref.shape[3]).astype(BF16)

    @pl.when(b % 2 == 0)
    def _():
        run(buf_a, 0, buf_b, 1)

    @pl.when(b % 2 == 1)
    def _():
        run(buf_b, 1, buf_a, 0)


def _gather(idx3, hn, cap):
    B, S, sub, lanes = hn.shape
    E = N_EXPERTS
    D = sub * lanes
    seq_buf = pltpu.VMEM((S, sub, lanes), F32)
    return pl.pallas_call(
        _gather_kernel, grid=(B, E),
        in_specs=[pl.BlockSpec((1, 1, cap), lambda b, e: (b * E + e, 0, 0), memory_space=pltpu.SMEM),
                  pl.BlockSpec(memory_space=pl.ANY)],
        out_specs=pl.BlockSpec((1, 1, cap, D), lambda b, e: (b, e, 0, 0)),
        out_shape=jax.ShapeDtypeStruct((B, E, cap, D), BF16),
        scratch_shapes=[seq_buf, seq_buf, pltpu.VMEM((cap, sub, lanes), F32), pltpu.SemaphoreType.DMA((2,))],
        compiler_params=_cparams(("arbitrary", "arbitrary")), name="moe_gather",
    )(idx3, hn)


def _ffn_kernel(xe_ref, wg_ref, wu_ref, wd_ref, ye_ref, wg_sc, wu_sc, wd_sc):
    f = pl.program_id(1)
    nb = xe_ref.shape[0]

    @pl.when(f == 0)
    def _():
        ye_ref[...] = jnp.zeros_like(ye_ref)

    wg_sc[...] = wg_ref[0].astype(BF16)
    wu_sc[...] = wu_ref[0].astype(BF16)
    wd_sc[...] = wd_ref[0].astype(BF16)

    def up(b):
        xb = xe_ref[b, 0]
        return _dot(xb, wg_sc[...]), _dot(xb, wu_sc[...])

    def down(b, h1, h2):
        hid = (h1 * _sigmoid(h1) * h2).astype(BF16)
        ye_ref[b, 0] += _dot(hid, wd_sc[...])

    hs = up(0)
    for b in range(nb):
        nxt = up(b + 1) if b + 1 < nb else None
        down(b, *hs)
        hs = nxt


def _ffn(xe, w_gate, w_up, w_down):
    B, E, cap, D = xe.shape
    F = w_gate.shape[2]
    tf = TF_FFN
    return pl.pallas_call(
        _ffn_kernel, grid=(E, F // tf),
        in_specs=[pl.BlockSpec((B, 1, cap, D), lambda e, f: (0, e, 0, 0)),
                  pl.BlockSpec((1, D, tf), lambda e, f: (e, 0, f)),
                  pl.BlockSpec((1, D, tf), lambda e, f: (e, 0, f)),
                  pl.BlockSpec((1, tf, D), lambda e, f: (e, f, 0))],
        out_specs=pl.BlockSpec((B, 1, cap, D), lambda e, f: (0, e, 0, 0)),
        out_shape=jax.ShapeDtypeStruct((B, E, cap, D), F32),
        scratch_shapes=[pltpu.VMEM((D, tf), BF16), pltpu.VMEM((D, tf), BF16), pltpu.VMEM((tf, D), BF16)],
        compiler_params=_cparams(("parallel", "arbitrary")), name="moe_ffn",
    )(xe, w_gate, w_up, w_down)


def _scatter_final_kernel(idx_ref, gate_ref, ye_ref, g_ref, x2_hbm, out_hbm, acc_sc, ye_sc, x_buf, o_buf, in_sem,
                          out_sem):
    b = pl.program_id(0)
    e = pl.program_id(1)
    cap = ye_ref.shape[2]
    S = acc_sc.shape[0]
    ch = x_buf.shape[1]
    D = x_buf.shape[2]

    @pl.when(e == 0)
    def _():
        acc_sc[...] = jnp.zeros_like(acc_sc)

    ye_sc[...] = ye_ref[0, 0].reshape(ye_sc.shape)

    def body(jb, _):
        base = pl.multiple_of(jb * SCATTER_GROUP, SCATTER_GROUP)
        ids = [idx_ref[0, 0, base + u] for u in range(SCATTER_GROUP)]
        new = [acc_sc[pl.ds(ids[u], 1)] + ye_sc[pl.ds(base + u, 1)] * gate_ref[0, 0, base + u]
               for u in range(SCATTER_GROUP)]
        for u in range(SCATTER_GROUP):
            acc_sc[pl.ds(ids[u], 1)] = new[u]
        return 0

    lax.fori_loop(0, cap // SCATTER_GROUP, body, 0)

    n_in = x_buf.shape[0]
    n_out = o_buf.shape[0]

    def in_copy(c):
        return pltpu.make_async_copy(x2_hbm.at[b, pl.ds(c * ch, ch)], x_buf.at[c % n_in], in_sem.at[c % n_in])

    def out_copy(c):
        return pltpu.make_async_copy(o_buf.at[c % n_out], out_hbm.at[b, pl.ds(c * ch, ch)], out_sem.at[c % n_out])

    @pl.when(e == pl.num_programs(1) - 1)
    def _():
        nch = S // ch
        ahead = n_in - 1
        for c in range(min(ahead, nch)):
            in_copy(c).start()
        for c in range(nch):
            in_copy(c).wait()
            if c + ahead < nch:
                in_copy(c + ahead).start()
            if c >= n_out:
                out_copy(c - n_out).wait()
            moe = acc_sc[c * ch:(c + 1) * ch].reshape(ch, D)
            o_buf[c % n_out] = _rms(x_buf[c % n_in] + moe, g_ref[...])
            out_copy(c).start()
        for c in range(max(nch - n_out, 0), nch):
            out_copy(c).wait()


def _scatter_final(idx3, gate3, ye, x2, g_final):
    B, E, cap, D = ye.shape
    S = x2.shape[1]
    lanes = D // SUBLANES
    smem = lambda: pl.BlockSpec((1, 1, cap), lambda b, e: (b * E + e, 0, 0), memory_space=pltpu.SMEM)
    return pl.pallas_call(
        _scatter_final_kernel, grid=(B, E),
        in_specs=[smem(), smem(), pl.BlockSpec((1, 1, cap, D), lambda b, e: (b, e, 0, 0)), _full((1, D)),
                  pl.BlockSpec(memory_space=pl.ANY)],
        out_specs=pl.BlockSpec(memory_space=pl.ANY),
        out_shape=jax.ShapeDtypeStruct((B, S, D), F32),
        scratch_shapes=[pltpu.VMEM((S, SUBLANES, lanes), F32), pltpu.VMEM((cap, SUBLANES, lanes), F32),
                        pltpu.VMEM((FINAL_IN_BUFS, FINAL_CHUNK, D), F32),
                        pltpu.VMEM((FINAL_OUT_BUFS, FINAL_CHUNK, D), F32),
                        pltpu.SemaphoreType.DMA((FINAL_IN_BUFS,)), pltpu.SemaphoreType.DMA((FINAL_OUT_BUFS,))],
        compiler_params=_cparams(("arbitrary", "arbitrary")), name="moe_scatter_final",
    )(idx3, gate3, ye, g_final.reshape(1, D), x2)


def kernel(x, mem, positions, g_mix, w_in, b_gates, conv_qk, g_q_a, w_q_b, g_kv_a, w_kv_b, g_head_mlstm,
           g_head_mla, w_out, g_mem_x, g_mem_kv, w_mem_q, w_mem_k, w_mem_v, w_mem_o, g_ffn, w_router,
           w_exp_gate, w_exp_up, w_exp_down, g_final):
    B, S, D = x.shape
    depth = g_mix.shape[0]
    assert depth == 1, "the MoE residual is folded into the final norm kernel, which assumes one layer"
    cap = EC_CAPACITY_FACTOR * S // N_EXPERTS
    pos_row = positions.astype(F32).reshape(B, 1, S)
    for l in range(depth):
        qT_m, k_m, vT_m, om, gc, gt, qT, kcat, vT = _inproj(
            x, pos_row, g_mix[l], w_in[l], b_gates[l], conv_qk[l], g_q_a[l], w_q_b[l], g_kv_a[l], w_kv_b[l])
        hf, hb = _mlstm(k_m, qT_m, vT_m, gc, gt)
        ya = _mla(qT, kcat, vT, g_head_mla[l])
        kmem, vmem = _memkv(mem, g_mem_kv[l], w_mem_k[l], w_mem_v[l])
        x2, hn, aff, aff_t = _post(x, hf, hb, om, ya, kmem, vmem, g_head_mlstm[l], w_out[l], g_mem_x[l],
                            w_mem_q[l], w_mem_o[l], g_ffn[l], w_router[l])
        idx, gate = _topk(aff, aff_t, cap)
        idx3 = idx.reshape(B * N_EXPERTS, 1, cap)
        xe = _gather(idx3, hn, cap)
        ye = _ffn(xe, w_exp_gate[l], w_exp_up[l], w_exp_down[l])
        x = _scatter_final(idx3, gate.reshape(B * N_EXPERTS, 1, cap), ye, x2, g_final)
    return x
```

```python
import math

import jax
import jax.numpy as jnp
from jax import lax
from jax.experimental import pallas as pl
from jax.experimental.pallas import tpu as pltpu

F32 = jnp.float32
BF16 = jnp.bfloat16
I32 = jnp.int32

EPS = 1e-6
N_MLSTM_HEADS = 4
MLSTM_HEAD_DIM = 128
D_MLSTM = N_MLSTM_HEADS * MLSTM_HEAD_DIM
N_MLA_HEADS = 8
MLA_NOPE_DIM = 64
MLA_ROPE_DIM = 32
MLA_QK_DIM = MLA_NOPE_DIM + MLA_ROPE_DIM
MLA_V_DIM = 64
MLA_V_ROWS = 80
Q_LORA_RANK = 256
KV_LORA_RANK = 128
ROPE_THETA = 10000.0
N_GATE = 4 * N_MLSTM_HEADS
N_MEM_HEADS = 4
N_EXPERTS = 16
EC_CAPACITY_FACTOR = 2

LANES = 128
SUBLANES = 8
VMEM_LIMIT_BYTES = 56 * 1024 * 1024

MLSTM_CHUNK = 256
MLSTM_PAD_ROWS = 16
MLSTM_CHUNKS_PER_STEP = 4
INPROJ_SPLIT = 2
TM_INPROJ = MLSTM_CHUNK * INPROJ_SPLIT
TQ_MLA = 256
MLA_CHUNKS = 16
TM_POST = 512
POST_SPLIT = 2
TF_FFN = 512
FINAL_CHUNK = 512
FINAL_IN_BUFS = 4
FINAL_OUT_BUFS = 3
CUM_BLOCK = 256
SCATTER_GROUP = 8
YE_RING = 3


def _cparams(sem):
    return pltpu.CompilerParams(dimension_semantics=sem, vmem_limit_bytes=VMEM_LIMIT_BYTES)


def _dot(a, b):
    return jnp.dot(a, b, preferred_element_type=F32)


def _dot_nt(a, b):
    return lax.dot_general(a, b, (((1,), (1,)), ((), ())), preferred_element_type=F32)


def _rms(x, g):
    return x * lax.rsqrt(jnp.mean(x * x, axis=-1, keepdims=True) + EPS) * g


def _sigmoid(x):
    return 1.0 / (1.0 + jnp.exp(-x))


def _full(shape):
    return pl.BlockSpec(shape, lambda *_: (0,) * len(shape))


def _inproj_kernel(x_ref, xp_ref, xn_ref, posr_ref, gmix_ref, wqk_ref, wvt_ref, wo_ref, wmisc_ref, wkr_ref, bg_ref,
                   conv_ref, gqa_ref, wq_ref, gkva_ref, wk_ref, wv_ref, vones_ref, invfc_ref,
                   qT_m_ref, k_ref, vT_m_ref, o_ref, gc_ref, gt_ref, qT_ref, kcat_ref, vT_ref):
    tm = x_ref.shape[1]
    hr = tm // INPROJ_SPLIT
    i = pl.program_id(1)
    last = pl.num_programs(1) - 1
    g = gmix_ref[...]
    xm_bf = _rms(x_ref[0], g).astype(BF16)
    halo = [(_rms(xp_ref[0], g) * (i > 0).astype(F32)).astype(BF16),
            (_rms(xn_ref[0], g) * (i < last).astype(F32)).astype(BF16)]

    def matmuls(u):
        r = slice(u * hr, (u + 1) * hr)
        xs = xm_bf[r]
        before = halo[0] if u == 0 else xm_bf[u * hr - SUBLANES:u * hr]
        after = halo[1] if u == INPROJ_SPLIT - 1 else xm_bf[(u + 1) * hr:(u + 1) * hr + SUBLANES]
        lhs = jnp.concatenate([before, xs, after], axis=0)
        pqk = _dot(lhs, wqk_ref[...])
        misc = _dot(xs, wmisc_ref[...])
        vT_m = _dot_nt(wvt_ref[...], xs)
        o_ref[0, r] = _dot(xs, wo_ref[...])
        krT = _dot_nt(wkr_ref[...], xs)
        ckvn = _rms(misc[:, Q_LORA_RANK:Q_LORA_RANK + KV_LORA_RANK], gkva_ref[...]).astype(BF16)
        cqn = _rms(misc[:, :Q_LORA_RANK], gqa_ref[...]).astype(BF16)
        return dict(r=r, pqk=pqk, vT_m=vT_m, krT=krT, gp=misc[:, Q_LORA_RANK + KV_LORA_RANK:] + bg_ref[...],
                    kn=_dot(ckvn, wk_ref[...]), qT=_dot_nt(wq_ref[...], cqn), vT_a=_dot_nt(wv_ref[...], ckvn))

    lane = lax.broadcasted_iota(I32, (hr, LANES), 1)
    is_f = ((lane >= 4) & (lane < 8)) | ((lane >= 12) & (lane < 16))
    row_id = lax.broadcasted_iota(I32, (hr, LANES), 0)
    cw = conv_ref[...]
    r0, r1, r2 = MLA_NOPE_DIM, MLA_NOPE_DIM + MLA_ROPE_DIM // 2, MLA_QK_DIM

    def scan(x, op, ident, suffix):
        step = 1
        while step < hr:
            if step < SUBLANES:
                ok = (row_id + step < hr) if suffix else (row_id >= step)
                shifted = jnp.where(ok, pltpu.roll(x, hr - step if suffix else step, 0), ident)
            else:
                pad = jnp.full((step, LANES), ident, F32)
                shifted = jnp.concatenate([x[step:], pad] if suffix else [pad, x[:hr - step]], axis=0)
            x = op(x, shifted)
            step *= 2
        return x

    def elementwise(m):
        r = m["r"]
        vT_m_ref[0, :, r] = m["vT_m"].astype(BF16)

        pqk = m["pqk"]
        up = pltpu.roll(pqk, 1, 0)[SUBLANES:SUBLANES + hr]
        dn = pltpu.roll(pqk, hr + 2 * SUBLANES - 1, 0)[SUBLANES:SUBLANES + hr]
        mid = pqk[SUBLANES:SUBLANES + hr]
        conv = up * cw[0:1] + mid * cw[1:2] + dn * cw[2:3]
        act = conv * _sigmoid(conv)
        qT_m_ref[0, :, r] = act[:, :D_MLSTM].T.astype(BF16)
        k_ref[0, r] = (act[:, D_MLSTM:] * (MLSTM_HEAD_DIM ** -0.5)).astype(BF16)

        gp = m["gp"]
        logsig = jnp.minimum(gp, 0.0) - jnp.log1p(jnp.exp(-jnp.abs(gp)))
        lf = jnp.where(is_f, logsig, gp)
        cum = jnp.where(lane < 8, scan(lf, jnp.add, 0.0, False), scan(lf, jnp.add, 0.0, True))
        a = lf - pltpu.roll(cum, LANES - 4, 1)
        gc = jnp.where(is_f, cum, a)
        run_max = jnp.where(lane < 8, scan(gc, jnp.maximum, -jnp.inf, False),
                            scan(gc, jnp.maximum, -jnp.inf, True))
        table = jnp.where(lane < N_GATE, gc, pltpu.roll(run_max, N_GATE, 1))
        gc_ref[0, r] = table
        gt_ref[0, :, r] = table.T[:2 * N_GATE]

        ang_t = invfc_ref[...] * posr_ref[0, :, r]
        cos_t = jnp.cos(ang_t)
        sin_t = jnp.sin(ang_t)

        def rope_rows(blk):
            t1 = blk[r0:r1]
            t2 = blk[r1:r2]
            return jnp.concatenate([blk[:r0], t1 * cos_t - t2 * sin_t, t2 * cos_t + t1 * sin_t, blk[r2:]], axis=0)

        kr = rope_rows(m["krT"]).T
        for h in range(N_MLA_HEADS):
            sl = slice(h * LANES, (h + 1) * LANES)
            kcat_ref[0, r, sl] = (m["kn"][:, sl] + kr).astype(BF16)
            qT_ref[0, sl, r] = rope_rows(m["qT"][sl]).astype(BF16)
        vT_ref[0, :, r] = (m["vT_a"] + vones_ref[...]).astype(BF16)

    parts = [matmuls(u) for u in range(INPROJ_SPLIT)]
    for m in parts:
        elementwise(m)


def _inproj(x, pos_row, g_mix, w_in, b_gates, conv_qk, g_q_a, w_q_b, g_kv_a, w_kv_b):
    B, S, D = x.shape
    tm = TM_INPROJ
    assert S % tm == 0
    nt = S // tm
    hb = tm // SUBLANES
    sizes = [D_MLSTM, D_MLSTM, D_MLSTM, D_MLSTM, N_GATE, Q_LORA_RANK, KV_LORA_RANK, MLA_ROPE_DIM]
    off = [sum(sizes[:n]) for n in range(len(sizes) + 1)]
    assert w_in.shape[1] == off[-1]
    wqk = w_in[:, off[0]:off[2]].astype(BF16)
    wvt = w_in[:, off[2]:off[3]].T.astype(BF16)
    wo = w_in[:, off[3]:off[4]].astype(BF16)
    w_gate = w_in[:, off[4]:off[5]]
    w_cq = w_in[:, off[5]:off[6]]
    w_ckv = w_in[:, off[6]:off[7]]
    w_kr = w_in[:, off[7]:off[8]]
    wkr = jnp.zeros((LANES, D), F32).at[MLA_NOPE_DIM:MLA_QK_DIM].set(w_kr.T).astype(BF16)
    gate_blk = jnp.zeros((D, LANES), F32).at[:, :N_GATE].set(w_gate)
    wmisc = jnp.concatenate([w_cq, w_ckv, gate_blk], axis=1).astype(BF16)
    bg = jnp.zeros((1, LANES), F32).at[0, :N_GATE].set(b_gates)
    pad_heads = lambda w3: jnp.pad(w3, ((0, 0), (0, 0), (0, LANES - w3.shape[2]))).reshape(w3.shape[0], -1)
    wq = pad_heads(w_q_b.reshape(-1, N_MLA_HEADS, MLA_QK_DIM)).T.astype(BF16)
    wkv3 = w_kv_b.reshape(-1, N_MLA_HEADS, MLA_NOPE_DIM + MLA_V_DIM)
    wk = pad_heads(wkv3[:, :, :MLA_NOPE_DIM]).astype(BF16)
    wv = jnp.pad(wkv3[:, :, MLA_NOPE_DIM:], ((0, 0), (0, 0), (0, MLA_V_ROWS - MLA_V_DIM)))
    wv = wv.reshape(wv.shape[0], -1).T.astype(BF16)
    vones = jnp.zeros((N_MLA_HEADS, MLA_V_ROWS, MLSTM_CHUNK), F32).at[:, MLA_V_DIM].set(1.0)
    vones = vones.reshape(-1, MLSTM_CHUNK)
    inv_freq = ROPE_THETA ** (-jnp.arange(0, MLA_ROPE_DIM, 2, dtype=F32) / MLA_ROPE_DIM)
    invf_col = inv_freq.reshape(-1, 1)

    row = lambda w: pl.BlockSpec((1, tm, w), lambda b, i: (b, i, 0))
    in_specs = [
        row(D),
        pl.BlockSpec((1, SUBLANES, D), lambda b, i: (b, jnp.maximum(i * hb - 1, 0), 0)),
        pl.BlockSpec((1, SUBLANES, D), lambda b, i: (b, jnp.minimum((i + 1) * hb, S // SUBLANES - 1), 0)),
        pl.BlockSpec((1, 1, tm), lambda b, i: (b, 0, i)),
        _full((1, D)), _full(wqk.shape), _full(wvt.shape), _full(wo.shape), _full(wmisc.shape), _full(wkr.shape),
        _full((1, LANES)), _full(conv_qk.shape), _full((1, Q_LORA_RANK)), _full(wq.shape), _full((1, KV_LORA_RANK)),
        _full(wk.shape), _full(wv.shape), _full(vones.shape), _full(invf_col.shape),
    ]
    out_shape = [
        jax.ShapeDtypeStruct((B, D_MLSTM, S), BF16),
        jax.ShapeDtypeStruct((B, S, D_MLSTM), BF16),
        jax.ShapeDtypeStruct((B, D_MLSTM, S), BF16),
        jax.ShapeDtypeStruct((B, S, D_MLSTM), F32),
        jax.ShapeDtypeStruct((B, S, LANES), F32),
        jax.ShapeDtypeStruct((B, 2 * N_GATE, S), F32),
        jax.ShapeDtypeStruct((B, N_MLA_HEADS * LANES, S), BF16),
        jax.ShapeDtypeStruct((B, S, N_MLA_HEADS * LANES), BF16),
        jax.ShapeDtypeStruct((B, N_MLA_HEADS * MLA_V_ROWS, S), BF16),
    ]
    out_specs = [
        pl.BlockSpec((1, D_MLSTM, tm), lambda b, i: (b, 0, i)),
        row(D_MLSTM),
        pl.BlockSpec((1, D_MLSTM, tm), lambda b, i: (b, 0, i)),
        row(D_MLSTM), row(LANES),
        pl.BlockSpec((1, 2 * N_GATE, tm), lambda b, i: (b, 0, i)),
        pl.BlockSpec((1, N_MLA_HEADS * LANES, tm), lambda b, i: (b, 0, i)),
        row(N_MLA_HEADS * LANES),
        pl.BlockSpec((1, N_MLA_HEADS * MLA_V_ROWS, tm), lambda b, i: (b, 0, i)),
    ]
    return pl.pallas_call(
        _inproj_kernel, grid=(B, nt), in_specs=in_specs, out_specs=out_specs, out_shape=out_shape,
        compiler_params=_cparams(("parallel", "parallel")), name="inproj",
    )(x, x, x, pos_row, g_mix.reshape(1, D), wqk, wvt, wo, wmisc, wkr, bg, conv_qk, g_q_a.reshape(1, -1), wq,
      g_kv_a.reshape(1, -1), wk, wv, vones, invf_col)


def _mlstm_kernel(kf_ref, qf_ref, vf_ref, gcf_ref, gtf_ref, kb_ref, qb_ref, vb_ref, gcb_ref, gtb_ref,
                  hf_ref, hb_ref, cn_ref, m_ref):
    T = MLSTM_CHUNK
    hd = MLSTM_HEAD_DIM

    @pl.when(pl.program_id(1) == 0)
    def _():
        cn_ref[...] = jnp.zeros_like(cn_ref)
        m_ref[...] = jnp.zeros_like(m_ref)

    s_idx = lax.broadcasted_iota(I32, (T, T), 0)
    t_idx = lax.broadcasted_iota(I32, (T, T), 1)
    ones_rows = (lax.broadcasted_iota(I32, (MLSTM_PAD_ROWS, T), 0) == 0).astype(BF16)
    neg_inf = F32(-jnp.inf)

    def chunk(c):
        streams = []
        for d, (k_ref, qT_ref, vT_ref, gc_ref, gt_ref, h_ref) in enumerate(
                ((kf_ref, qf_ref, vf_ref, gcf_ref, gtf_ref, hf_ref),
                 (kb_ref, qb_ref, vb_ref, gcb_ref, gtb_ref, hb_ref))):
            end = T - 1 if d == 0 else 0
            pos = c if d == 0 else MLSTM_CHUNKS_PER_STEP - 1 - c
            tok = slice(pos * T, (pos + 1) * T)
            gc = gc_ref[0, tok]
            gt = gt_ref[0, :, tok]
            for h in range(N_MLSTM_HEADS):
                st = d * N_MLSTM_HEADS + h
                la = d * 8 + h
                a_row = gt[la:la + 1]
                b_row = gt[la + 4:la + 5]
                m_prev = m_ref[st][0:1, 0:1]
                kh = k_ref[0, tok, h * hd:(h + 1) * hd]
                qTh = qT_ref[0, h * hd:(h + 1) * hd, tok]
                v_ext = jnp.concatenate([vT_ref[0, h * hd:(h + 1) * hd, tok], ones_rows], axis=0)
                cn = cn_ref[st]
                m_run = jnp.maximum(gt[N_GATE + la:N_GATE + la + 1], m_prev)
                m_end = m_run[:, end:end + 1]
                vw = (v_ext.astype(F32) * jnp.exp(a_row - m_end)).astype(BF16)
                streams.append(dict(
                    st=st, d=d, h=h, h_ref=h_ref, tok=tok, a_col=gc[:, la:la + 1], b_row=b_row, m_prev=m_prev,
                    m_run=m_run, m_end=m_end, g_sum=b_row[:, end:end + 1], cn=cn, v_ext=v_ext,
                    s=_dot(kh, qTh),
                    qcn=_dot(cn.astype(BF16), qTh),
                    upd=_dot(vw, kh)))
        for x in streams:
            mask = (s_idx <= t_idx) if x["d"] == 0 else (s_idx >= t_idx)
            e = jnp.exp(jnp.where(mask, x["a_col"] - x["m_run"], neg_inf))
            x["pw"] = (e * x["s"]).astype(BF16)
        for x in streams:
            x["pv"] = _dot(x["v_ext"], x["pw"])
        for x in streams:
            st, h = x["st"], x["h"]
            iw = jnp.exp(x["m_prev"] - x["m_run"])
            num = iw * x["qcn"][:hd] + x["pv"][:hd]
            den = iw * x["qcn"][hd:hd + 1] + x["pv"][hd:hd + 1]
            floor = jnp.exp(-(x["b_row"] + x["m_run"]))
            x["h_ref"][0, h * hd:(h + 1) * hd, x["tok"]] = num / jnp.maximum(jnp.abs(den), floor)
            cn_ref[st] = jnp.exp(x["m_prev"] - x["m_end"]) * x["cn"] + x["upd"]
            m_ref[st] = jnp.broadcast_to(x["g_sum"] + x["m_end"], (SUBLANES, LANES))

    for c in range(MLSTM_CHUNKS_PER_STEP):
        chunk(c)


def _mlstm(k, qT, vT, gc, gt):
    B, S, _ = k.shape
    T = MLSTM_CHUNK * MLSTM_CHUNKS_PER_STEP
    nc = S // T
    fwd = lambda b, j: (b, j, 0)
    bwd = lambda b, j: (b, nc - 1 - j, 0)
    fwd_t = lambda b, j: (b, 0, j)
    bwd_t = lambda b, j: (b, 0, nc - 1 - j)

    def specs(row_map, col_map):
        return [pl.BlockSpec((1, T, D_MLSTM), row_map), pl.BlockSpec((1, D_MLSTM, T), col_map),
                pl.BlockSpec((1, D_MLSTM, T), col_map), pl.BlockSpec((1, T, LANES), row_map),
                pl.BlockSpec((1, 2 * N_GATE, T), col_map)]

    n_state = 2 * N_MLSTM_HEADS
    return pl.pallas_call(
        _mlstm_kernel, grid=(B, nc),
        in_specs=specs(fwd, fwd_t) + specs(bwd, bwd_t),
        out_specs=[pl.BlockSpec((1, D_MLSTM, T), fwd_t), pl.BlockSpec((1, D_MLSTM, T), bwd_t)],
        out_shape=[jax.ShapeDtypeStruct((B, D_MLSTM, S), F32)] * 2,
        scratch_shapes=[pltpu.VMEM((n_state, MLSTM_HEAD_DIM + MLSTM_PAD_ROWS, MLSTM_HEAD_DIM), F32),
                        pltpu.VMEM((n_state, SUBLANES, LANES), F32)],
        compiler_params=_cparams(("parallel", "arbitrary")), name="mlstm",
    )(k, qT, vT, gc, gt, k, qT, vT, gc, gt)


def _mla_kernel(qT_ref, k_ref, vT_ref, g_ref, o_ref, sa_sc, sb_sc, ma_sc, mb_sc):
    t = pl.program_id(0)
    c = (MLA_QK_DIM ** -0.5) * math.log2(math.e)
    S = k_ref.shape[1]
    tq = qT_ref.shape[2]
    ck = S // MLA_CHUNKS

    @pl.when(t == 0)
    def _():
        sb_sc[...] = jnp.zeros(sb_sc.shape, F32)
        mb_sc[...] = jnp.zeros(mb_sc.shape, F32)

    def step(s_cur, m_cur, s_prev, m_prev_ref):
        m_prev = [m_prev_ref[h][0:1] for h in range(2)]
        m_run = [jnp.full((1, tq), -jnp.inf, F32) for _ in range(2)]
        accs = [None, None]
        for j in range(MLA_CHUNKS):
            rows = slice(j * ck, (j + 1) * ck)
            for h in range(2):
                s = _dot(k_ref[0, rows, h * LANES:(h + 1) * LANES], qT_ref[0, h * LANES:(h + 1) * LANES, :])
                m_run[h] = jnp.maximum(m_run[h], jnp.max(s, axis=0, keepdims=True))
                s_cur[h, rows, :] = s
            for h in range(2):
                p = jnp.exp2((s_prev[h, rows, :] - m_prev[h]) * c).astype(BF16)
                part = _dot(vT_ref[0, h * MLA_V_ROWS:(h + 1) * MLA_V_ROWS, rows], p)
                accs[h] = part if j == 0 else accs[h] + part
        for h in range(2):
            m_cur[h] = jnp.broadcast_to(m_run[h], (SUBLANES, tq))
        ys = []
        for h in range(2):
            acc = accs[h]
            o = acc[:MLA_V_DIM] / acc[MLA_V_DIM:MLA_V_DIM + 1]
            ms = jnp.mean(o * o, axis=0, keepdims=True)
            ys.append(o * lax.rsqrt(ms + EPS))
        o_ref[0] = jnp.concatenate(ys, axis=0).T * g_ref[...]

    @pl.when(t % 2 == 0)
    def _():
        step(sa_sc, ma_sc, sb_sc, mb_sc)

    @pl.when(t % 2 == 1)
    def _():
        step(sb_sc, mb_sc, sa_sc, ma_sc)


def _mla(qT, kcat, vT, g_head_mla):
    B, S, _ = kcat.shape
    tq = TQ_MLA
    npair = N_MLA_HEADS // 2
    nq = S // tq
    nt = B * npair * nq

    def tile(t):
        return t // (npair * nq), (t // nq) % npair, t % nq

    def a_map(f):
        return lambda t: f(*tile(jnp.minimum(t, nt - 1)))

    def b_map(f):
        return lambda t: f(*tile(jnp.maximum(t - 1, 0)))

    return pl.pallas_call(
        _mla_kernel, grid=(nt + 1,),
        in_specs=[pl.BlockSpec((1, 2 * LANES, tq), a_map(lambda b, p, i: (b, p, i))),
                  pl.BlockSpec((1, S, 2 * LANES), a_map(lambda b, p, i: (b, 0, p))),
                  pl.BlockSpec((1, 2 * MLA_V_ROWS, S), b_map(lambda b, p, i: (b, p, 0))),
                  pl.BlockSpec((1, LANES), b_map(lambda b, p, i: (0, p)))],
        out_specs=pl.BlockSpec((1, tq, LANES), b_map(lambda b, p, i: (b, i, p))),
        out_shape=jax.ShapeDtypeStruct((B, S, N_MLA_HEADS * MLA_V_DIM), F32),
        scratch_shapes=[pltpu.VMEM((2, S, tq), F32), pltpu.VMEM((2, S, tq), F32),
                        pltpu.VMEM((2, SUBLANES, tq), F32), pltpu.VMEM((2, SUBLANES, tq), F32)],
        compiler_params=_cparams(("arbitrary",)), name="mla_attn",
    )(qT, kcat, vT, g_head_mla.reshape(1, -1))


def _memkv_kernel(mem_ref, g_ref, wk_ref, wv_ref, k_ref, v_ref):
    mn = _rms(mem_ref[0], g_ref[...]).astype(BF16)
    k_ref[0] = _dot(mn, wk_ref[...]).astype(BF16)
    v_ref[0] = _dot(mn, wv_ref[...]).astype(BF16)


def _memkv(mem, g_mem_kv, w_k, w_v):
    B, M, D = mem.shape
    blk = pl.BlockSpec((1, M, D), lambda b: (b, 0, 0))
    return pl.pallas_call(
        _memkv_kernel, grid=(B,),
        in_specs=[blk, _full((1, D)), _full(w_k.shape), _full(w_v.shape)],
        out_specs=[blk, blk], out_shape=[jax.ShapeDtypeStruct((B, M, D), BF16)] * 2,
        compiler_params=_cparams(("parallel",)), name="memkv",
    )(mem, g_mem_kv.reshape(1, D), w_k.astype(BF16), w_v.astype(BF16))


def _post_kernel(x_ref, hf_ref, hb_ref, om_ref, ya_ref, km_ref, vm_ref, ghm_ref, wout_ref, gmx_ref, wmq_ref,
                 wmo_ref, gffn_ref, wrh_ref, wrl_ref, x2_ref, hn_ref, aff_ref, afft_ref):
    tm = x_ref.shape[1]
    D = x_ref.shape[2]
    hr = tm // POST_SPLIT
    subs = [slice(i * hr, (i + 1) * hr) for i in range(POST_SPLIT)]
    ghm = ghm_ref[...]
    dm = D // N_MEM_HEADS
    heads = [slice(h * dm, (h + 1) * dm) for h in range(N_MEM_HEADS)]

    ycat = []
    for r in subs:
        hm = (hf_ref[0, :, r] + hb_ref[0, :, r]).T
        gate = _sigmoid(om_ref[0, r])
        parts = []
        for h in range(N_MLSTM_HEADS):
            sl = slice(h * MLSTM_HEAD_DIM, (h + 1) * MLSTM_HEAD_DIM)
            parts.append(gate[:, sl] * _rms(hm[:, sl], ghm[:, sl]))
        ycat.append(jnp.concatenate(parts + [ya_ref[0, r]], axis=1).astype(BF16))
    x1 = [x_ref[0, r] + _dot(y, wout_ref[...]) for r, y in zip(subs, ycat)]

    xn = [_rms(v, gmx_ref[...]).astype(BF16) for v in x1]
    qm = [_dot(v, wmq_ref[...]).astype(BF16) for v in xn]
    km = km_ref[0]
    vm = vm_ref[0]
    scores = [[_dot_nt(q[:, sl], km[:, sl]) * (dm ** -0.5) for sl in heads] for q in qm]
    probs = []
    for per_sub in scores:
        ps = []
        for s in per_sub:
            e = jnp.exp(s - jnp.max(s, axis=1, keepdims=True))
            ps.append((e / jnp.sum(e, axis=1, keepdims=True)).astype(BF16))
        probs.append(ps)
    om = [jnp.concatenate([_dot(p, vm[:, sl]) for p, sl in zip(ps, heads)], axis=1).astype(BF16) for ps in probs]
    x2 = [v + _dot(o, wmo_ref[...]) for v, o in zip(x1, om)]

    hns = []
    for r, v in zip(subs, x2):
        x2_ref[0, r] = v
        hn = _rms(v, gffn_ref[...])
        hn_ref[0, r] = hn.reshape(hr, SUBLANES, D // SUBLANES)
        hi = hn.astype(BF16)
        hns.append((hi, (hn - hi.astype(F32)).astype(BF16)))
    wrh = wrh_ref[...]
    logits = [_dot(hi, wrh) + (_dot(hi, wrl_ref[...]) + _dot(lo, wrh)) for hi, lo in hns]
    valid = lax.broadcasted_iota(I32, (hr, LANES), 1) < N_EXPERTS
    for r, lg in zip(subs, logits):
        lg = jnp.where(valid, lg, -jnp.inf)
        e = jnp.exp(lg - jnp.max(lg, axis=1, keepdims=True))
        aff = e / jnp.sum(e, axis=1, keepdims=True)
        aff_ref[0, r] = aff
        afft_ref[0, :, r] = aff.T[:N_EXPERTS]


def _post(x, hf, hb, om, ya, kmem, vmem, g_head_mlstm, w_out, g_mem_x, w_mem_q, w_mem_o, g_ffn, w_router):
    B, S, D = x.shape
    tm = TM_POST
    M = kmem.shape[1]
    wr = jnp.zeros((D, LANES), F32).at[:, :N_EXPERTS].set(w_router)
    wrh = wr.astype(BF16)
    wrl = (wr - wrh.astype(F32)).astype(BF16)
    row = lambda w: pl.BlockSpec((1, tm, w), lambda b, i: (b, i, 0))
    col = lambda w: pl.BlockSpec((1, w, tm), lambda b, i: (b, 0, i))
    mem_spec = pl.BlockSpec((1, M, D), lambda b, i: (b, 0, 0))
    return pl.pallas_call(
        _post_kernel, grid=(B, S // tm),
        in_specs=[row(D), col(D_MLSTM), col(D_MLSTM), row(D_MLSTM), row(D_MLSTM), mem_spec, mem_spec,
                  _full((1, D_MLSTM)), _full((D, D)), _full((1, D)), _full((D, D)), _full((D, D)),
                  _full((1, D)), _full((D, LANES)), _full((D, LANES))],
        out_specs=[row(D), pl.BlockSpec((1, tm, SUBLANES, D // SUBLANES), lambda b, i: (b, i, 0, 0)), row(LANES),
                   pl.BlockSpec((1, N_EXPERTS, tm), lambda b, i: (b, 0, i))],
        out_shape=[jax.ShapeDtypeStruct((B, S, D), F32), jax.ShapeDtypeStruct((B, S, SUBLANES, D // SUBLANES), F32),
                   jax.ShapeDtypeStruct((B, S, LANES), F32), jax.ShapeDtypeStruct((B, N_EXPERTS, S), F32)],
        compiler_params=_cparams(("parallel", "parallel")), name="post_mixer",
    )(x, hf, hb, om, ya, kmem, vmem, g_head_mlstm.reshape(1, -1), w_out.astype(BF16), g_mem_x.reshape(1, D),
      w_mem_q.astype(BF16), w_mem_o.astype(BF16), g_ffn.reshape(1, D), wrh, wrl)


def _topk_kernel(aff_ref, afft_ref, affb_ref, idx_ref, gate_ref, idx_sc, gate_sc):
    S = aff_ref.shape[1]
    cap = idx_ref.shape[2]
    nblk = S // CUM_BLOCK

    def open_interval(c):
        lo, hi = c
        mid = 0.5 * (lo + hi)
        return jnp.max(jnp.where((mid > lo) & (mid < hi), 1.0, 0.0)) > 0.0

    def bisect(c):
        lo, hi = c
        mid = 0.5 * (lo + hi)
        cnt = jnp.sum((afft_ref[0] >= mid).astype(F32), axis=1, keepdims=True)
        ge = cnt >= cap
        return jnp.where(ge, mid, lo), jnp.where(ge, hi, mid)

    lo_c, hi_c = lax.while_loop(open_interval, bisect,
                                (jnp.zeros((N_EXPERTS, 1), F32), jnp.full((N_EXPERTS, 1), 2.0, F32)))
    need_c = cap - jnp.sum((afft_ref[0] >= hi_c).astype(F32), axis=1, keepdims=True)
    eye = (lax.broadcasted_iota(I32, (N_EXPERTS, LANES), 0) == lax.broadcasted_iota(I32, (N_EXPERTS, LANES), 1))
    to_row = lambda col: jnp.sum(jnp.where(eye, col, 0.0), axis=0, keepdims=True)
    lo, hi, need = to_row(lo_c), to_row(hi_c), to_row(need_c)

    r = lax.broadcasted_iota(I32, (CUM_BLOCK, CUM_BLOCK), 0)
    c = lax.broadcasted_iota(I32, (CUM_BLOCK, CUM_BLOCK), 1)
    tri = (c <= r).astype(BF16)

    def cum_body(blk, carry):
        ce, cs = carry
        rows = pl.ds(pl.multiple_of(blk * CUM_BLOCK, CUM_BLOCK), CUM_BLOCK)
        a = aff_ref[0, rows, :]
        sure = a >= hi
        tie = (a >= lo) & jnp.logical_not(sure)
        eq = tie.astype(F32)
        eq_incl = _dot(tri, eq.astype(BF16)) + ce
        sel = jnp.where(sure | (tie & (eq_incl - eq < need)), 1.0, 0.0)
        local_incl = _dot(tri, sel.astype(BF16))
        rank_local = jnp.where(sel > 0.0, local_incl - sel, -2.0).astype(BF16)
        base = lax.convert_element_type(blk * CUM_BLOCK, F32)
        for e in range(N_EXPERTS):
            slot_local = jnp.clip(j_row - cs[:, e:e + 1], -1.0, CUM_BLOCK + 1.0).astype(BF16)
            onehot = jnp.where(rank_local[:, e:e + 1] == slot_local, one_bf, zero_bf)
            a_row = affb_ref[0, e, pl.ds(blk, 1), :]
            a_hi = a_row.astype(BF16).astype(F32)
            a_mid = (a_row - a_hi).astype(BF16).astype(F32)
            a_lo = a_row - a_hi - a_mid
            lhs = jnp.concatenate([tok_rows, a_hi, a_mid, a_lo, pad_rows], axis=0).astype(BF16)
            hit = _dot(lhs, onehot)
            idx_sc[e:e + 1, :] += hit[0:1] + base * hit[1:2]
            gate_sc[e:e + 1, :] += hit[2:3] + hit[3:4] + hit[4:5]
        return eq_incl[CUM_BLOCK - 1:CUM_BLOCK], cs + local_incl[CUM_BLOCK - 1:CUM_BLOCK]

    j_row = lax.broadcasted_iota(I32, (1, cap), 1).astype(F32)
    tok = lax.broadcasted_iota(I32, (1, CUM_BLOCK), 1).astype(F32)
    tok_rows = jnp.concatenate([tok, jnp.ones((1, CUM_BLOCK), F32)], axis=0)
    pad_rows = jnp.zeros((SUBLANES - 5, CUM_BLOCK), F32)
    one_bf = jnp.ones((CUM_BLOCK, cap), BF16)
    zero_bf = jnp.zeros((CUM_BLOCK, cap), BF16)
    idx_sc[...] = jnp.zeros_like(idx_sc)
    gate_sc[...] = jnp.zeros_like(gate_sc)
    zero = jnp.zeros((1, LANES), F32)
    lax.fori_loop(0, nblk, cum_body, (zero, zero))
    idx_ref[0] = idx_sc[...].astype(I32)
    gate_ref[0] = gate_sc[...]


def _topk(aff, aff_t, cap):
    B, S, _ = aff.shape
    nblk = S // CUM_BLOCK
    out = pl.BlockSpec((1, N_EXPERTS, cap), lambda b: (b, 0, 0))
    return pl.pallas_call(
        _topk_kernel, grid=(B,),
        in_specs=[pl.BlockSpec((1, S, LANES), lambda b: (b, 0, 0)),
                  pl.BlockSpec((1, N_EXPERTS, S), lambda b: (b, 0, 0)),
                  pl.BlockSpec((1, N_EXPERTS, nblk, CUM_BLOCK), lambda b: (b, 0, 0, 0))],
        out_specs=[out, out],
        out_shape=[jax.ShapeDtypeStruct((B, N_EXPERTS, cap), I32), jax.ShapeDtypeStruct((B, N_EXPERTS, cap), F32)],
        scratch_shapes=[pltpu.VMEM((N_EXPERTS, cap), F32), pltpu.VMEM((N_EXPERTS, cap), F32)],
        compiler_params=_cparams(("parallel",)), name="topk",
    )(aff, aff_t, aff_t.reshape(B, N_EXPERTS, nblk, CUM_BLOCK))


def _gather_kernel(idx_ref, hn_hbm, xe_ref, buf_a, buf_b, rows_sc, sem):
    b = pl.program_id(0)
    e = pl.program_id(1)
    cap = xe_ref.shape[2]

    def fetch(seq, buf, slot):
        return pltpu.make_async_copy(hn_hbm.at[seq], buf, sem.at[slot])

    @pl.when((b == 0) & (e == 0))
    def _():
        fetch(0, buf_a, 0).start()

    def run(cur, cur_slot, nxt, nxt_slot):
        @pl.when(e == 0)
        def _():
            fetch(b, cur, cur_slot).wait()

            @pl.when(b + 1 < pl.num_programs(0))
            def _():
                fetch(b + 1, nxt, nxt_slot).start()

        def body(j, _):
            i = idx_ref[0, 0, j]
            rows_sc[pl.ds(j, 1)] = cur[pl.ds(i, 1)]
            return 0

        lax.fori_loop(0, cap, body, 0, unroll=8)
        xe_ref[0, 0] = rows_sc[...].reshape(cap, xe_ref.shape[3]).astype(BF16)

    @pl.when(b % 2 == 0)
    def _():
        run(buf_a, 0, buf_b, 1)

    @pl.when(b % 2 == 1)
    def _():
        run(buf_b, 1, buf_a, 0)


def _gather(idx3, hn, cap):
    B, S, sub, lanes = hn.shape
    E = N_EXPERTS
    D = sub * lanes
    seq_buf = pltpu.VMEM((S, sub, lanes), F32)
    return pl.pallas_call(
        _gather_kernel, grid=(B, E),
        in_specs=[pl.BlockSpec((1, 1, cap), lambda b, e: (b * E + e, 0, 0), memory_space=pltpu.SMEM),
                  pl.BlockSpec(memory_space=pl.ANY)],
        out_specs=pl.BlockSpec((1, 1, cap, D), lambda b, e: (b, e, 0, 0)),
        out_shape=jax.ShapeDtypeStruct((B, E, cap, D), BF16),
        scratch_shapes=[seq_buf, seq_buf, pltpu.VMEM((cap, sub, lanes), F32), pltpu.SemaphoreType.DMA((2,))],
        compiler_params=_cparams(("arbitrary", "arbitrary")), name="moe_gather",
    )(idx3, hn)


def _ffn_kernel(xe_ref, wg_ref, wu_ref, wd_ref, ye_ref, wg_sc, wu_sc, wd_sc):
    f = pl.program_id(1)
    nb = xe_ref.shape[0]

    @pl.when(f == 0)
    def _():
        ye_ref[...] = jnp.zeros_like(ye_ref)

    wg_sc[...] = wg_ref[0].astype(BF16)
    wu_sc[...] = wu_ref[0].astype(BF16)
    wd_sc[...] = wd_ref[0].astype(BF16)

    def up(b):
        xb = xe_ref[b, 0]
        return _dot(xb, wg_sc[...]), _dot(xb, wu_sc[...])

    def down(b, h1, h2):
        hid = (h1 * _sigmoid(h1) * h2).astype(BF16)
        ye_ref[b, 0] += _dot(hid, wd_sc[...])

    hs = up(0)
    for b in range(nb):
        nxt = up(b + 1) if b + 1 < nb else None
        down(b, *hs)
        hs = nxt


def _ffn(xe, w_gate, w_up, w_down):
    B, E, cap, D = xe.shape
    F = w_gate.shape[2]
    tf = TF_FFN
    return pl.pallas_call(
        _ffn_kernel, grid=(E, F // tf),
        in_specs=[pl.BlockSpec((B, 1, cap, D), lambda e, f: (0, e, 0, 0)),
                  pl.BlockSpec((1, D, tf), lambda e, f: (e, 0, f)),
                  pl.BlockSpec((1, D, tf), lambda e, f: (e, 0, f)),
                  pl.BlockSpec((1, tf, D), lambda e, f: (e, f, 0))],
        out_specs=pl.BlockSpec((B, 1, cap, D), lambda e, f: (0, e, 0, 0)),
        out_shape=jax.ShapeDtypeStruct((B, E, cap, D), F32),
        scratch_shapes=[pltpu.VMEM((D, tf), BF16), pltpu.VMEM((D, tf), BF16), pltpu.VMEM((tf, D), BF16)],
        compiler_params=_cparams(("parallel", "arbitrary")), name="moe_ffn",
    )(xe, w_gate, w_up, w_down)


def _scatter_final_kernel(idx_ref, gate_ref, g_ref, ye_hbm, x2_hbm, out_hbm, acc_sc, ye_sc, ye_buf, x_buf, o_buf,
                          ye_sem, in_sem, out_sem):
    b = pl.program_id(0)
    e = pl.program_id(1)
    n_e = pl.num_programs(1)
    total = pl.num_programs(0) * n_e
    t = b * n_e + e
    cap = ye_sc.shape[0]
    S = acc_sc.shape[0]
    ch = x_buf.shape[1]
    D = x_buf.shape[2]
    ring = ye_buf.shape[0]

    def ye_copy(step):
        slot = step % ring
        return pltpu.make_async_copy(ye_hbm.at[step // n_e, step % n_e], ye_buf.at[slot], ye_sem.at[slot])

    @pl.when(t == 0)
    def _():
        for s in range(ring - 1):
            ye_copy(s).start()

    @pl.when(e == 0)
    def _():
        acc_sc[...] = jnp.zeros_like(acc_sc)

    ye_copy(t).wait()

    @pl.when(t + ring - 1 < total)
    def _():
        ye_copy(t + ring - 1).start()

    ye_sc[...] = ye_buf[t % ring].reshape(ye_sc.shape)

    def body(jb, _):
        base = pl.multiple_of(jb * SCATTER_GROUP, SCATTER_GROUP)
        ids = [idx_ref[0, 0, base + u] for u in range(SCATTER_GROUP)]
        new = [acc_sc[pl.ds(ids[u], 1)] + ye_sc[pl.ds(base + u, 1)] * gate_ref[0, 0, base + u]
               for u in range(SCATTER_GROUP)]
        for u in range(SCATTER_GROUP):
            acc_sc[pl.ds(ids[u], 1)] = new[u]
        return 0

    lax.fori_loop(0, cap // SCATTER_GROUP, body, 0)

    n_in = x_buf.shape[0]
    n_out = o_buf.shape[0]

    def in_copy(c):
        return pltpu.make_async_copy(x2_hbm.at[b, pl.ds(c * ch, ch)], x_buf.at[c % n_in], in_sem.at[c % n_in])

    def out_copy(c):
        return pltpu.make_async_copy(o_buf.at[c % n_out], out_hbm.at[b, pl.ds(c * ch, ch)], out_sem.at[c % n_out])

    @pl.when(e == pl.num_programs(1) - 1)
    def _():
        nch = S // ch
        ahead = n_in - 1
        for c in range(min(ahead, nch)):
            in_copy(c).start()
        for c in range(nch):
            in_copy(c).wait()
            if c + ahead < nch:
                in_copy(c + ahead).start()
            if c >= n_out:
                out_copy(c - n_out).wait()
            moe = acc_sc[c * ch:(c + 1) * ch].reshape(ch, D)
            o_buf[c % n_out] = _rms(x_buf[c % n_in] + moe, g_ref[...])
            out_copy(c).start()
        for c in range(max(nch - n_out, 0), nch):
            out_copy(c).wait()


def _scatter_final(idx3, gate3, ye, x2, g_final):
    B, E, cap, D = ye.shape
    S = x2.shape[1]
    lanes = D // SUBLANES
    smem = lambda: pl.BlockSpec((1, 1, cap), lambda b, e: (b * E + e, 0, 0), memory_space=pltpu.SMEM)
    return pl.pallas_call(
        _scatter_final_kernel, grid=(B, E),
        in_specs=[smem(), smem(), _full((1, D)), pl.BlockSpec(memory_space=pl.ANY),
                  pl.BlockSpec(memory_space=pl.ANY)],
        out_specs=pl.BlockSpec(memory_space=pl.ANY),
        out_shape=jax.ShapeDtypeStruct((B, S, D), F32),
        scratch_shapes=[pltpu.VMEM((S, SUBLANES, lanes), F32), pltpu.VMEM((cap, SUBLANES, lanes), F32),
                        pltpu.VMEM((YE_RING, cap, D), F32),
                        pltpu.VMEM((FINAL_IN_BUFS, FINAL_CHUNK, D), F32),
                        pltpu.VMEM((FINAL_OUT_BUFS, FINAL_CHUNK, D), F32),
                        pltpu.SemaphoreType.DMA((YE_RING,)),
                        pltpu.SemaphoreType.DMA((FINAL_IN_BUFS,)), pltpu.SemaphoreType.DMA((FINAL_OUT_BUFS,))],
        compiler_params=_cparams(("arbitrary", "arbitrary")), name="moe_scatter_final",
    )(idx3, gate3, g_final.reshape(1, D), ye, x2)


def kernel(x, mem, positions, g_mix, w_in, b_gates, conv_qk, g_q_a, w_q_b, g_kv_a, w_kv_b, g_head_mlstm,
           g_head_mla, w_out, g_mem_x, g_mem_kv, w_mem_q, w_mem_k, w_mem_v, w_mem_o, g_ffn, w_router,
           w_exp_gate, w_exp_up, w_exp_down, g_final):
    B, S, D = x.shape
    depth = g_mix.shape[0]
    assert depth == 1, "the MoE residual is folded into the final norm kernel, which assumes one layer"
    cap = EC_CAPACITY_FACTOR * S // N_EXPERTS
    pos_row = positions.astype(F32).reshape(B, 1, S)
    for l in range(depth):
        qT_m, k_m, vT_m, om, gc, gt, qT, kcat, vT = _inproj(
            x, pos_row, g_mix[l], w_in[l], b_gates[l], conv_qk[l], g_q_a[l], w_q_b[l], g_kv_a[l], w_kv_b[l])
        hf, hb = _mlstm(k_m, qT_m, vT_m, gc, gt)
        ya = _mla(qT, kcat, vT, g_head_mla[l])
        kmem, vmem = _memkv(mem, g_mem_kv[l], w_mem_k[l], w_mem_v[l])
        x2, hn, aff, aff_t = _post(x, hf, hb, om, ya, kmem, vmem, g_head_mlstm[l], w_out[l], g_mem_x[l],
                            w_mem_q[l], w_mem_o[l], g_ffn[l], w_router[l])
        idx, gate = _topk(aff, aff_t, cap)
        idx3 = idx.reshape(B * N_EXPERTS, 1, cap)
        xe = _gather(idx3, hn, cap)
        ye = _ffn(xe, w_exp_gate[l], w_exp_up[l], w_exp_down[l])
        x = _scatter_final(idx3, gate.reshape(B * N_EXPERTS, 1, cap), ye, x2, g_final)
    return x
```
